```python
import math
import jax, jax.numpy as jnp
from jax import lax
import numpy as np


D_MODEL = 2048
BATCH = 1
SEQ = 8192
DEPTH = 1
DEC_BATCH = 128
DEC_SEQ = 4
PAST_LEN = 2048
PAGE_SIZE = 128

SSM_WIDTH = D_MODEL // 2
SSM_GROUP = 16
SSM_GROUPS = SSM_WIDTH // SSM_GROUP
SSM_STATE = 64
DT_MIN = 0.001
DT_MAX = 0.1
HEAD_DIM = 128
DILATION_PATTERNS = ((128, 1), (512, 4), (2048, 16))
N_PATTERNS = len(DILATION_PATTERNS)
HEADS_PER_GROUP = 4
N_HEADS = N_PATTERNS * HEADS_PER_GROUP
ATTN_WIDTH = N_HEADS * HEAD_DIM
ROT_DIM = HEAD_DIM // 4
ROPE_THETA = 500000.0
OFF_Q = SSM_WIDTH
OFF_K = OFF_Q + ATTN_WIDTH
OFF_V = OFF_K + ATTN_WIDTH
OFF_G = OFF_V + ATTN_WIDTH
IN_COLS = OFF_G + 2 * D_MODEL
N_EXPERT_GROUPS = 4
EXPERTS_PER_GROUP = 4
N_EXPERTS = N_EXPERT_GROUPS * EXPERTS_PER_GROUP
EXPERT_TOP_K = 2
EXPERT_FF = D_MODEL // 4
EPS = 1e-6
F32 = jnp.float32

kernel_name = 'hybrid_s5_dilated_attn_hmoe_step'


def rms_norm(x, g):
    xf = x.astype(F32)
    y = xf * lax.rsqrt(jnp.mean(xf * xf, axis=-1, keepdims=True) + EPS)
    return (y * g.astype(F32)).astype(x.dtype)


def rope_partial(x, pos):
    half = ROT_DIM // 2
    inv_freq = ROPE_THETA ** (-jnp.arange(half, dtype=F32) / half)
    ang = pos.astype(F32)[:, None] * inv_freq[None, :]
    cos = jnp.cos(ang)[None, :, None, :]
    sin = jnp.sin(ang)[None, :, None, :]
    xf = x.astype(F32)
    x1 = xf[..., :half]
    x2 = xf[..., half:ROT_DIM]
    out = jnp.concatenate([x1 * cos - x2 * sin, x2 * cos + x1 * sin, xf[..., ROT_DIM:]], axis=-1)
    return out.astype(x.dtype)


def _complex_combine(e1, e2):
    a1r, a1i, b1r, b1i = e1
    a2r, a2i, b2r, b2i = e2
    return (a2r * a1r - a2i * a1i,
            a2r * a1i + a2i * a1r,
            a2r * b1r - a2i * b1i + b2r,
            a2r * b1i + a2i * b1r + b2i)


def s5_branch(u, s0_re, s0_im, a_re, a_im, log_dt, b_re, b_im, c_re, c_im, d_skip):
    Bsz, T, _ = u.shape
    a_re = a_re.astype(F32)
    a_im = a_im.astype(F32)
    dt = jnp.exp(log_dt.astype(F32))[:, None]
    mag = jnp.exp(a_re * dt)
    abar_re = mag * jnp.cos(a_im * dt)
    abar_im = mag * jnp.sin(a_im * dt)
    nr = abar_re - 1.0
    ni = abar_im
    den = a_re * a_re + a_im * a_im
    z_re = (nr * a_re + ni * a_im) / den
    z_im = (ni * a_re - nr * a_im) / den
    b_re = b_re.astype(F32)
    b_im = b_im.astype(F32)
    bbar_re = z_re[..., None] * b_re - z_im[..., None] * b_im
    bbar_im = z_re[..., None] * b_im + z_im[..., None] * b_re
    ug = u.astype(F32).reshape(Bsz, T, SSM_GROUPS, SSM_GROUP)
    bu_re = jnp.einsum('btgn,gpn->btgp', ug, bbar_re)
    bu_im = jnp.einsum('btgn,gpn->btgp', ug, bbar_im)
    ar = jnp.broadcast_to(abar_re, bu_re.shape)
    ai = jnp.broadcast_to(abar_im, bu_re.shape)
    acum_re, acum_im, st_re, st_im = lax.associative_scan(_complex_combine, (ar, ai, bu_re, bu_im), axis=1)
    s0r = s0_re.astype(F32)[:, None]
    s0i = s0_im.astype(F32)[:, None]
    s_re = acum_re * s0r - acum_im * s0i + st_re
    s_im = acum_re * s0i + acum_im * s0r + st_im
    y = (jnp.einsum('gnp,btgp->btgn', c_re.astype(F32), s_re)
         - jnp.einsum('gnp,btgp->btgn', c_im.astype(F32), s_im))
    y = y.reshape(Bsz, T, SSM_WIDTH) + d_skip.astype(F32) * u.astype(F32)
    return y.astype(u.dtype), s_re[:, -1].astype(u.dtype), s_im[:, -1].astype(u.dtype)


def dilated_window_prompt(q, k, v, dilation, band):
    Bsz, T, H, Dh = q.shape
    M = T // dilation
    Mp = -(-M // band) * band
    nb = Mp // band

    def to_blocks(x):
        x = x.reshape(Bsz, M, dilation, H, Dh).transpose(0, 2, 1, 3, 4)
        x = jnp.pad(x, ((0, 0), (0, 0), (0, Mp - M), (0, 0), (0, 0)))
        return x.reshape(Bsz, dilation, nb, band, H, Dh)

    def with_prev(x):
        prev = jnp.pad(x[:, :, :-1], ((0, 0), (0, 0), (1, 0), (0, 0), (0, 0), (0, 0)))
        return jnp.concatenate([prev, x], axis=3)

    qb = to_blocks(q).astype(F32)
    kw = with_prev(to_blocks(k)).astype(F32)
    vw = with_prev(to_blocks(v)).astype(F32)
    s = jnp.einsum('brnqhd,brnkhd->brnhqk', qb, kw) * (Dh ** -0.5)
    qi = jnp.arange(band)[:, None]
    kj = jnp.arange(2 * band)[None, :]
    dist = qi + band - kj
    key_m = jnp.arange(nb)[:, None, None] * band + kj[None] - band
    valid = ((dist >= 0) & (dist <= band))[None] & (key_m >= 0)
    s = jnp.where(valid[None, None, :, None], s, -jnp.inf)
    mx = jnp.max(s, axis=-1, keepdims=True)
    p = jnp.exp(s - mx)
    l = jnp.sum(p, axis=-1, keepdims=True)
    o = jnp.einsum('brnhqk,brnkhd->brnqhd', p / l, vw)
    lse = (mx + jnp.log(l))[..., 0]
    o = o.reshape(Bsz, dilation, Mp, H, Dh)[:, :, :M].transpose(0, 2, 1, 3, 4).reshape(Bsz, T, H, Dh)
    lse = lse.transpose(0, 1, 2, 4, 3).reshape(Bsz, dilation, Mp, H)[:, :, :M]
    lse = lse.transpose(0, 2, 1, 3).reshape(Bsz, T, H)
    return o, lse


def dilated_window_sample(q, k, v, cache_kv, dilation, band):
    Wb = cache_kv.shape[1]
    S = q.shape[1]
    Dh = q.shape[-1]
    kf = jnp.concatenate([cache_kv[:, :, 0], k], axis=1)
    vf = jnp.concatenate([cache_kv[:, :, 1], v], axis=1)
    idx = Wb + jnp.arange(S)[:, None] - jnp.arange(band + 1)[None, :] * dilation
    valid = idx >= 0
    idx = jnp.maximum(idx, 0)
    kg = kf[:, idx].astype(F32)
    vg = vf[:, idx].astype(F32)
    s = jnp.einsum('nshd,nskhd->nshk', q.astype(F32), kg) * (Dh ** -0.5)
    s = jnp.where(valid[None, :, None, :], s, -jnp.inf)
    mx = jnp.max(s, axis=-1, keepdims=True)
    p = jnp.exp(s - mx)
    l = jnp.sum(p, axis=-1, keepdims=True)
    o = jnp.einsum('nshk,nskhd->nshd', p / l, vg)
    lse = (mx + jnp.log(l))[..., 0]
    return o, lse


def hier_moe(h, w_rg, b_rg, w_re, b_re, w_gu, w_down):
    Bsz, T, D = h.shape
    hf = h.reshape(-1, D)
    lg = (hf @ w_rg + b_rg).astype(F32)
    g_sel = jnp.argmax(lg, axis=-1)
    p_group = jnp.take_along_axis(jax.nn.softmax(lg, axis=-1), g_sel[:, None], axis=1)
    le = (hf @ w_re + b_re).astype(F32).reshape(-1, N_EXPERT_GROUPS, EXPERTS_PER_GROUP)
    le_sel = jnp.take_along_axis(le, g_sel[:, None, None], axis=1)[:, 0]
    top_v, top_i = lax.top_k(le_sel, EXPERT_TOP_K)
    w_top = jax.nn.softmax(top_v, axis=-1) * p_group
    e_idx = g_sel[:, None] * EXPERTS_PER_GROUP + top_i
    combine = jnp.sum(jax.nn.one_hot(e_idx, N_EXPERTS, dtype=F32) * w_top[..., None], axis=1)
    gu = jnp.einsum('nd,edf->nef', hf, w_gu)
    gate, up = jnp.split(gu, 2, axis=-1)
    act = jax.nn.silu(gate) * up * combine[..., None].astype(h.dtype)
    out = jnp.einsum('nef,efd->nd', act, w_down)
    return out.reshape(Bsz, T, D)


def _layer(x, c, pos, s0_re, s0_im, kv_caches, p):
    Bsz, T, _ = x.shape
    mod = jax.nn.silu(c) @ p['w_ada'] + p['b_ada']
    sh1, sc1, gt1, sh2, sc2, gt2 = [m[:, None, :] for m in jnp.split(mod, 6, axis=-1)]
    h = rms_norm(x, p['norm1_g']) * (1.0 + sc1) + sh1
    proj = h @ p['w_in']
    u, q, k, v, gates = jnp.split(proj, [OFF_Q, OFF_K, OFF_V, OFF_G], axis=-1)
    y, sT_re, sT_im = s5_branch(u, s0_re, s0_im, p['ssm_a_re'], p['ssm_a_im'], p['ssm_log_dt'],
                                p['ssm_b_re'], p['ssm_b_im'], p['ssm_c_re'], p['ssm_c_im'], p['ssm_d'])
    ga, gb = jnp.split(jax.nn.gelu(y) @ p['w_glu'], 2, axis=-1)
    branch_a = ga * jax.nn.sigmoid(gb)
    q = rope_partial(rms_norm(q.reshape(Bsz, T, N_HEADS, HEAD_DIM), p['q_norm_g']), pos)
    k = rope_partial(rms_norm(k.reshape(Bsz, T, N_HEADS, HEAD_DIM), p['k_norm_g']), pos)
    v = v.reshape(Bsz, T, N_HEADS, HEAD_DIM)
    outs, lses, new_kv = [], [], []
    for g, (window, dilation) in enumerate(DILATION_PATTERNS):
        hs = slice(g * HEADS_PER_GROUP, (g + 1) * HEADS_PER_GROUP)
        qg, kg, vg = q[:, :, hs], k[:, :, hs], v[:, :, hs]
        band = window // dilation
        if kv_caches is None:
            o, l = dilated_window_prompt(qg, kg, vg, dilation, band)
            new_kv.append(jnp.stack([kg, vg], axis=2)[:, -min(window, T):])
        else:
            o, l = dilated_window_sample(qg, kg, vg, kv_caches[g], dilation, band)
            new_kv.append(jnp.stack([kg, vg], axis=2))
        outs.append(o)
        lses.append(l)
    alpha = jax.nn.softmax(jnp.stack(lses, axis=0), axis=0)
    o = jnp.sum(alpha[..., None] * jnp.stack(outs, axis=0), axis=0)
    branch_b = o.reshape(Bsz, T, HEADS_PER_GROUP * HEAD_DIM).astype(x.dtype) @ p['w_attn_br']
    gate_a, gate_b = jnp.split(jax.nn.sigmoid(gates), 2, axis=-1)
    mixed = (gate_a * branch_a + gate_b * branch_b) @ p['w_out']
    x = x + gt1 * mixed
    h2 = rms_norm(x, p['norm2_g']) * (1.0 + sc2) + sh2
    x = x + gt2 * hier_moe(h2, p['w_router_group'], p['b_router_group'], p['w_router_expert'],
                           p['b_router_expert'], p['w_expert_gate_up'], p['w_expert_down'])
    return x, new_kv, sT_re, sT_im


def setup_inputs(seed: int = 0) -> dict:
    key = jax.random.key(seed)
    ks = jax.random.split(key, 40)

    def nrm(k, shape, scale):
        return jax.random.normal(k, shape, F32) * scale

    L = DEPTH
    wins = [min(w, PAST_LEN) for (w, _) in DILATION_PATTERNS]
    n_idx = jnp.arange(SSM_STATE, dtype=F32)
    return {
        'x_prompt': nrm(ks[0], (BATCH, SEQ, D_MODEL), 1.0),
        'x_sample': nrm(ks[1], (DEC_BATCH, DEC_SEQ, D_MODEL), 1.0),
        'cache_kv_w128': nrm(ks[2], (L, DEC_BATCH, wins[0], 2, HEADS_PER_GROUP, HEAD_DIM), 1.0),
        'cache_kv_w512': nrm(ks[3], (L, DEC_BATCH, wins[1], 2, HEADS_PER_GROUP, HEAD_DIM), 1.0),
        'cache_kv_w2048': nrm(ks[4], (L, DEC_BATCH, wins[2], 2, HEADS_PER_GROUP, HEAD_DIM), 1.0),
        'state_ssm_re': nrm(ks[5], (L, DEC_BATCH, SSM_GROUPS, SSM_STATE), 0.1),
        'state_ssm_im': nrm(ks[6], (L, DEC_BATCH, SSM_GROUPS, SSM_STATE), 0.1),
        'c_prompt': nrm(ks[7], (BATCH, D_MODEL), 1.0),
        'c_sample': nrm(ks[8], (DEC_BATCH, D_MODEL), 1.0),
        'w_ada': nrm(ks[9], (L, D_MODEL, 6 * D_MODEL), 0.5 * D_MODEL ** -0.5),
        'b_ada': nrm(ks[10], (L, 6 * D_MODEL), 0.01),
        'norm1_g': 1.0 + nrm(ks[11], (L, D_MODEL), 0.01),
        'norm2_g': 1.0 + nrm(ks[12], (L, D_MODEL), 0.01),
        'w_in': nrm(ks[13], (L, D_MODEL, IN_COLS), D_MODEL ** -0.5),
        'ssm_a_re': -0.5 + nrm(ks[14], (L, SSM_GROUPS, SSM_STATE), 0.01),
        'ssm_a_im': math.pi * n_idx + nrm(ks[15], (L, SSM_GROUPS, SSM_STATE), 0.01),
        'ssm_log_dt': jax.random.uniform(ks[16], (L, SSM_GROUPS), F32, math.log(DT_MIN), math.log(DT_MAX)),
        'ssm_b_re': nrm(ks[17], (L, SSM_GROUPS, SSM_STATE, SSM_GROUP), (2 * SSM_GROUP) ** -0.5),
        'ssm_b_im': nrm(ks[18], (L, SSM_GROUPS, SSM_STATE, SSM_GROUP), (2 * SSM_GROUP) ** -0.5),
        'ssm_c_re': nrm(ks[19], (L, SSM_GROUPS, SSM_GROUP, SSM_STATE), SSM_STATE ** -0.5),
        'ssm_c_im': nrm(ks[20], (L, SSM_GROUPS, SSM_GROUP, SSM_STATE), SSM_STATE ** -0.5),
        'ssm_d': nrm(ks[21], (L, SSM_WIDTH), 1.0),
        'w_glu': nrm(ks[22], (L, SSM_WIDTH, 2 * D_MODEL), SSM_WIDTH ** -0.5),
        'q_norm_g': 1.0 + nrm(ks[23], (L, HEAD_DIM), 0.01),
        'k_norm_g': 1.0 + nrm(ks[24], (L, HEAD_DIM), 0.01),
        'w_attn_br': nrm(ks[25], (L, HEADS_PER_GROUP * HEAD_DIM, D_MODEL), (HEADS_PER_GROUP * HEAD_DIM) ** -0.5),
        'w_out': nrm(ks[26], (L, D_MODEL, D_MODEL), D_MODEL ** -0.5),
        'w_router_group': nrm(ks[27], (L, D_MODEL, N_EXPERT_GROUPS), D_MODEL ** -0.5),
        'b_router_group': nrm(ks[28], (L, N_EXPERT_GROUPS), 0.01),
        'w_router_expert': nrm(ks[29], (L, D_MODEL, N_EXPERTS), D_MODEL ** -0.5),
        'b_router_expert': nrm(ks[30], (L, N_EXPERTS), 0.01),
        'w_expert_gate_up': nrm(ks[31], (L, N_EXPERTS, D_MODEL, 2 * EXPERT_FF), D_MODEL ** -0.5),
        'w_expert_down': nrm(ks[32], (L, N_EXPERTS, EXPERT_FF, D_MODEL), EXPERT_FF ** -0.5),
    }


def reference(x_prompt, x_sample, cache_kv_w128, cache_kv_w512, cache_kv_w2048, state_ssm_re, state_ssm_im,
              c_prompt, c_sample, w_ada, b_ada, norm1_g, norm2_g, w_in, ssm_a_re, ssm_a_im, ssm_log_dt,
              ssm_b_re, ssm_b_im, ssm_c_re, ssm_c_im, ssm_d, w_glu, q_norm_g, k_norm_g, w_attn_br, w_out,
              w_router_group, b_router_group, w_router_expert, b_router_expert, w_expert_gate_up, w_expert_down):
    pos_p = jnp.arange(x_prompt.shape[1], dtype=jnp.int32)
    pos_s = PAST_LEN + jnp.arange(x_sample.shape[1], dtype=jnp.int32)
    y_p, y_s = x_prompt, x_sample
    kvp = [[], [], []]
    kvs = [[], [], []]
    srp, sip, srs, sis = [], [], [], []
    for l in range(DEPTH):
        p = {'w_ada': w_ada[l], 'b_ada': b_ada[l], 'norm1_g': norm1_g[l], 'norm2_g': norm2_g[l],
             'w_in': w_in[l], 'ssm_a_re': ssm_a_re[l], 'ssm_a_im': ssm_a_im[l], 'ssm_log_dt': ssm_log_dt[l],
             'ssm_b_re': ssm_b_re[l], 'ssm_b_im': ssm_b_im[l], 'ssm_c_re': ssm_c_re[l], 'ssm_c_im': ssm_c_im[l],
             'ssm_d': ssm_d[l], 'w_glu': w_glu[l], 'q_norm_g': q_norm_g[l], 'k_norm_g': k_norm_g[l],
             'w_attn_br': w_attn_br[l], 'w_out': w_out[l], 'w_router_group': w_router_group[l],
             'b_router_group': b_router_group[l], 'w_router_expert': w_router_expert[l],
             'b_router_expert': b_router_expert[l], 'w_expert_gate_up': w_expert_gate_up[l],
             'w_expert_down': w_expert_down[l]}
        s0 = jnp.zeros((y_p.shape[0], SSM_GROUPS, SSM_STATE), y_p.dtype)
        y_p, nkv_p, sre_p, sim_p = _layer(y_p, c_prompt, pos_p, s0, s0, None, p)
        y_s, nkv_s, sre_s, sim_s = _layer(y_s, c_sample, pos_s, state_ssm_re[l], state_ssm_im[l],
                                          (cache_kv_w128[l], cache_kv_w512[l], cache_kv_w2048[l]), p)
        for g in range(N_PATTERNS):
            kvp[g].append(nkv_p[g])
            kvs[g].append(nkv_s[g])
        srp.append(sre_p)
        sip.append(sim_p)
        srs.append(sre_s)
        sis.append(sim_s)
    kv128_p = jnp.stack(kvp[0], axis=0)
    kv512_p = jnp.stack(kvp[1], axis=0)
    kv2048_p = jnp.stack(kvp[2], axis=0)
    ssm_re_p = jnp.stack(srp, axis=0)
    ssm_im_p = jnp.stack(sip, axis=0)
    kv128_s = jnp.stack(kvs[0], axis=0)
    kv512_s = jnp.stack(kvs[1], axis=0)
    kv2048_s = jnp.stack(kvs[2], axis=0)
    ssm_re_s = jnp.stack(srs, axis=0)
    ssm_im_s = jnp.stack(sis, axis=0)
    return (y_p, y_s, kv128_p, kv512_p, kv2048_p, ssm_re_p, ssm_im_p, kv128_s, kv512_s, kv2048_s, ssm_re_s, ssm_im_s)
```

```python
import functools
import math

import numpy as np
import jax
import jax.numpy as jnp
from jax import lax
from jax.experimental import pallas as pl
from jax.experimental.pallas import tpu as pltpu

F32 = jnp.float32
BF16 = jnp.bfloat16

D_MODEL = 2048
PAST_LEN = 2048
SSM_WIDTH = D_MODEL // 2
SSM_GROUP = 16
SSM_GROUPS = SSM_WIDTH // SSM_GROUP
SSM_STATE = 64
SSM_FLAT = SSM_GROUPS * SSM_STATE
HEAD_DIM = 128
DILATION_PATTERNS = ((128, 1), (512, 4), (2048, 16))
N_PATTERNS = 3
HEADS_PER_GROUP = 4
GROUP_WIDTH = HEADS_PER_GROUP * HEAD_DIM
ATTN_WIDTH = N_PATTERNS * GROUP_WIDTH
ROT_DIM = HEAD_DIM // 4
ROPE_THETA = 500000.0
OFF_Q = SSM_WIDTH
OFF_K = OFF_Q + ATTN_WIDTH
OFF_V = OFF_K + ATTN_WIDTH
OFF_G = OFF_V + ATTN_WIDTH
IN_COLS = OFF_G + 2 * D_MODEL
N_EXPERT_GROUPS = 4
EXPERTS_PER_GROUP = 4
N_EXPERTS = 16
EXPERT_FF = D_MODEL // 4
EPS = 1e-6
NEG = -1e30

LANES = 128
VMEM_LIMIT = 56 * 1024 * 1024

TOK_TILE = 512
SMALL_TOK_TILE = 256
PROJ_TM = 1088
PROJ_TN = 512
SCAN_L = 32
SCAN_NC = 16
SCAN_TT = SCAN_L * SCAN_NC
SSM_BLK = 8
SSM_BLK_STATE = SSM_BLK * SSM_STATE
ATT_SB = 2048
BAND = 128


def _cparams(sem, vmem=VMEM_LIMIT):
    return pltpu.CompilerParams(dimension_semantics=sem, vmem_limit_bytes=vmem)


def _sigmoid(x):
    return 1.0 / (1.0 + jnp.exp(-x))


def _gelu_tanh(x):
    c = math.sqrt(2.0 / math.pi)
    return 0.5 * x * (1.0 + jnp.tanh(c * (x + 0.044715 * (x * x * x))))


def _ada_kernel(c_ref, w_ref, b_ref, o_ref):
    c = c_ref[...]
    cs = (c * _sigmoid(c)).astype(BF16)
    o_ref[...] = jnp.dot(cs, w_ref[...].astype(BF16), preferred_element_type=F32) + b_ref[...]


def _ada(c_all, w_ada, b_ada):
    rows = c_all.shape[0]
    n_out = w_ada.shape[1]
    tn = 1024
    return pl.pallas_call(
        _ada_kernel,
        grid=(n_out // tn,),
        in_specs=[pl.BlockSpec((rows, D_MODEL), lambda n: (0, 0)),
                  pl.BlockSpec((D_MODEL, tn), lambda n: (0, n)),
                  pl.BlockSpec((1, tn), lambda n: (0, n))],
        out_specs=pl.BlockSpec((rows, tn), lambda n: (0, n)),
        out_shape=jax.ShapeDtypeStruct((rows, n_out), F32),
        compiler_params=_cparams(("arbitrary",)),
        name="ada_mod",
    )(c_all, w_ada, b_ada.reshape(1, n_out))


def _modnorm_kernel(xp_ref, xs_ref, g_ref, scp_ref, shp_ref, scs_ref, shs_ref, o_ref, *, n_prompt_tiles):
    is_s = pl.program_id(0) >= n_prompt_tiles
    x = jnp.where(is_s, xs_ref[...], xp_ref[...])
    sc = jnp.where(is_s, scs_ref[...], scp_ref[...])
    sh = jnp.where(is_s, shs_ref[...], shp_ref[...])
    ms = jnp.mean(x * x, axis=-1, keepdims=True)
    y = x * lax.rsqrt(ms + EPS) * g_ref[...]
    o_ref[...] = (y * (1.0 + sc) + sh).astype(o_ref.dtype)


def _modnorm(x_p, x_s, g, sc_p, sh_p, sc_s, sh_s):
    tp, ts = x_p.shape[0], x_s.shape[0]
    tm = TOK_TILE
    npt, nst = tp // tm, ts // tm
    row = lambda i: (jnp.minimum(i, npt - 1), 0)
    srow = lambda i: (jnp.maximum(i - npt, 0), 0)
    const = lambda i: (0, 0)
    return pl.pallas_call(
        functools.partial(_modnorm_kernel, n_prompt_tiles=npt),
        grid=(npt + nst,),
        in_specs=[pl.BlockSpec((tm, D_MODEL), row), pl.BlockSpec((tm, D_MODEL), srow),
                  pl.BlockSpec((1, D_MODEL), const), pl.BlockSpec((1, D_MODEL), const),
                  pl.BlockSpec((1, D_MODEL), const), pl.BlockSpec((tm, D_MODEL), srow),
                  pl.BlockSpec((tm, D_MODEL), srow)],
        out_specs=pl.BlockSpec((tm, D_MODEL), lambda i: (i, 0)),
        out_shape=jax.ShapeDtypeStruct((tp + ts, D_MODEL), BF16),
        compiler_params=_cparams(("arbitrary",)),
        name="modnorm1",
    )(x_p, x_s, g, sc_p, sh_p, sc_s, sh_s)


def _inproj_kernel(h_ref, w_ref, rc_ref, rs1_ref, rs2_ref, qg_ref, kg_ref, o_ref, wbf_ref, *, q_tiles, k_tiles):
    n = pl.program_id(0)

    @pl.when(pl.program_id(1) == 0)
    def _():
        wbf_ref[...] = w_ref[...].astype(BF16)

    acc = jnp.dot(h_ref[...], wbf_ref[...], preferred_element_type=F32)
    is_q = (n >= q_tiles[0]) & (n < q_tiles[1])
    is_k = (n >= k_tiles[0]) & (n < k_tiles[1])

    @pl.when(is_q | is_k)
    def _():
        gain = jnp.where(is_q, qg_ref[...], kg_ref[...])
        rc, rs1, rs2 = rc_ref[...], rs1_ref[...], rs2_ref[...]
        for hd in range(PROJ_TN // HEAD_DIM):
            x = acc[:, hd * HEAD_DIM:(hd + 1) * HEAD_DIM]
            ms = jnp.mean(x * x, axis=-1, keepdims=True)
            y = x * lax.rsqrt(ms + EPS) * gain
            half = ROT_DIM // 2
            up = pltpu.roll(y, HEAD_DIM - half, 1)
            dn = pltpu.roll(y, half, 1)
            o_ref[:, hd * HEAD_DIM:(hd + 1) * HEAD_DIM] = y * rc + up * rs1 + dn * rs2

    @pl.when(jnp.logical_not(is_q | is_k))
    def _():
        o_ref[...] = acc


def _inproj(h, w_in, rc, rs1, rs2, qg, kg):
    n_tok = h.shape[0]
    tm, tn = PROJ_TM, PROJ_TN
    q_tiles = (OFF_Q // tn, OFF_K // tn)
    k_tiles = (OFF_K // tn, OFF_V // tn)
    tab = pl.BlockSpec((tm, HEAD_DIM), lambda n, m: (m, 0))
    gain = pl.BlockSpec((1, HEAD_DIM), lambda n, m: (0, 0))
    return pl.pallas_call(
        functools.partial(_inproj_kernel, q_tiles=q_tiles, k_tiles=k_tiles),
        grid=(IN_COLS // tn, n_tok // tm),
        in_specs=[pl.BlockSpec((tm, D_MODEL), lambda n, m: (m, 0)),
                  pl.BlockSpec((D_MODEL, tn), lambda n, m: (0, n)),
                  tab, tab, tab, gain, gain],
        out_specs=pl.BlockSpec((tm, tn), lambda n, m: (m, n)),
        out_shape=jax.ShapeDtypeStruct((n_tok, IN_COLS), F32),
        scratch_shapes=[pltpu.VMEM((D_MODEL, tn), BF16)],
        compiler_params=_cparams(("arbitrary", "arbitrary")),
        name="in_proj",
    )(h, w_in, rc, rs1, rs2, qg, kg)


def _ssm_prep_kernel(are_ref, aim_ref, ldt_ref, arer_ref, aimr_ref, ldtr_ref, bre_ref, bim_ref,
                     pre_ref, pim_ref, bbre_ref, bbim_ref):
    def discretise(a_re, a_im, log_dt):
        dt = jnp.exp(log_dt)
        mag = jnp.exp(a_re * dt)
        return mag * jnp.cos(a_im * dt), mag * jnp.sin(a_im * dt)

    ab_re, ab_im = discretise(are_ref[...], aim_ref[...], ldt_ref[...])
    p_re, p_im = ab_re, ab_im
    for i in range(SCAN_L):
        pre_ref[i:i + 1, :] = p_re
        pim_ref[i:i + 1, :] = p_im
        p_re, p_im = p_re * ab_re - p_im * ab_im, p_re * ab_im + p_im * ab_re

    a_re, a_im = arer_ref[...], aimr_ref[...]
    r_re, r_im = discretise(a_re, a_im, ldtr_ref[...])
    nr, ni = r_re - 1.0, r_im
    den = a_re * a_re + a_im * a_im
    z_re = (nr * a_re + ni * a_im) / den
    z_im = (ni * a_re - nr * a_im) / den
    b_re, b_im = bre_ref[...], bim_ref[...]
    bbre_ref[...] = z_re * b_re - z_im * b_im
    bbim_ref[...] = z_re * b_im + z_im * b_re


def _ssm_prep(a_re, a_im, log_dt, b_re, b_im):
    g, p, n = b_re.shape
    flat = lambda x: x.reshape(1, g * p)
    rep = lambda x: jnp.repeat(x, n, axis=1)
    ldt_gp = jnp.broadcast_to(log_dt[:, None], (g, p))
    ldt_rep = jnp.broadcast_to(log_dt[:, None], (g, p * n))
    out_shape = [jax.ShapeDtypeStruct((SCAN_L, g * p), F32)] * 2 + [jax.ShapeDtypeStruct((g, p * n), F32)] * 2
    return pl.pallas_call(_ssm_prep_kernel, out_shape=out_shape, name="ssm_prep")(
        flat(a_re), flat(a_im), flat(ldt_gp), rep(a_re), rep(a_im), ldt_rep,
        b_re.reshape(g, p * n), b_im.reshape(g, p * n))


def _ssm_block_matrices(bb_re, bb_im, c_re, c_im):
    g, p, n = SSM_GROUPS, SSM_STATE, SSM_GROUP
    nb = g // SSM_BLK
    eye = jnp.eye(SSM_BLK, dtype=F32)

    def in_mat(bb):
        x = bb.reshape(nb, SSM_BLK, p, n)
        return jnp.einsum('bgpm,gh->bgmhp', x, eye).reshape(nb, SSM_BLK * n, SSM_BLK * p)

    def out_mat(c):
        x = c.reshape(nb, SSM_BLK, n, p)
        return jnp.einsum('bgnp,gh->bgphn', x, eye).reshape(nb, SSM_BLK * p, SSM_BLK * n)

    b_mat = jnp.concatenate([in_mat(bb_re), in_mat(bb_im)], axis=2).astype(BF16)
    c_mat = jnp.concatenate([out_mat(c_re), -out_mat(c_im)], axis=1).astype(BF16)
    return b_mat, c_mat


def _cmul_add(a_re, a_im, s_re, s_im, b_re, b_im):
    return a_re * s_re - a_im * s_im + b_re, a_re * s_im + a_im * s_re + b_im


def _s5_prompt_kernel(u_ref, d_ref, pre_ref, pim_ref, bm_ref, cm_ref, y_ref, fre_ref, fim_ref,
                      up_scr, bu_scr, lhs_scr, in_re_scr, in_im_scr, car_re, car_im, yn_scr):
    nc, ln, w = SCAN_NC, SCAN_L, SSM_BLK_STATE

    @pl.when(pl.program_id(1) == 0)
    def _():
        car_re[...] = jnp.zeros_like(car_re)
        car_im[...] = jnp.zeros_like(car_im)

    for i in range(ln):
        up_scr[i * nc:(i + 1) * nc, :] = u_ref[pl.ds(i, nc, stride=ln), :]
    up = up_scr[...]
    bu_scr[...] = jnp.dot(up.astype(BF16), bm_ref[0], preferred_element_type=F32)

    a_re = jnp.broadcast_to(pre_ref[0:1, :], (nc, w))
    a_im = jnp.broadcast_to(pim_ref[0:1, :], (nc, w))

    def local_step(i, carry):
        s_re, s_im = carry
        r0 = pl.multiple_of(i * nc, nc)
        s_re, s_im = _cmul_add(a_re, a_im, s_re, s_im, bu_scr[pl.ds(r0, nc), 0:w], bu_scr[pl.ds(r0, nc), w:2 * w])
        bu_scr[pl.ds(r0, nc), 0:w] = s_re
        bu_scr[pl.ds(r0, nc), w:2 * w] = s_im
        return s_re, s_im

    zero = jnp.zeros((nc, w), F32)
    f_re, f_im = lax.fori_loop(0, ln, local_step, (zero, zero))

    al_re, al_im = pre_ref[ln - 1:ln, :], pim_ref[ln - 1:ln, :]
    c_re, c_im = car_re[...], car_im[...]
    for c in range(nc):
        in_re_scr[c:c + 1, :] = c_re
        in_im_scr[c:c + 1, :] = c_im
        c_re, c_im = _cmul_add(al_re, al_im, c_re, c_im, f_re[c:c + 1, :], f_im[c:c + 1, :])
    car_re[...] = c_re
    car_im[...] = c_im
    fre_ref[...] = c_re
    fim_ref[...] = c_im
    in_re, in_im = in_re_scr[...], in_im_scr[...]

    def fix_step(i, _):
        r0 = pl.multiple_of(i * nc, nc)
        p_re = jnp.broadcast_to(pre_ref[pl.ds(i, 1), :], (nc, w))
        p_im = jnp.broadcast_to(pim_ref[pl.ds(i, 1), :], (nc, w))
        s_re, s_im = _cmul_add(p_re, p_im, in_re, in_im, bu_scr[pl.ds(r0, nc), 0:w], bu_scr[pl.ds(r0, nc), w:2 * w])
        lhs_scr[pl.ds(r0, nc), 0:w] = s_re.astype(BF16)
        lhs_scr[pl.ds(r0, nc), w:2 * w] = s_im.astype(BF16)
        return 0

    lax.fori_loop(0, ln, fix_step, 0)
    y = jnp.dot(lhs_scr[...], cm_ref[0], preferred_element_type=F32) + d_ref[...] * up
    for i in range(ln):
        yn_scr[pl.ds(i, nc, stride=ln), :] = y[i * nc:(i + 1) * nc, :]
    y_ref[...] = _gelu_tanh(yn_scr[...]).astype(y_ref.dtype)


def _s5_prompt(proj, n_prompt, d_skip, pw_re, pw_im, b_mat, c_mat):
    nb = SSM_GROUPS // SSM_BLK
    tt, w = SCAN_TT, SSM_BLK_STATE
    return pl.pallas_call(
        _s5_prompt_kernel,
        grid=(nb, n_prompt // tt),
        in_specs=[pl.BlockSpec((tt, LANES), lambda j, i: (i, j)),
                  pl.BlockSpec((1, LANES), lambda j, i: (0, j)),
                  pl.BlockSpec((SCAN_L, w), lambda j, i: (0, j)),
                  pl.BlockSpec((SCAN_L, w), lambda j, i: (0, j)),
                  pl.BlockSpec((1, LANES, 2 * w), lambda j, i: (j, 0, 0)),
                  pl.BlockSpec((1, 2 * w, LANES), lambda j, i: (j, 0, 0))],
        out_specs=[pl.BlockSpec((tt, LANES), lambda j, i: (i, j)),
                   pl.BlockSpec((1, w), lambda j, i: (0, j)),
                   pl.BlockSpec((1, w), lambda j, i: (0, j))],
        out_shape=[jax.ShapeDtypeStruct((n_prompt, SSM_WIDTH), BF16),
                   jax.ShapeDtypeStruct((1, SSM_FLAT), F32),
                   jax.ShapeDtypeStruct((1, SSM_FLAT), F32)],
        scratch_shapes=[pltpu.VMEM((tt, LANES), F32), pltpu.VMEM((tt, 2 * w), F32), pltpu.VMEM((tt, 2 * w), BF16),
                        pltpu.VMEM((SCAN_NC, w), F32), pltpu.VMEM((SCAN_NC, w), F32),
                        pltpu.VMEM((1, w), F32), pltpu.VMEM((1, w), F32), pltpu.VMEM((tt, LANES), F32)],
        compiler_params=_cparams(("arbitrary", "arbitrary")),
        name="s5_prompt",
    )(proj, d_skip, pw_re, pw_im, b_mat, c_mat)


def _s5_sample_kernel(u_ref, d_ref, pre_ref, pim_ref, bm_ref, cm_ref, s0re_ref, s0im_ref,
                      y_ref, fre_ref, fim_ref, up_scr, bu_scr, lhs_scr, yn_scr, *, n_seq, n_step):
    w = SSM_BLK_STATE
    rb = 16
    for s in range(n_step):
        up_scr[s * n_seq:(s + 1) * n_seq, :] = u_ref[pl.ds(s, n_seq, stride=n_step), :]
    up = up_scr[...]
    bu_scr[...] = jnp.dot(up.astype(BF16), bm_ref[0], preferred_element_type=F32)
    a_re = jnp.broadcast_to(pre_ref[0:1, :], (rb, w))
    a_im = jnp.broadcast_to(pim_ref[0:1, :], (rb, w))

    def seq_block(b, _):
        r0 = pl.multiple_of(b * rb, rb)
        s_re, s_im = s0re_ref[pl.ds(r0, rb), :], s0im_ref[pl.ds(r0, rb), :]
        for s in range(n_step):
            rows = pl.ds(pl.multiple_of(s * n_seq + r0, rb), rb)
            s_re, s_im = _cmul_add(a_re, a_im, s_re, s_im, bu_scr[rows, 0:w], bu_scr[rows, w:2 * w])
            lhs_scr[rows, 0:w] = s_re.astype(BF16)
            lhs_scr[rows, w:2 * w] = s_im.astype(BF16)
        fre_ref[pl.ds(r0, rb), :] = s_re
        fim_ref[pl.ds(r0, rb), :] = s_im
        return 0

    lax.fori_loop(0, n_seq // rb, seq_block, 0)
    y = jnp.dot(lhs_scr[...], cm_ref[0], preferred_element_type=F32) + d_ref[...] * up
    for s in range(n_step):
        yn_scr[pl.ds(s, n_seq, stride=n_step), :] = y[s * n_seq:(s + 1) * n_seq, :]
    y_ref[...] = _gelu_tanh(yn_scr[...]).astype(y_ref.dtype)


def _s5_sample(proj, n_prompt, n_seq, n_step, d_skip, pw_re, pw_im, b_mat, c_mat, s0_re, s0_im):
    nb = SSM_GROUPS // SSM_BLK
    rows, w = n_seq * n_step, SSM_BLK_STATE
    rblk = n_prompt // rows
    return pl.pallas_call(
        functools.partial(_s5_sample_kernel, n_seq=n_seq, n_step=n_step),
        grid=(nb,),
        in_specs=[pl.BlockSpec((rows, LANES), lambda j: (rblk, j)),
                  pl.BlockSpec((1, LANES), lambda j: (0, j)),
                  pl.BlockSpec((SCAN_L, w), lambda j: (0, j)),
                  pl.BlockSpec((SCAN_L, w), lambda j: (0, j)),
                  pl.BlockSpec((1, LANES, 2 * w), lambda j: (j, 0, 0)),
                  pl.BlockSpec((1, 2 * w, LANES), lambda j: (j, 0, 0)),
                  pl.BlockSpec((n_seq, w), lambda j: (0, j)),
                  pl.BlockSpec((n_seq, w), lambda j: (0, j))],
        out_specs=[pl.BlockSpec((rows, LANES), lambda j: (0, j)),
                   pl.BlockSpec((n_seq, w), lambda j: (0, j)),
                   pl.BlockSpec((n_seq, w), lambda j: (0, j))],
        out_shape=[jax.ShapeDtypeStruct((rows, SSM_WIDTH), BF16),
                   jax.ShapeDtypeStruct((n_seq, SSM_FLAT), F32),
                   jax.ShapeDtypeStruct((n_seq, SSM_FLAT), F32)],
        scratch_shapes=[pltpu.VMEM((rows, LANES), F32), pltpu.VMEM((rows, 2 * w), F32),
                        pltpu.VMEM((rows, 2 * w), BF16), pltpu.VMEM((rows, LANES), F32)],
        compiler_params=_cparams(("arbitrary",)),
        name="s5_sample",
    )(proj, d_skip, pw_re, pw_im, b_mat, c_mat, s0_re, s0_im)


def _attn_prompt_kernel(*refs):
    ins, o_ref, scr = refs[:15], refs[15], refs[16:]
    sb = pl.program_id(0)
    scale = HEAD_DIM ** -0.5
    qi = lax.broadcasted_iota(jnp.int32, (BAND, 2 * BAND), 0)
    kj = lax.broadcasted_iota(jnp.int32, (BAND, 2 * BAND), 1)
    dist = qi + BAND - kj
    band_ok = (dist >= 0) & (dist <= BAND)

    for g, (_, dil) in enumerate(DILATION_PATTERNS):
        q_ref, k_ref, v_ref, kp_ref, vp_ref = ins[5 * g:5 * g + 5]
        kbuf, vbuf, o_scr, m_scr, l_scr = scr[5 * g:5 * g + 5]
        pre = BAND * dil
        kbuf[0:pre, :] = kp_ref[...]
        kbuf[pre:pre + ATT_SB, :] = k_ref[...]
        vbuf[0:pre, :] = vp_ref[...]
        vbuf[pre:pre + ATT_SB, :] = v_ref[...]
        nblk = ATT_SB // pre

        def block(idx, _, dil=dil, pre=pre, nblk=nblk, q_ref=q_ref, kbuf=kbuf, vbuf=vbuf,
                  o_scr=o_scr, m_scr=m_scr, l_scr=l_scr):
            r = idx // nblk
            b = idx - r * nblk
            row0 = r + b * pre
            if dil == 1:
                q_rows = pl.ds(pl.multiple_of(row0, BAND), BAND)
                kv_rows = pl.ds(pl.multiple_of(row0, BAND), 2 * BAND)
            else:
                q_rows = pl.ds(row0, BAND, stride=dil)
                kv_rows = pl.ds(row0, 2 * BAND, stride=dil)
            q = (q_ref[q_rows, :] * scale).astype(BF16)
            kw = kbuf[kv_rows, :].astype(BF16)
            vw = vbuf[kv_rows, :].astype(BF16)
            s = lax.dot_general(q, kw, (((1,), (1,)), ((), ())), preferred_element_type=F32)
            s = jnp.where(band_ok & ((kj >= BAND) | (sb > 0) | (b > 0)), s, NEG)
            m = jnp.max(s, axis=-1, keepdims=True)
            p = jnp.exp(s - m)
            l = jnp.sum(p, axis=-1, keepdims=True)
            o = jnp.dot(p.astype(BF16), vw, preferred_element_type=F32)
            o_scr[q_rows, :] = o
            m_scr[q_rows, :] = jnp.broadcast_to(m, (BAND, HEAD_DIM))
            l_scr[q_rows, :] = jnp.broadcast_to(l, (BAND, HEAD_DIM))
            return 0

        lax.fori_loop(0, ATT_SB // BAND, block, 0)

    ms = [scr[5 * g + 3][...] for g in range(N_PATTERNS)]
    mx = jnp.maximum(jnp.maximum(ms[0], ms[1]), ms[2])
    num = jnp.zeros((ATT_SB, HEAD_DIM), F32)
    den = jnp.zeros((ATT_SB, HEAD_DIM), F32)
    for g in range(N_PATTERNS):
        wgt = jnp.exp(ms[g] - mx)
        num = num + wgt * scr[5 * g + 2][...]
        den = den + wgt * scr[5 * g + 4][...]
    o_ref[...] = num / den


def _attn_prompt(proj, n_prompt):
    hcol = lambda off, g, j: (off + g * GROUP_WIDTH) // HEAD_DIM + j
    in_specs, scratch = [], []
    for g, (_, dil) in enumerate(DILATION_PATTERNS):
        pre = BAND * dil
        per = ATT_SB // pre
        cur = lambda off, g=g: pl.BlockSpec((ATT_SB, HEAD_DIM), lambda sb, j: (sb, hcol(off, g, j)))
        prev = lambda off, g=g, per=per, pre=pre: pl.BlockSpec(
            (pre, HEAD_DIM), lambda sb, j: (jnp.maximum(sb * per - 1, 0), hcol(off, g, j)))
        in_specs += [cur(OFF_Q), cur(OFF_K), cur(OFF_V), prev(OFF_K), prev(OFF_V)]
        scratch += [pltpu.VMEM((pre + ATT_SB, HEAD_DIM), F32), pltpu.VMEM((pre + ATT_SB, HEAD_DIM), F32),
                    pltpu.VMEM((ATT_SB, HEAD_DIM), F32), pltpu.VMEM((ATT_SB, HEAD_DIM), F32),
                    pltpu.VMEM((ATT_SB, HEAD_DIM), F32)]
    return pl.pallas_call(
        _attn_prompt_kernel,
        grid=(n_prompt // ATT_SB, HEADS_PER_GROUP),
        in_specs=in_specs,
        out_specs=pl.BlockSpec((ATT_SB, HEAD_DIM), lambda sb, j: (sb, j)),
        out_shape=jax.ShapeDtypeStruct((n_prompt, GROUP_WIDTH), F32),
        scratch_shapes=scratch,
        compiler_params=_cparams(("arbitrary", "arbitrary")),
        name="attn_prompt",
    )(*([proj] * 15))


SEQ_PER_STEP = 2


def _sample_bias(n_step):
    rows = HEADS_PER_GROUP * n_step
    step = np.arange(rows) % n_step
    cache_bias, new_bias = [], []
    for (window, dil) in DILATION_PATTERNS:
        wb = min(window, PAST_LEN)
        band = window // dil
        if dil >= n_step:
            res, i = np.meshgrid(np.arange(n_step), np.arange(wb // dil), indexing='ij')
            c = (i * dil + res).reshape(-1)
        else:
            c = np.arange(wb)
        delta = wb + step[:, None] - c[None, :]
        ok = (delta >= 0) & (delta % dil == 0) & (delta // dil <= band)
        cache_bias.append(np.where(ok, 0.0, NEG).astype(np.float32))
        nb = np.full((SEQ_PER_STEP, rows, LANES), NEG, np.float32)
        for a in range(SEQ_PER_STEP):
            for sp in range(n_step):
                dl = step - sp
                okn = (dl >= 0) & (dl % dil == 0) & (dl // dil <= band)
                nb[a, :, a * n_step + sp] = np.where(okn, 0.0, NEG)
        new_bias.append(nb)
    return cache_bias, new_bias


def _attn_sample_kernel(*refs, n_step):
    (q0, k0, v0, q1, k1, v1, q2, k2, v2, c0, c1, c2, cb0, cb1, cb2, nb0, nb1, nb2, o_ref) = refs
    qs, ks, vs = (q0, q1, q2), (k0, k1, k2), (v0, v1, v2)
    caches, cbias, nbias = (c0, c1, c2), (cb0, cb1, cb2), (nb0, nb1, nb2)
    rows = HEADS_PER_GROUP * n_step
    gw = GROUP_WIDTH
    scale = HEAD_DIM ** -0.5
    row_head = lax.broadcasted_iota(jnp.int32, (rows, gw), 0) // n_step
    lane_head = lax.broadcasted_iota(jnp.int32, (rows, gw), 1) // HEAD_DIM
    own_head = row_head == lane_head
    nt = (((1,), (1,)), ((), ()))
    pad = jnp.zeros((LANES - SEQ_PER_STEP * n_step, gw), F32)

    for a in range(SEQ_PER_STEP):
        pieces = []
        for g, (window, dil) in enumerate(DILATION_PATTERNS):
            q = qs[g][a * n_step:(a + 1) * n_step, :] * scale
            qbd = jnp.where(own_head, jnp.concatenate([q] * HEADS_PER_GROUP, axis=0), 0.0).astype(BF16)
            cache = caches[g]
            if dil >= n_step:
                row_w = 2 * gw
                sc = jnp.concatenate(
                    [lax.dot_general(qbd, cache[a, :, r * row_w:r * row_w + gw].astype(BF16), nt,
                                     preferred_element_type=F32) for r in range(n_step)], axis=1)
                vals = [cache[a, :, r * row_w + gw:(r + 1) * row_w].astype(BF16) for r in range(n_step)]
            else:
                sc = lax.dot_general(qbd, cache[a, :, 0:gw].astype(BF16), nt, preferred_element_type=F32)
                vals = [cache[a, :, gw:2 * gw].astype(BF16)]
            pieces.append((sc + cbias[g][...], vals))
            k_new = jnp.concatenate([ks[g][...], pad], axis=0).astype(BF16)
            v_new = jnp.concatenate([vs[g][...], pad], axis=0).astype(BF16)
            sn = lax.dot_general(qbd, k_new, nt, preferred_element_type=F32) + nbias[g][a]
            pieces.append((sn, [v_new]))
        m = functools.reduce(jnp.maximum, [jnp.max(s, axis=-1, keepdims=True) for s, _ in pieces])
        l = jnp.zeros((rows, 1), F32)
        acc = jnp.zeros((rows, gw), F32)
        for s, vals in pieces:
            p = jnp.exp(s - m)
            l = l + jnp.sum(p, axis=-1, keepdims=True)
            pb = p.astype(BF16)
            nk = pb.shape[1] // len(vals)
            for r, v in enumerate(vals):
                acc = acc + jnp.dot(pb[:, r * nk:(r + 1) * nk], v, preferred_element_type=F32)
        acc = jnp.where(own_head, acc, 0.0)
        o16 = functools.reduce(lambda x, y: x + y,
                               [acc[:, h * HEAD_DIM:(h + 1) * HEAD_DIM] for h in range(HEADS_PER_GROUP)]) / l
        for h in range(HEADS_PER_GROUP):
            o_ref[a * n_step:(a + 1) * n_step, h * HEAD_DIM:(h + 1) * HEAD_DIM] = o16[h * n_step:(h + 1) * n_step, :]


def _attn_sample(proj, n_prompt, n_seq, n_step, caches):
    rows = SEQ_PER_STEP * n_step
    rblk = n_prompt // rows
    cache_bias, new_bias = _sample_bias(n_step)
    tok = lambda off, g: pl.BlockSpec((rows, GROUP_WIDTH), lambda i: (rblk + i, (off + g * GROUP_WIDTH) // GROUP_WIDTH))
    in_specs, args = [], []
    for g in range(N_PATTERNS):
        in_specs += [tok(OFF_Q, g), tok(OFF_K, g), tok(OFF_V, g)]
        args += [proj, proj, proj]
    for g, (window, dil) in enumerate(DILATION_PATTERNS):
        c = caches[g]
        wb = c.shape[1]
        row_w = 2 * GROUP_WIDTH
        if dil >= n_step:
            c = c.reshape(n_seq, wb // dil, dil * row_w)
            in_specs.append(pl.BlockSpec((SEQ_PER_STEP, wb // dil, n_step * row_w), lambda i: (i, 0, 0)))
        else:
            c = c.reshape(n_seq, wb, row_w)
            in_specs.append(pl.BlockSpec((SEQ_PER_STEP, wb, row_w), lambda i: (i, 0, 0)))
        args.append(c)
    for b in cache_bias:
        in_specs.append(pl.BlockSpec(b.shape, lambda i: (0, 0)))
        args.append(jnp.asarray(b))
    for b in new_bias:
        in_specs.append(pl.BlockSpec(b.shape, lambda i: (0, 0, 0)))
        args.append(jnp.asarray(b))
    return pl.pallas_call(
        functools.partial(_attn_sample_kernel, n_step=n_step),
        grid=(n_seq // SEQ_PER_STEP,),
        in_specs=in_specs,
        out_specs=pl.BlockSpec((rows, GROUP_WIDTH), lambda i: (i, 0)),
        out_shape=jax.ShapeDtypeStruct((n_seq * n_step, GROUP_WIDTH), F32),
        compiler_params=_cparams(("arbitrary",)),
        name="attn_sample",
    )(*args)


def _mix_kernel(yp_ref, ys_ref, op_ref, os_ref, wa_ref, wb_ref, wbr_ref, ga_ref, gb_ref, o_ref,
                wa_bf, wb_bf, wbr_bf, *, n_prompt_tiles):
    i = pl.program_id(1)

    @pl.when(i == 0)
    def _():
        wa_bf[...] = wa_ref[...].astype(BF16)
        wb_bf[...] = wb_ref[...].astype(BF16)
        wbr_bf[...] = wbr_ref[...].astype(BF16)

    is_s = i >= n_prompt_tiles
    y = jnp.where(is_s, ys_ref[...], yp_ref[...])
    o = jnp.where(is_s, os_ref[...], op_ref[...]).astype(BF16)
    glu_a = jnp.dot(y, wa_bf[...], preferred_element_type=F32)
    glu_b = jnp.dot(y, wb_bf[...], preferred_element_type=F32)
    branch_a = glu_a * _sigmoid(glu_b)
    branch_b = jnp.dot(o, wbr_bf[...], preferred_element_type=F32)
    o_ref[...] = (_sigmoid(ga_ref[...]) * branch_a + _sigmoid(gb_ref[...]) * branch_b).astype(o_ref.dtype)


def _mix(y_p, y_s, o_p, o_s, w_glu, w_attn_br, proj):
    tp, ts = y_p.shape[0], y_s.shape[0]
    tm, tn = TOK_TILE, 512
    npt, nst = tp // tm, ts // tm
    ncol = D_MODEL // tn
    prow = lambda n, i: (jnp.minimum(i, npt - 1), 0)
    srow = lambda n, i: (jnp.maximum(i - npt, 0), 0)
    return pl.pallas_call(
        functools.partial(_mix_kernel, n_prompt_tiles=npt),
        grid=(ncol, npt + nst),
        in_specs=[pl.BlockSpec((tm, SSM_WIDTH), prow), pl.BlockSpec((tm, SSM_WIDTH), srow),
                  pl.BlockSpec((tm, GROUP_WIDTH), prow), pl.BlockSpec((tm, GROUP_WIDTH), srow),
                  pl.BlockSpec((SSM_WIDTH, tn), lambda n, i: (0, n)),
                  pl.BlockSpec((SSM_WIDTH, tn), lambda n, i: (0, ncol + n)),
                  pl.BlockSpec((GROUP_WIDTH, tn), lambda n, i: (0, n)),
                  pl.BlockSpec((tm, tn), lambda n, i: (i, OFF_G // tn + n)),
                  pl.BlockSpec((tm, tn), lambda n, i: (i, OFF_G // tn + ncol + n))],
        out_specs=pl.BlockSpec((tm, tn), lambda n, i: (i, n)),
        out_shape=jax.ShapeDtypeStruct((tp + ts, D_MODEL), BF16),
        scratch_shapes=[pltpu.VMEM((SSM_WIDTH, tn), BF16), pltpu.VMEM((SSM_WIDTH, tn), BF16),
                        pltpu.VMEM((GROUP_WIDTH, tn), BF16)],
        compiler_params=_cparams(("arbitrary", "arbitrary")),
        name="glu_mix",
    )(y_p, y_s, o_p, o_s, w_glu, w_glu, w_attn_br, proj, proj)


def _route(logits):
    lane = lax.broadcasted_iota(jnp.int32, logits.shape, 1).astype(F32)
    big = 1000.0
    first = lambda cond: jnp.min(jnp.where(cond, lane, big), axis=-1, keepdims=True)
    is_g = lane < N_EXPERT_GROUPS
    lg = jnp.where(is_g, logits, NEG)
    mg = jnp.max(lg, axis=-1, keepdims=True)
    g_sel = first(lg == mg)
    p_group = 1.0 / jnp.sum(jnp.where(is_g, jnp.exp(lg - mg), 0.0), axis=-1, keepdims=True)
    e_lo = N_EXPERT_GROUPS + EXPERTS_PER_GROUP * g_sel
    le = jnp.where((lane >= e_lo) & (lane < e_lo + EXPERTS_PER_GROUP), logits, NEG)
    v1 = jnp.max(le, axis=-1, keepdims=True)
    i1 = first(le == v1)
    le2 = jnp.where(lane == i1, NEG, le)
    v2 = jnp.max(le2, axis=-1, keepdims=True)
    i2 = first(le2 == v2)
    e2 = jnp.exp(v2 - v1)
    w1 = p_group / (1.0 + e2)
    w2 = p_group * e2 / (1.0 + e2)
    return jnp.where(lane == i1, w1, 0.0) + jnp.where(lane == i2, w2, 0.0)


def _outproj_kernel(mix_ref, w_ref, xp_ref, xs_ref, g_ref, wr_ref, br_ref,
                    gtp_ref, scp_ref, shp_ref, gts_ref, scs_ref, shs_ref,
                    x1_ref, h2_ref, comb_ref, *, n_prompt_tiles):
    is_s = pl.program_id(0) >= n_prompt_tiles
    x = jnp.where(is_s, xs_ref[...], xp_ref[...])
    gt = jnp.where(is_s, gts_ref[...], gtp_ref[...])
    sc = jnp.where(is_s, scs_ref[...], scp_ref[...])
    sh = jnp.where(is_s, shs_ref[...], shp_ref[...])
    x1 = x + gt * jnp.dot(mix_ref[...], w_ref[...], preferred_element_type=F32)
    x1_ref[...] = x1
    ms = jnp.mean(x1 * x1, axis=-1, keepdims=True)
    h2 = (x1 * lax.rsqrt(ms + EPS) * g_ref[...]) * (1.0 + sc) + sh
    h2_ref[...] = h2.astype(h2_ref.dtype)
    logits = jnp.dot(h2, wr_ref[...], preferred_element_type=F32, precision=lax.Precision.HIGHEST) + br_ref[...]
    comb_ref[...] = _route(logits)


def _outproj(mixed, w_out_bf, x_p, x_s, g2, w_router, b_router, gt_p, sc_p, sh_p, gt_s, sc_s, sh_s):
    tp, ts = x_p.shape[0], x_s.shape[0]
    tm = SMALL_TOK_TILE
    npt, nst = tp // tm, ts // tm
    prow = lambda i: (jnp.minimum(i, npt - 1), 0)
    srow = lambda i: (jnp.maximum(i - npt, 0), 0)
    const = lambda i: (0, 0)
    vec = pl.BlockSpec((1, D_MODEL), const)
    svec = pl.BlockSpec((tm, D_MODEL), srow)
    full = lambda i: (i, 0)
    return pl.pallas_call(
        functools.partial(_outproj_kernel, n_prompt_tiles=npt),
        grid=(npt + nst,),
        in_specs=[pl.BlockSpec((tm, D_MODEL), full), pl.BlockSpec((D_MODEL, D_MODEL), const),
                  pl.BlockSpec((tm, D_MODEL), prow), pl.BlockSpec((tm, D_MODEL), srow),
                  vec, pl.BlockSpec((D_MODEL, LANES), const), pl.BlockSpec((1, LANES), const),
                  vec, vec, vec, svec, svec, svec],
        out_specs=[pl.BlockSpec((tm, D_MODEL), full), pl.BlockSpec((tm, D_MODEL), full),
                   pl.BlockSpec((tm, LANES), full)],
        out_shape=[jax.ShapeDtypeStruct((tp + ts, D_MODEL), F32),
                   jax.ShapeDtypeStruct((tp + ts, D_MODEL), BF16),
                   jax.ShapeDtypeStruct((tp + ts, LANES), F32)],
        compiler_params=_cparams(("arbitrary",)),
        name="out_proj_norm2_router",
    )(mixed, w_out_bf, x_p, x_s, g2, w_router, b_router, gt_p, sc_p, sh_p, gt_s, sc_s, sh_s)


def _moe_kernel(h_ref, wgu_ref, wd_ref, comb_ref, x1_ref, gtp_ref, gts_ref, yp_ref, ys_ref, acc_ref, *, n_prompt_tiles):
    m, e = pl.program_id(0), pl.program_id(1)

    @pl.when(e == 0)
    def _():
        acc_ref[...] = jnp.zeros_like(acc_ref)

    gu = jnp.dot(h_ref[...], wgu_ref[0], preferred_element_type=F32)
    gate, up = gu[:, :EXPERT_FF], gu[:, EXPERT_FF:]
    comb = comb_ref[...]
    lane = lax.broadcasted_iota(jnp.int32, comb.shape, 1)
    w_e = jnp.sum(jnp.where(lane == N_EXPERT_GROUPS + e, comb, 0.0), axis=-1, keepdims=True)
    act = (gate * _sigmoid(gate)) * up * w_e
    acc_ref[...] += jnp.dot(act.astype(BF16), wd_ref[0], preferred_element_type=F32)

    last = e == pl.num_programs(1) - 1

    @pl.when(last & (m < n_prompt_tiles))
    def _():
        yp_ref[...] = x1_ref[...] + gtp_ref[...] * acc_ref[...]

    @pl.when(last & (m >= n_prompt_tiles))
    def _():
        ys_ref[...] = x1_ref[...] + gts_ref[...] * acc_ref[...]


def _moe(h2, w_gu_bf, w_down_bf, comb, x1, gt_p, gt_s, tp, ts):
    tm = SMALL_TOK_TILE
    npt, nst = tp // tm, ts // tm
    row = lambda m, e: (m, 0)
    return pl.pallas_call(
        functools.partial(_moe_kernel, n_prompt_tiles=npt),
        grid=(npt + nst, N_EXPERTS),
        in_specs=[pl.BlockSpec((tm, D_MODEL), row),
                  pl.BlockSpec((1, D_MODEL, 2 * EXPERT_FF), lambda m, e: (e, 0, 0)),
                  pl.BlockSpec((1, EXPERT_FF, D_MODEL), lambda m, e: (e, 0, 0)),
                  pl.BlockSpec((tm, LANES), row), pl.BlockSpec((tm, D_MODEL), row),
                  pl.BlockSpec((1, D_MODEL), lambda m, e: (0, 0)),
                  pl.BlockSpec((tm, D_MODEL), lambda m, e: (jnp.maximum(m - npt, 0), 0))],
        out_specs=[pl.BlockSpec((tm, D_MODEL), lambda m, e: (jnp.minimum(m, npt - 1), 0)),
                   pl.BlockSpec((tm, D_MODEL), lambda m, e: (jnp.maximum(m - npt, 0), 0))],
        out_shape=[jax.ShapeDtypeStruct((tp, D_MODEL), F32), jax.ShapeDtypeStruct((ts, D_MODEL), F32)],
        scratch_shapes=[pltpu.VMEM((tm, D_MODEL), F32)],
        compiler_params=_cparams(("arbitrary", "arbitrary")),
        name="moe",
    )(h2, w_gu_bf, w_down_bf, comb, x1, gt_p, gt_s)


def _rope_tables(n_prompt, n_seq, n_step):
    half = ROT_DIM // 2
    inv_freq = ROPE_THETA ** (-jnp.arange(half, dtype=F32) / half)
    pos = jnp.concatenate([jnp.arange(n_prompt, dtype=jnp.int32),
                           jnp.tile(PAST_LEN + jnp.arange(n_step, dtype=jnp.int32), n_seq)])
    ang = pos.astype(F32)[:, None] * inv_freq[None, :]
    cos, sin = jnp.cos(ang), jnp.sin(ang)
    n = pos.shape[0]
    one = jnp.ones((n, HEAD_DIM - ROT_DIM), F32)
    zero = jnp.zeros((n, HEAD_DIM - ROT_DIM), F32)
    zh = jnp.zeros((n, half), F32)
    rc = jnp.concatenate([cos, cos, one], axis=1)
    rs1 = jnp.concatenate([-sin, zh, zero], axis=1)
    rs2 = jnp.concatenate([zh, sin, zero], axis=1)
    return rc, rs1, rs2


def kernel(x_prompt, x_sample, cache_kv_w128, cache_kv_w512, cache_kv_w2048, state_ssm_re, state_ssm_im,
           c_prompt, c_sample, w_ada, b_ada, norm1_g, norm2_g, w_in, ssm_a_re, ssm_a_im, ssm_log_dt,
           ssm_b_re, ssm_b_im, ssm_c_re, ssm_c_im, ssm_d, w_glu, q_norm_g, k_norm_g, w_attn_br, w_out,
           w_router_group, b_router_group, w_router_expert, b_router_expert, w_expert_gate_up, w_expert_down):
    assert x_prompt.shape[0] == 1 and w_ada.shape[0] == 1
    n_prompt = x_prompt.shape[1]
    n_seq, n_step = x_sample.shape[0], x_sample.shape[1]
    n_samp = n_seq * n_step
    assert n_samp % TOK_TILE == 0 and n_prompt % ATT_SB == 0 and (n_prompt + n_samp) % PROJ_TM == 0
    x_p = x_prompt.reshape(n_prompt, D_MODEL)
    x_s = x_sample.reshape(n_samp, D_MODEL)

    pad = (-(n_seq + 1)) % 8
    c_all = jnp.concatenate([c_sample, c_prompt, jnp.zeros((pad, D_MODEL), F32)], axis=0)
    mod = _ada(c_all, w_ada[0], b_ada[0])
    mod_p = [mod[n_seq:n_seq + 1, k * D_MODEL:(k + 1) * D_MODEL] for k in range(6)]
    mod_s = [jnp.repeat(mod[:n_seq, k * D_MODEL:(k + 1) * D_MODEL], n_step, axis=0) for k in range(6)]
    sh1_p, sc1_p, gt1_p, sh2_p, sc2_p, gt2_p = mod_p
    sh1_s, sc1_s, gt1_s, sh2_s, sc2_s, gt2_s = mod_s

    h1 = _modnorm(x_p, x_s, norm1_g[0].reshape(1, D_MODEL), sc1_p, sh1_p, sc1_s, sh1_s)
    rc, rs1, rs2 = _rope_tables(n_prompt, n_seq, n_step)
    proj = _inproj(h1, w_in[0], rc, rs1, rs2, q_norm_g[0].reshape(1, HEAD_DIM), k_norm_g[0].reshape(1, HEAD_DIM))

    pw_re, pw_im, bb_re, bb_im = _ssm_prep(ssm_a_re[0], ssm_a_im[0], ssm_log_dt[0], ssm_b_re[0], ssm_b_im[0])
    b_mat, c_mat = _ssm_block_matrices(bb_re, bb_im, ssm_c_re[0], ssm_c_im[0])
    d_skip = ssm_d[0].reshape(1, SSM_WIDTH)
    yg_p, fre_p, fim_p = _s5_prompt(proj, n_prompt, d_skip, pw_re, pw_im, b_mat, c_mat)
    yg_s, fre_s, fim_s = _s5_sample(proj, n_prompt, n_seq, n_step, d_skip, pw_re, pw_im, b_mat, c_mat,
                                    state_ssm_re[0].reshape(n_seq, SSM_FLAT), state_ssm_im[0].reshape(n_seq, SSM_FLAT))

    o_p = _attn_prompt(proj, n_prompt)
    o_s = _attn_sample(proj, n_prompt, n_seq, n_step, (cache_kv_w128[0], cache_kv_w512[0], cache_kv_w2048[0]))

    mixed = _mix(yg_p, yg_s, o_p, o_s, w_glu[0], w_attn_br[0], proj)

    w_router = jnp.concatenate([w_router_group[0], w_router_expert[0],
                                jnp.zeros((D_MODEL, LANES - N_EXPERT_GROUPS - N_EXPERTS), F32)], axis=1)
    b_router = jnp.concatenate([b_router_group[0], b_router_expert[0],
                                jnp.zeros((LANES - N_EXPERT_GROUPS - N_EXPERTS,), F32)]).reshape(1, LANES)
    x1, h2, comb = _outproj(mixed, w_out[0].astype(BF16), x_p, x_s, norm2_g[0].reshape(1, D_MODEL),
                            w_router, b_router, gt1_p, sc2_p, sh2_p, gt1_s, sc2_s, sh2_s)
    y_p, y_s = _moe(h2, w_expert_gate_up[0].astype(BF16), w_expert_down[0].astype(BF16), comb, x1,
                    gt2_p, gt2_s, n_prompt, n_samp)

    kv_p, kv_s = [], []
    for g, (window, _) in enumerate(DILATION_PATTERNS):
        kc = slice(OFF_K + g * GROUP_WIDTH, OFF_K + (g + 1) * GROUP_WIDTH)
        vc = slice(OFF_V + g * GROUP_WIDTH, OFF_V + (g + 1) * GROUP_WIDTH)
        keep = min(window, n_prompt)
        rows_p = slice(n_prompt - keep, n_prompt)
        kp = proj[rows_p, kc].reshape(1, 1, keep, 1, HEADS_PER_GROUP, HEAD_DIM)
        vp = proj[rows_p, vc].reshape(1, 1, keep, 1, HEADS_PER_GROUP, HEAD_DIM)
        kv_p.append(jnp.concatenate([kp, vp], axis=3))
        ksm = proj[n_prompt:, kc].reshape(1, n_seq, n_step, 1, HEADS_PER_GROUP, HEAD_DIM)
        vsm = proj[n_prompt:, vc].reshape(1, n_seq, n_step, 1, HEADS_PER_GROUP, HEAD_DIM)
        kv_s.append(jnp.concatenate([ksm, vsm], axis=3))

    state_shape_p = (1, 1, SSM_GROUPS, SSM_STATE)
    state_shape_s = (1, n_seq, SSM_GROUPS, SSM_STATE)
    return (y_p.reshape(1, n_prompt, D_MODEL), y_s.reshape(n_seq, n_step, D_MODEL),
            kv_p[0], kv_p[1], kv_p[2], fre_p.reshape(state_shape_p), fim_p.reshape(state_shape_p),
            kv_s[0], kv_s[1], kv_s[2], fre_s.reshape(state_shape_s), fim_s.reshape(state_shape_s))
```

```python
import functools
import math

import numpy as np
import jax
import jax.numpy as jnp
from jax import lax
from jax.experimental import pallas as pl
from jax.experimental.pallas import tpu as pltpu

F32 = jnp.float32
BF16 = jnp.bfloat16

D_MODEL = 2048
PAST_LEN = 2048
SSM_WIDTH = D_MODEL // 2
SSM_GROUP = 16
SSM_GROUPS = SSM_WIDTH // SSM_GROUP
SSM_STATE = 64
SSM_FLAT = SSM_GROUPS * SSM_STATE
HEAD_DIM = 128
DILATION_PATTERNS = ((128, 1), (512, 4), (2048, 16))
N_PATTERNS = 3
HEADS_PER_GROUP = 4
GROUP_WIDTH = HEADS_PER_GROUP * HEAD_DIM
ATTN_WIDTH = N_PATTERNS * GROUP_WIDTH
ROT_DIM = HEAD_DIM // 4
ROPE_THETA = 500000.0
OFF_Q = SSM_WIDTH
OFF_K = OFF_Q + ATTN_WIDTH
OFF_V = OFF_K + ATTN_WIDTH
OFF_G = OFF_V + ATTN_WIDTH
IN_COLS = OFF_G + 2 * D_MODEL
N_EXPERT_GROUPS = 4
EXPERTS_PER_GROUP = 4
N_EXPERTS = 16
EXPERT_FF = D_MODEL // 4
EPS = 1e-6
NEG = -1e30

LANES = 128
VMEM_LIMIT = 56 * 1024 * 1024

TOK_TILE = 512
SMALL_TOK_TILE = 256
PROJ_TM = 1088
PROJ_TN = 512
SCAN_L = 32
SCAN_NC = 16
SCAN_TT = SCAN_L * SCAN_NC
SSM_BLK = 8
SSM_BLK_STATE = SSM_BLK * SSM_STATE
MOE_TM = 256
ROUTE_E1, ROUTE_E2, ROUTE_W1, ROUTE_W2 = 0, 1, 2, 3
ATT_SB = 2048
BAND = 128


def _cparams(sem, vmem=VMEM_LIMIT):
    return pltpu.CompilerParams(dimension_semantics=sem, vmem_limit_bytes=vmem)


def _sigmoid(x):
    return 1.0 / (1.0 + jnp.exp(-x))


def _gelu_tanh(x):
    c = math.sqrt(2.0 / math.pi)
    return 0.5 * x * (1.0 + jnp.tanh(c * (x + 0.044715 * (x * x * x))))


def _ada_kernel(c_ref, w_ref, b_ref, o_ref):
    c = c_ref[...]
    cs = (c * _sigmoid(c)).astype(BF16)
    o_ref[...] = jnp.dot(cs, w_ref[...].astype(BF16), preferred_element_type=F32) + b_ref[...]


def _ada(c_all, w_ada, b_ada):
    rows = c_all.shape[0]
    n_out = w_ada.shape[1]
    tn = 1024
    return pl.pallas_call(
        _ada_kernel,
        grid=(n_out // tn,),
        in_specs=[pl.BlockSpec((rows, D_MODEL), lambda n: (0, 0)),
                  pl.BlockSpec((D_MODEL, tn), lambda n: (0, n)),
                  pl.BlockSpec((1, tn), lambda n: (0, n))],
        out_specs=pl.BlockSpec((rows, tn), lambda n: (0, n)),
        out_shape=jax.ShapeDtypeStruct((rows, n_out), F32),
        compiler_params=_cparams(("arbitrary",)),
        name="ada_mod",
    )(c_all, w_ada, b_ada.reshape(1, n_out))


def _modnorm_kernel(xp_ref, xs_ref, g_ref, scp_ref, shp_ref, scs_ref, shs_ref, o_ref, *, n_prompt_tiles):
    is_s = pl.program_id(0) >= n_prompt_tiles
    x = jnp.where(is_s, xs_ref[...], xp_ref[...])
    sc = jnp.where(is_s, scs_ref[...], scp_ref[...])
    sh = jnp.where(is_s, shs_ref[...], shp_ref[...])
    ms = jnp.mean(x * x, axis=-1, keepdims=True)
    y = x * lax.rsqrt(ms + EPS) * g_ref[...]
    o_ref[...] = (y * (1.0 + sc) + sh).astype(o_ref.dtype)


def _modnorm(x_p, x_s, g, sc_p, sh_p, sc_s, sh_s):
    tp, ts = x_p.shape[0], x_s.shape[0]
    tm = TOK_TILE
    npt, nst = tp // tm, ts // tm
    row = lambda i: (jnp.minimum(i, npt - 1), 0)
    srow = lambda i: (jnp.maximum(i - npt, 0), 0)
    const = lambda i: (0, 0)
    return pl.pallas_call(
        functools.partial(_modnorm_kernel, n_prompt_tiles=npt),
        grid=(npt + nst,),
        in_specs=[pl.BlockSpec((tm, D_MODEL), row), pl.BlockSpec((tm, D_MODEL), srow),
                  pl.BlockSpec((1, D_MODEL), const), pl.BlockSpec((1, D_MODEL), const),
                  pl.BlockSpec((1, D_MODEL), const), pl.BlockSpec((tm, D_MODEL), srow),
                  pl.BlockSpec((tm, D_MODEL), srow)],
        out_specs=pl.BlockSpec((tm, D_MODEL), lambda i: (i, 0)),
        out_shape=jax.ShapeDtypeStruct((tp + ts, D_MODEL), BF16),
        compiler_params=_cparams(("arbitrary",)),
        name="modnorm1",
    )(x_p, x_s, g, sc_p, sh_p, sc_s, sh_s)


def _inproj_kernel(h_ref, w_ref, rc_ref, rs1_ref, rs2_ref, qg_ref, kg_ref, o_ref, wbf_ref, *, q_tiles, k_tiles):
    n = pl.program_id(0)

    @pl.when(pl.program_id(1) == 0)
    def _():
        wbf_ref[...] = w_ref[...].astype(BF16)

    acc = jnp.dot(h_ref[...], wbf_ref[...], preferred_element_type=F32)
    is_q = (n >= q_tiles[0]) & (n < q_tiles[1])
    is_k = (n >= k_tiles[0]) & (n < k_tiles[1])

    @pl.when(is_q | is_k)
    def _():
        gain = jnp.where(is_q, qg_ref[...], kg_ref[...])
        rc, rs1, rs2 = rc_ref[...], rs1_ref[...], rs2_ref[...]
        for hd in range(PROJ_TN // HEAD_DIM):
            x = acc[:, hd * HEAD_DIM:(hd + 1) * HEAD_DIM]
            ms = jnp.mean(x * x, axis=-1, keepdims=True)
            y = x * lax.rsqrt(ms + EPS) * gain
            half = ROT_DIM // 2
            up = pltpu.roll(y, HEAD_DIM - half, 1)
            dn = pltpu.roll(y, half, 1)
            o_ref[:, hd * HEAD_DIM:(hd + 1) * HEAD_DIM] = y * rc + up * rs1 + dn * rs2

    @pl.when(jnp.logical_not(is_q | is_k))
    def _():
        o_ref[...] = acc


def _inproj(h, w_in, rc, rs1, rs2, qg, kg):
    n_tok = h.shape[0]
    tm, tn = PROJ_TM, PROJ_TN
    q_tiles = (OFF_Q // tn, OFF_K // tn)
    k_tiles = (OFF_K // tn, OFF_V // tn)
    tab = pl.BlockSpec((tm, HEAD_DIM), lambda n, m: (m, 0))
    gain = pl.BlockSpec((1, HEAD_DIM), lambda n, m: (0, 0))
    return pl.pallas_call(
        functools.partial(_inproj_kernel, q_tiles=q_tiles, k_tiles=k_tiles),
        grid=(IN_COLS // tn, n_tok // tm),
        in_specs=[pl.BlockSpec((tm, D_MODEL), lambda n, m: (m, 0)),
                  pl.BlockSpec((D_MODEL, tn), lambda n, m: (0, n)),
                  tab, tab, tab, gain, gain],
        out_specs=pl.BlockSpec((tm, tn), lambda n, m: (m, n)),
        out_shape=jax.ShapeDtypeStruct((n_tok, IN_COLS), F32),
        scratch_shapes=[pltpu.VMEM((D_MODEL, tn), BF16)],
        compiler_params=_cparams(("arbitrary", "arbitrary")),
        name="in_proj",
    )(h, w_in, rc, rs1, rs2, qg, kg)


def _ssm_prep_kernel(are_ref, aim_ref, ldt_ref, arer_ref, aimr_ref, ldtr_ref, bre_ref, bim_ref,
                     pre_ref, pim_ref, bbre_ref, bbim_ref):
    def discretise(a_re, a_im, log_dt):
        dt = jnp.exp(log_dt)
        mag = jnp.exp(a_re * dt)
        return mag * jnp.cos(a_im * dt), mag * jnp.sin(a_im * dt)

    ab_re, ab_im = discretise(are_ref[...], aim_ref[...], ldt_ref[...])
    p_re, p_im = ab_re, ab_im
    for i in range(SCAN_L):
        pre_ref[i:i + 1, :] = p_re
        pim_ref[i:i + 1, :] = p_im
        p_re, p_im = p_re * ab_re - p_im * ab_im, p_re * ab_im + p_im * ab_re

    a_re, a_im = arer_ref[...], aimr_ref[...]
    r_re, r_im = discretise(a_re, a_im, ldtr_ref[...])
    nr, ni = r_re - 1.0, r_im
    den = a_re * a_re + a_im * a_im
    z_re = (nr * a_re + ni * a_im) / den
    z_im = (ni * a_re - nr * a_im) / den
    b_re, b_im = bre_ref[...], bim_ref[...]
    bbre_ref[...] = z_re * b_re - z_im * b_im
    bbim_ref[...] = z_re * b_im + z_im * b_re


def _ssm_prep(a_re, a_im, log_dt, b_re, b_im):
    g, p, n = b_re.shape
    flat = lambda x: x.reshape(1, g * p)
    rep = lambda x: jnp.repeat(x, n, axis=1)
    ldt_gp = jnp.broadcast_to(log_dt[:, None], (g, p))
    ldt_rep = jnp.broadcast_to(log_dt[:, None], (g, p * n))
    out_shape = [jax.ShapeDtypeStruct((SCAN_L, g * p), F32)] * 2 + [jax.ShapeDtypeStruct((g, p * n), F32)] * 2
    return pl.pallas_call(_ssm_prep_kernel, out_shape=out_shape, name="ssm_prep")(
        flat(a_re), flat(a_im), flat(ldt_gp), rep(a_re), rep(a_im), ldt_rep,
        b_re.reshape(g, p * n), b_im.reshape(g, p * n))


def _ssm_block_matrices(bb_re, bb_im, c_re, c_im):
    g, p, n = SSM_GROUPS, SSM_STATE, SSM_GROUP
    nb = g // SSM_BLK
    eye = jnp.eye(SSM_BLK, dtype=F32)

    def in_mat(bb):
        x = bb.reshape(nb, SSM_BLK, p, n)
        return jnp.einsum('bgpm,gh->bgmhp', x, eye).reshape(nb, SSM_BLK * n, SSM_BLK * p)

    def out_mat(c):
        x = c.reshape(nb, SSM_BLK, n, p)
        return jnp.einsum('bgnp,gh->bgphn', x, eye).reshape(nb, SSM_BLK * p, SSM_BLK * n)

    b_mat = jnp.concatenate([in_mat(bb_re), in_mat(bb_im)], axis=2).astype(BF16)
    c_mat = jnp.concatenate([out_mat(c_re), -out_mat(c_im)], axis=1).astype(BF16)
    return b_mat, c_mat


def _cmul_add(a_re, a_im, s_re, s_im, b_re, b_im):
    return a_re * s_re - a_im * s_im + b_re, a_re * s_im + a_im * s_re + b_im


def _s5_prompt_kernel(u_ref, d_ref, pre_ref, pim_ref, bm_ref, cm_ref, y_ref, fre_ref, fim_ref,
                      up_scr, bu_scr, lhs_scr, in_re_scr, in_im_scr, car_re, car_im, yn_scr):
    nc, ln, w = SCAN_NC, SCAN_L, SSM_BLK_STATE

    @pl.when(pl.program_id(1) == 0)
    def _():
        car_re[...] = jnp.zeros_like(car_re)
        car_im[...] = jnp.zeros_like(car_im)

    for i in range(ln):
        up_scr[i * nc:(i + 1) * nc, :] = u_ref[pl.ds(i, nc, stride=ln), :]
    up = up_scr[...]
    bu_scr[...] = jnp.dot(up.astype(BF16), bm_ref[0], preferred_element_type=F32)

    a_re = jnp.broadcast_to(pre_ref[0:1, :], (nc, w))
    a_im = jnp.broadcast_to(pim_ref[0:1, :], (nc, w))

    def local_step(i, carry):
        s_re, s_im = carry
        r0 = pl.multiple_of(i * nc, nc)
        s_re, s_im = _cmul_add(a_re, a_im, s_re, s_im, bu_scr[pl.ds(r0, nc), 0:w], bu_scr[pl.ds(r0, nc), w:2 * w])
        bu_scr[pl.ds(r0, nc), 0:w] = s_re
        bu_scr[pl.ds(r0, nc), w:2 * w] = s_im
        return s_re, s_im

    zero = jnp.zeros((nc, w), F32)
    f_re, f_im = lax.fori_loop(0, ln, local_step, (zero, zero))

    al_re, al_im = pre_ref[ln - 1:ln, :], pim_ref[ln - 1:ln, :]
    c_re, c_im = car_re[...], car_im[...]
    for c in range(nc):
        in_re_scr[c:c + 1, :] = c_re
        in_im_scr[c:c + 1, :] = c_im
        c_re, c_im = _cmul_add(al_re, al_im, c_re, c_im, f_re[c:c + 1, :], f_im[c:c + 1, :])
    car_re[...] = c_re
    car_im[...] = c_im
    fre_ref[...] = c_re
    fim_ref[...] = c_im
    in_re, in_im = in_re_scr[...], in_im_scr[...]

    def fix_step(i, _):
        r0 = pl.multiple_of(i * nc, nc)
        p_re = jnp.broadcast_to(pre_ref[pl.ds(i, 1), :], (nc, w))
        p_im = jnp.broadcast_to(pim_ref[pl.ds(i, 1), :], (nc, w))
        s_re, s_im = _cmul_add(p_re, p_im, in_re, in_im, bu_scr[pl.ds(r0, nc), 0:w], bu_scr[pl.ds(r0, nc), w:2 * w])
        lhs_scr[pl.ds(r0, nc), 0:w] = s_re.astype(BF16)
        lhs_scr[pl.ds(r0, nc), w:2 * w] = s_im.astype(BF16)
        return 0

    lax.fori_loop(0, ln, fix_step, 0)
    y = jnp.dot(lhs_scr[...], cm_ref[0], preferred_element_type=F32) + d_ref[...] * up
    for i in range(ln):
        yn_scr[pl.ds(i, nc, stride=ln), :] = y[i * nc:(i + 1) * nc, :]
    y_ref[...] = _gelu_tanh(yn_scr[...]).astype(y_ref.dtype)


def _s5_prompt(proj, n_prompt, d_skip, pw_re, pw_im, b_mat, c_mat):
    nb = SSM_GROUPS // SSM_BLK
    tt, w = SCAN_TT, SSM_BLK_STATE
    return pl.pallas_call(
        _s5_prompt_kernel,
        grid=(nb, n_prompt // tt),
        in_specs=[pl.BlockSpec((tt, LANES), lambda j, i: (i, j)),
                  pl.BlockSpec((1, LANES), lambda j, i: (0, j)),
                  pl.BlockSpec((SCAN_L, w), lambda j, i: (0, j)),
                  pl.BlockSpec((SCAN_L, w), lambda j, i: (0, j)),
                  pl.BlockSpec((1, LANES, 2 * w), lambda j, i: (j, 0, 0)),
                  pl.BlockSpec((1, 2 * w, LANES), lambda j, i: (j, 0, 0))],
        out_specs=[pl.BlockSpec((tt, LANES), lambda j, i: (i, j)),
                   pl.BlockSpec((1, w), lambda j, i: (0, j)),
                   pl.BlockSpec((1, w), lambda j, i: (0, j))],
        out_shape=[jax.ShapeDtypeStruct((n_prompt, SSM_WIDTH), BF16),
                   jax.ShapeDtypeStruct((1, SSM_FLAT), F32),
                   jax.ShapeDtypeStruct((1, SSM_FLAT), F32)],
        scratch_shapes=[pltpu.VMEM((tt, LANES), F32), pltpu.VMEM((tt, 2 * w), F32), pltpu.VMEM((tt, 2 * w), BF16),
                        pltpu.VMEM((SCAN_NC, w), F32), pltpu.VMEM((SCAN_NC, w), F32),
                        pltpu.VMEM((1, w), F32), pltpu.VMEM((1, w), F32), pltpu.VMEM((tt, LANES), F32)],
        compiler_params=_cparams(("arbitrary", "arbitrary")),
        name="s5_prompt",
    )(proj, d_skip, pw_re, pw_im, b_mat, c_mat)


def _s5_sample_kernel(u_ref, d_ref, pre_ref, pim_ref, bm_ref, cm_ref, s0re_ref, s0im_ref,
                      y_ref, fre_ref, fim_ref, up_scr, bu_scr, lhs_scr, yn_scr, *, n_seq, n_step):
    w = SSM_BLK_STATE
    rb = 16
    for s in range(n_step):
        up_scr[s * n_seq:(s + 1) * n_seq, :] = u_ref[pl.ds(s, n_seq, stride=n_step), :]
    up = up_scr[...]
    bu_scr[...] = jnp.dot(up.astype(BF16), bm_ref[0], preferred_element_type=F32)
    a_re = jnp.broadcast_to(pre_ref[0:1, :], (rb, w))
    a_im = jnp.broadcast_to(pim_ref[0:1, :], (rb, w))

    def seq_block(b, _):
        r0 = pl.multiple_of(b * rb, rb)
        s_re, s_im = s0re_ref[pl.ds(r0, rb), :], s0im_ref[pl.ds(r0, rb), :]
        for s in range(n_step):
            rows = pl.ds(pl.multiple_of(s * n_seq + r0, rb), rb)
            s_re, s_im = _cmul_add(a_re, a_im, s_re, s_im, bu_scr[rows, 0:w], bu_scr[rows, w:2 * w])
            lhs_scr[rows, 0:w] = s_re.astype(BF16)
            lhs_scr[rows, w:2 * w] = s_im.astype(BF16)
        fre_ref[pl.ds(r0, rb), :] = s_re
        fim_ref[pl.ds(r0, rb), :] = s_im
        return 0

    lax.fori_loop(0, n_seq // rb, seq_block, 0)
    y = jnp.dot(lhs_scr[...], cm_ref[0], preferred_element_type=F32) + d_ref[...] * up
    for s in range(n_step):
        yn_scr[pl.ds(s, n_seq, stride=n_step), :] = y[s * n_seq:(s + 1) * n_seq, :]
    y_ref[...] = _gelu_tanh(yn_scr[...]).astype(y_ref.dtype)


def _s5_sample(proj, n_prompt, n_seq, n_step, d_skip, pw_re, pw_im, b_mat, c_mat, s0_re, s0_im):
    nb = SSM_GROUPS // SSM_BLK
    rows, w = n_seq * n_step, SSM_BLK_STATE
    rblk = n_prompt // rows
    return pl.pallas_call(
        functools.partial(_s5_sample_kernel, n_seq=n_seq, n_step=n_step),
        grid=(nb,),
        in_specs=[pl.BlockSpec((rows, LANES), lambda j: (rblk, j)),
                  pl.BlockSpec((1, LANES), lambda j: (0, j)),
                  pl.BlockSpec((SCAN_L, w), lambda j: (0, j)),
                  pl.BlockSpec((SCAN_L, w), lambda j: (0, j)),
                  pl.BlockSpec((1, LANES, 2 * w), lambda j: (j, 0, 0)),
                  pl.BlockSpec((1, 2 * w, LANES), lambda j: (j, 0, 0)),
                  pl.BlockSpec((n_seq, w), lambda j: (0, j)),
                  pl.BlockSpec((n_seq, w), lambda j: (0, j))],
        out_specs=[pl.BlockSpec((rows, LANES), lambda j: (0, j)),
                   pl.BlockSpec((n_seq, w), lambda j: (0, j)),
                   pl.BlockSpec((n_seq, w), lambda j: (0, j))],
        out_shape=[jax.ShapeDtypeStruct((rows, SSM_WIDTH), BF16),
                   jax.ShapeDtypeStruct((n_seq, SSM_FLAT), F32),
                   jax.ShapeDtypeStruct((n_seq, SSM_FLAT), F32)],
        scratch_shapes=[pltpu.VMEM((rows, LANES), F32), pltpu.VMEM((rows, 2 * w), F32),
                        pltpu.VMEM((rows, 2 * w), BF16), pltpu.VMEM((rows, LANES), F32)],
        compiler_params=_cparams(("arbitrary",)),
        name="s5_sample",
    )(proj, d_skip, pw_re, pw_im, b_mat, c_mat, s0_re, s0_im)


def _attn_prompt_kernel(*refs):
    ins, o_ref, scr = refs[:15], refs[15], refs[16:]
    sb = pl.program_id(0)
    scale = HEAD_DIM ** -0.5
    qi = lax.broadcasted_iota(jnp.int32, (BAND, 2 * BAND), 0)
    kj = lax.broadcasted_iota(jnp.int32, (BAND, 2 * BAND), 1)
    dist = qi + BAND - kj
    band_ok = (dist >= 0) & (dist <= BAND)

    for g, (_, dil) in enumerate(DILATION_PATTERNS):
        q_ref, k_ref, v_ref, kp_ref, vp_ref = ins[5 * g:5 * g + 5]
        kbuf, vbuf, o_scr, m_scr, l_scr = scr[5 * g:5 * g + 5]
        pre = BAND * dil
        kbuf[0:pre, :] = kp_ref[...]
        kbuf[pre:pre + ATT_SB, :] = k_ref[...]
        vbuf[0:pre, :] = vp_ref[...]
        vbuf[pre:pre + ATT_SB, :] = v_ref[...]
        nblk = ATT_SB // pre

        def block(idx, _, dil=dil, pre=pre, nblk=nblk, q_ref=q_ref, kbuf=kbuf, vbuf=vbuf,
                  o_scr=o_scr, m_scr=m_scr, l_scr=l_scr):
            r = idx // nblk
            b = idx - r * nblk
            row0 = r + b * pre
            if dil == 1:
                q_rows = pl.ds(pl.multiple_of(row0, BAND), BAND)
                kv_rows = pl.ds(pl.multiple_of(row0, BAND), 2 * BAND)
            else:
                q_rows = pl.ds(row0, BAND, stride=dil)
                kv_rows = pl.ds(row0, 2 * BAND, stride=dil)
            q = (q_ref[q_rows, :] * scale).astype(BF16)
            kw = kbuf[kv_rows, :].astype(BF16)
            vw = vbuf[kv_rows, :].astype(BF16)
            s = lax.dot_general(q, kw, (((1,), (1,)), ((), ())), preferred_element_type=F32)
            s = jnp.where(band_ok & ((kj >= BAND) | (sb > 0) | (b > 0)), s, NEG)
            m = jnp.max(s, axis=-1, keepdims=True)
            p = jnp.exp(s - m)
            l = jnp.sum(p, axis=-1, keepdims=True)
            o = jnp.dot(p.astype(BF16), vw, preferred_element_type=F32)
            o_scr[q_rows, :] = o
            m_scr[q_rows, :] = jnp.broadcast_to(m, (BAND, HEAD_DIM))
            l_scr[q_rows, :] = jnp.broadcast_to(l, (BAND, HEAD_DIM))
            return 0

        lax.fori_loop(0, ATT_SB // BAND, block, 0)

    ms = [scr[5 * g + 3][...] for g in range(N_PATTERNS)]
    mx = jnp.maximum(jnp.maximum(ms[0], ms[1]), ms[2])
    num = jnp.zeros((ATT_SB, HEAD_DIM), F32)
    den = jnp.zeros((ATT_SB, HEAD_DIM), F32)
    for g in range(N_PATTERNS):
        wgt = jnp.exp(ms[g] - mx)
        num = num + wgt * scr[5 * g + 2][...]
        den = den + wgt * scr[5 * g + 4][...]
    o_ref[...] = num / den


def _attn_prompt(proj, n_prompt):
    hcol = lambda off, g, j: (off + g * GROUP_WIDTH) // HEAD_DIM + j
    in_specs, scratch = [], []
    for g, (_, dil) in enumerate(DILATION_PATTERNS):
        pre = BAND * dil
        per = ATT_SB // pre
        cur = lambda off, g=g: pl.BlockSpec((ATT_SB, HEAD_DIM), lambda sb, j: (sb, hcol(off, g, j)))
        prev = lambda off, g=g, per=per, pre=pre: pl.BlockSpec(
            (pre, HEAD_DIM), lambda sb, j: (jnp.maximum(sb * per - 1, 0), hcol(off, g, j)))
        in_specs += [cur(OFF_Q), cur(OFF_K), cur(OFF_V), prev(OFF_K), prev(OFF_V)]
        scratch += [pltpu.VMEM((pre + ATT_SB, HEAD_DIM), F32), pltpu.VMEM((pre + ATT_SB, HEAD_DIM), F32),
                    pltpu.VMEM((ATT_SB, HEAD_DIM), F32), pltpu.VMEM((ATT_SB, HEAD_DIM), F32),
                    pltpu.VMEM((ATT_SB, HEAD_DIM), F32)]
    return pl.pallas_call(
        _attn_prompt_kernel,
        grid=(n_prompt // ATT_SB, HEADS_PER_GROUP),
        in_specs=in_specs,
        out_specs=pl.BlockSpec((ATT_SB, HEAD_DIM), lambda sb, j: (sb, j)),
        out_shape=jax.ShapeDtypeStruct((n_prompt, GROUP_WIDTH), F32),
        scratch_shapes=scratch,
        compiler_params=_cparams(("arbitrary", "arbitrary")),
        name="attn_prompt",
    )(*([proj] * 15))


SEQ_PER_STEP = 2
KV_PLANES = 2 * HEADS_PER_GROUP


def _compact_pitch(n_step):
    tiles = n_step * KV_PLANES // 8
    return 8 * (tiles + 1 - tiles % 2)


def _sample_bias(n_step):
    rows = HEADS_PER_GROUP * n_step
    step = np.arange(rows) % n_step
    cache_bias, new_bias = [], []
    for (window, dil) in DILATION_PATTERNS:
        wb = min(window, PAST_LEN)
        band = window // dil
        if dil > n_step:
            res, i = np.meshgrid(np.arange(n_step), np.arange(wb // dil), indexing='ij')
            c = (i * dil + res).reshape(-1)
        else:
            c = np.arange(wb)
        delta = wb + step[:, None] - c[None, :]
        ok = (delta >= 0) & (delta % dil == 0) & (delta // dil <= band)
        cache_bias.append(np.where(ok, 0.0, NEG).astype(np.float32))
        nb = np.full((SEQ_PER_STEP, rows, LANES), NEG, np.float32)
        for a in range(SEQ_PER_STEP):
            for sp in range(n_step):
                dl = step - sp
                okn = (dl >= 0) & (dl % dil == 0) & (dl // dil <= band)
                nb[a, :, a * n_step + sp] = np.where(okn, 0.0, NEG)
        new_bias.append(nb)
    return cache_bias, new_bias


def _attn_sample_kernel(*refs, n_step):
    (q0, k0, v0, q1, k1, v1, q2, k2, v2, c0, c1, c2, cb0, cb1, cb2, nb0, nb1, nb2, o_ref) = refs
    qs, ks, vs = (q0, q1, q2), (k0, k1, k2), (v0, v1, v2)
    caches, cbias, nbias = (c0, c1, c2), (cb0, cb1, cb2), (nb0, nb1, nb2)
    rows = HEADS_PER_GROUP * n_step
    gw = GROUP_WIDTH
    scale = HEAD_DIM ** -0.5
    row_head = lax.broadcasted_iota(jnp.int32, (rows, gw), 0) // n_step
    lane_head = lax.broadcasted_iota(jnp.int32, (rows, gw), 1) // HEAD_DIM
    own_head = row_head == lane_head
    nt = (((1,), (1,)), ((), ()))
    pad = jnp.zeros((LANES - SEQ_PER_STEP * n_step, gw), F32)

    def planes(load):
        k = jnp.concatenate([load(h) for h in range(HEADS_PER_GROUP)], axis=1)
        v = jnp.concatenate([load(HEADS_PER_GROUP + h) for h in range(HEADS_PER_GROUP)], axis=1)
        return k.astype(BF16), v.astype(BF16)

    for a in range(SEQ_PER_STEP):
        pieces = []
        for g, (window, dil) in enumerate(DILATION_PATTERNS):
            q = qs[g][a * n_step:(a + 1) * n_step, :] * scale
            qbd = jnp.where(own_head, jnp.concatenate([q] * HEADS_PER_GROUP, axis=0), 0.0).astype(BF16)
            cache = caches[g]
            per_seq = cache.shape[0] // SEQ_PER_STEP
            if dil > n_step:
                kvs = [planes(lambda p, r=r: cache[pl.ds(a * per_seq, per_seq), r * KV_PLANES + p, :])
                       for r in range(n_step)]
            else:
                kvs = [planes(lambda p: cache[pl.ds(a * per_seq + p, per_seq // KV_PLANES, stride=KV_PLANES), :])]
            sc = jnp.concatenate([lax.dot_general(qbd, k, nt, preferred_element_type=F32) for k, _ in kvs], axis=1)
            pieces.append((sc + cbias[g][...], [v for _, v in kvs]))
            k_new = jnp.concatenate([ks[g][...], pad], axis=0).astype(BF16)
            v_new = jnp.concatenate([vs[g][...], pad], axis=0).astype(BF16)
            sn = lax.dot_general(qbd, k_new, nt, preferred_element_type=F32) + nbias[g][a]
            pieces.append((sn, [v_new]))
        m = functools.reduce(jnp.maximum, [jnp.max(s, axis=-1, keepdims=True) for s, _ in pieces])
        l = jnp.zeros((rows, 1), F32)
        acc = jnp.zeros((rows, gw), F32)
        for s, vals in pieces:
            p = jnp.exp(s - m)
            l = l + jnp.sum(p, axis=-1, keepdims=True)
            pb = p.astype(BF16)
            nk = pb.shape[1] // len(vals)
            for r, v in enumerate(vals):
                acc = acc + jnp.dot(pb[:, r * nk:(r + 1) * nk], v, preferred_element_type=F32)
        acc = jnp.where(own_head, acc, 0.0)
        o16 = functools.reduce(lambda x, y: x + y,
                               [acc[:, h * HEAD_DIM:(h + 1) * HEAD_DIM] for h in range(HEADS_PER_GROUP)]) / l
        for h in range(HEADS_PER_GROUP):
            o_ref[a * n_step:(a + 1) * n_step, h * HEAD_DIM:(h + 1) * HEAD_DIM] = o16[h * n_step:(h + 1) * n_step, :]


def _attn_sample(proj, n_prompt, n_seq, n_step, caches):
    rows = SEQ_PER_STEP * n_step
    rblk = n_prompt // rows
    cache_bias, new_bias = _sample_bias(n_step)
    tok = lambda off, g: pl.BlockSpec((rows, GROUP_WIDTH), lambda i: (rblk + i, (off + g * GROUP_WIDTH) // GROUP_WIDTH))
    in_specs, args = [], []
    for g in range(N_PATTERNS):
        in_specs += [tok(OFF_Q, g), tok(OFF_K, g), tok(OFF_V, g)]
        args += [proj, proj, proj]
    for g, (window, dil) in enumerate(DILATION_PATTERNS):
        c = caches[g]
        wb = c.shape[1]
        if dil > n_step:
            pitch = _compact_pitch(n_step)
            assert pitch <= dil * KV_PLANES and wb % dil == 0
            c = c.reshape(n_seq * (wb // dil), dil * KV_PLANES, HEAD_DIM)
            in_specs.append(pl.BlockSpec((SEQ_PER_STEP * (wb // dil), pitch, HEAD_DIM), lambda i: (i, 0, 0)))
        else:
            c = c.reshape(n_seq * wb * KV_PLANES, HEAD_DIM)
            in_specs.append(pl.BlockSpec((SEQ_PER_STEP * wb * KV_PLANES, HEAD_DIM), lambda i: (i, 0)))
        args.append(c)
    for b in cache_bias:
        in_specs.append(pl.BlockSpec(b.shape, lambda i: (0, 0)))
        args.append(jnp.asarray(b))
    for b in new_bias:
        in_specs.append(pl.BlockSpec(b.shape, lambda i: (0, 0, 0)))
        args.append(jnp.asarray(b))
    return pl.pallas_call(
        functools.partial(_attn_sample_kernel, n_step=n_step),
        grid=(n_seq // SEQ_PER_STEP,),
        in_specs=in_specs,
        out_specs=pl.BlockSpec((rows, GROUP_WIDTH), lambda i: (i, 0)),
        out_shape=jax.ShapeDtypeStruct((n_seq * n_step, GROUP_WIDTH), F32),
        compiler_params=_cparams(("arbitrary",)),
        name="attn_sample",
    )(*args)


def _mix_kernel(yp_ref, ys_ref, op_ref, os_ref, wa_ref, wb_ref, wbr_ref, ga_ref, gb_ref, o_ref,
                wa_bf, wb_bf, wbr_bf, *, n_prompt_tiles):
    i = pl.program_id(1)

    @pl.when(i == 0)
    def _():
        wa_bf[...] = wa_ref[...].astype(BF16)
        wb_bf[...] = wb_ref[...].astype(BF16)
        wbr_bf[...] = wbr_ref[...].astype(BF16)

    is_s = i >= n_prompt_tiles
    y = jnp.where(is_s, ys_ref[...], yp_ref[...])
    o = jnp.where(is_s, os_ref[...], op_ref[...]).astype(BF16)
    glu_a = jnp.dot(y, wa_bf[...], preferred_element_type=F32)
    glu_b = jnp.dot(y, wb_bf[...], preferred_element_type=F32)
    branch_a = glu_a * _sigmoid(glu_b)
    branch_b = jnp.dot(o, wbr_bf[...], preferred_element_type=F32)
    o_ref[...] = (_sigmoid(ga_ref[...]) * branch_a + _sigmoid(gb_ref[...]) * branch_b).astype(o_ref.dtype)


def _mix(y_p, y_s, o_p, o_s, w_glu, w_attn_br, proj):
    tp, ts = y_p.shape[0], y_s.shape[0]
    tm, tn = TOK_TILE, 512
    npt, nst = tp // tm, ts // tm
    ncol = D_MODEL // tn
    prow = lambda n, i: (jnp.minimum(i, npt - 1), 0)
    srow = lambda n, i: (jnp.maximum(i - npt, 0), 0)
    return pl.pallas_call(
        functools.partial(_mix_kernel, n_prompt_tiles=npt),
        grid=(ncol, npt + nst),
        in_specs=[pl.BlockSpec((tm, SSM_WIDTH), prow), pl.BlockSpec((tm, SSM_WIDTH), srow),
                  pl.BlockSpec((tm, GROUP_WIDTH), prow), pl.BlockSpec((tm, GROUP_WIDTH), srow),
                  pl.BlockSpec((SSM_WIDTH, tn), lambda n, i: (0, n)),
                  pl.BlockSpec((SSM_WIDTH, tn), lambda n, i: (0, ncol + n)),
                  pl.BlockSpec((GROUP_WIDTH, tn), lambda n, i: (0, n)),
                  pl.BlockSpec((tm, tn), lambda n, i: (i, OFF_G // tn + n)),
                  pl.BlockSpec((tm, tn), lambda n, i: (i, OFF_G // tn + ncol + n))],
        out_specs=pl.BlockSpec((tm, tn), lambda n, i: (i, n)),
        out_shape=jax.ShapeDtypeStruct((tp + ts, D_MODEL), BF16),
        scratch_shapes=[pltpu.VMEM((SSM_WIDTH, tn), BF16), pltpu.VMEM((SSM_WIDTH, tn), BF16),
                        pltpu.VMEM((GROUP_WIDTH, tn), BF16)],
        compiler_params=_cparams(("arbitrary", "arbitrary")),
        name="glu_mix",
    )(y_p, y_s, o_p, o_s, w_glu, w_glu, w_attn_br, proj, proj)


def _route(logits):
    lane = lax.broadcasted_iota(jnp.int32, logits.shape, 1).astype(F32)
    big = 1000.0
    first = lambda cond: jnp.min(jnp.where(cond, lane, big), axis=-1, keepdims=True)
    is_g = lane < N_EXPERT_GROUPS
    lg = jnp.where(is_g, logits, NEG)
    mg = jnp.max(lg, axis=-1, keepdims=True)
    g_sel = first(lg == mg)
    p_group = 1.0 / jnp.sum(jnp.where(is_g, jnp.exp(lg - mg), 0.0), axis=-1, keepdims=True)
    e_lo = N_EXPERT_GROUPS + EXPERTS_PER_GROUP * g_sel
    le = jnp.where((lane >= e_lo) & (lane < e_lo + EXPERTS_PER_GROUP), logits, NEG)
    v1 = jnp.max(le, axis=-1, keepdims=True)
    i1 = first(le == v1)
    le2 = jnp.where(lane == i1, NEG, le)
    v2 = jnp.max(le2, axis=-1, keepdims=True)
    i2 = first(le2 == v2)
    e2 = jnp.exp(v2 - v1)
    w1 = p_group / (1.0 + e2)
    w2 = p_group * e2 / (1.0 + e2)
    pick = lambda k, val: jnp.where(lane == k, val, 0.0)
    return (pick(ROUTE_E1, i1 - N_EXPERT_GROUPS) + pick(ROUTE_E2, i2 - N_EXPERT_GROUPS)
            + pick(ROUTE_W1, w1) + pick(ROUTE_W2, w2))


def _outproj_kernel(mix_ref, w_ref, xp_ref, xs_ref, g_ref, wr_ref, br_ref,
                    gtp_ref, scp_ref, shp_ref, gts_ref, scs_ref, shs_ref,
                    x1_ref, h2_ref, route_ref, *, n_prompt_tiles):
    is_s = pl.program_id(0) >= n_prompt_tiles
    x = jnp.where(is_s, xs_ref[...], xp_ref[...])
    gt = jnp.where(is_s, gts_ref[...], gtp_ref[...])
    sc = jnp.where(is_s, scs_ref[...], scp_ref[...])
    sh = jnp.where(is_s, shs_ref[...], shp_ref[...])
    x1 = x + gt * jnp.dot(mix_ref[...], w_ref[...], preferred_element_type=F32)
    x1_ref[...] = x1
    ms = jnp.mean(x1 * x1, axis=-1, keepdims=True)
    h2 = (x1 * lax.rsqrt(ms + EPS) * g_ref[...]) * (1.0 + sc) + sh
    h2_ref[...] = h2.astype(h2_ref.dtype)
    logits = jnp.dot(h2, wr_ref[...], preferred_element_type=F32, precision=lax.Precision.HIGHEST) + br_ref[...]
    route_ref[...] = _route(logits)


def _outproj(mixed, w_out_bf, x_p, x_s, g2, w_router, b_router, gt_p, sc_p, sh_p, gt_s, sc_s, sh_s):
    tp, ts = x_p.shape[0], x_s.shape[0]
    tm = SMALL_TOK_TILE
    npt, nst = tp // tm, ts // tm
    prow = lambda i: (jnp.minimum(i, npt - 1), 0)
    srow = lambda i: (jnp.maximum(i - npt, 0), 0)
    const = lambda i: (0, 0)
    vec = pl.BlockSpec((1, D_MODEL), const)
    svec = pl.BlockSpec((tm, D_MODEL), srow)
    full = lambda i: (i, 0)
    return pl.pallas_call(
        functools.partial(_outproj_kernel, n_prompt_tiles=npt),
        grid=(npt + nst,),
        in_specs=[pl.BlockSpec((tm, D_MODEL), full), pl.BlockSpec((D_MODEL, D_MODEL), const),
                  pl.BlockSpec((tm, D_MODEL), prow), pl.BlockSpec((tm, D_MODEL), srow),
                  vec, pl.BlockSpec((D_MODEL, LANES), const), pl.BlockSpec((1, LANES), const),
                  vec, vec, vec, svec, svec, svec],
        out_specs=[pl.BlockSpec((tm, D_MODEL), full), pl.BlockSpec((tm, D_MODEL), full),
                   pl.BlockSpec((tm, LANES), full)],
        out_shape=[jax.ShapeDtypeStruct((tp + ts, D_MODEL), F32),
                   jax.ShapeDtypeStruct((tp + ts, D_MODEL), F32),
                   jax.ShapeDtypeStruct((tp + ts, LANES), F32)],
        compiler_params=_cparams(("arbitrary",)),
        name="out_proj_norm2_router",
    )(mixed, w_out_bf, x_p, x_s, g2, w_router, b_router, gt_p, sc_p, sh_p, gt_s, sc_s, sh_s)


def _dispatch_plan(route, tm):
    e = route[:, ROUTE_E1:ROUTE_E2 + 1].astype(jnp.int32).reshape(-1)
    n_pairs = e.shape[0]
    onehot = (e[:, None] == jnp.arange(N_EXPERTS, dtype=jnp.int32)[None, :]).astype(jnp.int32)
    csum = jnp.cumsum(onehot, axis=0)
    rank = jnp.sum(onehot * csum, axis=1) - 1
    tiles_per_expert = (csum[-1] + tm - 1) // tm
    tile_end = jnp.cumsum(tiles_per_expert)
    tile_start = tile_end - tiles_per_expert
    dest = (tile_start[e] * tm + rank).astype(jnp.int32)
    max_tiles = n_pairs // tm + N_EXPERTS
    k = jnp.arange(max_tiles, dtype=jnp.int32)
    tile_expert = jnp.minimum(jnp.sum((k[:, None] >= tile_end[None, :]).astype(jnp.int32), axis=1), N_EXPERTS - 1)
    n_used = tile_end[-1].astype(jnp.int32)
    last_expert = jnp.take(tile_expert, n_used - 1)
    tile_expert = jnp.where(k < n_used, tile_expert, last_expert).astype(jnp.int32)
    return dest, tile_expert, n_used.reshape(1), max_tiles


def _pair_copies(dest_ref, tile, rows, make):
    def each(fn):
        def body(r, _):
            for k in range(2):
                fn(make(r, k, dest_ref[(tile * rows + r) * 2 + k]))
            return 0
        lax.fori_loop(0, rows, body, 0)
    each(lambda c: c.start())
    each(lambda c: c.wait())


def _dispatch_kernel(dest_ref, h_ref, init_ref, xs_ref, sem):
    del init_ref
    make = lambda r, k, d: pltpu.make_async_copy(h_ref.at[pl.ds(r, 1)], xs_ref.at[pl.ds(d, 1)], sem)
    _pair_copies(dest_ref, pl.program_id(0), h_ref.shape[0], make)


def _dispatch(dest, h2, n_slots):
    n_tok = h2.shape[0]
    tm = SMALL_TOK_TILE
    return pl.pallas_call(
        _dispatch_kernel,
        grid_spec=pltpu.PrefetchScalarGridSpec(
            num_scalar_prefetch=1, grid=(n_tok // tm,),
            in_specs=[pl.BlockSpec((tm, D_MODEL), lambda i, d: (i, 0)), pl.BlockSpec(memory_space=pl.ANY)],
            out_specs=pl.BlockSpec(memory_space=pl.ANY),
            scratch_shapes=[pltpu.SemaphoreType.DMA(())]),
        out_shape=jax.ShapeDtypeStruct((n_slots, D_MODEL), F32),
        input_output_aliases={2: 0},
        compiler_params=_cparams(("arbitrary",)),
        name="moe_dispatch",
    )(dest, h2, jnp.zeros((n_slots, D_MODEL), F32))


def _experts_kernel(te_ref, used_ref, xs_ref, wgu_ref, wd_ref, y_ref, wgu_bf, wd_bf):
    k = pl.program_id(0)
    new_expert = (k == 0) | (te_ref[k] != te_ref[jnp.maximum(k - 1, 0)])

    @pl.when(new_expert)
    def _():
        wgu_bf[...] = wgu_ref[0].astype(BF16)
        wd_bf[...] = wd_ref[0].astype(BF16)

    @pl.when(k < used_ref[0])
    def _():
        gu = jnp.dot(xs_ref[...].astype(BF16), wgu_bf[...], preferred_element_type=F32)
        gate, up = gu[:, :EXPERT_FF], gu[:, EXPERT_FF:]
        act = (gate * _sigmoid(gate)) * up
        y_ref[...] = jnp.dot(act.astype(BF16), wd_bf[...], preferred_element_type=F32)

    @pl.when(k >= used_ref[0])
    def _():
        y_ref[...] = jnp.zeros_like(y_ref)


def _experts(tile_expert, n_used, xs, w_gu, w_down, max_tiles):
    tm = MOE_TM
    return pl.pallas_call(
        _experts_kernel,
        grid_spec=pltpu.PrefetchScalarGridSpec(
            num_scalar_prefetch=2, grid=(max_tiles,),
            in_specs=[pl.BlockSpec((tm, D_MODEL), lambda k, te, nu: (k, 0)),
                      pl.BlockSpec((1, D_MODEL, 2 * EXPERT_FF), lambda k, te, nu: (te[k], 0, 0)),
                      pl.BlockSpec((1, EXPERT_FF, D_MODEL), lambda k, te, nu: (te[k], 0, 0))],
            out_specs=pl.BlockSpec((tm, D_MODEL), lambda k, te, nu: (k, 0)),
            scratch_shapes=[pltpu.VMEM((D_MODEL, 2 * EXPERT_FF), BF16), pltpu.VMEM((EXPERT_FF, D_MODEL), BF16)]),
        out_shape=jax.ShapeDtypeStruct((max_tiles * tm, D_MODEL), F32),
        compiler_params=_cparams(("arbitrary",)),
        name="moe_experts",
    )(tile_expert, n_used, xs, w_gu, w_down)


def _combine_kernel(dest_ref, y_hbm, x1_ref, route_ref, gtp_ref, gts_ref, yp_ref, ys_ref, buf, sem, *, n_prompt_tiles):
    i = pl.program_id(0)
    make = lambda r, k, d: pltpu.make_async_copy(y_hbm.at[pl.ds(d, 1)], buf.at[k, pl.ds(r, 1)], sem)
    _pair_copies(dest_ref, i, x1_ref.shape[0], make)
    route = route_ref[...]
    lane = lax.broadcasted_iota(jnp.int32, route.shape, 1)
    w1 = jnp.sum(jnp.where(lane == ROUTE_W1, route, 0.0), axis=-1, keepdims=True)
    w2 = jnp.sum(jnp.where(lane == ROUTE_W2, route, 0.0), axis=-1, keepdims=True)
    moe = w1 * buf[0] + w2 * buf[1]

    @pl.when(i < n_prompt_tiles)
    def _():
        yp_ref[...] = x1_ref[...] + gtp_ref[...] * moe

    @pl.when(i >= n_prompt_tiles)
    def _():
        ys_ref[...] = x1_ref[...] + gts_ref[...] * moe


def _combine(dest, y_slots, x1, route, gt_p, gt_s, tp, ts):
    tm = SMALL_TOK_TILE
    npt, nst = tp // tm, ts // tm
    row = lambda i, d: (i, 0)
    return pl.pallas_call(
        functools.partial(_combine_kernel, n_prompt_tiles=npt),
        grid_spec=pltpu.PrefetchScalarGridSpec(
            num_scalar_prefetch=1, grid=(npt + nst,),
            in_specs=[pl.BlockSpec(memory_space=pl.ANY),
                      pl.BlockSpec((tm, D_MODEL), row), pl.BlockSpec((tm, LANES), row),
                      pl.BlockSpec((1, D_MODEL), lambda i, d: (0, 0)),
                      pl.BlockSpec((tm, D_MODEL), lambda i, d: (jnp.maximum(i - npt, 0), 0))],
            out_specs=[pl.BlockSpec((tm, D_MODEL), lambda i, d: (jnp.minimum(i, npt - 1), 0)),
                       pl.BlockSpec((tm, D_MODEL), lambda i, d: (jnp.maximum(i - npt, 0), 0))],
            scratch_shapes=[pltpu.VMEM((2, tm, D_MODEL), F32), pltpu.SemaphoreType.DMA(())]),
        out_shape=[jax.ShapeDtypeStruct((tp, D_MODEL), F32), jax.ShapeDtypeStruct((ts, D_MODEL), F32)],
        compiler_params=_cparams(("arbitrary",)),
        name="moe_combine",
    )(dest, y_slots, x1, route, gt_p, gt_s)


def _moe(h2, route, x1, w_gu, w_down, gt_p, gt_s, tp, ts):
    dest, tile_expert, n_used, max_tiles = _dispatch_plan(route, MOE_TM)
    xs = _dispatch(dest, h2, max_tiles * MOE_TM)
    y_slots = _experts(tile_expert, n_used, xs, w_gu, w_down, max_tiles)
    return _combine(dest, y_slots, x1, route, gt_p, gt_s, tp, ts)


def _rope_tables(n_prompt, n_seq, n_step):
    half = ROT_DIM // 2
    inv_freq = ROPE_THETA ** (-jnp.arange(half, dtype=F32) / half)
    pos = jnp.concatenate([jnp.arange(n_prompt, dtype=jnp.int32),
                           jnp.tile(PAST_LEN + jnp.arange(n_step, dtype=jnp.int32), n_seq)])
    ang = pos.astype(F32)[:, None] * inv_freq[None, :]
    cos, sin = jnp.cos(ang), jnp.sin(ang)
    n = pos.shape[0]
    one = jnp.ones((n, HEAD_DIM - ROT_DIM), F32)
    zero = jnp.zeros((n, HEAD_DIM - ROT_DIM), F32)
    zh = jnp.zeros((n, half), F32)
    rc = jnp.concatenate([cos, cos, one], axis=1)
    rs1 = jnp.concatenate([-sin, zh, zero], axis=1)
    rs2 = jnp.concatenate([zh, sin, zero], axis=1)
    return rc, rs1, rs2


def kernel(x_prompt, x_sample, cache_kv_w128, cache_kv_w512, cache_kv_w2048, state_ssm_re, state_ssm_im,
           c_prompt, c_sample, w_ada, b_ada, norm1_g, norm2_g, w_in, ssm_a_re, ssm_a_im, ssm_log_dt,
           ssm_b_re, ssm_b_im, ssm_c_re, ssm_c_im, ssm_d, w_glu, q_norm_g, k_norm_g, w_attn_br, w_out,
           w_router_group, b_router_group, w_router_expert, b_router_expert, w_expert_gate_up, w_expert_down):
    assert x_prompt.shape[0] == 1 and w_ada.shape[0] == 1
    n_prompt = x_prompt.shape[1]
    n_seq, n_step = x_sample.shape[0], x_sample.shape[1]
    n_samp = n_seq * n_step
    assert n_samp % TOK_TILE == 0 and n_prompt % ATT_SB == 0 and (n_prompt + n_samp) % PROJ_TM == 0
    x_p = x_prompt.reshape(n_prompt, D_MODEL)
    x_s = x_sample.reshape(n_samp, D_MODEL)

    pad = (-(n_seq + 1)) % 8
    c_all = jnp.concatenate([c_sample, c_prompt, jnp.zeros((pad, D_MODEL), F32)], axis=0)
    mod = _ada(c_all, w_ada[0], b_ada[0])
    mod_p = [mod[n_seq:n_seq + 1, k * D_MODEL:(k + 1) * D_MODEL] for k in range(6)]
    mod_s = [jnp.repeat(mod[:n_seq, k * D_MODEL:(k + 1) * D_MODEL], n_step, axis=0) for k in range(6)]
    sh1_p, sc1_p, gt1_p, sh2_p, sc2_p, gt2_p = mod_p
    sh1_s, sc1_s, gt1_s, sh2_s, sc2_s, gt2_s = mod_s

    h1 = _modnorm(x_p, x_s, norm1_g[0].reshape(1, D_MODEL), sc1_p, sh1_p, sc1_s, sh1_s)
    rc, rs1, rs2 = _rope_tables(n_prompt, n_seq, n_step)
    proj = _inproj(h1, w_in[0], rc, rs1, rs2, q_norm_g[0].reshape(1, HEAD_DIM), k_norm_g[0].reshape(1, HEAD_DIM))

    pw_re, pw_im, bb_re, bb_im = _ssm_prep(ssm_a_re[0], ssm_a_im[0], ssm_log_dt[0], ssm_b_re[0], ssm_b_im[0])
    b_mat, c_mat = _ssm_block_matrices(bb_re, bb_im, ssm_c_re[0], ssm_c_im[0])
    d_skip = ssm_d[0].reshape(1, SSM_WIDTH)
    yg_p, fre_p, fim_p = _s5_prompt(proj, n_prompt, d_skip, pw_re, pw_im, b_mat, c_mat)
    yg_s, fre_s, fim_s = _s5_sample(proj, n_prompt, n_seq, n_step, d_skip, pw_re, pw_im, b_mat, c_mat,
                                    state_ssm_re[0].reshape(n_seq, SSM_FLAT), state_ssm_im[0].reshape(n_seq, SSM_FLAT))

    o_p = _attn_prompt(proj, n_prompt)
    o_s = _attn_sample(proj, n_prompt, n_seq, n_step, (cache_kv_w128[0], cache_kv_w512[0], cache_kv_w2048[0]))

    mixed = _mix(yg_p, yg_s, o_p, o_s, w_glu[0], w_attn_br[0], proj)

    w_router = jnp.concatenate([w_router_group[0], w_router_expert[0],
                                jnp.zeros((D_MODEL, LANES - N_EXPERT_GROUPS - N_EXPERTS), F32)], axis=1)
    b_router = jnp.concatenate([b_router_group[0], b_router_expert[0],
                                jnp.zeros((LANES - N_EXPERT_GROUPS - N_EXPERTS,), F32)]).reshape(1, LANES)
    x1, h2, route = _outproj(mixed, w_out[0].astype(BF16), x_p, x_s, norm2_g[0].reshape(1, D_MODEL),
                             w_router, b_router, gt1_p, sc2_p, sh2_p, gt1_s, sc2_s, sh2_s)
    y_p, y_s = _moe(h2, route, x1, w_expert_gate_up[0], w_expert_down[0], gt2_p, gt2_s, n_prompt, n_samp)

    kv_p, kv_s = [], []
    for g, (window, _) in enumerate(DILATION_PATTERNS):
        kc = slice(OFF_K + g * GROUP_WIDTH, OFF_K + (g + 1) * GROUP_WIDTH)
        vc = slice(OFF_V + g * GROUP_WIDTH, OFF_V + (g + 1) * GROUP_WIDTH)
        keep = min(window, n_prompt)
        rows_p = slice(n_prompt - keep, n_prompt)
        kp = proj[rows_p, kc].reshape(1, 1, keep, 1, HEADS_PER_GROUP, HEAD_DIM)
        vp = proj[rows_p, vc].reshape(1, 1, keep, 1, HEADS_PER_GROUP, HEAD_DIM)
        kv_p.append(jnp.concatenate([kp, vp], axis=3))
        ksm = proj[n_prompt:, kc].reshape(1, n_seq, n_step, 1, HEADS_PER_GROUP, HEAD_DIM)
        vsm = proj[n_prompt:, vc].reshape(1, n_seq, n_step, 1, HEADS_PER_GROUP, HEAD_DIM)
        kv_s.append(jnp.concatenate([ksm, vsm], axis=3))

    state_shape_p = (1, 1, SSM_GROUPS, SSM_STATE)
    state_shape_s = (1, n_seq, SSM_GROUPS, SSM_STATE)
    return (y_p.reshape(1, n_prompt, D_MODEL), y_s.reshape(n_seq, n_step, D_MODEL),
            kv_p[0], kv_p[1], kv_p[2], fre_p.reshape(state_shape_p), fim_p.reshape(state_shape_p),
            kv_s[0], kv_s[1], kv_s[2], fre_s.reshape(state_shape_s), fim_s.reshape(state_shape_s))
```

```python
import functools
import math

import numpy as np
import jax
import jax.numpy as jnp
from jax import lax
from jax.experimental import pallas as pl
from jax.experimental.pallas import tpu as pltpu

F32 = jnp.float32
BF16 = jnp.bfloat16

D_MODEL = 2048
PAST_LEN = 2048
SSM_WIDTH = D_MODEL // 2
SSM_GROUP = 16
SSM_GROUPS = SSM_WIDTH // SSM_GROUP
SSM_STATE = 64
SSM_FLAT = SSM_GROUPS * SSM_STATE
HEAD_DIM = 128
DILATION_PATTERNS = ((128, 1), (512, 4), (2048, 16))
N_PATTERNS = 3
HEADS_PER_GROUP = 4
GROUP_WIDTH = HEADS_PER_GROUP * HEAD_DIM
ATTN_WIDTH = N_PATTERNS * GROUP_WIDTH
ROT_DIM = HEAD_DIM // 4
ROPE_THETA = 500000.0
OFF_Q = SSM_WIDTH
OFF_K = OFF_Q + ATTN_WIDTH
OFF_V = OFF_K + ATTN_WIDTH
OFF_G = OFF_V + ATTN_WIDTH
IN_COLS = OFF_G + 2 * D_MODEL
N_EXPERT_GROUPS = 4
EXPERTS_PER_GROUP = 4
N_EXPERTS = 16
EXPERT_FF = D_MODEL // 4
EPS = 1e-6
NEG = -1e30

LANES = 128
VMEM_LIMIT = 56 * 1024 * 1024

TOK_TILE = 512
SMALL_TOK_TILE = 256
PROJ_TM = 544
PROJ_TN = 2432
SCAN_L = 32
SCAN_NC = 16
SCAN_TT = SCAN_L * SCAN_NC
SSM_BLK = 8
SSM_BLK_STATE = SSM_BLK * SSM_STATE
MOE_TM = 256
ROUTE_E1, ROUTE_E2, ROUTE_W1, ROUTE_W2 = 0, 1, 2, 3
ATT_SB = 2048
BAND = 128
ATT_UNROLL = 8


def _cparams(sem, vmem=VMEM_LIMIT):
    return pltpu.CompilerParams(dimension_semantics=sem, vmem_limit_bytes=vmem)


def _sigmoid(x):
    return 1.0 / (1.0 + jnp.exp(-x))


def _gelu_tanh(x):
    c = math.sqrt(2.0 / math.pi)
    return 0.5 * x * (1.0 + jnp.tanh(c * (x + 0.044715 * (x * x * x))))


def _ada_kernel(c_ref, w_ref, b_ref, o_ref):
    c = c_ref[...]
    cs = (c * _sigmoid(c)).astype(BF16)
    o_ref[...] = jnp.dot(cs, w_ref[...].astype(BF16), preferred_element_type=F32) + b_ref[...]


def _ada(c_all, w_ada, b_ada):
    rows = c_all.shape[0]
    n_out = w_ada.shape[1]
    tn = 1024
    return pl.pallas_call(
        _ada_kernel,
        grid=(n_out // tn,),
        in_specs=[pl.BlockSpec((rows, D_MODEL), lambda n: (0, 0)),
                  pl.BlockSpec((D_MODEL, tn), lambda n: (0, n)),
                  pl.BlockSpec((1, tn), lambda n: (0, n))],
        out_specs=pl.BlockSpec((rows, tn), lambda n: (0, n)),
        out_shape=jax.ShapeDtypeStruct((rows, n_out), F32),
        compiler_params=_cparams(("arbitrary",)),
        name="ada_mod",
    )(c_all, w_ada, b_ada.reshape(1, n_out))


def _modnorm_kernel(xp_ref, xs_ref, g_ref, scp_ref, shp_ref, scs_ref, shs_ref, o_ref, *, n_prompt_tiles):
    is_s = pl.program_id(0) >= n_prompt_tiles
    x = jnp.where(is_s, xs_ref[...], xp_ref[...])
    sc = jnp.where(is_s, scs_ref[...], scp_ref[...])
    sh = jnp.where(is_s, shs_ref[...], shp_ref[...])
    ms = jnp.mean(x * x, axis=-1, keepdims=True)
    y = x * lax.rsqrt(ms + EPS) * g_ref[...]
    o_ref[...] = (y * (1.0 + sc) + sh).astype(o_ref.dtype)


def _modnorm(x_p, x_s, g, sc_p, sh_p, sc_s, sh_s):
    tp, ts = x_p.shape[0], x_s.shape[0]
    tm = TOK_TILE
    npt, nst = tp // tm, ts // tm
    row = lambda i: (jnp.minimum(i, npt - 1), 0)
    srow = lambda i: (jnp.maximum(i - npt, 0), 0)
    const = lambda i: (0, 0)
    return pl.pallas_call(
        functools.partial(_modnorm_kernel, n_prompt_tiles=npt),
        grid=(npt + nst,),
        in_specs=[pl.BlockSpec((tm, D_MODEL), row), pl.BlockSpec((tm, D_MODEL), srow),
                  pl.BlockSpec((1, D_MODEL), const), pl.BlockSpec((1, D_MODEL), const),
                  pl.BlockSpec((1, D_MODEL), const), pl.BlockSpec((tm, D_MODEL), srow),
                  pl.BlockSpec((tm, D_MODEL), srow)],
        out_specs=pl.BlockSpec((tm, D_MODEL), lambda i: (i, 0)),
        out_shape=jax.ShapeDtypeStruct((tp + ts, D_MODEL), BF16),
        compiler_params=_cparams(("arbitrary",)),
        name="modnorm1",
    )(x_p, x_s, g, sc_p, sh_p, sc_s, sh_s)


def _inproj_kernel(h_ref, w_ref, rc_ref, rs1_ref, rs2_ref, qg_ref, kg_ref, o_ref, wbf_ref, *, heads_per_tile):
    n = pl.program_id(0)

    @pl.when(pl.program_id(1) == 0)
    def _():
        wbf_ref[...] = w_ref[...].astype(BF16)

    half = ROT_DIM // 2
    q_heads = range(OFF_Q // HEAD_DIM, OFF_K // HEAD_DIM)
    k_heads = range(OFF_K // HEAD_DIM, OFF_V // HEAD_DIM)
    pair = 2 * HEAD_DIM

    def tile(col_tile):
        h = h_ref[...]
        for c0 in range(0, heads_per_tile * HEAD_DIM, pair):
            width = min(pair, heads_per_tile * HEAD_DIM - c0)
            acc = jnp.dot(h, wbf_ref[:, c0:c0 + width], preferred_element_type=F32)
            for c in range(c0, c0 + width, HEAD_DIM):
                slot = col_tile * heads_per_tile + c // HEAD_DIM
                x = acc[:, c - c0:c - c0 + HEAD_DIM]
                if slot in q_heads or slot in k_heads:
                    gain = qg_ref[...] if slot in q_heads else kg_ref[...]
                    ms = jnp.mean(x * x, axis=-1, keepdims=True)
                    y = x * lax.rsqrt(ms + EPS) * gain
                    up = pltpu.roll(y, HEAD_DIM - half, 1)
                    dn = pltpu.roll(y, half, 1)
                    x = y * rc_ref[...] + up * rs1_ref[...] + dn * rs2_ref[...]
                o_ref[:, c:c + HEAD_DIM] = x

    for col_tile in range(IN_COLS // (heads_per_tile * HEAD_DIM)):
        pl.when(n == col_tile)(functools.partial(tile, col_tile))


def _inproj(h, w_in, rc, rs1, rs2, qg, kg):
    n_tok = h.shape[0]
    tm, tn = PROJ_TM, PROJ_TN
    tab = pl.BlockSpec((tm, HEAD_DIM), lambda n, m: (m, 0))
    gain = pl.BlockSpec((1, HEAD_DIM), lambda n, m: (0, 0))
    return pl.pallas_call(
        functools.partial(_inproj_kernel, heads_per_tile=tn // HEAD_DIM),
        grid=(IN_COLS // tn, n_tok // tm),
        in_specs=[pl.BlockSpec((tm, D_MODEL), lambda n, m: (m, 0)),
                  pl.BlockSpec((D_MODEL, tn), lambda n, m: (0, n), pipeline_mode=pl.Buffered(1)),
                  tab, tab, tab, gain, gain],
        out_specs=pl.BlockSpec((tm, tn), lambda n, m: (m, n)),
        out_shape=jax.ShapeDtypeStruct((n_tok, IN_COLS), F32),
        scratch_shapes=[pltpu.VMEM((D_MODEL, tn), BF16)],
        compiler_params=_cparams(("arbitrary", "arbitrary")),
        name="in_proj",
    )(h, w_in, rc, rs1, rs2, qg, kg)


def _ssm_prep_kernel(are_ref, aim_ref, ldt_ref, arer_ref, aimr_ref, ldtr_ref, bre_ref, bim_ref,
                     pre_ref, pim_ref, bbre_ref, bbim_ref):
    def discretise(a_re, a_im, log_dt):
        dt = jnp.exp(log_dt)
        mag = jnp.exp(a_re * dt)
        return mag * jnp.cos(a_im * dt), mag * jnp.sin(a_im * dt)

    ab_re, ab_im = discretise(are_ref[...], aim_ref[...], ldt_ref[...])
    p_re, p_im = ab_re, ab_im
    for i in range(SCAN_L):
        pre_ref[i:i + 1, :] = p_re
        pim_ref[i:i + 1, :] = p_im
        p_re, p_im = p_re * ab_re - p_im * ab_im, p_re * ab_im + p_im * ab_re

    a_re, a_im = arer_ref[...], aimr_ref[...]
    r_re, r_im = discretise(a_re, a_im, ldtr_ref[...])
    nr, ni = r_re - 1.0, r_im
    den = a_re * a_re + a_im * a_im
    z_re = (nr * a_re + ni * a_im) / den
    z_im = (ni * a_re - nr * a_im) / den
    b_re, b_im = bre_ref[...], bim_ref[...]
    bbre_ref[...] = z_re * b_re - z_im * b_im
    bbim_ref[...] = z_re * b_im + z_im * b_re


def _ssm_prep(a_re, a_im, log_dt, b_re, b_im):
    g, p, n = b_re.shape
    flat = lambda x: x.reshape(1, g * p)
    rep = lambda x: jnp.repeat(x, n, axis=1)
    ldt_gp = jnp.broadcast_to(log_dt[:, None], (g, p))
    ldt_rep = jnp.broadcast_to(log_dt[:, None], (g, p * n))
    out_shape = [jax.ShapeDtypeStruct((SCAN_L, g * p), F32)] * 2 + [jax.ShapeDtypeStruct((g, p * n), F32)] * 2
    return pl.pallas_call(_ssm_prep_kernel, out_shape=out_shape, name="ssm_prep")(
        flat(a_re), flat(a_im), flat(ldt_gp), rep(a_re), rep(a_im), ldt_rep,
        b_re.reshape(g, p * n), b_im.reshape(g, p * n))


def _ssm_block_matrices(bb_re, bb_im, c_re, c_im):
    g, p, n = SSM_GROUPS, SSM_STATE, SSM_GROUP
    nb = g // SSM_BLK
    eye = jnp.eye(SSM_BLK, dtype=F32)

    def in_mat(bb):
        x = bb.reshape(nb, SSM_BLK, p, n)
        return jnp.einsum('bgpm,gh->bgmhp', x, eye).reshape(nb, SSM_BLK * n, SSM_BLK * p)

    def out_mat(c):
        x = c.reshape(nb, SSM_BLK, n, p)
        return jnp.einsum('bgnp,gh->bgphn', x, eye).reshape(nb, SSM_BLK * p, SSM_BLK * n)

    b_mat = jnp.concatenate([in_mat(bb_re), in_mat(bb_im)], axis=2).astype(BF16)
    c_mat = jnp.concatenate([out_mat(c_re), -out_mat(c_im)], axis=1).astype(BF16)
    return b_mat, c_mat


def _cmul_add(a_re, a_im, s_re, s_im, b_re, b_im):
    return a_re * s_re - a_im * s_im + b_re, a_re * s_im + a_im * s_re + b_im


def _s5_prompt_kernel(u_ref, d_ref, pre_ref, pim_ref, bm_ref, cm_ref, y_ref, fre_ref, fim_ref,
                      up_scr, bu_scr, lhs_scr, in_re_scr, in_im_scr, car_re, car_im, yn_scr):
    nc, ln, w = SCAN_NC, SCAN_L, SSM_BLK_STATE

    @pl.when(pl.program_id(1) == 0)
    def _():
        car_re[...] = jnp.zeros_like(car_re)
        car_im[...] = jnp.zeros_like(car_im)

    for i in range(ln):
        up_scr[i * nc:(i + 1) * nc, :] = u_ref[pl.ds(i, nc, stride=ln), :]
    up = up_scr[...]
    bu_scr[...] = jnp.dot(up.astype(BF16), bm_ref[0], preferred_element_type=F32)

    a_re = jnp.broadcast_to(pre_ref[0:1, :], (nc, w))
    a_im = jnp.broadcast_to(pim_ref[0:1, :], (nc, w))

    def local_step(i, carry):
        s_re, s_im = carry
        r0 = pl.multiple_of(i * nc, nc)
        s_re, s_im = _cmul_add(a_re, a_im, s_re, s_im, bu_scr[pl.ds(r0, nc), 0:w], bu_scr[pl.ds(r0, nc), w:2 * w])
        bu_scr[pl.ds(r0, nc), 0:w] = s_re
        bu_scr[pl.ds(r0, nc), w:2 * w] = s_im
        return s_re, s_im

    zero = jnp.zeros((nc, w), F32)
    f_re, f_im = lax.fori_loop(0, ln, local_step, (zero, zero))

    al_re, al_im = pre_ref[ln - 1:ln, :], pim_ref[ln - 1:ln, :]
    c_re, c_im = car_re[...], car_im[...]
    for c in range(nc):
        in_re_scr[c:c + 1, :] = c_re
        in_im_scr[c:c + 1, :] = c_im
        c_re, c_im = _cmul_add(al_re, al_im, c_re, c_im, f_re[c:c + 1, :], f_im[c:c + 1, :])
    car_re[...] = c_re
    car_im[...] = c_im
    fre_ref[...] = c_re
    fim_ref[...] = c_im
    in_re, in_im = in_re_scr[...], in_im_scr[...]

    def fix_step(i, _):
        r0 = pl.multiple_of(i * nc, nc)
        p_re = jnp.broadcast_to(pre_ref[pl.ds(i, 1), :], (nc, w))
        p_im = jnp.broadcast_to(pim_ref[pl.ds(i, 1), :], (nc, w))
        s_re, s_im = _cmul_add(p_re, p_im, in_re, in_im, bu_scr[pl.ds(r0, nc), 0:w], bu_scr[pl.ds(r0, nc), w:2 * w])
        lhs_scr[pl.ds(r0, nc), 0:w] = s_re.astype(BF16)
        lhs_scr[pl.ds(r0, nc), w:2 * w] = s_im.astype(BF16)
        return 0

    lax.fori_loop(0, ln, fix_step, 0)
    y = jnp.dot(lhs_scr[...], cm_ref[0], preferred_element_type=F32) + d_ref[...] * up
    for i in range(ln):
        yn_scr[pl.ds(i, nc, stride=ln), :] = y[i * nc:(i + 1) * nc, :]
    y_ref[...] = _gelu_tanh(yn_scr[...]).astype(y_ref.dtype)


def _s5_prompt(proj, n_prompt, d_skip, pw_re, pw_im, b_mat, c_mat):
    nb = SSM_GROUPS // SSM_BLK
    tt, w = SCAN_TT, SSM_BLK_STATE
    return pl.pallas_call(
        _s5_prompt_kernel,
        grid=(nb, n_prompt // tt),
        in_specs=[pl.BlockSpec((tt, LANES), lambda j, i: (i, j)),
                  pl.BlockSpec((1, LANES), lambda j, i: (0, j)),
                  pl.BlockSpec((SCAN_L, w), lambda j, i: (0, j)),
                  pl.BlockSpec((SCAN_L, w), lambda j, i: (0, j)),
                  pl.BlockSpec((1, LANES, 2 * w), lambda j, i: (j, 0, 0)),
                  pl.BlockSpec((1, 2 * w, LANES), lambda j, i: (j, 0, 0))],
        out_specs=[pl.BlockSpec((tt, LANES), lambda j, i: (i, j)),
                   pl.BlockSpec((1, w), lambda j, i: (0, j)),
                   pl.BlockSpec((1, w), lambda j, i: (0, j))],
        out_shape=[jax.ShapeDtypeStruct((n_prompt, SSM_WIDTH), BF16),
                   jax.ShapeDtypeStruct((1, SSM_FLAT), F32),
                   jax.ShapeDtypeStruct((1, SSM_FLAT), F32)],
        scratch_shapes=[pltpu.VMEM((tt, LANES), F32), pltpu.VMEM((tt, 2 * w), F32), pltpu.VMEM((tt, 2 * w), BF16),
                        pltpu.VMEM((SCAN_NC, w), F32), pltpu.VMEM((SCAN_NC, w), F32),
                        pltpu.VMEM((1, w), F32), pltpu.VMEM((1, w), F32), pltpu.VMEM((tt, LANES), F32)],
        compiler_params=_cparams(("arbitrary", "arbitrary")),
        name="s5_prompt",
    )(proj, d_skip, pw_re, pw_im, b_mat, c_mat)


def _s5_sample_kernel(u_ref, d_ref, pre_ref, pim_ref, bm_ref, cm_ref, s0re_ref, s0im_ref,
                      y_ref, fre_ref, fim_ref, up_scr, bu_scr, lhs_scr, yn_scr, *, n_seq, n_step):
    w = SSM_BLK_STATE
    rb = 16
    for s in range(n_step):
        up_scr[s * n_seq:(s + 1) * n_seq, :] = u_ref[pl.ds(s, n_seq, stride=n_step), :]
    up = up_scr[...]
    bu_scr[...] = jnp.dot(up.astype(BF16), bm_ref[0], preferred_element_type=F32)
    a_re = jnp.broadcast_to(pre_ref[0:1, :], (rb, w))
    a_im = jnp.broadcast_to(pim_ref[0:1, :], (rb, w))

    def seq_block(b, _):
        r0 = pl.multiple_of(b * rb, rb)
        s_re, s_im = s0re_ref[pl.ds(r0, rb), :], s0im_ref[pl.ds(r0, rb), :]
        for s in range(n_step):
            rows = pl.ds(pl.multiple_of(s * n_seq + r0, rb), rb)
            s_re, s_im = _cmul_add(a_re, a_im, s_re, s_im, bu_scr[rows, 0:w], bu_scr[rows, w:2 * w])
            lhs_scr[rows, 0:w] = s_re.astype(BF16)
            lhs_scr[rows, w:2 * w] = s_im.astype(BF16)
        fre_ref[pl.ds(r0, rb), :] = s_re
        fim_ref[pl.ds(r0, rb), :] = s_im
        return 0

    lax.fori_loop(0, n_seq // rb, seq_block, 0)
    y = jnp.dot(lhs_scr[...], cm_ref[0], preferred_element_type=F32) + d_ref[...] * up
    for s in range(n_step):
        yn_scr[pl.ds(s, n_seq, stride=n_step), :] = y[s * n_seq:(s + 1) * n_seq, :]
    y_ref[...] = _gelu_tanh(yn_scr[...]).astype(y_ref.dtype)


def _s5_sample(proj, n_prompt, n_seq, n_step, d_skip, pw_re, pw_im, b_mat, c_mat, s0_re, s0_im):
    nb = SSM_GROUPS // SSM_BLK
    rows, w = n_seq * n_step, SSM_BLK_STATE
    rblk = n_prompt // rows
    return pl.pallas_call(
        functools.partial(_s5_sample_kernel, n_seq=n_seq, n_step=n_step),
        grid=(nb,),
        in_specs=[pl.BlockSpec((rows, LANES), lambda j: (rblk, j)),
                  pl.BlockSpec((1, LANES), lambda j: (0, j)),
                  pl.BlockSpec((SCAN_L, w), lambda j: (0, j)),
                  pl.BlockSpec((SCAN_L, w), lambda j: (0, j)),
                  pl.BlockSpec((1, LANES, 2 * w), lambda j: (j, 0, 0)),
                  pl.BlockSpec((1, 2 * w, LANES), lambda j: (j, 0, 0)),
                  pl.BlockSpec((n_seq, w), lambda j: (0, j)),
                  pl.BlockSpec((n_seq, w), lambda j: (0, j))],
        out_specs=[pl.BlockSpec((rows, LANES), lambda j: (0, j)),
                   pl.BlockSpec((n_seq, w), lambda j: (0, j)),
                   pl.BlockSpec((n_seq, w), lambda j: (0, j))],
        out_shape=[jax.ShapeDtypeStruct((rows, SSM_WIDTH), BF16),
                   jax.ShapeDtypeStruct((n_seq, SSM_FLAT), F32),
                   jax.ShapeDtypeStruct((n_seq, SSM_FLAT), F32)],
        scratch_shapes=[pltpu.VMEM((rows, LANES), F32), pltpu.VMEM((rows, 2 * w), F32),
                        pltpu.VMEM((rows, 2 * w), BF16), pltpu.VMEM((rows, LANES), F32)],
        compiler_params=_cparams(("arbitrary",)),
        name="s5_sample",
    )(proj, d_skip, pw_re, pw_im, b_mat, c_mat, s0_re, s0_im)


def _attn_prompt_kernel(*refs):
    ins, o_ref, scr = refs[:15], refs[15], refs[16:]
    sb = pl.program_id(0)
    scale = HEAD_DIM ** -0.5
    qi = lax.broadcasted_iota(jnp.int32, (BAND, 2 * BAND), 0)
    kj = lax.broadcasted_iota(jnp.int32, (BAND, 2 * BAND), 1)
    dist = qi + BAND - kj
    band_ok = (dist >= 0) & (dist <= BAND)

    for g, (_, dil) in enumerate(DILATION_PATTERNS):
        q_ref, k_ref, v_ref, kp_ref, vp_ref = ins[5 * g:5 * g + 5]
        kbuf, vbuf, o_scr, m_scr, l_scr = scr[5 * g:5 * g + 5]
        pre = BAND * dil
        kbuf[0:pre, :] = kp_ref[...]
        kbuf[pre:pre + ATT_SB, :] = k_ref[...]
        vbuf[0:pre, :] = vp_ref[...]
        vbuf[pre:pre + ATT_SB, :] = v_ref[...]
        nblk = ATT_SB // pre

        def block(idx, _, dil=dil, pre=pre, nblk=nblk, q_ref=q_ref, kbuf=kbuf, vbuf=vbuf,
                  o_scr=o_scr, m_scr=m_scr, l_scr=l_scr):
            r = idx // nblk
            b = idx - r * nblk
            row0 = r + b * pre
            if dil == 1:
                q_rows = pl.ds(pl.multiple_of(row0, BAND), BAND)
                kv_rows = pl.ds(pl.multiple_of(row0, BAND), 2 * BAND)
            else:
                q_rows = pl.ds(row0, BAND, stride=dil)
                kv_rows = pl.ds(row0, 2 * BAND, stride=dil)
            q = (q_ref[q_rows, :] * scale).astype(BF16)
            kw = kbuf[kv_rows, :].astype(BF16)
            vw = vbuf[kv_rows, :].astype(BF16)
            s = lax.dot_general(q, kw, (((1,), (1,)), ((), ())), preferred_element_type=F32)
            s = jnp.where(band_ok & ((kj >= BAND) | (sb > 0) | (b > 0)), s, NEG)
            m = jnp.max(s, axis=-1, keepdims=True)
            p = jnp.exp(s - m)
            l = jnp.sum(p, axis=-1, keepdims=True)
            o = jnp.dot(p.astype(BF16), vw, preferred_element_type=F32)
            o_scr[q_rows, :] = o
            m_scr[q_rows, :] = jnp.broadcast_to(m, (BAND, HEAD_DIM))
            l_scr[q_rows, :] = jnp.broadcast_to(l, (BAND, HEAD_DIM))
            return 0

        lax.fori_loop(0, ATT_SB // BAND, block, 0, unroll=ATT_UNROLL)

    ms = [scr[5 * g + 3][...] for g in range(N_PATTERNS)]
    mx = jnp.maximum(jnp.maximum(ms[0], ms[1]), ms[2])
    num = jnp.zeros((ATT_SB, HEAD_DIM), F32)
    den = jnp.zeros((ATT_SB, HEAD_DIM), F32)
    for g in range(N_PATTERNS):
        wgt = jnp.exp(ms[g] - mx)
        num = num + wgt * scr[5 * g + 2][...]
        den = den + wgt * scr[5 * g + 4][...]
    o_ref[...] = num / den


def _attn_prompt(proj, n_prompt):
    hcol = lambda off, g, j: (off + g * GROUP_WIDTH) // HEAD_DIM + j
    in_specs, scratch = [], []
    for g, (_, dil) in enumerate(DILATION_PATTERNS):
        pre = BAND * dil
        per = ATT_SB // pre
        cur = lambda off, g=g: pl.BlockSpec((ATT_SB, HEAD_DIM), lambda sb, j: (sb, hcol(off, g, j)))
        prev = lambda off, g=g, per=per, pre=pre: pl.BlockSpec(
            (pre, HEAD_DIM), lambda sb, j: (jnp.maximum(sb * per - 1, 0), hcol(off, g, j)))
        in_specs += [cur(OFF_Q), cur(OFF_K), cur(OFF_V), prev(OFF_K), prev(OFF_V)]
        scratch += [pltpu.VMEM((pre + ATT_SB, HEAD_DIM), F32), pltpu.VMEM((pre + ATT_SB, HEAD_DIM), F32),
                    pltpu.VMEM((ATT_SB, HEAD_DIM), F32), pltpu.VMEM((ATT_SB, HEAD_DIM), F32),
                    pltpu.VMEM((ATT_SB, HEAD_DIM), F32)]
    return pl.pallas_call(
        _attn_prompt_kernel,
        grid=(n_prompt // ATT_SB, HEADS_PER_GROUP),
        in_specs=in_specs,
        out_specs=pl.BlockSpec((ATT_SB, HEAD_DIM), lambda sb, j: (sb, j)),
        out_shape=jax.ShapeDtypeStruct((n_prompt, GROUP_WIDTH), F32),
        scratch_shapes=scratch,
        compiler_params=_cparams(("arbitrary", "arbitrary")),
        name="attn_prompt",
    )(*([proj] * 15))


SEQ_PER_STEP = 2
KV_PLANES = 2 * HEADS_PER_GROUP


def _compact_pitch(n_step):
    tiles = n_step * KV_PLANES // 8
    return 8 * (tiles + 1 - tiles % 2)


def _sample_bias(n_step):
    rows = HEADS_PER_GROUP * n_step
    step = np.arange(rows) % n_step
    cache_bias, new_bias = [], []
    for (window, dil) in DILATION_PATTERNS:
        wb = min(window, PAST_LEN)
        band = window // dil
        if dil > n_step:
            res, i = np.meshgrid(np.arange(n_step), np.arange(wb // dil), indexing='ij')
            c = (i * dil + res).reshape(-1)
        else:
            c = np.arange(wb)
        delta = wb + step[:, None] - c[None, :]
        ok = (delta >= 0) & (delta % dil == 0) & (delta // dil <= band)
        cache_bias.append(np.where(ok, 0.0, NEG).astype(np.float32))
        nb = np.full((SEQ_PER_STEP, rows, LANES), NEG, np.float32)
        for a in range(SEQ_PER_STEP):
            for sp in range(n_step):
                dl = step - sp
                okn = (dl >= 0) & (dl % dil == 0) & (dl // dil <= band)
                nb[a, :, a * n_step + sp] = np.where(okn, 0.0, NEG)
        new_bias.append(nb)
    return cache_bias, new_bias


def _attn_sample_kernel(*refs, n_step):
    (q0, k0, v0, q1, k1, v1, q2, k2, v2, c0, c1, c2, cb0, cb1, cb2, nb0, nb1, nb2, o_ref) = refs
    qs, ks, vs = (q0, q1, q2), (k0, k1, k2), (v0, v1, v2)
    caches, cbias, nbias = (c0, c1, c2), (cb0, cb1, cb2), (nb0, nb1, nb2)
    rows = HEADS_PER_GROUP * n_step
    gw = GROUP_WIDTH
    scale = HEAD_DIM ** -0.5
    row_head = lax.broadcasted_iota(jnp.int32, (rows, gw), 0) // n_step
    lane_head = lax.broadcasted_iota(jnp.int32, (rows, gw), 1) // HEAD_DIM
    own_head = row_head == lane_head
    nt = (((1,), (1,)), ((), ()))
    pad = jnp.zeros((LANES - SEQ_PER_STEP * n_step, gw), F32)

    def planes(load):
        k = jnp.concatenate([load(h) for h in range(HEADS_PER_GROUP)], axis=1)
        v = jnp.concatenate([load(HEADS_PER_GROUP + h) for h in range(HEADS_PER_GROUP)], axis=1)
        return k.astype(BF16), v.astype(BF16)

    for a in range(SEQ_PER_STEP):
        pieces = []
        for g, (window, dil) in enumerate(DILATION_PATTERNS):
            q = qs[g][a * n_step:(a + 1) * n_step, :] * scale
            qbd = jnp.where(own_head, jnp.concatenate([q] * HEADS_PER_GROUP, axis=0), 0.0).astype(BF16)
            cache = caches[g]
            per_seq = cache.shape[0] // SEQ_PER_STEP
            if dil > n_step:
                kvs = [planes(lambda p, r=r: cache[pl.ds(a * per_seq, per_seq), r * KV_PLANES + p, :])
                       for r in range(n_step)]
            else:
                kvs = [planes(lambda p: cache[pl.ds(a * per_seq + p, per_seq // KV_PLANES, stride=KV_PLANES), :])]
            sc = jnp.concatenate([lax.dot_general(qbd, k, nt, preferred_element_type=F32) for k, _ in kvs], axis=1)
            pieces.append((sc + cbias[g][...], [v for _, v in kvs]))
            k_new = jnp.concatenate([ks[g][...], pad], axis=0).astype(BF16)
            v_new = jnp.concatenate([vs[g][...], pad], axis=0).astype(BF16)
            sn = lax.dot_general(qbd, k_new, nt, preferred_element_type=F32) + nbias[g][a]
            pieces.append((sn, [v_new]))
        m = functools.reduce(jnp.maximum, [jnp.max(s, axis=-1, keepdims=True) for s, _ in pieces])
        l = jnp.zeros((rows, 1), F32)
        acc = jnp.zeros((rows, gw), F32)
        for s, vals in pieces:
            p = jnp.exp(s - m)
            l = l + jnp.sum(p, axis=-1, keepdims=True)
            pb = p.astype(BF16)
            nk = pb.shape[1] // len(vals)
            for r, v in enumerate(vals):
                acc = acc + jnp.dot(pb[:, r * nk:(r + 1) * nk], v, preferred_element_type=F32)
        acc = jnp.where(own_head, acc, 0.0)
        o16 = functools.reduce(lambda x, y: x + y,
                               [acc[:, h * HEAD_DIM:(h + 1) * HEAD_DIM] for h in range(HEADS_PER_GROUP)]) / l
        for h in range(HEADS_PER_GROUP):
            o_ref[a * n_step:(a + 1) * n_step, h * HEAD_DIM:(h + 1) * HEAD_DIM] = o16[h * n_step:(h + 1) * n_step, :]


def _attn_sample(proj, n_prompt, n_seq, n_step, caches):
    rows = SEQ_PER_STEP * n_step
    rblk = n_prompt // rows
    cache_bias, new_bias = _sample_bias(n_step)
    tok = lambda off, g: pl.BlockSpec((rows, GROUP_WIDTH), lambda i: (rblk + i, (off + g * GROUP_WIDTH) // GROUP_WIDTH))
    in_specs, args = [], []
    for g in range(N_PATTERNS):
        in_specs += [tok(OFF_Q, g), tok(OFF_K, g), tok(OFF_V, g)]
        args += [proj, proj, proj]
    for g, (window, dil) in enumerate(DILATION_PATTERNS):
        c = caches[g]
        wb = c.shape[1]
        if dil > n_step:
            pitch = _compact_pitch(n_step)
            assert pitch <= dil * KV_PLANES and wb % dil == 0
            c = c.reshape(n_seq * (wb // dil), dil * KV_PLANES, HEAD_DIM)
            in_specs.append(pl.BlockSpec((SEQ_PER_STEP * (wb // dil), pitch, HEAD_DIM), lambda i: (i, 0, 0)))
        else:
            c = c.reshape(n_seq * wb * KV_PLANES, HEAD_DIM)
            in_specs.append(pl.BlockSpec((SEQ_PER_STEP * wb * KV_PLANES, HEAD_DIM), lambda i: (i, 0)))
        args.append(c)
    for b in cache_bias:
        in_specs.append(pl.BlockSpec(b.shape, lambda i: (0, 0)))
        args.append(jnp.asarray(b))
    for b in new_bias:
        in_specs.append(pl.BlockSpec(b.shape, lambda i: (0, 0, 0)))
        args.append(jnp.asarray(b))
    return pl.pallas_call(
        functools.partial(_attn_sample_kernel, n_step=n_step),
        grid=(n_seq // SEQ_PER_STEP,),
        in_specs=in_specs,
        out_specs=pl.BlockSpec((rows, GROUP_WIDTH), lambda i: (i, 0)),
        out_shape=jax.ShapeDtypeStruct((n_seq * n_step, GROUP_WIDTH), F32),
        compiler_params=_cparams(("arbitrary",)),
        name="attn_sample",
    )(*args)


def _mix_kernel(yp_ref, ys_ref, op_ref, os_ref, wa_ref, wb_ref, wbr_ref, ga_ref, gb_ref, o_ref,
                wa_bf, wb_bf, wbr_bf, *, n_prompt_tiles):
    i = pl.program_id(1)

    @pl.when(i == 0)
    def _():
        wa_bf[...] = wa_ref[...].astype(BF16)
        wb_bf[...] = wb_ref[...].astype(BF16)
        wbr_bf[...] = wbr_ref[...].astype(BF16)

    is_s = i >= n_prompt_tiles
    y = jnp.where(is_s, ys_ref[...], yp_ref[...])
    o = jnp.where(is_s, os_ref[...], op_ref[...]).astype(BF16)
    glu_a = jnp.dot(y, wa_bf[...], preferred_element_type=F32)
    glu_b = jnp.dot(y, wb_bf[...], preferred_element_type=F32)
    branch_a = glu_a * _sigmoid(glu_b)
    branch_b = jnp.dot(o, wbr_bf[...], preferred_element_type=F32)
    o_ref[...] = (_sigmoid(ga_ref[...]) * branch_a + _sigmoid(gb_ref[...]) * branch_b).astype(o_ref.dtype)


def _mix(y_p, y_s, o_p, o_s, w_glu, w_attn_br, proj):
    tp, ts = y_p.shape[0], y_s.shape[0]
    tm, tn = TOK_TILE, 512
    npt, nst = tp // tm, ts // tm
    ncol = D_MODEL // tn
    prow = lambda n, i: (jnp.minimum(i, npt - 1), 0)
    srow = lambda n, i: (jnp.maximum(i - npt, 0), 0)
    return pl.pallas_call(
        functools.partial(_mix_kernel, n_prompt_tiles=npt),
        grid=(ncol, npt + nst),
        in_specs=[pl.BlockSpec((tm, SSM_WIDTH), prow), pl.BlockSpec((tm, SSM_WIDTH), srow),
                  pl.BlockSpec((tm, GROUP_WIDTH), prow), pl.BlockSpec((tm, GROUP_WIDTH), srow),
                  pl.BlockSpec((SSM_WIDTH, tn), lambda n, i: (0, n)),
                  pl.BlockSpec((SSM_WIDTH, tn), lambda n, i: (0, ncol + n)),
                  pl.BlockSpec((GROUP_WIDTH, tn), lambda n, i: (0, n)),
                  pl.BlockSpec((tm, tn), lambda n, i: (i, OFF_G // tn + n)),
                  pl.BlockSpec((tm, tn), lambda n, i: (i, OFF_G // tn + ncol + n))],
        out_specs=pl.BlockSpec((tm, tn), lambda n, i: (i, n)),
        out_shape=jax.ShapeDtypeStruct((tp + ts, D_MODEL), BF16),
        scratch_shapes=[pltpu.VMEM((SSM_WIDTH, tn), BF16), pltpu.VMEM((SSM_WIDTH, tn), BF16),
                        pltpu.VMEM((GROUP_WIDTH, tn), BF16)],
        compiler_params=_cparams(("arbitrary", "arbitrary")),
        name="glu_mix",
    )(y_p, y_s, o_p, o_s, w_glu, w_glu, w_attn_br, proj, proj)


def _route(logits):
    lane = lax.broadcasted_iota(jnp.int32, logits.shape, 1).astype(F32)
    big = 1000.0
    first = lambda cond: jnp.min(jnp.where(cond, lane, big), axis=-1, keepdims=True)
    is_g = lane < N_EXPERT_GROUPS
    lg = jnp.where(is_g, logits, NEG)
    mg = jnp.max(lg, axis=-1, keepdims=True)
    g_sel = first(lg == mg)
    p_group = 1.0 / jnp.sum(jnp.where(is_g, jnp.exp(lg - mg), 0.0), axis=-1, keepdims=True)
    e_lo = N_EXPERT_GROUPS + EXPERTS_PER_GROUP * g_sel
    le = jnp.where((lane >= e_lo) & (lane < e_lo + EXPERTS_PER_GROUP), logits, NEG)
    v1 = jnp.max(le, axis=-1, keepdims=True)
    i1 = first(le == v1)
    le2 = jnp.where(lane == i1, NEG, le)
    v2 = jnp.max(le2, axis=-1, keepdims=True)
    i2 = first(le2 == v2)
    e2 = jnp.exp(v2 - v1)
    w1 = p_group / (1.0 + e2)
    w2 = p_group * e2 / (1.0 + e2)
    pick = lambda k, val: jnp.where(lane == k, val, 0.0)
    return (pick(ROUTE_E1, i1 - N_EXPERT_GROUPS) + pick(ROUTE_E2, i2 - N_EXPERT_GROUPS)
            + pick(ROUTE_W1, w1) + pick(ROUTE_W2, w2))


def _outproj_kernel(mix_ref, w_ref, xp_ref, xs_ref, g_ref, wr_ref, br_ref,
                    gtp_ref, scp_ref, shp_ref, gts_ref, scs_ref, shs_ref,
                    x1_ref, h2_ref, route_ref, *, n_prompt_tiles):
    is_s = pl.program_id(0) >= n_prompt_tiles
    x = jnp.where(is_s, xs_ref[...], xp_ref[...])
    gt = jnp.where(is_s, gts_ref[...], gtp_ref[...])
    sc = jnp.where(is_s, scs_ref[...], scp_ref[...])
    sh = jnp.where(is_s, shs_ref[...], shp_ref[...])
    x1 = x + gt * jnp.dot(mix_ref[...], w_ref[...], preferred_element_type=F32)
    x1_ref[...] = x1
    ms = jnp.mean(x1 * x1, axis=-1, keepdims=True)
    h2 = (x1 * lax.rsqrt(ms + EPS) * g_ref[...]) * (1.0 + sc) + sh
    h2_ref[...] = h2.astype(h2_ref.dtype)
    def split(v):
        high = v.astype(BF16)
        return high, (v - high.astype(F32)).astype(BF16)

    r = jnp.dot(jnp.concatenate(split(h2), axis=0), jnp.concatenate(split(wr_ref[...]), axis=1),
                preferred_element_type=F32)
    tm = h2.shape[0]
    logits = (r[:tm, :LANES] + r[:tm, LANES:]) + (r[tm:, :LANES] + r[tm:, LANES:]) + br_ref[...]
    route_ref[...] = _route(logits)


def _outproj(mixed, w_out_bf, x_p, x_s, g2, w_router, b_router, gt_p, sc_p, sh_p, gt_s, sc_s, sh_s):
    tp, ts = x_p.shape[0], x_s.shape[0]
    tm = SMALL_TOK_TILE
    npt, nst = tp // tm, ts // tm
    prow = lambda i: (jnp.minimum(i, npt - 1), 0)
    srow = lambda i: (jnp.maximum(i - npt, 0), 0)
    const = lambda i: (0, 0)
    vec = pl.BlockSpec((1, D_MODEL), const)
    svec = pl.BlockSpec((tm, D_MODEL), srow)
    full = lambda i: (i, 0)
    return pl.pallas_call(
        functools.partial(_outproj_kernel, n_prompt_tiles=npt),
        grid=(npt + nst,),
        in_specs=[pl.BlockSpec((tm, D_MODEL), full), pl.BlockSpec((D_MODEL, D_MODEL), const),
                  pl.BlockSpec((tm, D_MODEL), prow), pl.BlockSpec((tm, D_MODEL), srow),
                  vec, pl.BlockSpec((D_MODEL, LANES), const), pl.BlockSpec((1, LANES), const),
                  vec, vec, vec, svec, svec, svec],
        out_specs=[pl.BlockSpec((tm, D_MODEL), full), pl.BlockSpec((tm, D_MODEL), full),
                   pl.BlockSpec((tm, LANES), full)],
        out_shape=[jax.ShapeDtypeStruct((tp + ts, D_MODEL), F32),
                   jax.ShapeDtypeStruct((tp + ts, D_MODEL), F32),
                   jax.ShapeDtypeStruct((tp + ts, LANES), F32)],
        compiler_params=_cparams(("arbitrary",)),
        name="out_proj_norm2_router",
    )(mixed, w_out_bf, x_p, x_s, g2, w_router, b_router, gt_p, sc_p, sh_p, gt_s, sc_s, sh_s)


def _dispatch_plan(route, tm):
    e = route[:, ROUTE_E1:ROUTE_E2 + 1].astype(jnp.int32).reshape(-1)
    n_pairs = e.shape[0]
    onehot = (e[:, None] == jnp.arange(N_EXPERTS, dtype=jnp.int32)[None, :]).astype(jnp.int32)
    csum = jnp.cumsum(onehot, axis=0)
    rank = jnp.sum(onehot * csum, axis=1) - 1
    tiles_per_expert = (csum[-1] + tm - 1) // tm
    tile_end = jnp.cumsum(tiles_per_expert)
    tile_start = tile_end - tiles_per_expert
    dest = (tile_start[e] * tm + rank).astype(jnp.int32)
    max_tiles = n_pairs // tm + N_EXPERTS
    k = jnp.arange(max_tiles, dtype=jnp.int32)
    tile_expert = jnp.minimum(jnp.sum((k[:, None] >= tile_end[None, :]).astype(jnp.int32), axis=1), N_EXPERTS - 1)
    n_used = tile_end[-1].astype(jnp.int32)
    last_expert = jnp.take(tile_expert, n_used - 1)
    tile_expert = jnp.where(k < n_used, tile_expert, last_expert).astype(jnp.int32)
    return dest, tile_expert, n_used.reshape(1), max_tiles


def _pair_copies(dest_ref, tile, rows, make, make_all):
    def body(r, _):
        for k in range(2):
            make(r, k, dest_ref[(tile * rows + r) * 2 + k]).start()
        return 0
    lax.fori_loop(0, rows, body, 0, unroll=8)
    for k in range(2):
        make_all(k).wait()


def _dispatch_kernel(dest_ref, h_ref, init_ref, xs_ref, sem):
    del init_ref
    rows = h_ref.shape[0]
    make = lambda r, k, d: pltpu.make_async_copy(h_ref.at[pl.ds(r, 1)], xs_ref.at[pl.ds(d, 1)], sem)
    make_all = lambda k: pltpu.make_async_copy(h_ref, xs_ref.at[pl.ds(0, rows)], sem)
    _pair_copies(dest_ref, pl.program_id(0), rows, make, make_all)


def _dispatch(dest, h2, n_slots):
    n_tok = h2.shape[0]
    tm = SMALL_TOK_TILE
    return pl.pallas_call(
        _dispatch_kernel,
        grid_spec=pltpu.PrefetchScalarGridSpec(
            num_scalar_prefetch=1, grid=(n_tok // tm,),
            in_specs=[pl.BlockSpec((tm, D_MODEL), lambda i, d: (i, 0)), pl.BlockSpec(memory_space=pl.ANY)],
            out_specs=pl.BlockSpec(memory_space=pl.ANY),
            scratch_shapes=[pltpu.SemaphoreType.DMA(())]),
        out_shape=jax.ShapeDtypeStruct((n_slots, D_MODEL), F32),
        input_output_aliases={2: 0},
        compiler_params=_cparams(("arbitrary",)),
        name="moe_dispatch",
    )(dest, h2, jnp.zeros((n_slots, D_MODEL), F32))


def _experts_kernel(te_ref, used_ref, xs_ref, wgu_ref, wd_ref, y_ref, wgu_bf, wd_bf):
    k = pl.program_id(0)
    new_expert = (k == 0) | (te_ref[k] != te_ref[jnp.maximum(k - 1, 0)])

    @pl.when(new_expert)
    def _():
        wgu_bf[...] = wgu_ref[0].astype(BF16)
        wd_bf[...] = wd_ref[0].astype(BF16)

    @pl.when(k < used_ref[0])
    def _():
        gu = jnp.dot(xs_ref[...].astype(BF16), wgu_bf[...], preferred_element_type=F32)
        gate, up = gu[:, :EXPERT_FF], gu[:, EXPERT_FF:]
        act = (gate * _sigmoid(gate)) * up
        y_ref[...] = jnp.dot(act.astype(BF16), wd_bf[...], preferred_element_type=F32)

    @pl.when(k >= used_ref[0])
    def _():
        y_ref[...] = jnp.zeros_like(y_ref)


def _experts(tile_expert, n_used, xs, w_gu, w_down, max_tiles):
    tm = MOE_TM
    return pl.pallas_call(
        _experts_kernel,
        grid_spec=pltpu.PrefetchScalarGridSpec(
            num_scalar_prefetch=2, grid=(max_tiles,),
            in_specs=[pl.BlockSpec((tm, D_MODEL), lambda k, te, nu: (k, 0)),
                      pl.BlockSpec((1, D_MODEL, 2 * EXPERT_FF), lambda k, te, nu: (te[k], 0, 0)),
                      pl.BlockSpec((1, EXPERT_FF, D_MODEL), lambda k, te, nu: (te[k], 0, 0))],
            out_specs=pl.BlockSpec((tm, D_MODEL), lambda k, te, nu: (k, 0)),
            scratch_shapes=[pltpu.VMEM((D_MODEL, 2 * EXPERT_FF), BF16), pltpu.VMEM((EXPERT_FF, D_MODEL), BF16)]),
        out_shape=jax.ShapeDtypeStruct((max_tiles * tm, D_MODEL), F32),
        compiler_params=_cparams(("arbitrary",)),
        name="moe_experts",
    )(tile_expert, n_used, xs, w_gu, w_down)


def _combine_kernel(dest_ref, y_hbm, x1_ref, route_ref, gtp_ref, gts_ref, yp_ref, ys_ref, buf, sem, *, n_prompt_tiles):
    i = pl.program_id(0)
    make = lambda r, k, d: pltpu.make_async_copy(y_hbm.at[pl.ds(d, 1)], buf.at[k, pl.ds(r, 1)], sem)
    rows = x1_ref.shape[0]
    make_all = lambda k: pltpu.make_async_copy(y_hbm.at[pl.ds(0, rows)], buf.at[k], sem)
    _pair_copies(dest_ref, i, rows, make, make_all)
    route = route_ref[...]
    lane = lax.broadcasted_iota(jnp.int32, route.shape, 1)
    w1 = jnp.sum(jnp.where(lane == ROUTE_W1, route, 0.0), axis=-1, keepdims=True)
    w2 = jnp.sum(jnp.where(lane == ROUTE_W2, route, 0.0), axis=-1, keepdims=True)
    moe = w1 * buf[0] + w2 * buf[1]

    @pl.when(i < n_prompt_tiles)
    def _():
        yp_ref[...] = x1_ref[...] + gtp_ref[...] * moe

    @pl.when(i >= n_prompt_tiles)
    def _():
        ys_ref[...] = x1_ref[...] + gts_ref[...] * moe


def _combine(dest, y_slots, x1, route, gt_p, gt_s, tp, ts):
    tm = SMALL_TOK_TILE
    npt, nst = tp // tm, ts // tm
    row = lambda i, d: (i, 0)
    return pl.pallas_call(
        functools.partial(_combine_kernel, n_prompt_tiles=npt),
        grid_spec=pltpu.PrefetchScalarGridSpec(
            num_scalar_prefetch=1, grid=(npt + nst,),
            in_specs=[pl.BlockSpec(memory_space=pl.ANY),
                      pl.BlockSpec((tm, D_MODEL), row), pl.BlockSpec((tm, LANES), row),
                      pl.BlockSpec((1, D_MODEL), lambda i, d: (0, 0)),
                      pl.BlockSpec((tm, D_MODEL), lambda i, d: (jnp.maximum(i - npt, 0), 0))],
            out_specs=[pl.BlockSpec((tm, D_MODEL), lambda i, d: (jnp.minimum(i, npt - 1), 0)),
                       pl.BlockSpec((tm, D_MODEL), lambda i, d: (jnp.maximum(i - npt, 0), 0))],
            scratch_shapes=[pltpu.VMEM((2, tm, D_MODEL), F32), pltpu.SemaphoreType.DMA(())]),
        out_shape=[jax.ShapeDtypeStruct((tp, D_MODEL), F32), jax.ShapeDtypeStruct((ts, D_MODEL), F32)],
        compiler_params=_cparams(("arbitrary",)),
        name="moe_combine",
    )(dest, y_slots, x1, route, gt_p, gt_s)


def _moe(h2, route, x1, w_gu, w_down, gt_p, gt_s, tp, ts):
    dest, tile_expert, n_used, max_tiles = _dispatch_plan(route, MOE_TM)
    xs = _dispatch(dest, h2, max_tiles * MOE_TM)
    y_slots = _experts(tile_expert, n_used, xs, w_gu, w_down, max_tiles)
    return _combine(dest, y_slots, x1, route, gt_p, gt_s, tp, ts)


def _rope_tables(n_prompt, n_seq, n_step):
    half = ROT_DIM // 2
    inv_freq = ROPE_THETA ** (-jnp.arange(half, dtype=F32) / half)
    pos = jnp.concatenate([jnp.arange(n_prompt, dtype=jnp.int32),
                           jnp.tile(PAST_LEN + jnp.arange(n_step, dtype=jnp.int32), n_seq)])
    ang = pos.astype(F32)[:, None] * inv_freq[None, :]
    cos, sin = jnp.cos(ang), jnp.sin(ang)
    n = pos.shape[0]
    one = jnp.ones((n, HEAD_DIM - ROT_DIM), F32)
    zero = jnp.zeros((n, HEAD_DIM - ROT_DIM), F32)
    zh = jnp.zeros((n, half), F32)
    rc = jnp.concatenate([cos, cos, one], axis=1)
    rs1 = jnp.concatenate([-sin, zh, zero], axis=1)
    rs2 = jnp.concatenate([zh, sin, zero], axis=1)
    return rc, rs1, rs2


def kernel(x_prompt, x_sample, cache_kv_w128, cache_kv_w512, cache_kv_w2048, state_ssm_re, state_ssm_im,
           c_prompt, c_sample, w_ada, b_ada, norm1_g, norm2_g, w_in, ssm_a_re, ssm_a_im, ssm_log_dt,
           ssm_b_re, ssm_b_im, ssm_c_re, ssm_c_im, ssm_d, w_glu, q_norm_g, k_norm_g, w_attn_br, w_out,
           w_router_group, b_router_group, w_router_expert, b_router_expert, w_expert_gate_up, w_expert_down):
    assert x_prompt.shape[0] == 1 and w_ada.shape[0] == 1
    n_prompt = x_prompt.shape[1]
    n_seq, n_step = x_sample.shape[0], x_sample.shape[1]
    n_samp = n_seq * n_step
    assert n_samp % TOK_TILE == 0 and n_prompt % ATT_SB == 0 and (n_prompt + n_samp) % PROJ_TM == 0
    x_p = x_prompt.reshape(n_prompt, D_MODEL)
    x_s = x_sample.reshape(n_samp, D_MODEL)

    pad = (-(n_seq + 1)) % 8
    c_all = jnp.concatenate([c_sample, c_prompt, jnp.zeros((pad, D_MODEL), F32)], axis=0)
    mod = _ada(c_all, w_ada[0], b_ada[0])
    mod_p = [mod[n_seq:n_seq + 1, k * D_MODEL:(k + 1) * D_MODEL] for k in range(6)]
    mod_s = [jnp.repeat(mod[:n_seq, k * D_MODEL:(k + 1) * D_MODEL], n_step, axis=0) for k in range(6)]
    sh1_p, sc1_p, gt1_p, sh2_p, sc2_p, gt2_p = mod_p
    sh1_s, sc1_s, gt1_s, sh2_s, sc2_s, gt2_s = mod_s

    h1 = _modnorm(x_p, x_s, norm1_g[0].reshape(1, D_MODEL), sc1_p, sh1_p, sc1_s, sh1_s)
    rc, rs1, rs2 = _rope_tables(n_prompt, n_seq, n_step)
    proj = _inproj(h1, w_in[0], rc, rs1, rs2, q_norm_g[0].reshape(1, HEAD_DIM), k_norm_g[0].reshape(1, HEAD_DIM))

    pw_re, pw_im, bb_re, bb_im = _ssm_prep(ssm_a_re[0], ssm_a_im[0], ssm_log_dt[0], ssm_b_re[0], ssm_b_im[0])
    b_mat, c_mat = _ssm_block_matrices(bb_re, bb_im, ssm_c_re[0], ssm_c_im[0])
    d_skip = ssm_d[0].reshape(1, SSM_WIDTH)
    yg_p, fre_p, fim_p = _s5_prompt(proj, n_prompt, d_skip, pw_re, pw_im, b_mat, c_mat)
    yg_s, fre_s, fim_s = _s5_sample(proj, n_prompt, n_seq, n_step, d_skip, pw_re, pw_im, b_mat, c_mat,
                                    state_ssm_re[0].reshape(n_seq, SSM_FLAT), state_ssm_im[0].reshape(n_seq, SSM_FLAT))

    o_p = _attn_prompt(proj, n_prompt)
    o_s = _attn_sample(proj, n_prompt, n_seq, n_step, (cache_kv_w128[0], cache_kv_w512[0], cache_kv_w2048[0]))

    mixed = _mix(yg_p, yg_s, o_p, o_s, w_glu[0], w_attn_br[0], proj)

    w_router = jnp.concatenate([w_router_group[0], w_router_expert[0],
                                jnp.zeros((D_MODEL, LANES - N_EXPERT_GROUPS - N_EXPERTS), F32)], axis=1)
    b_router = jnp.concatenate([b_router_group[0], b_router_expert[0],
                                jnp.zeros((LANES - N_EXPERT_GROUPS - N_EXPERTS,), F32)]).reshape(1, LANES)
    x1, h2, route = _outproj(mixed, w_out[0].astype(BF16), x_p, x_s, norm2_g[0].reshape(1, D_MODEL),
                             w_router, b_router, gt1_p, sc2_p, sh2_p, gt1_s, sc2_s, sh2_s)
    y_p, y_s = _moe(h2, route, x1, w_expert_gate_up[0], w_expert_down[0], gt2_p, gt2_s, n_prompt, n_samp)

    kv_p, kv_s = [], []
    for g, (window, _) in enumerate(DILATION_PATTERNS):
        kc = slice(OFF_K + g * GROUP_WIDTH, OFF_K + (g + 1) * GROUP_WIDTH)
        vc = slice(OFF_V + g * GROUP_WIDTH, OFF_V + (g + 1) * GROUP_WIDTH)
        keep = min(window, n_prompt)
        rows_p = slice(n_prompt - keep, n_prompt)
        kp = proj[rows_p, kc].reshape(1, 1, keep, 1, HEADS_PER_GROUP, HEAD_DIM)
        vp = proj[rows_p, vc].reshape(1, 1, keep, 1, HEADS_PER_GROUP, HEAD_DIM)
        kv_p.append(jnp.concatenate([kp, vp], axis=3))
        ksm = proj[n_prompt:, kc].reshape(1, n_seq, n_step, 1, HEADS_PER_GROUP, HEAD_DIM)
        vsm = proj[n_prompt:, vc].reshape(1, n_seq, n_step, 1, HEADS_PER_GROUP, HEAD_DIM)
        kv_s.append(jnp.concatenate([ksm, vsm], axis=3))

    state_shape_p = (1, 1, SSM_GROUPS, SSM_STATE)
    state_shape_s = (1, n_seq, SSM_GROUPS, SSM_STATE)
    return (y_p.reshape(1, n_prompt, D_MODEL), y_s.reshape(n_seq, n_step, D_MODEL),
            kv_p[0], kv_p[1], kv_p[2], fre_p.reshape(state_shape_p), fim_p.reshape(state_shape_p),
            kv_s[0], kv_s[1], kv_s[2], fre_s.reshape(state_shape_s), fim_s.reshape(state_shape_s))
```

```python
import functools
import math

import numpy as np
import jax
import jax.numpy as jnp
from jax import lax
from jax.experimental import pallas as pl
from jax.experimental.pallas import tpu as pltpu

F32 = jnp.float32
BF16 = jnp.bfloat16

D_MODEL = 2048
PAST_LEN = 2048
SSM_WIDTH = D_MODEL // 2
SSM_GROUP = 16
SSM_GROUPS = SSM_WIDTH // SSM_GROUP
SSM_STATE = 64
SSM_FLAT = SSM_GROUPS * SSM_STATE
HEAD_DIM = 128
DILATION_PATTERNS = ((128, 1), (512, 4), (2048, 16))
N_PATTERNS = 3
HEADS_PER_GROUP = 4
GROUP_WIDTH = HEADS_PER_GROUP * HEAD_DIM
ATTN_WIDTH = N_PATTERNS * GROUP_WIDTH
ROT_DIM = HEAD_DIM // 4
ROPE_THETA = 500000.0
OFF_Q = SSM_WIDTH
OFF_K = OFF_Q + ATTN_WIDTH
OFF_V = OFF_K + ATTN_WIDTH
OFF_G = OFF_V + ATTN_WIDTH
IN_COLS = OFF_G + 2 * D_MODEL
N_EXPERT_GROUPS = 4
EXPERTS_PER_GROUP = 4
N_EXPERTS = 16
EXPERT_FF = D_MODEL // 4
EPS = 1e-6
NEG = -1e30

LANES = 128
SUBLANES = 8
VMEM_LIMIT = 56 * 1024 * 1024

TOK_TILE = 512
SMALL_TOK_TILE = 256
PROJ_TM = 544
PROJ_TN = 2432
SCAN_L = 32
SCAN_NC = 16
SCAN_TT = SCAN_L * SCAN_NC
SSM_BLK = 8
SSM_BLK_STATE = SSM_BLK * SSM_STATE
SSM_PAR = 2
MOE_TM = 256
ROUTE_E1, ROUTE_E2, ROUTE_W1, ROUTE_W2 = 0, 1, 2, 3
ATT_SB = 2048
BAND = 128
ATT_UNROLL = 8


def _cparams(sem, vmem=VMEM_LIMIT):
    return pltpu.CompilerParams(dimension_semantics=sem, vmem_limit_bytes=vmem)


def _sigmoid(x):
    return 1.0 / (1.0 + jnp.exp(-x))


def _gelu_tanh(x):
    c = math.sqrt(2.0 / math.pi)
    return 0.5 * x * (1.0 + jnp.tanh(c * (x + 0.044715 * (x * x * x))))


def _ada_kernel(c_ref, w_ref, b_ref, o_ref):
    c = c_ref[...]
    cs = (c * _sigmoid(c)).astype(BF16)
    o_ref[...] = jnp.dot(cs, w_ref[...].astype(BF16), preferred_element_type=F32) + b_ref[...]


def _ada(c_all, w_ada, b_ada):
    rows = c_all.shape[0]
    n_out = w_ada.shape[1]
    tn = 1024
    return pl.pallas_call(
        _ada_kernel,
        grid=(n_out // tn,),
        in_specs=[pl.BlockSpec((rows, D_MODEL), lambda n: (0, 0)),
                  pl.BlockSpec((D_MODEL, tn), lambda n: (0, n)),
                  pl.BlockSpec((1, tn), lambda n: (0, n))],
        out_specs=pl.BlockSpec((rows, tn), lambda n: (0, n)),
        out_shape=jax.ShapeDtypeStruct((rows, n_out), F32),
        compiler_params=_cparams(("arbitrary",)),
        name="ada_mod",
    )(c_all, w_ada, b_ada.reshape(1, n_out))


def _modnorm_kernel(xp_ref, xs_ref, g_ref, scp_ref, shp_ref, scs_ref, shs_ref, o_ref, *, n_prompt_tiles):
    is_s = pl.program_id(0) >= n_prompt_tiles
    x = jnp.where(is_s, xs_ref[...], xp_ref[...])
    sc = jnp.where(is_s, scs_ref[...], scp_ref[...])
    sh = jnp.where(is_s, shs_ref[...], shp_ref[...])
    ms = jnp.mean(x * x, axis=-1, keepdims=True)
    y = x * lax.rsqrt(ms + EPS) * g_ref[...]
    o_ref[...] = (y * (1.0 + sc) + sh).astype(o_ref.dtype)


def _modnorm(x_p, x_s, g, sc_p, sh_p, sc_s, sh_s):
    tp, ts = x_p.shape[0], x_s.shape[0]
    tm = TOK_TILE
    npt, nst = tp // tm, ts // tm
    row = lambda i: (jnp.minimum(i, npt - 1), 0)
    srow = lambda i: (jnp.maximum(i - npt, 0), 0)
    const = lambda i: (0, 0)
    return pl.pallas_call(
        functools.partial(_modnorm_kernel, n_prompt_tiles=npt),
        grid=(npt + nst,),
        in_specs=[pl.BlockSpec((tm, D_MODEL), row), pl.BlockSpec((tm, D_MODEL), srow),
                  pl.BlockSpec((1, D_MODEL), const), pl.BlockSpec((1, D_MODEL), const),
                  pl.BlockSpec((1, D_MODEL), const), pl.BlockSpec((tm, D_MODEL), srow),
                  pl.BlockSpec((tm, D_MODEL), srow)],
        out_specs=pl.BlockSpec((tm, D_MODEL), lambda i: (i, 0)),
        out_shape=jax.ShapeDtypeStruct((tp + ts, D_MODEL), BF16),
        compiler_params=_cparams(("arbitrary",)),
        name="modnorm1",
    )(x_p, x_s, g, sc_p, sh_p, sc_s, sh_s)


def _inproj_kernel(h_ref, w_ref, rc_ref, rs1_ref, rs2_ref, qg_ref, kg_ref, o_ref, wbf_ref, *, heads_per_tile):
    n = pl.program_id(0)

    @pl.when(pl.program_id(1) == 0)
    def _():
        wbf_ref[...] = w_ref[...].astype(BF16)

    half = ROT_DIM // 2
    q_heads = range(OFF_Q // HEAD_DIM, OFF_K // HEAD_DIM)
    k_heads = range(OFF_K // HEAD_DIM, OFF_V // HEAD_DIM)
    pair = 2 * HEAD_DIM

    def tile(col_tile):
        h = h_ref[...]
        for c0 in range(0, heads_per_tile * HEAD_DIM, pair):
            width = min(pair, heads_per_tile * HEAD_DIM - c0)
            acc = jnp.dot(h, wbf_ref[:, c0:c0 + width], preferred_element_type=F32)
            for c in range(c0, c0 + width, HEAD_DIM):
                slot = col_tile * heads_per_tile + c // HEAD_DIM
                x = acc[:, c - c0:c - c0 + HEAD_DIM]
                if slot in q_heads or slot in k_heads:
                    gain = qg_ref[...] if slot in q_heads else kg_ref[...]
                    ms = jnp.mean(x * x, axis=-1, keepdims=True)
                    y = x * lax.rsqrt(ms + EPS) * gain
                    up = pltpu.roll(y, HEAD_DIM - half, 1)
                    dn = pltpu.roll(y, half, 1)
                    x = y * rc_ref[...] + up * rs1_ref[...] + dn * rs2_ref[...]
                o_ref[:, c:c + HEAD_DIM] = x

    for col_tile in range(IN_COLS // (heads_per_tile * HEAD_DIM)):
        pl.when(n == col_tile)(functools.partial(tile, col_tile))


def _inproj(h, w_in, rc, rs1, rs2, qg, kg):
    n_tok = h.shape[0]
    tm, tn = PROJ_TM, PROJ_TN
    tab = pl.BlockSpec((tm, HEAD_DIM), lambda n, m: (m, 0))
    gain = pl.BlockSpec((1, HEAD_DIM), lambda n, m: (0, 0))
    return pl.pallas_call(
        functools.partial(_inproj_kernel, heads_per_tile=tn // HEAD_DIM),
        grid=(IN_COLS // tn, n_tok // tm),
        in_specs=[pl.BlockSpec((tm, D_MODEL), lambda n, m: (m, 0)),
                  pl.BlockSpec((D_MODEL, tn), lambda n, m: (0, n), pipeline_mode=pl.Buffered(1)),
                  tab, tab, tab, gain, gain],
        out_specs=pl.BlockSpec((tm, tn), lambda n, m: (m, n)),
        out_shape=jax.ShapeDtypeStruct((n_tok, IN_COLS), F32),
        scratch_shapes=[pltpu.VMEM((D_MODEL, tn), BF16)],
        compiler_params=_cparams(("arbitrary", "arbitrary")),
        name="in_proj",
    )(h, w_in, rc, rs1, rs2, qg, kg)


def _ssm_prep_kernel(are_ref, aim_ref, ldt_ref, arer_ref, aimr_ref, ldtr_ref, bre_ref, bim_ref,
                     pre_ref, pim_ref, bbre_ref, bbim_ref):
    def discretise(a_re, a_im, log_dt):
        dt = jnp.exp(log_dt)
        mag = jnp.exp(a_re * dt)
        return mag * jnp.cos(a_im * dt), mag * jnp.sin(a_im * dt)

    ab_re, ab_im = discretise(are_ref[...], aim_ref[...], ldt_ref[...])
    p_re, p_im = ab_re, ab_im
    for i in range(SCAN_L):
        pre_ref[i:i + 1, :] = p_re
        pim_ref[i:i + 1, :] = p_im
        p_re, p_im = p_re * ab_re - p_im * ab_im, p_re * ab_im + p_im * ab_re

    a_re, a_im = arer_ref[...], aimr_ref[...]
    r_re, r_im = discretise(a_re, a_im, ldtr_ref[...])
    nr, ni = r_re - 1.0, r_im
    den = a_re * a_re + a_im * a_im
    z_re = (nr * a_re + ni * a_im) / den
    z_im = (ni * a_re - nr * a_im) / den
    b_re, b_im = bre_ref[...], bim_ref[...]
    bbre_ref[...] = z_re * b_re - z_im * b_im
    bbim_ref[...] = z_re * b_im + z_im * b_re


def _ssm_prep(a_re, a_im, log_dt, b_re, b_im):
    g, p, n = b_re.shape
    flat = lambda x: x.reshape(1, g * p)
    rep = lambda x: jnp.repeat(x, n, axis=1)
    ldt_gp = jnp.broadcast_to(log_dt[:, None], (g, p))
    ldt_rep = jnp.broadcast_to(log_dt[:, None], (g, p * n))
    out_shape = [jax.ShapeDtypeStruct((SCAN_L, g * p), F32)] * 2 + [jax.ShapeDtypeStruct((g, p * n), F32)] * 2
    return pl.pallas_call(_ssm_prep_kernel, out_shape=out_shape, name="ssm_prep")(
        flat(a_re), flat(a_im), flat(ldt_gp), rep(a_re), rep(a_im), ldt_rep,
        b_re.reshape(g, p * n), b_im.reshape(g, p * n))


def _ssm_block_matrices(bb_re, bb_im, c_re, c_im):
    g, p, n = SSM_GROUPS, SSM_STATE, SSM_GROUP
    nb = g // SSM_BLK
    eye = jnp.eye(SSM_BLK, dtype=F32)

    def in_mat(bb):
        x = bb.reshape(nb, SSM_BLK, p, n)
        return jnp.einsum('bgpm,gh->bgmhp', x, eye).reshape(nb, SSM_BLK * n, SSM_BLK * p)

    def out_mat(c):
        x = c.reshape(nb, SSM_BLK, n, p)
        return jnp.einsum('bgnp,gh->bgphn', x, eye).reshape(nb, SSM_BLK * p, SSM_BLK * n)

    b_mat = jnp.concatenate([in_mat(bb_re), in_mat(bb_im)], axis=2).astype(BF16)
    c_mat = jnp.concatenate([out_mat(c_re), -out_mat(c_im)], axis=1).astype(BF16)
    return b_mat, c_mat


def _cmul_add(a_re, a_im, s_re, s_im, b_re, b_im):
    return a_re * s_re - a_im * s_im + b_re, a_re * s_im + a_im * s_re + b_im


def _s5_prompt_kernel(*refs):
    par = SSM_PAR
    u_refs = refs[:par]
    (d_ref, pre_ref, pim_ref, bm_ref, cm_ref, y_ref, fre_ref, fim_ref,
     up_scr, bu_scr, lhs_scr, in_re_scr, in_im_scr, car_re, car_im, yn_scr) = refs[par:]
    nc, ln, w = SCAN_NC, SCAN_L, SSM_BLK_STATE

    @pl.when(pl.program_id(1) == 0)
    def _():
        car_re[...] = jnp.zeros_like(car_re)
        car_im[...] = jnp.zeros_like(car_im)

    for b in range(par):
        lanes = slice(b * w, (b + 1) * w)
        rows = lambda i: slice(i * nc, (i + 1) * nc)
        for i in range(ln):
            up_scr[b, rows(i), :] = u_refs[b][pl.ds(i, nc, stride=ln), :]
        up = up_scr[b]
        bu_scr[b] = jnp.dot(up.astype(BF16), bm_ref[b], preferred_element_type=F32)

        a_re = jnp.broadcast_to(pre_ref[0:1, lanes], (nc, w))
        a_im = jnp.broadcast_to(pim_ref[0:1, lanes], (nc, w))
        s_re = s_im = jnp.zeros((nc, w), F32)
        for i in range(ln):
            s_re, s_im = _cmul_add(a_re, a_im, s_re, s_im, bu_scr[b, rows(i), 0:w], bu_scr[b, rows(i), w:2 * w])
            bu_scr[b, rows(i), 0:w] = s_re
            bu_scr[b, rows(i), w:2 * w] = s_im

        al_re, al_im = pre_ref[ln - 1:ln, lanes], pim_ref[ln - 1:ln, lanes]
        c_re, c_im = car_re[b], car_im[b]
        for c in range(nc):
            in_re_scr[b, c:c + 1, :] = c_re
            in_im_scr[b, c:c + 1, :] = c_im
            c_re, c_im = _cmul_add(al_re, al_im, c_re, c_im, s_re[c:c + 1, :], s_im[c:c + 1, :])
        car_re[b] = c_re
        car_im[b] = c_im
        fre_ref[:, lanes] = c_re
        fim_ref[:, lanes] = c_im
        in_re, in_im = in_re_scr[b], in_im_scr[b]

        for i in range(ln):
            p_re = jnp.broadcast_to(pre_ref[i:i + 1, lanes], (nc, w))
            p_im = jnp.broadcast_to(pim_ref[i:i + 1, lanes], (nc, w))
            f_re, f_im = _cmul_add(p_re, p_im, in_re, in_im, bu_scr[b, rows(i), 0:w], bu_scr[b, rows(i), w:2 * w])
            lhs_scr[b, rows(i), 0:w] = f_re.astype(BF16)
            lhs_scr[b, rows(i), w:2 * w] = f_im.astype(BF16)

        y = (jnp.dot(lhs_scr[b], cm_ref[b], preferred_element_type=F32)
             + d_ref[:, b * LANES:(b + 1) * LANES] * up)
        for i in range(ln):
            yn_scr[b, pl.ds(i, nc, stride=ln), :] = y[rows(i), :]
        y_ref[:, b * LANES:(b + 1) * LANES] = _gelu_tanh(yn_scr[b]).astype(y_ref.dtype)


def _s5_prompt(proj, n_prompt, d_skip, pw_re, pw_im, b_mat, c_mat):
    par = SSM_PAR
    nb = SSM_GROUPS // SSM_BLK // par
    tt, w = SCAN_TT, SSM_BLK_STATE
    u_spec = lambda b: pl.BlockSpec((tt, LANES), lambda j, i: (i, par * j + b))
    return pl.pallas_call(
        _s5_prompt_kernel,
        grid=(nb, n_prompt // tt),
        in_specs=[u_spec(b) for b in range(par)] + [
            pl.BlockSpec((1, par * LANES), lambda j, i: (0, j)),
            pl.BlockSpec((SCAN_L, par * w), lambda j, i: (0, j)),
            pl.BlockSpec((SCAN_L, par * w), lambda j, i: (0, j)),
            pl.BlockSpec((par, LANES, 2 * w), lambda j, i: (j, 0, 0)),
            pl.BlockSpec((par, 2 * w, LANES), lambda j, i: (j, 0, 0))],
        out_specs=[pl.BlockSpec((tt, par * LANES), lambda j, i: (i, j)),
                   pl.BlockSpec((1, par * w), lambda j, i: (0, j)),
                   pl.BlockSpec((1, par * w), lambda j, i: (0, j))],
        out_shape=[jax.ShapeDtypeStruct((n_prompt, SSM_WIDTH), BF16),
                   jax.ShapeDtypeStruct((1, SSM_FLAT), F32),
                   jax.ShapeDtypeStruct((1, SSM_FLAT), F32)],
        scratch_shapes=[pltpu.VMEM((par, tt, LANES), F32), pltpu.VMEM((par, tt, 2 * w), F32),
                        pltpu.VMEM((par, tt, 2 * w), BF16),
                        pltpu.VMEM((par, SCAN_NC, w), F32), pltpu.VMEM((par, SCAN_NC, w), F32),
                        pltpu.VMEM((par, 1, w), F32), pltpu.VMEM((par, 1, w), F32),
                        pltpu.VMEM((par, tt, LANES), F32)],
        compiler_params=_cparams(("arbitrary", "arbitrary")),
        name="s5_prompt",
    )(*([proj] * par), d_skip, pw_re, pw_im, b_mat, c_mat)


def _s5_sample_kernel(u_ref, d_ref, pre_ref, pim_ref, bm_ref, cm_ref, s0re_ref, s0im_ref,
                      y_ref, fre_ref, fim_ref, up_scr, bu_scr, lhs_scr, yn_scr, *, n_seq, n_step):
    w = SSM_BLK_STATE
    rb = 16
    for s in range(n_step):
        up_scr[s * n_seq:(s + 1) * n_seq, :] = u_ref[pl.ds(s, n_seq, stride=n_step), :]
    up = up_scr[...]
    bu_scr[...] = jnp.dot(up.astype(BF16), bm_ref[0], preferred_element_type=F32)
    a_re = jnp.broadcast_to(pre_ref[0:1, :], (rb, w))
    a_im = jnp.broadcast_to(pim_ref[0:1, :], (rb, w))

    def seq_block(b, _):
        r0 = pl.multiple_of(b * rb, rb)
        s_re, s_im = s0re_ref[pl.ds(r0, rb), :], s0im_ref[pl.ds(r0, rb), :]
        for s in range(n_step):
            rows = pl.ds(pl.multiple_of(s * n_seq + r0, rb), rb)
            s_re, s_im = _cmul_add(a_re, a_im, s_re, s_im, bu_scr[rows, 0:w], bu_scr[rows, w:2 * w])
            lhs_scr[rows, 0:w] = s_re.astype(BF16)
            lhs_scr[rows, w:2 * w] = s_im.astype(BF16)
        fre_ref[pl.ds(r0, rb), :] = s_re
        fim_ref[pl.ds(r0, rb), :] = s_im
        return 0

    lax.fori_loop(0, n_seq // rb, seq_block, 0)
    y = jnp.dot(lhs_scr[...], cm_ref[0], preferred_element_type=F32) + d_ref[...] * up
    for s in range(n_step):
        yn_scr[pl.ds(s, n_seq, stride=n_step), :] = y[s * n_seq:(s + 1) * n_seq, :]
    y_ref[...] = _gelu_tanh(yn_scr[...]).astype(y_ref.dtype)


def _s5_sample(proj, n_prompt, n_seq, n_step, d_skip, pw_re, pw_im, b_mat, c_mat, s0_re, s0_im):
    nb = SSM_GROUPS // SSM_BLK
    rows, w = n_seq * n_step, SSM_BLK_STATE
    rblk = n_prompt // rows
    return pl.pallas_call(
        functools.partial(_s5_sample_kernel, n_seq=n_seq, n_step=n_step),
        grid=(nb,),
        in_specs=[pl.BlockSpec((rows, LANES), lambda j: (rblk, j)),
                  pl.BlockSpec((1, LANES), lambda j: (0, j)),
                  pl.BlockSpec((SCAN_L, w), lambda j: (0, j)),
                  pl.BlockSpec((SCAN_L, w), lambda j: (0, j)),
                  pl.BlockSpec((1, LANES, 2 * w), lambda j: (j, 0, 0)),
                  pl.BlockSpec((1, 2 * w, LANES), lambda j: (j, 0, 0)),
                  pl.BlockSpec((n_seq, w), lambda j: (0, j)),
                  pl.BlockSpec((n_seq, w), lambda j: (0, j))],
        out_specs=[pl.BlockSpec((rows, LANES), lambda j: (0, j)),
                   pl.BlockSpec((n_seq, w), lambda j: (0, j)),
                   pl.BlockSpec((n_seq, w), lambda j: (0, j))],
        out_shape=[jax.ShapeDtypeStruct((rows, SSM_WIDTH), BF16),
                   jax.ShapeDtypeStruct((n_seq, SSM_FLAT), F32),
                   jax.ShapeDtypeStruct((n_seq, SSM_FLAT), F32)],
        scratch_shapes=[pltpu.VMEM((rows, LANES), F32), pltpu.VMEM((rows, 2 * w), F32),
                        pltpu.VMEM((rows, 2 * w), BF16), pltpu.VMEM((rows, LANES), F32)],
        compiler_params=_cparams(("arbitrary",)),
        name="s5_sample",
    )(proj, d_skip, pw_re, pw_im, b_mat, c_mat, s0_re, s0_im)


def _attn_prompt_kernel(*refs):
    ins, o_ref, scr = refs[:15], refs[15], refs[16:]
    sb = pl.program_id(0)
    scale = HEAD_DIM ** -0.5
    qi = lax.broadcasted_iota(jnp.int32, (BAND, 2 * BAND), 0)
    kj = lax.broadcasted_iota(jnp.int32, (BAND, 2 * BAND), 1)
    dist = qi + BAND - kj
    band_ok = (dist >= 0) & (dist <= BAND)

    for g, (_, dil) in enumerate(DILATION_PATTERNS):
        q_ref, k_ref, v_ref, kp_ref, vp_ref = ins[5 * g:5 * g + 5]
        kbuf, vbuf, o_scr, m_scr, l_scr = scr[5 * g:5 * g + 5]
        pre = BAND * dil
        kbuf[0:pre, :] = kp_ref[...]
        kbuf[pre:pre + ATT_SB, :] = k_ref[...]
        vbuf[0:pre, :] = vp_ref[...]
        vbuf[pre:pre + ATT_SB, :] = v_ref[...]
        nblk = ATT_SB // pre

        def block(idx, _, dil=dil, pre=pre, nblk=nblk, q_ref=q_ref, kbuf=kbuf, vbuf=vbuf,
                  o_scr=o_scr, m_scr=m_scr, l_scr=l_scr):
            r = idx // nblk
            b = idx - r * nblk
            row0 = r + b * pre
            if dil == 1:
                q_rows = pl.ds(pl.multiple_of(row0, BAND), BAND)
                kv_rows = pl.ds(pl.multiple_of(row0, BAND), 2 * BAND)
            else:
                q_rows = pl.ds(row0, BAND, stride=dil)
                kv_rows = pl.ds(row0, 2 * BAND, stride=dil)
            q = (q_ref[q_rows, :] * scale).astype(BF16)
            kw = kbuf[kv_rows, :].astype(BF16)
            vw = vbuf[kv_rows, :].astype(BF16)
            s = lax.dot_general(q, kw, (((1,), (1,)), ((), ())), preferred_element_type=F32)
            s = jnp.where(band_ok & ((kj >= BAND) | (sb > 0) | (b > 0)), s, NEG)
            m = jnp.max(s, axis=-1, keepdims=True)
            p = jnp.exp(s - m)
            l = jnp.sum(p, axis=-1, keepdims=True)
            o = jnp.dot(p.astype(BF16), vw, preferred_element_type=F32)
            o_scr[q_rows, :] = o
            m_scr[q_rows, :] = jnp.broadcast_to(m, (BAND, HEAD_DIM))
            l_scr[q_rows, :] = jnp.broadcast_to(l, (BAND, HEAD_DIM))
            return 0

        lax.fori_loop(0, ATT_SB // BAND, block, 0, unroll=ATT_UNROLL)

    ms = [scr[5 * g + 3][...] for g in range(N_PATTERNS)]
    mx = jnp.maximum(jnp.maximum(ms[0], ms[1]), ms[2])
    num = jnp.zeros((ATT_SB, HEAD_DIM), F32)
    den = jnp.zeros((ATT_SB, HEAD_DIM), F32)
    for g in range(N_PATTERNS):
        wgt = jnp.exp(ms[g] - mx)
        num = num + wgt * scr[5 * g + 2][...]
        den = den + wgt * scr[5 * g + 4][...]
    o_ref[...] = num / den


def _attn_prompt(proj, n_prompt):
    hcol = lambda off, g, j: (off + g * GROUP_WIDTH) // HEAD_DIM + j
    in_specs, scratch = [], []
    for g, (_, dil) in enumerate(DILATION_PATTERNS):
        pre = BAND * dil
        per = ATT_SB // pre
        cur = lambda off, g=g: pl.BlockSpec((ATT_SB, HEAD_DIM), lambda sb, j: (sb, hcol(off, g, j)))
        prev = lambda off, g=g, per=per, pre=pre: pl.BlockSpec(
            (pre, HEAD_DIM), lambda sb, j: (jnp.maximum(sb * per - 1, 0), hcol(off, g, j)))
        in_specs += [cur(OFF_Q), cur(OFF_K), cur(OFF_V), prev(OFF_K), prev(OFF_V)]
        scratch += [pltpu.VMEM((pre + ATT_SB, HEAD_DIM), F32), pltpu.VMEM((pre + ATT_SB, HEAD_DIM), F32),
                    pltpu.VMEM((ATT_SB, HEAD_DIM), F32), pltpu.VMEM((ATT_SB, HEAD_DIM), F32),
                    pltpu.VMEM((ATT_SB, HEAD_DIM), F32)]
    return pl.pallas_call(
        _attn_prompt_kernel,
        grid=(n_prompt // ATT_SB, HEADS_PER_GROUP),
        in_specs=in_specs,
        out_specs=pl.BlockSpec((ATT_SB, HEAD_DIM), lambda sb, j: (sb, j)),
        out_shape=jax.ShapeDtypeStruct((n_prompt, GROUP_WIDTH), F32),
        scratch_shapes=scratch,
        compiler_params=_cparams(("arbitrary", "arbitrary")),
        name="attn_prompt",
    )(*([proj] * 15))


SEQ_PER_STEP = 2
KV_PLANES = 2 * HEADS_PER_GROUP


def _compact_pitch(n_step):
    tiles = n_step * KV_PLANES // 8
    return 8 * (tiles + 1 - tiles % 2)


def _sample_bias(n_step):
    rows = HEADS_PER_GROUP * n_step
    step = np.arange(rows) % n_step
    cache_bias, new_bias = [], []
    for (window, dil) in DILATION_PATTERNS:
        wb = min(window, PAST_LEN)
        band = window // dil
        if dil > n_step:
            res, i = np.meshgrid(np.arange(n_step), np.arange(wb // dil), indexing='ij')
            c = (i * dil + res).reshape(-1)
        else:
            c = np.arange(wb)
        delta = wb + step[:, None] - c[None, :]
        ok = (delta >= 0) & (delta % dil == 0) & (delta // dil <= band)
        cache_bias.append(np.where(ok, 0.0, NEG).astype(np.float32))
        nb = np.full((SEQ_PER_STEP, rows, LANES), NEG, np.float32)
        for a in range(SEQ_PER_STEP):
            for sp in range(n_step):
                dl = step - sp
                okn = (dl >= 0) & (dl % dil == 0) & (dl // dil <= band)
                nb[a, :, a * n_step + sp] = np.where(okn, 0.0, NEG)
        new_bias.append(nb)
    return cache_bias, new_bias


def _attn_sample_kernel(*refs, n_step):
    (q0, k0, v0, q1, k1, v1, q2, k2, v2, c0, c1, c2, cb0, cb1, cb2, nb0, nb1, nb2, o_ref) = refs
    qs, ks, vs = (q0, q1, q2), (k0, k1, k2), (v0, v1, v2)
    caches, cbias, nbias = (c0, c1, c2), (cb0, cb1, cb2), (nb0, nb1, nb2)
    rows = HEADS_PER_GROUP * n_step
    gw = GROUP_WIDTH
    scale = HEAD_DIM ** -0.5
    row_head = lax.broadcasted_iota(jnp.int32, (rows, gw), 0) // n_step
    lane_head = lax.broadcasted_iota(jnp.int32, (rows, gw), 1) // HEAD_DIM
    own_head = row_head == lane_head
    nt = (((1,), (1,)), ((), ()))
    pad = jnp.zeros((LANES - SEQ_PER_STEP * n_step, gw), F32)

    def planes(load):
        k = jnp.concatenate([load(h) for h in range(HEADS_PER_GROUP)], axis=1)
        v = jnp.concatenate([load(HEADS_PER_GROUP + h) for h in range(HEADS_PER_GROUP)], axis=1)
        return k.astype(BF16), v.astype(BF16)

    for a in range(SEQ_PER_STEP):
        pieces = []
        for g, (window, dil) in enumerate(DILATION_PATTERNS):
            q = qs[g][a * n_step:(a + 1) * n_step, :] * scale
            qbd = jnp.where(own_head, jnp.concatenate([q] * HEADS_PER_GROUP, axis=0), 0.0).astype(BF16)
            cache = caches[g]
            per_seq = cache.shape[0] // SEQ_PER_STEP
            if dil > n_step:
                kvs = [planes(lambda p, r=r: cache[pl.ds(a * per_seq, per_seq), r * KV_PLANES + p, :])
                       for r in range(n_step)]
            else:
                kvs = [planes(lambda p: cache[pl.ds(a * per_seq + p, per_seq // KV_PLANES, stride=KV_PLANES), :])]
            sc = jnp.concatenate([lax.dot_general(qbd, k, nt, preferred_element_type=F32) for k, _ in kvs], axis=1)
            pieces.append((sc + cbias[g][...], [v for _, v in kvs]))
            k_new = jnp.concatenate([ks[g][...], pad], axis=0).astype(BF16)
            v_new = jnp.concatenate([vs[g][...], pad], axis=0).astype(BF16)
            sn = lax.dot_general(qbd, k_new, nt, preferred_element_type=F32) + nbias[g][a]
            pieces.append((sn, [v_new]))
        m = functools.reduce(jnp.maximum, [jnp.max(s, axis=-1, keepdims=True) for s, _ in pieces])
        l = jnp.zeros((rows, 1), F32)
        acc = jnp.zeros((rows, gw), F32)
        for s, vals in pieces:
            p = jnp.exp(s - m)
            l = l + jnp.sum(p, axis=-1, keepdims=True)
            pb = p.astype(BF16)
            nk = pb.shape[1] // len(vals)
            for r, v in enumerate(vals):
                acc = acc + jnp.dot(pb[:, r * nk:(r + 1) * nk], v, preferred_element_type=F32)
        acc = jnp.where(own_head, acc, 0.0)
        o16 = functools.reduce(lambda x, y: x + y,
                               [acc[:, h * HEAD_DIM:(h + 1) * HEAD_DIM] for h in range(HEADS_PER_GROUP)]) / l
        for h in range(HEADS_PER_GROUP):
            o_ref[a * n_step:(a + 1) * n_step, h * HEAD_DIM:(h + 1) * HEAD_DIM] = o16[h * n_step:(h + 1) * n_step, :]


def _attn_sample(proj, n_prompt, n_seq, n_step, caches):
    rows = SEQ_PER_STEP * n_step
    rblk = n_prompt // rows
    cache_bias, new_bias = _sample_bias(n_step)
    tok = lambda off, g: pl.BlockSpec((rows, GROUP_WIDTH), lambda i: (rblk + i, (off + g * GROUP_WIDTH) // GROUP_WIDTH))
    in_specs, args = [], []
    for g in range(N_PATTERNS):
        in_specs += [tok(OFF_Q, g), tok(OFF_K, g), tok(OFF_V, g)]
        args += [proj, proj, proj]
    for g, (window, dil) in enumerate(DILATION_PATTERNS):
        c = caches[g]
        wb = c.shape[1]
        if dil > n_step:
            pitch = _compact_pitch(n_step)
            assert pitch <= dil * KV_PLANES and wb % dil == 0
            c = c.reshape(n_seq * (wb // dil), dil * KV_PLANES, HEAD_DIM)
            in_specs.append(pl.BlockSpec((SEQ_PER_STEP * (wb // dil), pitch, HEAD_DIM), lambda i: (i, 0, 0)))
        else:
            c = c.reshape(n_seq * wb * KV_PLANES, HEAD_DIM)
            in_specs.append(pl.BlockSpec((SEQ_PER_STEP * wb * KV_PLANES, HEAD_DIM), lambda i: (i, 0)))
        args.append(c)
    for b in cache_bias:
        in_specs.append(pl.BlockSpec(b.shape, lambda i: (0, 0)))
        args.append(jnp.asarray(b))
    for b in new_bias:
        in_specs.append(pl.BlockSpec(b.shape, lambda i: (0, 0, 0)))
        args.append(jnp.asarray(b))
    return pl.pallas_call(
        functools.partial(_attn_sample_kernel, n_step=n_step),
        grid=(n_seq // SEQ_PER_STEP,),
        in_specs=in_specs,
        out_specs=pl.BlockSpec((rows, GROUP_WIDTH), lambda i: (i, 0)),
        out_shape=jax.ShapeDtypeStruct((n_seq * n_step, GROUP_WIDTH), F32),
        compiler_params=_cparams(("arbitrary",)),
        name="attn_sample",
    )(*args)


def _mix_kernel(yp_ref, ys_ref, op_ref, os_ref, wa_ref, wb_ref, wbr_ref, ga_ref, gb_ref, o_ref,
                wa_bf, wb_bf, wbr_bf, *, n_prompt_tiles):
    i = pl.program_id(1)

    @pl.when(i == 0)
    def _():
        wa_bf[...] = wa_ref[...].astype(BF16)
        wb_bf[...] = wb_ref[...].astype(BF16)
        wbr_bf[...] = wbr_ref[...].astype(BF16)

    is_s = i >= n_prompt_tiles
    y = jnp.where(is_s, ys_ref[...], yp_ref[...])
    o = jnp.where(is_s, os_ref[...], op_ref[...]).astype(BF16)
    glu_a = jnp.dot(y, wa_bf[...], preferred_element_type=F32)
    glu_b = jnp.dot(y, wb_bf[...], preferred_element_type=F32)
    branch_a = glu_a * _sigmoid(glu_b)
    branch_b = jnp.dot(o, wbr_bf[...], preferred_element_type=F32)
    o_ref[...] = (_sigmoid(ga_ref[...]) * branch_a + _sigmoid(gb_ref[...]) * branch_b).astype(o_ref.dtype)


def _mix(y_p, y_s, o_p, o_s, w_glu, w_attn_br, proj):
    tp, ts = y_p.shape[0], y_s.shape[0]
    tm, tn = TOK_TILE, 512
    npt, nst = tp // tm, ts // tm
    ncol = D_MODEL // tn
    prow = lambda n, i: (jnp.minimum(i, npt - 1), 0)
    srow = lambda n, i: (jnp.maximum(i - npt, 0), 0)
    return pl.pallas_call(
        functools.partial(_mix_kernel, n_prompt_tiles=npt),
        grid=(ncol, npt + nst),
        in_specs=[pl.BlockSpec((tm, SSM_WIDTH), prow), pl.BlockSpec((tm, SSM_WIDTH), srow),
                  pl.BlockSpec((tm, GROUP_WIDTH), prow), pl.BlockSpec((tm, GROUP_WIDTH), srow),
                  pl.BlockSpec((SSM_WIDTH, tn), lambda n, i: (0, n)),
                  pl.BlockSpec((SSM_WIDTH, tn), lambda n, i: (0, ncol + n)),
                  pl.BlockSpec((GROUP_WIDTH, tn), lambda n, i: (0, n)),
                  pl.BlockSpec((tm, tn), lambda n, i: (i, OFF_G // tn + n)),
                  pl.BlockSpec((tm, tn), lambda n, i: (i, OFF_G // tn + ncol + n))],
        out_specs=pl.BlockSpec((tm, tn), lambda n, i: (i, n)),
        out_shape=jax.ShapeDtypeStruct((tp + ts, D_MODEL), BF16),
        scratch_shapes=[pltpu.VMEM((SSM_WIDTH, tn), BF16), pltpu.VMEM((SSM_WIDTH, tn), BF16),
                        pltpu.VMEM((GROUP_WIDTH, tn), BF16)],
        compiler_params=_cparams(("arbitrary", "arbitrary")),
        name="glu_mix",
    )(y_p, y_s, o_p, o_s, w_glu, w_glu, w_attn_br, proj, proj)


def _route(logits):
    lane = lax.broadcasted_iota(jnp.int32, logits.shape, 1).astype(F32)
    big = 1000.0
    first = lambda cond: jnp.min(jnp.where(cond, lane, big), axis=-1, keepdims=True)
    is_g = lane < N_EXPERT_GROUPS
    lg = jnp.where(is_g, logits, NEG)
    mg = jnp.max(lg, axis=-1, keepdims=True)
    g_sel = first(lg == mg)
    p_group = 1.0 / jnp.sum(jnp.where(is_g, jnp.exp(lg - mg), 0.0), axis=-1, keepdims=True)
    e_lo = N_EXPERT_GROUPS + EXPERTS_PER_GROUP * g_sel
    le = jnp.where((lane >= e_lo) & (lane < e_lo + EXPERTS_PER_GROUP), logits, NEG)
    v1 = jnp.max(le, axis=-1, keepdims=True)
    i1 = first(le == v1)
    le2 = jnp.where(lane == i1, NEG, le)
    v2 = jnp.max(le2, axis=-1, keepdims=True)
    i2 = first(le2 == v2)
    e2 = jnp.exp(v2 - v1)
    w1 = p_group / (1.0 + e2)
    w2 = p_group * e2 / (1.0 + e2)
    pick = lambda k, val: jnp.where(lane == k, val, 0.0)
    return (pick(ROUTE_E1, i1 - N_EXPERT_GROUPS) + pick(ROUTE_E2, i2 - N_EXPERT_GROUPS)
            + pick(ROUTE_W1, w1) + pick(ROUTE_W2, w2))


def _outproj_kernel(mix_ref, w_ref, xp_ref, xs_ref, g_ref, wr_ref, br_ref,
                    gtp_ref, scp_ref, shp_ref, gts_ref, scs_ref, shs_ref,
                    x1_ref, h2_ref, route_ref, *, n_prompt_tiles):
    is_s = pl.program_id(0) >= n_prompt_tiles
    x = jnp.where(is_s, xs_ref[...], xp_ref[...])
    gt = jnp.where(is_s, gts_ref[...], gtp_ref[...])
    sc = jnp.where(is_s, scs_ref[...], scp_ref[...])
    sh = jnp.where(is_s, shs_ref[...], shp_ref[...])
    x1 = x + gt * jnp.dot(mix_ref[...], w_ref[...], preferred_element_type=F32)
    x1_ref[...] = x1
    ms = jnp.mean(x1 * x1, axis=-1, keepdims=True)
    h2 = (x1 * lax.rsqrt(ms + EPS) * g_ref[...]) * (1.0 + sc) + sh
    h2_ref[...] = h2.astype(h2_ref.dtype)
    def split(v):
        high = v.astype(BF16)
        return high, (v - high.astype(F32)).astype(BF16)

    r = jnp.dot(jnp.concatenate(split(h2), axis=0), jnp.concatenate(split(wr_ref[...]), axis=1),
                preferred_element_type=F32)
    tm = h2.shape[0]
    logits = (r[:tm, :LANES] + r[:tm, LANES:]) + (r[tm:, :LANES] + r[tm:, LANES:]) + br_ref[...]
    route_ref[...] = _route(logits)


def _outproj(mixed, w_out_bf, x_p, x_s, g2, w_router, b_router, gt_p, sc_p, sh_p, gt_s, sc_s, sh_s):
    tp, ts = x_p.shape[0], x_s.shape[0]
    tm = SMALL_TOK_TILE
    npt, nst = tp // tm, ts // tm
    prow = lambda i: (jnp.minimum(i, npt - 1), 0)
    srow = lambda i: (jnp.maximum(i - npt, 0), 0)
    const = lambda i: (0, 0)
    vec = pl.BlockSpec((1, D_MODEL), const)
    svec = pl.BlockSpec((tm, D_MODEL), srow)
    full = lambda i: (i, 0)
    return pl.pallas_call(
        functools.partial(_outproj_kernel, n_prompt_tiles=npt),
        grid=(npt + nst,),
        in_specs=[pl.BlockSpec((tm, D_MODEL), full), pl.BlockSpec((D_MODEL, D_MODEL), const),
                  pl.BlockSpec((tm, D_MODEL), prow), pl.BlockSpec((tm, D_MODEL), srow),
                  vec, pl.BlockSpec((D_MODEL, LANES), const), pl.BlockSpec((1, LANES), const),
                  vec, vec, vec, svec, svec, svec],
        out_specs=[pl.BlockSpec((tm, D_MODEL), full), pl.BlockSpec((tm, D_MODEL), full),
                   pl.BlockSpec((tm, LANES), full)],
        out_shape=[jax.ShapeDtypeStruct((tp + ts, D_MODEL), F32),
                   jax.ShapeDtypeStruct((tp + ts, D_MODEL), F32),
                   jax.ShapeDtypeStruct((tp + ts, LANES), F32)],
        compiler_params=_cparams(("arbitrary",)),
        name="out_proj_norm2_router",
    )(mixed, w_out_bf, x_p, x_s, g2, w_router, b_router, gt_p, sc_p, sh_p, gt_s, sc_s, sh_s)


def _dispatch_plan(route, tm):
    e = route[:, ROUTE_E1:ROUTE_E2 + 1].astype(jnp.int32).reshape(-1)
    n_pairs = e.shape[0]
    onehot = (e[:, None] == jnp.arange(N_EXPERTS, dtype=jnp.int32)[None, :]).astype(jnp.int32)
    csum = jnp.cumsum(onehot, axis=0)
    rank = jnp.sum(onehot * csum, axis=1) - 1
    tiles_per_expert = (csum[-1] + tm - 1) // tm
    tile_end = jnp.cumsum(tiles_per_expert)
    tile_start = tile_end - tiles_per_expert
    dest = (tile_start[e] * tm + rank).astype(jnp.int32)
    max_tiles = n_pairs // tm + N_EXPERTS
    k = jnp.arange(max_tiles, dtype=jnp.int32)
    tile_expert = jnp.minimum(jnp.sum((k[:, None] >= tile_end[None, :]).astype(jnp.int32), axis=1), N_EXPERTS - 1)
    n_used = tile_end[-1].astype(jnp.int32)
    last_expert = jnp.take(tile_expert, n_used - 1)
    tile_expert = jnp.where(k < n_used, tile_expert, last_expert).astype(jnp.int32)
    pad_end = (jnp.concatenate([tile_end, tile_end[-1:]]) * tm).astype(jnp.int32)
    pad_len = (tiles_per_expert * tm - csum[-1]).astype(jnp.int32)
    return dest, pad_end, pad_len, tile_expert, n_used.reshape(1), max_tiles


def _pair_copies(dest_ref, tile, rows, make, make_all):
    def body(r, _):
        for k in range(2):
            make(r, k, dest_ref[(tile * rows + r) * 2 + k]).start()
        return 0
    lax.fori_loop(0, rows, body, 0, unroll=8)
    for k in range(2):
        make_all(k).wait()


def _dispatch_kernel(dest_ref, pad_end_ref, pad_len_ref, h_ref, xs_ref, zero_buf, sem, pad_sem):
    rows = h_ref.shape[0]

    @pl.when(pl.program_id(0) == 0)
    def _():
        zero_buf[...] = jnp.zeros_like(zero_buf)

        def pad_copies(fn):
            for e in range(N_EXPERTS):
                end, left = pad_end_ref[e], pad_len_ref[e]
                size = zero_buf.shape[0]
                while size >= SUBLANES:
                    take = (left & size) != 0
                    end = end - jnp.where(take, size, 0)

                    @pl.when(take)
                    def _(start=end, size=size):
                        fn(pltpu.make_async_copy(zero_buf.at[pl.ds(0, size)],
                                                 xs_ref.at[pl.ds(pl.multiple_of(start, size), size)], pad_sem))

                    size //= 2
                for r in range(1, SUBLANES):
                    @pl.when((left & (SUBLANES - 1)) >= r)
                    def _(row=end - r):
                        fn(pltpu.make_async_copy(zero_buf.at[pl.ds(0, 1)], xs_ref.at[pl.ds(row, 1)], pad_sem))
            size = zero_buf.shape[0]
            tail = pad_end_ref[N_EXPERTS]
            for t in range(N_EXPERTS * MOE_TM // size):
                @pl.when(tail + t * size < xs_ref.shape[0])
                def _(start=tail + t * size):
                    fn(pltpu.make_async_copy(zero_buf, xs_ref.at[pl.ds(pl.multiple_of(start, size), size)], pad_sem))

        pad_copies(lambda c: c.start())
        pad_copies(lambda c: c.wait())

    make = lambda r, k, d: pltpu.make_async_copy(h_ref.at[pl.ds(r, 1)], xs_ref.at[pl.ds(d, 1)], sem)
    make_all = lambda k: pltpu.make_async_copy(h_ref, xs_ref.at[pl.ds(0, rows)], sem)
    _pair_copies(dest_ref, pl.program_id(0), rows, make, make_all)


def _dispatch(dest, pad_end, pad_len, h2, n_slots):
    n_tok = h2.shape[0]
    tm = SMALL_TOK_TILE
    return pl.pallas_call(
        _dispatch_kernel,
        grid_spec=pltpu.PrefetchScalarGridSpec(
            num_scalar_prefetch=3, grid=(n_tok // tm,),
            in_specs=[pl.BlockSpec((tm, D_MODEL), lambda i, d, ps, pn: (i, 0))],
            out_specs=pl.BlockSpec(memory_space=pl.ANY),
            scratch_shapes=[pltpu.VMEM((MOE_TM // 2, D_MODEL), F32),
                            pltpu.SemaphoreType.DMA(()), pltpu.SemaphoreType.DMA(())]),
        out_shape=jax.ShapeDtypeStruct((n_slots, D_MODEL), F32),
        compiler_params=_cparams(("arbitrary",)),
        name="moe_dispatch",
    )(dest, pad_end, pad_len, h2)


def _experts_kernel(te_ref, used_ref, xs_ref, wgu_ref, wd_ref, y_ref, wgu_bf, wd_bf):
    k = pl.program_id(0)
    new_expert = (k == 0) | (te_ref[k] != te_ref[jnp.maximum(k - 1, 0)])

    @pl.when(new_expert)
    def _():
        wgu_bf[...] = wgu_ref[0].astype(BF16)
        wd_bf[...] = wd_ref[0].astype(BF16)

    @pl.when(k < used_ref[0])
    def _():
        gu = jnp.dot(xs_ref[...].astype(BF16), wgu_bf[...], preferred_element_type=F32)
        gate, up = gu[:, :EXPERT_FF], gu[:, EXPERT_FF:]
        act = (gate * _sigmoid(gate)) * up
        y_ref[...] = jnp.dot(act.astype(BF16), wd_bf[...], preferred_element_type=F32)

    @pl.when(k >= used_ref[0])
    def _():
        y_ref[...] = jnp.zeros_like(y_ref)


def _experts(tile_expert, n_used, xs, w_gu, w_down, max_tiles):
    tm = MOE_TM
    row = lambda k, te, nu: (jnp.minimum(k, nu[0] - 1), 0)
    return pl.pallas_call(
        _experts_kernel,
        grid_spec=pltpu.PrefetchScalarGridSpec(
            num_scalar_prefetch=2, grid=(max_tiles,),
            in_specs=[pl.BlockSpec((tm, D_MODEL), row),
                      pl.BlockSpec((1, D_MODEL, 2 * EXPERT_FF), lambda k, te, nu: (te[k], 0, 0)),
                      pl.BlockSpec((1, EXPERT_FF, D_MODEL), lambda k, te, nu: (te[k], 0, 0))],
            out_specs=pl.BlockSpec((tm, D_MODEL), lambda k, te, nu: (k, 0)),
            scratch_shapes=[pltpu.VMEM((D_MODEL, 2 * EXPERT_FF), BF16), pltpu.VMEM((EXPERT_FF, D_MODEL), BF16)]),
        out_shape=jax.ShapeDtypeStruct((max_tiles * tm, D_MODEL), F32),
        compiler_params=_cparams(("arbitrary",)),
        name="moe_experts",
    )(tile_expert, n_used, xs, w_gu, w_down)


def _combine_kernel(dest_ref, y_hbm, x1_ref, route_ref, gtp_ref, gts_ref, yp_ref, ys_ref, buf, sem, *, n_prompt_tiles):
    i = pl.program_id(0)
    make = lambda r, k, d: pltpu.make_async_copy(y_hbm.at[pl.ds(d, 1)], buf.at[k, pl.ds(r, 1)], sem)
    rows = x1_ref.shape[0]
    make_all = lambda k: pltpu.make_async_copy(y_hbm.at[pl.ds(0, rows)], buf.at[k], sem)
    _pair_copies(dest_ref, i, rows, make, make_all)
    route = route_ref[...]
    lane = lax.broadcasted_iota(jnp.int32, route.shape, 1)
    w1 = jnp.sum(jnp.where(lane == ROUTE_W1, route, 0.0), axis=-1, keepdims=True)
    w2 = jnp.sum(jnp.where(lane == ROUTE_W2, route, 0.0), axis=-1, keepdims=True)
    moe = w1 * buf[0] + w2 * buf[1]

    @pl.when(i < n_prompt_tiles)
    def _():
        yp_ref[...] = x1_ref[...] + gtp_ref[...] * moe

    @pl.when(i >= n_prompt_tiles)
    def _():
        ys_ref[...] = x1_ref[...] + gts_ref[...] * moe


def _combine(dest, y_slots, x1, route, gt_p, gt_s, tp, ts):
    tm = SMALL_TOK_TILE
    npt, nst = tp // tm, ts // tm
    row = lambda i, d: (i, 0)
    return pl.pallas_call(
        functools.partial(_combine_kernel, n_prompt_tiles=npt),
        grid_spec=pltpu.PrefetchScalarGridSpec(
            num_scalar_prefetch=1, grid=(npt + nst,),
            in_specs=[pl.BlockSpec(memory_space=pl.ANY),
                      pl.BlockSpec((tm, D_MODEL), row), pl.BlockSpec((tm, LANES), row),
                      pl.BlockSpec((1, D_MODEL), lambda i, d: (0, 0)),
                      pl.BlockSpec((tm, D_MODEL), lambda i, d: (jnp.maximum(i - npt, 0), 0))],
            out_specs=[pl.BlockSpec((tm, D_MODEL), lambda i, d: (jnp.minimum(i, npt - 1), 0)),
                       pl.BlockSpec((tm, D_MODEL), lambda i, d: (jnp.maximum(i - npt, 0), 0))],
            scratch_shapes=[pltpu.VMEM((2, tm, D_MODEL), F32), pltpu.SemaphoreType.DMA(())]),
        out_shape=[jax.ShapeDtypeStruct((tp, D_MODEL), F32), jax.ShapeDtypeStruct((ts, D_MODEL), F32)],
        compiler_params=_cparams(("arbitrary",)),
        name="moe_combine",
    )(dest, y_slots, x1, route, gt_p, gt_s)


def _moe(h2, route, x1, w_gu, w_down, gt_p, gt_s, tp, ts):
    dest, pad_end, pad_len, tile_expert, n_used, max_tiles = _dispatch_plan(route, MOE_TM)
    xs = _dispatch(dest, pad_end, pad_len, h2, max_tiles * MOE_TM)
    y_slots = _experts(tile_expert, n_used, xs, w_gu, w_down, max_tiles)
    return _combine(dest, y_slots, x1, route, gt_p, gt_s, tp, ts)


def _rope_tables(n_prompt, n_seq, n_step):
    half = ROT_DIM // 2
    inv_freq = ROPE_THETA ** (-jnp.arange(half, dtype=F32) / half)
    pos = jnp.concatenate([jnp.arange(n_prompt, dtype=jnp.int32),
                           jnp.tile(PAST_LEN + jnp.arange(n_step, dtype=jnp.int32), n_seq)])
    ang = pos.astype(F32)[:, None] * inv_freq[None, :]
    cos, sin = jnp.cos(ang), jnp.sin(ang)
    n = pos.shape[0]
    one = jnp.ones((n, HEAD_DIM - ROT_DIM), F32)
    zero = jnp.zeros((n, HEAD_DIM - ROT_DIM), F32)
    zh = jnp.zeros((n, half), F32)
    rc = jnp.concatenate([cos, cos, one], axis=1)
    rs1 = jnp.concatenate([-sin, zh, zero], axis=1)
    rs2 = jnp.concatenate([zh, sin, zero], axis=1)
    return rc, rs1, rs2


def kernel(x_prompt, x_sample, cache_kv_w128, cache_kv_w512, cache_kv_w2048, state_ssm_re, state_ssm_im,
           c_prompt, c_sample, w_ada, b_ada, norm1_g, norm2_g, w_in, ssm_a_re, ssm_a_im, ssm_log_dt,
           ssm_b_re, ssm_b_im, ssm_c_re, ssm_c_im, ssm_d, w_glu, q_norm_g, k_norm_g, w_attn_br, w_out,
           w_router_group, b_router_group, w_router_expert, b_router_expert, w_expert_gate_up, w_expert_down):
    assert x_prompt.shape[0] == 1 and w_ada.shape[0] == 1
    n_prompt = x_prompt.shape[1]
    n_seq, n_step = x_sample.shape[0], x_sample.shape[1]
    n_samp = n_seq * n_step
    assert n_samp % TOK_TILE == 0 and n_prompt % ATT_SB == 0 and (n_prompt + n_samp) % PROJ_TM == 0
    x_p = x_prompt.reshape(n_prompt, D_MODEL)
    x_s = x_sample.reshape(n_samp, D_MODEL)

    pad = (-(n_seq + 1)) % 8
    c_all = jnp.concatenate([c_sample, c_prompt, jnp.zeros((pad, D_MODEL), F32)], axis=0)
    mod = _ada(c_all, w_ada[0], b_ada[0])
    mod_p = [mod[n_seq:n_seq + 1, k * D_MODEL:(k + 1) * D_MODEL] for k in range(6)]
    mod_s = [jnp.repeat(mod[:n_seq, k * D_MODEL:(k + 1) * D_MODEL], n_step, axis=0) for k in range(6)]
    sh1_p, sc1_p, gt1_p, sh2_p, sc2_p, gt2_p = mod_p
    sh1_s, sc1_s, gt1_s, sh2_s, sc2_s, gt2_s = mod_s

    h1 = _modnorm(x_p, x_s, norm1_g[0].reshape(1, D_MODEL), sc1_p, sh1_p, sc1_s, sh1_s)
    rc, rs1, rs2 = _rope_tables(n_prompt, n_seq, n_step)
    proj = _inproj(h1, w_in[0], rc, rs1, rs2, q_norm_g[0].reshape(1, HEAD_DIM), k_norm_g[0].reshape(1, HEAD_DIM))

    pw_re, pw_im, bb_re, bb_im = _ssm_prep(ssm_a_re[0], ssm_a_im[0], ssm_log_dt[0], ssm_b_re[0], ssm_b_im[0])
    b_mat, c_mat = _ssm_block_matrices(bb_re, bb_im, ssm_c_re[0], ssm_c_im[0])
    d_skip = ssm_d[0].reshape(1, SSM_WIDTH)
    yg_p, fre_p, fim_p = _s5_prompt(proj, n_prompt, d_skip, pw_re, pw_im, b_mat, c_mat)
    yg_s, fre_s, fim_s = _s5_sample(proj, n_prompt, n_seq, n_step, d_skip, pw_re, pw_im, b_mat, c_mat,
                                    state_ssm_re[0].reshape(n_seq, SSM_FLAT), state_ssm_im[0].reshape(n_seq, SSM_FLAT))

    o_p = _attn_prompt(proj, n_prompt)
    o_s = _attn_sample(proj, n_prompt, n_seq, n_step, (cache_kv_w128[0], cache_kv_w512[0], cache_kv_w2048[0]))

    mixed = _mix(yg_p, yg_s, o_p, o_s, w_glu[0], w_attn_br[0], proj)

    w_router = jnp.concatenate([w_router_group[0], w_router_expert[0],
                                jnp.zeros((D_MODEL, LANES - N_EXPERT_GROUPS - N_EXPERTS), F32)], axis=1)
    b_router = jnp.concatenate([b_router_group[0], b_router_expert[0],
                                jnp.zeros((LANES - N_EXPERT_GROUPS - N_EXPERTS,), F32)]).reshape(1, LANES)
    x1, h2, route = _outproj(mixed, w_out[0].astype(BF16), x_p, x_s, norm2_g[0].reshape(1, D_MODEL),
                             w_router, b_router, gt1_p, sc2_p, sh2_p, gt1_s, sc2_s, sh2_s)
    y_p, y_s = _moe(h2, route, x1, w_expert_gate_up[0], w_expert_down[0], gt2_p, gt2_s, n_prompt, n_samp)

    kv_p, kv_s = [], []
    for g, (window, _) in enumerate(DILATION_PATTERNS):
        kc = slice(OFF_K + g * GROUP_WIDTH, OFF_K + (g + 1) * GROUP_WIDTH)
        vc = slice(OFF_V + g * GROUP_WIDTH, OFF_V + (g + 1) * GROUP_WIDTH)
        keep = min(window, n_prompt)
        rows_p = slice(n_prompt - keep, n_prompt)
        kp = proj[rows_p, kc].reshape(1, 1, keep, 1, HEADS_PER_GROUP, HEAD_DIM)
        vp = proj[rows_p, vc].reshape(1, 1, keep, 1, HEADS_PER_GROUP, HEAD_DIM)
        kv_p.append(jnp.concatenate([kp, vp], axis=3))
        ksm = proj[n_prompt:, kc].reshape(1, n_seq, n_step, 1, HEADS_PER_GROUP, HEAD_DIM)
        vsm = proj[n_prompt:, vc].reshape(1, n_seq, n_step, 1, HEADS_PER_GROUP, HEAD_DIM)
        kv_s.append(jnp.concatenate([ksm, vsm], axis=3))

    state_shape_p = (1, 1, SSM_GROUPS, SSM_STATE)
    state_shape_s = (1, n_seq, SSM_GROUPS, SSM_STATE)
    return (y_p.reshape(1, n_prompt, D_MODEL), y_s.reshape(n_seq, n_step, D_MODEL),
            kv_p[0], kv_p[1], kv_p[2], fre_p.reshape(state_shape_p), fim_p.reshape(state_shape_p),
            kv_s[0], kv_s[1], kv_s[2], fre_s.reshape(state_shape_s), fim_s.reshape(state_shape_s))
```

```python
import functools
import math

import numpy as np
import jax
import jax.numpy as jnp
from jax import lax
from jax.experimental import pallas as pl
from jax.experimental.pallas import tpu as pltpu

F32 = jnp.float32
BF16 = jnp.bfloat16

D_MODEL = 2048
PAST_LEN = 2048
SSM_WIDTH = D_MODEL // 2
SSM_GROUP = 16
SSM_GROUPS = SSM_WIDTH // SSM_GROUP
SSM_STATE = 64
SSM_FLAT = SSM_GROUPS * SSM_STATE
HEAD_DIM = 128
DILATION_PATTERNS = ((128, 1), (512, 4), (2048, 16))
N_PATTERNS = 3
HEADS_PER_GROUP = 4
GROUP_WIDTH = HEADS_PER_GROUP * HEAD_DIM
ATTN_WIDTH = N_PATTERNS * GROUP_WIDTH
ROT_DIM = HEAD_DIM // 4
ROPE_THETA = 500000.0
OFF_Q = SSM_WIDTH
OFF_K = OFF_Q + ATTN_WIDTH
OFF_V = OFF_K + ATTN_WIDTH
OFF_G = OFF_V + ATTN_WIDTH
IN_COLS = OFF_G + 2 * D_MODEL
N_EXPERT_GROUPS = 4
EXPERTS_PER_GROUP = 4
N_EXPERTS = 16
EXPERT_FF = D_MODEL // 4
EPS = 1e-6
NEG = -1e30

LANES = 128
SUBLANES = 8
VMEM_LIMIT = 56 * 1024 * 1024

TOK_TILE = 512
SMALL_TOK_TILE = 256
PROJ_TM = 544
PROJ_TN = 2432
SCAN_L = 32
SCAN_NC = 16
SCAN_TT = SCAN_L * SCAN_NC
SSM_BLK = 8
SSM_BLK_STATE = SSM_BLK * SSM_STATE
SSM_PAR = 2
MOE_TM = 256
ROUTE_E1, ROUTE_E2, ROUTE_W1, ROUTE_W2 = 0, 1, 2, 3
ATT_SB = 2048
BAND = 128
ATT_UNROLL = 8


def _cparams(sem, vmem=VMEM_LIMIT):
    return pltpu.CompilerParams(dimension_semantics=sem, vmem_limit_bytes=vmem)


def _sigmoid(x):
    return 0.5 * jnp.tanh(0.5 * x) + 0.5


def _row_halves(rows):
    half = rows // 2
    return (slice(0, half), slice(half, rows))


def _gelu_tanh(x):
    c = math.sqrt(2.0 / math.pi)
    return 0.5 * x * (1.0 + jnp.tanh(c * (x + 0.044715 * (x * x * x))))


def _ada_kernel(c_ref, w_ref, b_ref, o_ref):
    c = c_ref[...]
    cs = (c * _sigmoid(c)).astype(BF16)
    o_ref[...] = jnp.dot(cs, w_ref[...].astype(BF16), preferred_element_type=F32) + b_ref[...]


def _ada(c_all, w_ada, b_ada):
    rows = c_all.shape[0]
    n_out = w_ada.shape[1]
    tn = 1024
    return pl.pallas_call(
        _ada_kernel,
        grid=(n_out // tn,),
        in_specs=[pl.BlockSpec((rows, D_MODEL), lambda n: (0, 0)),
                  pl.BlockSpec((D_MODEL, tn), lambda n: (0, n)),
                  pl.BlockSpec((1, tn), lambda n: (0, n))],
        out_specs=pl.BlockSpec((rows, tn), lambda n: (0, n)),
        out_shape=jax.ShapeDtypeStruct((rows, n_out), F32),
        compiler_params=_cparams(("arbitrary",)),
        name="ada_mod",
    )(c_all, w_ada, b_ada.reshape(1, n_out))


def _modnorm_kernel(xp_ref, xs_ref, g_ref, scp_ref, shp_ref, scs_ref, shs_ref, o_ref, *, n_prompt_tiles):
    is_s = pl.program_id(0) >= n_prompt_tiles

    def norm(x_ref, sc_ref, sh_ref):
        x = x_ref[...]
        ms = jnp.mean(x * x, axis=-1, keepdims=True)
        y = x * lax.rsqrt(ms + EPS) * g_ref[...]
        o_ref[...] = (y * (1.0 + sc_ref[...]) + sh_ref[...]).astype(o_ref.dtype)

    pl.when(jnp.logical_not(is_s))(functools.partial(norm, xp_ref, scp_ref, shp_ref))
    pl.when(is_s)(functools.partial(norm, xs_ref, scs_ref, shs_ref))


def _modnorm(x_p, x_s, g, sc_p, sh_p, sc_s, sh_s):
    tp, ts = x_p.shape[0], x_s.shape[0]
    tm = TOK_TILE
    npt, nst = tp // tm, ts // tm
    row = lambda i: (jnp.minimum(i, npt - 1), 0)
    srow = lambda i: (jnp.maximum(i - npt, 0), 0)
    const = lambda i: (0, 0)
    return pl.pallas_call(
        functools.partial(_modnorm_kernel, n_prompt_tiles=npt),
        grid=(npt + nst,),
        in_specs=[pl.BlockSpec((tm, D_MODEL), row), pl.BlockSpec((tm, D_MODEL), srow),
                  pl.BlockSpec((1, D_MODEL), const), pl.BlockSpec((1, D_MODEL), const),
                  pl.BlockSpec((1, D_MODEL), const), pl.BlockSpec((tm, D_MODEL), srow),
                  pl.BlockSpec((tm, D_MODEL), srow)],
        out_specs=pl.BlockSpec((tm, D_MODEL), lambda i: (i, 0)),
        out_shape=jax.ShapeDtypeStruct((tp + ts, D_MODEL), BF16),
        compiler_params=_cparams(("arbitrary",)),
        name="modnorm1",
    )(x_p, x_s, g, sc_p, sh_p, sc_s, sh_s)


def _inproj_kernel(h_ref, w_ref, rc_ref, rs1_ref, rs2_ref, qg_ref, kg_ref, o_ref, wbf_ref, *, heads_per_tile):
    n = pl.program_id(0)

    @pl.when(pl.program_id(1) == 0)
    def _():
        wbf_ref[...] = w_ref[...].astype(BF16)

    half = ROT_DIM // 2
    q_heads = range(OFF_Q // HEAD_DIM, OFF_K // HEAD_DIM)
    k_heads = range(OFF_K // HEAD_DIM, OFF_V // HEAD_DIM)
    pair = 2 * HEAD_DIM

    def tile(col_tile):
        h = h_ref[...]
        for c0 in range(0, heads_per_tile * HEAD_DIM, pair):
            width = min(pair, heads_per_tile * HEAD_DIM - c0)
            acc = jnp.dot(h, wbf_ref[:, c0:c0 + width], preferred_element_type=F32)
            for c in range(c0, c0 + width, HEAD_DIM):
                slot = col_tile * heads_per_tile + c // HEAD_DIM
                x = acc[:, c - c0:c - c0 + HEAD_DIM]
                if slot in q_heads or slot in k_heads:
                    gain = qg_ref[...] if slot in q_heads else kg_ref[...]
                    ms = jnp.mean(x * x, axis=-1, keepdims=True)
                    y = x * lax.rsqrt(ms + EPS) * gain
                    up = pltpu.roll(y, HEAD_DIM - half, 1)
                    dn = pltpu.roll(y, half, 1)
                    x = y * rc_ref[...] + up * rs1_ref[...] + dn * rs2_ref[...]
                o_ref[:, c:c + HEAD_DIM] = x

    for col_tile in range(IN_COLS // (heads_per_tile * HEAD_DIM)):
        pl.when(n == col_tile)(functools.partial(tile, col_tile))


def _inproj(h, w_in, rc, rs1, rs2, qg, kg):
    n_tok = h.shape[0]
    tm, tn = PROJ_TM, PROJ_TN
    tab = pl.BlockSpec((tm, HEAD_DIM), lambda n, m: (m, 0))
    gain = pl.BlockSpec((1, HEAD_DIM), lambda n, m: (0, 0))
    return pl.pallas_call(
        functools.partial(_inproj_kernel, heads_per_tile=tn // HEAD_DIM),
        grid=(IN_COLS // tn, n_tok // tm),
        in_specs=[pl.BlockSpec((tm, D_MODEL), lambda n, m: (m, 0)),
                  pl.BlockSpec((D_MODEL, tn), lambda n, m: (0, n), pipeline_mode=pl.Buffered(1)),
                  tab, tab, tab, gain, gain],
        out_specs=pl.BlockSpec((tm, tn), lambda n, m: (m, n)),
        out_shape=jax.ShapeDtypeStruct((n_tok, IN_COLS), F32),
        scratch_shapes=[pltpu.VMEM((D_MODEL, tn), BF16)],
        compiler_params=_cparams(("arbitrary", "arbitrary")),
        name="in_proj",
    )(h, w_in, rc, rs1, rs2, qg, kg)


def _ssm_prep_kernel(are_ref, aim_ref, ldt_ref, arer_ref, aimr_ref, ldtr_ref, bre_ref, bim_ref,
                     pre_ref, pim_ref, bbre_ref, bbim_ref):
    def discretise(a_re, a_im, log_dt):
        dt = jnp.exp(log_dt)
        mag = jnp.exp(a_re * dt)
        return mag * jnp.cos(a_im * dt), mag * jnp.sin(a_im * dt)

    ab_re, ab_im = discretise(are_ref[...], aim_ref[...], ldt_ref[...])
    p_re, p_im = ab_re, ab_im
    for i in range(SCAN_L):
        pre_ref[i:i + 1, :] = p_re
        pim_ref[i:i + 1, :] = p_im
        p_re, p_im = p_re * ab_re - p_im * ab_im, p_re * ab_im + p_im * ab_re

    a_re, a_im = arer_ref[...], aimr_ref[...]
    r_re, r_im = discretise(a_re, a_im, ldtr_ref[...])
    nr, ni = r_re - 1.0, r_im
    den = a_re * a_re + a_im * a_im
    z_re = (nr * a_re + ni * a_im) / den
    z_im = (ni * a_re - nr * a_im) / den
    b_re, b_im = bre_ref[...], bim_ref[...]
    bbre_ref[...] = z_re * b_re - z_im * b_im
    bbim_ref[...] = z_re * b_im + z_im * b_re


def _ssm_prep(a_re, a_im, log_dt, b_re, b_im):
    g, p, n = b_re.shape
    flat = lambda x: x.reshape(1, g * p)
    rep = lambda x: jnp.repeat(x, n, axis=1)
    ldt_gp = jnp.broadcast_to(log_dt[:, None], (g, p))
    ldt_rep = jnp.broadcast_to(log_dt[:, None], (g, p * n))
    out_shape = [jax.ShapeDtypeStruct((SCAN_L, g * p), F32)] * 2 + [jax.ShapeDtypeStruct((g, p * n), F32)] * 2
    return pl.pallas_call(_ssm_prep_kernel, out_shape=out_shape, name="ssm_prep")(
        flat(a_re), flat(a_im), flat(ldt_gp), rep(a_re), rep(a_im), ldt_rep,
        b_re.reshape(g, p * n), b_im.reshape(g, p * n))


def _ssm_block_matrices(bb_re, bb_im, c_re, c_im):
    g, p, n = SSM_GROUPS, SSM_STATE, SSM_GROUP
    nb = g // SSM_BLK
    eye = jnp.eye(SSM_BLK, dtype=F32)

    def in_mat(bb):
        x = bb.reshape(nb, SSM_BLK, p, n)
        return jnp.einsum('bgpm,gh->bgmhp', x, eye).reshape(nb, SSM_BLK * n, SSM_BLK * p)

    def out_mat(c):
        x = c.reshape(nb, SSM_BLK, n, p)
        return jnp.einsum('bgnp,gh->bgphn', x, eye).reshape(nb, SSM_BLK * p, SSM_BLK * n)

    b_mat = jnp.concatenate([in_mat(bb_re), in_mat(bb_im)], axis=2).astype(BF16)
    c_mat = jnp.concatenate([out_mat(c_re), -out_mat(c_im)], axis=1).astype(BF16)
    return b_mat, c_mat


def _cmul_add(a_re, a_im, s_re, s_im, b_re, b_im):
    return a_re * s_re - a_im * s_im + b_re, a_re * s_im + a_im * s_re + b_im


def _s5_prompt_kernel(*refs):
    par = SSM_PAR
    u_refs = refs[:par]
    (d_ref, pre_ref, pim_ref, bm_ref, cm_ref, y_ref, fre_ref, fim_ref,
     up_scr, bu_scr, lhs_scr, in_re_scr, in_im_scr, car_re, car_im, yn_scr) = refs[par:]
    nc, ln, w = SCAN_NC, SCAN_L, SSM_BLK_STATE

    @pl.when(pl.program_id(1) == 0)
    def _():
        car_re[...] = jnp.zeros_like(car_re)
        car_im[...] = jnp.zeros_like(car_im)

    for b in range(par):
        lanes = slice(b * w, (b + 1) * w)
        rows = lambda i: slice(i * nc, (i + 1) * nc)
        for i in range(ln):
            up_scr[b, rows(i), :] = u_refs[b][pl.ds(i, nc, stride=ln), :]
        up = up_scr[b]
        bu_scr[b] = jnp.dot(up.astype(BF16), bm_ref[b], preferred_element_type=F32)

        a_re = jnp.broadcast_to(pre_ref[0:1, lanes], (nc, w))
        a_im = jnp.broadcast_to(pim_ref[0:1, lanes], (nc, w))
        s_re = s_im = jnp.zeros((nc, w), F32)
        for i in range(ln):
            s_re, s_im = _cmul_add(a_re, a_im, s_re, s_im, bu_scr[b, rows(i), 0:w], bu_scr[b, rows(i), w:2 * w])
            bu_scr[b, rows(i), 0:w] = s_re
            bu_scr[b, rows(i), w:2 * w] = s_im

        al_re, al_im = pre_ref[ln - 1:ln, lanes], pim_ref[ln - 1:ln, lanes]
        c_re, c_im = car_re[b], car_im[b]
        for c in range(nc):
            in_re_scr[b, c:c + 1, :] = c_re
            in_im_scr[b, c:c + 1, :] = c_im
            c_re, c_im = _cmul_add(al_re, al_im, c_re, c_im, s_re[c:c + 1, :], s_im[c:c + 1, :])
        car_re[b] = c_re
        car_im[b] = c_im
        fre_ref[:, lanes] = c_re
        fim_ref[:, lanes] = c_im
        in_re, in_im = in_re_scr[b], in_im_scr[b]

        for i in range(ln):
            p_re = jnp.broadcast_to(pre_ref[i:i + 1, lanes], (nc, w))
            p_im = jnp.broadcast_to(pim_ref[i:i + 1, lanes], (nc, w))
            f_re, f_im = _cmul_add(p_re, p_im, in_re, in_im, bu_scr[b, rows(i), 0:w], bu_scr[b, rows(i), w:2 * w])
            lhs_scr[b, rows(i), 0:w] = f_re.astype(BF16)
            lhs_scr[b, rows(i), w:2 * w] = f_im.astype(BF16)

        y = (jnp.dot(lhs_scr[b], cm_ref[b], preferred_element_type=F32)
             + d_ref[:, b * LANES:(b + 1) * LANES] * up)
        for i in range(ln):
            yn_scr[b, pl.ds(i, nc, stride=ln), :] = y[rows(i), :]
        y_ref[:, b * LANES:(b + 1) * LANES] = _gelu_tanh(yn_scr[b]).astype(y_ref.dtype)


def _s5_prompt(proj, n_prompt, d_skip, pw_re, pw_im, b_mat, c_mat):
    par = SSM_PAR
    nb = SSM_GROUPS // SSM_BLK // par
    tt, w = SCAN_TT, SSM_BLK_STATE
    u_spec = lambda b: pl.BlockSpec((tt, LANES), lambda j, i: (i, par * j + b))
    return pl.pallas_call(
        _s5_prompt_kernel,
        grid=(nb, n_prompt // tt),
        in_specs=[u_spec(b) for b in range(par)] + [
            pl.BlockSpec((1, par * LANES), lambda j, i: (0, j)),
            pl.BlockSpec((SCAN_L, par * w), lambda j, i: (0, j)),
            pl.BlockSpec((SCAN_L, par * w), lambda j, i: (0, j)),
            pl.BlockSpec((par, LANES, 2 * w), lambda j, i: (j, 0, 0)),
            pl.BlockSpec((par, 2 * w, LANES), lambda j, i: (j, 0, 0))],
        out_specs=[pl.BlockSpec((tt, par * LANES), lambda j, i: (i, j)),
                   pl.BlockSpec((1, par * w), lambda j, i: (0, j)),
                   pl.BlockSpec((1, par * w), lambda j, i: (0, j))],
        out_shape=[jax.ShapeDtypeStruct((n_prompt, SSM_WIDTH), BF16),
                   jax.ShapeDtypeStruct((1, SSM_FLAT), F32),
                   jax.ShapeDtypeStruct((1, SSM_FLAT), F32)],
        scratch_shapes=[pltpu.VMEM((par, tt, LANES), F32), pltpu.VMEM((par, tt, 2 * w), F32),
                        pltpu.VMEM((par, tt, 2 * w), BF16),
                        pltpu.VMEM((par, SCAN_NC, w), F32), pltpu.VMEM((par, SCAN_NC, w), F32),
                        pltpu.VMEM((par, 1, w), F32), pltpu.VMEM((par, 1, w), F32),
                        pltpu.VMEM((par, tt, LANES), F32)],
        compiler_params=_cparams(("arbitrary", "arbitrary")),
        name="s5_prompt",
    )(*([proj] * par), d_skip, pw_re, pw_im, b_mat, c_mat)


def _s5_sample_kernel(u_ref, d_ref, pre_ref, pim_ref, bm_ref, cm_ref, s0re_ref, s0im_ref,
                      y_ref, fre_ref, fim_ref, up_scr, bu_scr, lhs_scr, yn_scr, *, n_seq, n_step):
    w = SSM_BLK_STATE
    rb = 16
    for s in range(n_step):
        up_scr[s * n_seq:(s + 1) * n_seq, :] = u_ref[pl.ds(s, n_seq, stride=n_step), :]
    up = up_scr[...]
    bu_scr[...] = jnp.dot(up.astype(BF16), bm_ref[0], preferred_element_type=F32)
    a_re = jnp.broadcast_to(pre_ref[0:1, :], (rb, w))
    a_im = jnp.broadcast_to(pim_ref[0:1, :], (rb, w))

    def seq_block(b, _):
        r0 = pl.multiple_of(b * rb, rb)
        s_re, s_im = s0re_ref[pl.ds(r0, rb), :], s0im_ref[pl.ds(r0, rb), :]
        for s in range(n_step):
            rows = pl.ds(pl.multiple_of(s * n_seq + r0, rb), rb)
            s_re, s_im = _cmul_add(a_re, a_im, s_re, s_im, bu_scr[rows, 0:w], bu_scr[rows, w:2 * w])
            lhs_scr[rows, 0:w] = s_re.astype(BF16)
            lhs_scr[rows, w:2 * w] = s_im.astype(BF16)
        fre_ref[pl.ds(r0, rb), :] = s_re
        fim_ref[pl.ds(r0, rb), :] = s_im
        return 0

    lax.fori_loop(0, n_seq // rb, seq_block, 0)
    y = jnp.dot(lhs_scr[...], cm_ref[0], preferred_element_type=F32) + d_ref[...] * up
    for s in range(n_step):
        yn_scr[pl.ds(s, n_seq, stride=n_step), :] = y[s * n_seq:(s + 1) * n_seq, :]
    y_ref[...] = _gelu_tanh(yn_scr[...]).astype(y_ref.dtype)


def _s5_sample(proj, n_prompt, n_seq, n_step, d_skip, pw_re, pw_im, b_mat, c_mat, s0_re, s0_im):
    nb = SSM_GROUPS // SSM_BLK
    rows, w = n_seq * n_step, SSM_BLK_STATE
    rblk = n_prompt // rows
    return pl.pallas_call(
        functools.partial(_s5_sample_kernel, n_seq=n_seq, n_step=n_step),
        grid=(nb,),
        in_specs=[pl.BlockSpec((rows, LANES), lambda j: (rblk, j)),
                  pl.BlockSpec((1, LANES), lambda j: (0, j)),
                  pl.BlockSpec((SCAN_L, w), lambda j: (0, j)),
                  pl.BlockSpec((SCAN_L, w), lambda j: (0, j)),
                  pl.BlockSpec((1, LANES, 2 * w), lambda j: (j, 0, 0)),
                  pl.BlockSpec((1, 2 * w, LANES), lambda j: (j, 0, 0)),
                  pl.BlockSpec((n_seq, w), lambda j: (0, j)),
                  pl.BlockSpec((n_seq, w), lambda j: (0, j))],
        out_specs=[pl.BlockSpec((rows, LANES), lambda j: (0, j)),
                   pl.BlockSpec((n_seq, w), lambda j: (0, j)),
                   pl.BlockSpec((n_seq, w), lambda j: (0, j))],
        out_shape=[jax.ShapeDtypeStruct((rows, SSM_WIDTH), BF16),
                   jax.ShapeDtypeStruct((n_seq, SSM_FLAT), F32),
                   jax.ShapeDtypeStruct((n_seq, SSM_FLAT), F32)],
        scratch_shapes=[pltpu.VMEM((rows, LANES), F32), pltpu.VMEM((rows, 2 * w), F32),
                        pltpu.VMEM((rows, 2 * w), BF16), pltpu.VMEM((rows, LANES), F32)],
        compiler_params=_cparams(("arbitrary",)),
        name="s5_sample",
    )(proj, d_skip, pw_re, pw_im, b_mat, c_mat, s0_re, s0_im)


def _attn_prompt_kernel(*refs):
    ins, o_ref, scr = refs[:15], refs[15], refs[16:]
    sb = pl.program_id(0)
    scale = HEAD_DIM ** -0.5
    qi = lax.broadcasted_iota(jnp.int32, (BAND, 2 * BAND), 0)
    kj = lax.broadcasted_iota(jnp.int32, (BAND, 2 * BAND), 1)
    dist = qi + BAND - kj
    band_ok = (dist >= 0) & (dist <= BAND)

    for g, (_, dil) in enumerate(DILATION_PATTERNS):
        q_ref, k_ref, v_ref, kp_ref, vp_ref = ins[5 * g:5 * g + 5]
        kbuf, vbuf, o_scr, m_scr, l_scr = scr[5 * g:5 * g + 5]
        pre = BAND * dil
        kbuf[0:pre, :] = kp_ref[...]
        kbuf[pre:pre + ATT_SB, :] = k_ref[...]
        vbuf[0:pre, :] = vp_ref[...]
        vbuf[pre:pre + ATT_SB, :] = v_ref[...]
        nblk = ATT_SB // pre

        def block(idx, _, dil=dil, pre=pre, nblk=nblk, q_ref=q_ref, kbuf=kbuf, vbuf=vbuf,
                  o_scr=o_scr, m_scr=m_scr, l_scr=l_scr):
            r = idx // nblk
            b = idx - r * nblk
            row0 = r + b * pre
            if dil == 1:
                q_rows = pl.ds(pl.multiple_of(row0, BAND), BAND)
                kv_rows = pl.ds(pl.multiple_of(row0, BAND), 2 * BAND)
            else:
                q_rows = pl.ds(row0, BAND, stride=dil)
                kv_rows = pl.ds(row0, 2 * BAND, stride=dil)
            q = (q_ref[q_rows, :] * scale).astype(BF16)
            kw = kbuf[kv_rows, :].astype(BF16)
            vw = vbuf[kv_rows, :].astype(BF16)
            s = lax.dot_general(q, kw, (((1,), (1,)), ((), ())), preferred_element_type=F32)
            s = jnp.where(band_ok & ((kj >= BAND) | (sb > 0) | (b > 0)), s, NEG)
            m = jnp.max(s, axis=-1, keepdims=True)
            p = jnp.exp(s - m)
            l = jnp.sum(p, axis=-1, keepdims=True)
            o = jnp.dot(p.astype(BF16), vw, preferred_element_type=F32)
            o_scr[q_rows, :] = o
            m_scr[q_rows, :] = jnp.broadcast_to(m, (BAND, HEAD_DIM))
            l_scr[q_rows, :] = jnp.broadcast_to(l, (BAND, HEAD_DIM))
            return 0

        lax.fori_loop(0, ATT_SB // BAND, block, 0, unroll=ATT_UNROLL)

    ms = [scr[5 * g + 3][...] for g in range(N_PATTERNS)]
    mx = jnp.maximum(jnp.maximum(ms[0], ms[1]), ms[2])
    num = jnp.zeros((ATT_SB, HEAD_DIM), F32)
    den = jnp.zeros((ATT_SB, HEAD_DIM), F32)
    for g in range(N_PATTERNS):
        wgt = jnp.exp(ms[g] - mx)
        num = num + wgt * scr[5 * g + 2][...]
        den = den + wgt * scr[5 * g + 4][...]
    o_ref[...] = num / den


def _attn_prompt(proj, n_prompt):
    hcol = lambda off, g, j: (off + g * GROUP_WIDTH) // HEAD_DIM + j
    in_specs, scratch = [], []
    for g, (_, dil) in enumerate(DILATION_PATTERNS):
        pre = BAND * dil
        per = ATT_SB // pre
        cur = lambda off, g=g: pl.BlockSpec((ATT_SB, HEAD_DIM), lambda sb, j: (sb, hcol(off, g, j)))
        prev = lambda off, g=g, per=per, pre=pre: pl.BlockSpec(
            (pre, HEAD_DIM), lambda sb, j: (jnp.maximum(sb * per - 1, 0), hcol(off, g, j)))
        in_specs += [cur(OFF_Q), cur(OFF_K), cur(OFF_V), prev(OFF_K), prev(OFF_V)]
        scratch += [pltpu.VMEM((pre + ATT_SB, HEAD_DIM), F32), pltpu.VMEM((pre + ATT_SB, HEAD_DIM), F32),
                    pltpu.VMEM((ATT_SB, HEAD_DIM), F32), pltpu.VMEM((ATT_SB, HEAD_DIM), F32),
                    pltpu.VMEM((ATT_SB, HEAD_DIM), F32)]
    return pl.pallas_call(
        _attn_prompt_kernel,
        grid=(n_prompt // ATT_SB, HEADS_PER_GROUP),
        in_specs=in_specs,
        out_specs=pl.BlockSpec((ATT_SB, HEAD_DIM), lambda sb, j: (sb, j)),
        out_shape=jax.ShapeDtypeStruct((n_prompt, GROUP_WIDTH), F32),
        scratch_shapes=scratch,
        compiler_params=_cparams(("arbitrary", "arbitrary")),
        name="attn_prompt",
    )(*([proj] * 15))


SEQ_PER_STEP = 2
KV_PLANES = 2 * HEADS_PER_GROUP


def _compact_pitch(n_step):
    tiles = n_step * KV_PLANES // 8
    return 8 * (tiles + 1 - tiles % 2)


def _sample_bias(n_step):
    rows = HEADS_PER_GROUP * n_step
    step = np.arange(rows) % n_step
    cache_bias, new_bias = [], []
    for (window, dil) in DILATION_PATTERNS:
        wb = min(window, PAST_LEN)
        band = window // dil
        if dil > n_step:
            res, i = np.meshgrid(np.arange(n_step), np.arange(wb // dil), indexing='ij')
            c = (i * dil + res).reshape(-1)
        else:
            c = np.arange(wb)
        delta = wb + step[:, None] - c[None, :]
        ok = (delta >= 0) & (delta % dil == 0) & (delta // dil <= band)
        cache_bias.append(np.where(ok, 0.0, NEG).astype(np.float32))
        nb = np.full((SEQ_PER_STEP, rows, LANES), NEG, np.float32)
        for a in range(SEQ_PER_STEP):
            for sp in range(n_step):
                dl = step - sp
                okn = (dl >= 0) & (dl % dil == 0) & (dl // dil <= band)
                nb[a, :, a * n_step + sp] = np.where(okn, 0.0, NEG)
        new_bias.append(nb)
    return cache_bias, new_bias


def _attn_sample_kernel(*refs, n_step):
    (q0, k0, v0, q1, k1, v1, q2, k2, v2, c0, c1, c2, cb0, cb1, cb2, nb0, nb1, nb2, o_ref) = refs
    qs, ks, vs = (q0, q1, q2), (k0, k1, k2), (v0, v1, v2)
    caches, cbias, nbias = (c0, c1, c2), (cb0, cb1, cb2), (nb0, nb1, nb2)
    rows = HEADS_PER_GROUP * n_step
    gw = GROUP_WIDTH
    scale = HEAD_DIM ** -0.5
    row_head = lax.broadcasted_iota(jnp.int32, (rows, gw), 0) // n_step
    lane_head = lax.broadcasted_iota(jnp.int32, (rows, gw), 1) // HEAD_DIM
    own_head = row_head == lane_head
    nt = (((1,), (1,)), ((), ()))
    pad = jnp.zeros((LANES - SEQ_PER_STEP * n_step, gw), F32)

    def planes(load):
        k = jnp.concatenate([load(h) for h in range(HEADS_PER_GROUP)], axis=1)
        v = jnp.concatenate([load(HEADS_PER_GROUP + h) for h in range(HEADS_PER_GROUP)], axis=1)
        return k.astype(BF16), v.astype(BF16)

    for a in range(SEQ_PER_STEP):
        pieces = []
        for g, (window, dil) in enumerate(DILATION_PATTERNS):
            q = qs[g][a * n_step:(a + 1) * n_step, :] * scale
            qbd = jnp.where(own_head, jnp.concatenate([q] * HEADS_PER_GROUP, axis=0), 0.0).astype(BF16)
            cache = caches[g]
            per_seq = cache.shape[0] // SEQ_PER_STEP
            if dil > n_step:
                pitch = cache.shape[1]
                flat = cache.reshape(cache.shape[0] * pitch, HEAD_DIM)
                kvs = [planes(lambda p, r=r: flat[pl.ds(a * per_seq * pitch + r * KV_PLANES + p, per_seq,
                                                        stride=pitch), :])
                       for r in range(n_step)]
            else:
                kvs = [planes(lambda p: cache[pl.ds(a * per_seq + p, per_seq // KV_PLANES, stride=KV_PLANES), :])]
            sc = jnp.concatenate([lax.dot_general(qbd, k, nt, preferred_element_type=F32) for k, _ in kvs], axis=1)
            pieces.append((sc + cbias[g][...], [v for _, v in kvs]))
            k_new = jnp.concatenate([ks[g][...], pad], axis=0).astype(BF16)
            v_new = jnp.concatenate([vs[g][...], pad], axis=0).astype(BF16)
            sn = lax.dot_general(qbd, k_new, nt, preferred_element_type=F32) + nbias[g][a]
            pieces.append((sn, [v_new]))
        m = functools.reduce(jnp.maximum, [jnp.max(s, axis=-1, keepdims=True) for s, _ in pieces])
        l = jnp.zeros((rows, 1), F32)
        acc = jnp.zeros((rows, gw), F32)
        for s, vals in pieces:
            p = jnp.exp(s - m)
            l = l + jnp.sum(p, axis=-1, keepdims=True)
            pb = p.astype(BF16)
            nk = pb.shape[1] // len(vals)
            for r, v in enumerate(vals):
                acc = acc + jnp.dot(pb[:, r * nk:(r + 1) * nk], v, preferred_element_type=F32)
        acc = jnp.where(own_head, acc, 0.0)
        o16 = functools.reduce(lambda x, y: x + y,
                               [acc[:, h * HEAD_DIM:(h + 1) * HEAD_DIM] for h in range(HEADS_PER_GROUP)]) / l
        for h in range(HEADS_PER_GROUP):
            o_ref[a * n_step:(a + 1) * n_step, h * HEAD_DIM:(h + 1) * HEAD_DIM] = o16[h * n_step:(h + 1) * n_step, :]


def _attn_sample(proj, n_prompt, n_seq, n_step, caches):
    rows = SEQ_PER_STEP * n_step
    rblk = n_prompt // rows
    cache_bias, new_bias = _sample_bias(n_step)
    tok = lambda off, g: pl.BlockSpec((rows, GROUP_WIDTH), lambda i: (rblk + i, (off + g * GROUP_WIDTH) // GROUP_WIDTH))
    in_specs, args = [], []
    for g in range(N_PATTERNS):
        in_specs += [tok(OFF_Q, g), tok(OFF_K, g), tok(OFF_V, g)]
        args += [proj, proj, proj]
    for g, (window, dil) in enumerate(DILATION_PATTERNS):
        c = caches[g]
        wb = c.shape[1]
        if dil > n_step:
            pitch = _compact_pitch(n_step)
            assert pitch <= dil * KV_PLANES and wb % dil == 0
            c = c.reshape(n_seq * (wb // dil), dil * KV_PLANES, HEAD_DIM)
            in_specs.append(pl.BlockSpec((SEQ_PER_STEP * (wb // dil), pitch, HEAD_DIM), lambda i: (i, 0, 0)))
        else:
            c = c.reshape(n_seq * wb * KV_PLANES, HEAD_DIM)
            in_specs.append(pl.BlockSpec((SEQ_PER_STEP * wb * KV_PLANES, HEAD_DIM), lambda i: (i, 0)))
        args.append(c)
    for b in cache_bias:
        in_specs.append(pl.BlockSpec(b.shape, lambda i: (0, 0)))
        args.append(jnp.asarray(b))
    for b in new_bias:
        in_specs.append(pl.BlockSpec(b.shape, lambda i: (0, 0, 0)))
        args.append(jnp.asarray(b))
    return pl.pallas_call(
        functools.partial(_attn_sample_kernel, n_step=n_step),
        grid=(n_seq // SEQ_PER_STEP,),
        in_specs=in_specs,
        out_specs=pl.BlockSpec((rows, GROUP_WIDTH), lambda i: (i, 0)),
        out_shape=jax.ShapeDtypeStruct((n_seq * n_step, GROUP_WIDTH), F32),
        compiler_params=_cparams(("arbitrary",)),
        name="attn_sample",
    )(*args)


def _mix_kernel(yp_ref, ys_ref, op_ref, os_ref, wa_ref, wb_ref, wbr_ref, ga_ref, gb_ref, o_ref,
                wa_bf, wb_bf, wbr_bf, *, n_prompt_tiles):
    i = pl.program_id(1)

    @pl.when(i == 0)
    def _():
        wa_bf[...] = wa_ref[...].astype(BF16)
        wb_bf[...] = wb_ref[...].astype(BF16)
        wbr_bf[...] = wbr_ref[...].astype(BF16)

    is_s = i >= n_prompt_tiles
    for rows in _row_halves(o_ref.shape[0]):
        y = jnp.where(is_s, ys_ref[rows, :], yp_ref[rows, :])
        o = jnp.where(is_s, os_ref[rows, :], op_ref[rows, :]).astype(BF16)
        glu_a = jnp.dot(y, wa_bf[...], preferred_element_type=F32)
        glu_b = jnp.dot(y, wb_bf[...], preferred_element_type=F32)
        branch_a = glu_a * _sigmoid(glu_b)
        branch_b = jnp.dot(o, wbr_bf[...], preferred_element_type=F32)
        o_ref[rows, :] = (_sigmoid(ga_ref[rows, :]) * branch_a
                          + _sigmoid(gb_ref[rows, :]) * branch_b).astype(o_ref.dtype)


def _mix(y_p, y_s, o_p, o_s, w_glu, w_attn_br, proj):
    tp, ts = y_p.shape[0], y_s.shape[0]
    tm, tn = TOK_TILE, 512
    npt, nst = tp // tm, ts // tm
    ncol = D_MODEL // tn
    prow = lambda n, i: (jnp.minimum(i, npt - 1), 0)
    srow = lambda n, i: (jnp.maximum(i - npt, 0), 0)
    return pl.pallas_call(
        functools.partial(_mix_kernel, n_prompt_tiles=npt),
        grid=(ncol, npt + nst),
        in_specs=[pl.BlockSpec((tm, SSM_WIDTH), prow), pl.BlockSpec((tm, SSM_WIDTH), srow),
                  pl.BlockSpec((tm, GROUP_WIDTH), prow), pl.BlockSpec((tm, GROUP_WIDTH), srow),
                  pl.BlockSpec((SSM_WIDTH, tn), lambda n, i: (0, n)),
                  pl.BlockSpec((SSM_WIDTH, tn), lambda n, i: (0, ncol + n)),
                  pl.BlockSpec((GROUP_WIDTH, tn), lambda n, i: (0, n)),
                  pl.BlockSpec((tm, tn), lambda n, i: (i, OFF_G // tn + n)),
                  pl.BlockSpec((tm, tn), lambda n, i: (i, OFF_G // tn + ncol + n))],
        out_specs=pl.BlockSpec((tm, tn), lambda n, i: (i, n)),
        out_shape=jax.ShapeDtypeStruct((tp + ts, D_MODEL), BF16),
        scratch_shapes=[pltpu.VMEM((SSM_WIDTH, tn), BF16), pltpu.VMEM((SSM_WIDTH, tn), BF16),
                        pltpu.VMEM((GROUP_WIDTH, tn), BF16)],
        compiler_params=_cparams(("arbitrary", "arbitrary")),
        name="glu_mix",
    )(y_p, y_s, o_p, o_s, w_glu, w_glu, w_attn_br, proj, proj)


def _route(logits):
    lane = lax.broadcasted_iota(jnp.int32, logits.shape, 1).astype(F32)
    big = 1000.0
    first = lambda cond: jnp.min(jnp.where(cond, lane, big), axis=-1, keepdims=True)
    is_g = lane < N_EXPERT_GROUPS
    lg = jnp.where(is_g, logits, NEG)
    mg = jnp.max(lg, axis=-1, keepdims=True)
    g_sel = first(lg == mg)
    p_group = 1.0 / jnp.sum(jnp.where(is_g, jnp.exp(lg - mg), 0.0), axis=-1, keepdims=True)
    e_lo = N_EXPERT_GROUPS + EXPERTS_PER_GROUP * g_sel
    le = jnp.where((lane >= e_lo) & (lane < e_lo + EXPERTS_PER_GROUP), logits, NEG)
    v1 = jnp.max(le, axis=-1, keepdims=True)
    i1 = first(le == v1)
    le2 = jnp.where(lane == i1, NEG, le)
    v2 = jnp.max(le2, axis=-1, keepdims=True)
    i2 = first(le2 == v2)
    e2 = jnp.exp(v2 - v1)
    w1 = p_group / (1.0 + e2)
    w2 = p_group * e2 / (1.0 + e2)
    pick = lambda k, val: jnp.where(lane == k, val, 0.0)
    return (pick(ROUTE_E1, i1 - N_EXPERT_GROUPS) + pick(ROUTE_E2, i2 - N_EXPERT_GROUPS)
            + pick(ROUTE_W1, w1) + pick(ROUTE_W2, w2))


def _outproj_kernel(mix_ref, w_ref, xp_ref, xs_ref, g_ref, wr_ref, br_ref,
                    gtp_ref, scp_ref, shp_ref, gts_ref, scs_ref, shs_ref,
                    x1_ref, h2_ref, route_ref, *, n_prompt_tiles):
    is_s = pl.program_id(0) >= n_prompt_tiles

    def split(v):
        high = v.astype(BF16)
        return high, (v - high.astype(F32)).astype(BF16)

    x = jnp.where(is_s, xs_ref[...], xp_ref[...])
    gt = jnp.where(is_s, gts_ref[...], gtp_ref[...])
    sc = jnp.where(is_s, scs_ref[...], scp_ref[...])
    sh = jnp.where(is_s, shs_ref[...], shp_ref[...])
    x1 = x + gt * jnp.dot(mix_ref[...], w_ref[...], preferred_element_type=F32)
    x1_ref[...] = x1
    ms = jnp.mean(x1 * x1, axis=-1, keepdims=True)
    h2 = (x1 * lax.rsqrt(ms + EPS) * g_ref[...]) * (1.0 + sc) + sh
    h2_ref[...] = h2.astype(h2_ref.dtype)
    r = jnp.dot(jnp.concatenate(split(h2), axis=0), jnp.concatenate(split(wr_ref[...]), axis=1),
                preferred_element_type=F32)
    n = h2.shape[0]
    logits = (r[:n, :LANES] + r[:n, LANES:]) + (r[n:, :LANES] + r[n:, LANES:]) + br_ref[...]
    route_ref[...] = _route(logits)


def _outproj(mixed, w_out_bf, x_p, x_s, g2, w_router, b_router, gt_p, sc_p, sh_p, gt_s, sc_s, sh_s):
    tp, ts = x_p.shape[0], x_s.shape[0]
    tm = SMALL_TOK_TILE
    npt, nst = tp // tm, ts // tm
    prow = lambda i: (jnp.minimum(i, npt - 1), 0)
    srow = lambda i: (jnp.maximum(i - npt, 0), 0)
    const = lambda i: (0, 0)
    vec = pl.BlockSpec((1, D_MODEL), const)
    svec = pl.BlockSpec((tm, D_MODEL), srow)
    full = lambda i: (i, 0)
    return pl.pallas_call(
        functools.partial(_outproj_kernel, n_prompt_tiles=npt),
        grid=(npt + nst,),
        in_specs=[pl.BlockSpec((tm, D_MODEL), full), pl.BlockSpec((D_MODEL, D_MODEL), const),
                  pl.BlockSpec((tm, D_MODEL), prow), pl.BlockSpec((tm, D_MODEL), srow),
                  vec, pl.BlockSpec((D_MODEL, LANES), const), pl.BlockSpec((1, LANES), const),
                  vec, vec, vec, svec, svec, svec],
        out_specs=[pl.BlockSpec((tm, D_MODEL), full), pl.BlockSpec((tm, D_MODEL), full),
                   pl.BlockSpec((tm, LANES), full)],
        out_shape=[jax.ShapeDtypeStruct((tp + ts, D_MODEL), F32),
                   jax.ShapeDtypeStruct((tp + ts, D_MODEL), F32),
                   jax.ShapeDtypeStruct((tp + ts, LANES), F32)],
        compiler_params=_cparams(("arbitrary",)),
        name="out_proj_norm2_router",
    )(mixed, w_out_bf, x_p, x_s, g2, w_router, b_router, gt_p, sc_p, sh_p, gt_s, sc_s, sh_s)


def _dispatch_plan(route, tm):
    e = route[:, ROUTE_E1:ROUTE_E2 + 1].astype(jnp.int32).reshape(-1)
    n_pairs = e.shape[0]
    onehot = (e[:, None] == jnp.arange(N_EXPERTS, dtype=jnp.int32)[None, :]).astype(jnp.int32)
    csum = jnp.cumsum(onehot, axis=0)
    rank = jnp.sum(onehot * csum, axis=1) - 1
    tiles_per_expert = (csum[-1] + tm - 1) // tm
    tile_end = jnp.cumsum(tiles_per_expert)
    tile_start = tile_end - tiles_per_expert
    dest = (tile_start[e] * tm + rank).astype(jnp.int32)
    max_tiles = n_pairs // tm + N_EXPERTS
    k = jnp.arange(max_tiles, dtype=jnp.int32)
    tile_expert = jnp.minimum(jnp.sum((k[:, None] >= tile_end[None, :]).astype(jnp.int32), axis=1), N_EXPERTS - 1)
    n_used = tile_end[-1].astype(jnp.int32)
    last_expert = jnp.take(tile_expert, n_used - 1)
    tile_expert = jnp.where(k < n_used, tile_expert, last_expert).astype(jnp.int32)
    pad_end = (jnp.concatenate([tile_end, tile_end[-1:]]) * tm).astype(jnp.int32)
    pad_len = (tiles_per_expert * tm - csum[-1]).astype(jnp.int32)
    return dest, pad_end, pad_len, tile_expert, n_used.reshape(1), max_tiles


def _start_pair_copies(dest_ref, tile, rows, make):
    def body(r, _):
        for k in range(2):
            make(r, k, dest_ref[(tile * rows + r) * 2 + k]).start()
        return 0
    lax.fori_loop(0, rows, body, 0, unroll=8)


def _pair_copies(dest_ref, tile, rows, make, make_all):
    _start_pair_copies(dest_ref, tile, rows, make)
    for k in range(2):
        make_all(k).wait()


def _dispatch_kernel(dest_ref, pad_end_ref, pad_len_ref, h_ref, xs_ref, zero_buf, sem, pad_sem):
    rows = h_ref.shape[0]

    @pl.when(pl.program_id(0) == 0)
    def _():
        zero_buf[...] = jnp.zeros_like(zero_buf)

        def pad_copies(fn):
            for e in range(N_EXPERTS):
                end, left = pad_end_ref[e], pad_len_ref[e]
                size = zero_buf.shape[0]
                while size >= SUBLANES:
                    take = (left & size) != 0
                    end = end - jnp.where(take, size, 0)

                    @pl.when(take)
                    def _(start=end, size=size):
                        fn(pltpu.make_async_copy(zero_buf.at[pl.ds(0, size)],
                                                 xs_ref.at[pl.ds(pl.multiple_of(start, size), size)], pad_sem))

                    size //= 2
                for r in range(1, SUBLANES):
                    @pl.when((left & (SUBLANES - 1)) >= r)
                    def _(row=end - r):
                        fn(pltpu.make_async_copy(zero_buf.at[pl.ds(0, 1)], xs_ref.at[pl.ds(row, 1)], pad_sem))
            size = zero_buf.shape[0]
            tail = pad_end_ref[N_EXPERTS]
            for t in range(N_EXPERTS * MOE_TM // size):
                @pl.when(tail + t * size < xs_ref.shape[0])
                def _(start=tail + t * size):
                    fn(pltpu.make_async_copy(zero_buf, xs_ref.at[pl.ds(pl.multiple_of(start, size), size)], pad_sem))

        pad_copies(lambda c: c.start())
        pad_copies(lambda c: c.wait())

    make = lambda r, k, d: pltpu.make_async_copy(h_ref.at[pl.ds(r, 1)], xs_ref.at[pl.ds(d, 1)], sem)
    make_all = lambda k: pltpu.make_async_copy(h_ref, xs_ref.at[pl.ds(0, rows)], sem)
    _pair_copies(dest_ref, pl.program_id(0), rows, make, make_all)


def _dispatch(dest, pad_end, pad_len, h2, n_slots):
    n_tok = h2.shape[0]
    tm = SMALL_TOK_TILE
    return pl.pallas_call(
        _dispatch_kernel,
        grid_spec=pltpu.PrefetchScalarGridSpec(
            num_scalar_prefetch=3, grid=(n_tok // tm,),
            in_specs=[pl.BlockSpec((tm, D_MODEL), lambda i, d, ps, pn: (i, 0))],
            out_specs=pl.BlockSpec(memory_space=pl.ANY),
            scratch_shapes=[pltpu.VMEM((MOE_TM // 2, D_MODEL), F32),
                            pltpu.SemaphoreType.DMA(()), pltpu.SemaphoreType.DMA(())]),
        out_shape=jax.ShapeDtypeStruct((n_slots, D_MODEL), F32),
        compiler_params=_cparams(("arbitrary",)),
        name="moe_dispatch",
    )(dest, pad_end, pad_len, h2)


def _experts_kernel(te_ref, used_ref, xs_ref, wgu_ref, wd_ref, y_ref, wgu_bf, wd_bf):
    k = pl.program_id(0)
    new_expert = (k == 0) | (te_ref[k] != te_ref[jnp.maximum(k - 1, 0)])

    @pl.when(new_expert)
    def _():
        wgu_bf[...] = wgu_ref[0].astype(BF16)
        wd_bf[...] = wd_ref[0].astype(BF16)

    @pl.when(k < used_ref[0])
    def _():
        gu = jnp.dot(xs_ref[...].astype(BF16), wgu_bf[...], preferred_element_type=F32)
        gate, up = gu[:, :EXPERT_FF], gu[:, EXPERT_FF:]
        act = (gate * _sigmoid(gate)) * up
        y_ref[...] = jnp.dot(act.astype(BF16), wd_bf[...], preferred_element_type=F32)

    @pl.when(k >= used_ref[0])
    def _():
        y_ref[...] = jnp.zeros_like(y_ref)


def _experts(tile_expert, n_used, xs, w_gu, w_down, max_tiles):
    tm = MOE_TM
    row = lambda k, te, nu: (jnp.minimum(k, nu[0] - 1), 0)
    return pl.pallas_call(
        _experts_kernel,
        grid_spec=pltpu.PrefetchScalarGridSpec(
            num_scalar_prefetch=2, grid=(max_tiles,),
            in_specs=[pl.BlockSpec((tm, D_MODEL), row),
                      pl.BlockSpec((1, D_MODEL, 2 * EXPERT_FF), lambda k, te, nu: (te[k], 0, 0)),
                      pl.BlockSpec((1, EXPERT_FF, D_MODEL), lambda k, te, nu: (te[k], 0, 0))],
            out_specs=pl.BlockSpec((tm, D_MODEL), lambda k, te, nu: (k, 0)),
            scratch_shapes=[pltpu.VMEM((D_MODEL, 2 * EXPERT_FF), BF16), pltpu.VMEM((EXPERT_FF, D_MODEL), BF16)]),
        out_shape=jax.ShapeDtypeStruct((max_tiles * tm, D_MODEL), F32),
        compiler_params=_cparams(("arbitrary",)),
        name="moe_experts",
    )(tile_expert, n_used, xs, w_gu, w_down)


def _combine_kernel(dest_ref, y_hbm, x1_ref, route_ref, gtp_ref, gts_ref, yp_ref, ys_ref, buf, sem, *, n_prompt_tiles):
    i = pl.program_id(0)
    rows = x1_ref.shape[0]

    def gather(tile, slot):
        make = lambda r, k, d: pltpu.make_async_copy(y_hbm.at[pl.ds(d, 1)], buf.at[slot, k, pl.ds(r, 1)],
                                                     sem.at[slot])
        _start_pair_copies(dest_ref, tile, rows, make)

    @pl.when(i == 0)
    def _():
        gather(0, 0)

    slot = lax.rem(i, 2)

    @pl.when(i + 1 < pl.num_programs(0))
    def _():
        gather(i + 1, 1 - slot)

    for k in range(2):
        pltpu.make_async_copy(y_hbm.at[pl.ds(0, rows)], buf.at[slot, k], sem.at[slot]).wait()
    route = route_ref[...]
    lane = lax.broadcasted_iota(jnp.int32, route.shape, 1)
    w1 = jnp.sum(jnp.where(lane == ROUTE_W1, route, 0.0), axis=-1, keepdims=True)
    w2 = jnp.sum(jnp.where(lane == ROUTE_W2, route, 0.0), axis=-1, keepdims=True)
    moe = w1 * buf[slot, 0] + w2 * buf[slot, 1]

    @pl.when(i < n_prompt_tiles)
    def _():
        yp_ref[...] = x1_ref[...] + gtp_ref[...] * moe

    @pl.when(i >= n_prompt_tiles)
    def _():
        ys_ref[...] = x1_ref[...] + gts_ref[...] * moe


def _combine(dest, y_slots, x1, route, gt_p, gt_s, tp, ts):
    tm = SMALL_TOK_TILE
    npt, nst = tp // tm, ts // tm
    row = lambda i, d: (i, 0)
    return pl.pallas_call(
        functools.partial(_combine_kernel, n_prompt_tiles=npt),
        grid_spec=pltpu.PrefetchScalarGridSpec(
            num_scalar_prefetch=1, grid=(npt + nst,),
            in_specs=[pl.BlockSpec(memory_space=pl.ANY),
                      pl.BlockSpec((tm, D_MODEL), row), pl.BlockSpec((tm, LANES), row),
                      pl.BlockSpec((1, D_MODEL), lambda i, d: (0, 0)),
                      pl.BlockSpec((tm, D_MODEL), lambda i, d: (jnp.maximum(i - npt, 0), 0))],
            out_specs=[pl.BlockSpec((tm, D_MODEL), lambda i, d: (jnp.minimum(i, npt - 1), 0)),
                       pl.BlockSpec((tm, D_MODEL), lambda i, d: (jnp.maximum(i - npt, 0), 0))],
            scratch_shapes=[pltpu.VMEM((2, 2, tm, D_MODEL), F32), pltpu.SemaphoreType.DMA((2,))]),
        out_shape=[jax.ShapeDtypeStruct((tp, D_MODEL), F32), jax.ShapeDtypeStruct((ts, D_MODEL), F32)],
        compiler_params=_cparams(("arbitrary",)),
        name="moe_combine",
    )(dest, y_slots, x1, route, gt_p, gt_s)


def _moe(h2, route, x1, w_gu, w_down, gt_p, gt_s, tp, ts):
    dest, pad_end, pad_len, tile_expert, n_used, max_tiles = _dispatch_plan(route, MOE_TM)
    xs = _dispatch(dest, pad_end, pad_len, h2, max_tiles * MOE_TM)
    y_slots = _experts(tile_expert, n_used, xs, w_gu, w_down, max_tiles)
    return _combine(dest, y_slots, x1, route, gt_p, gt_s, tp, ts)


def _rope_tables(n_prompt, n_seq, n_step):
    half = ROT_DIM // 2
    inv_freq = ROPE_THETA ** (-jnp.arange(half, dtype=F32) / half)
    pos = jnp.concatenate([jnp.arange(n_prompt, dtype=jnp.int32),
                           jnp.tile(PAST_LEN + jnp.arange(n_step, dtype=jnp.int32), n_seq)])
    ang = pos.astype(F32)[:, None] * inv_freq[None, :]
    cos, sin = jnp.cos(ang), jnp.sin(ang)
    n = pos.shape[0]
    one = jnp.ones((n, HEAD_DIM - ROT_DIM), F32)
    zero = jnp.zeros((n, HEAD_DIM - ROT_DIM), F32)
    zh = jnp.zeros((n, half), F32)
    rc = jnp.concatenate([cos, cos, one], axis=1)
    rs1 = jnp.concatenate([-sin, zh, zero], axis=1)
    rs2 = jnp.concatenate([zh, sin, zero], axis=1)
    return rc, rs1, rs2


def kernel(x_prompt, x_sample, cache_kv_w128, cache_kv_w512, cache_kv_w2048, state_ssm_re, state_ssm_im,
           c_prompt, c_sample, w_ada, b_ada, norm1_g, norm2_g, w_in, ssm_a_re, ssm_a_im, ssm_log_dt,
           ssm_b_re, ssm_b_im, ssm_c_re, ssm_c_im, ssm_d, w_glu, q_norm_g, k_norm_g, w_attn_br, w_out,
           w_router_group, b_router_group, w_router_expert, b_router_expert, w_expert_gate_up, w_expert_down):
    assert x_prompt.shape[0] == 1 and w_ada.shape[0] == 1
    n_prompt = x_prompt.shape[1]
    n_seq, n_step = x_sample.shape[0], x_sample.shape[1]
    n_samp = n_seq * n_step
    assert n_samp % TOK_TILE == 0 and n_prompt % ATT_SB == 0 and (n_prompt + n_samp) % PROJ_TM == 0
    x_p = x_prompt.reshape(n_prompt, D_MODEL)
    x_s = x_sample.reshape(n_samp, D_MODEL)

    pad = (-(n_seq + 1)) % 8
    c_all = jnp.concatenate([c_sample, c_prompt, jnp.zeros((pad, D_MODEL), F32)], axis=0)
    mod = _ada(c_all, w_ada[0], b_ada[0])
    mod_p = [mod[n_seq:n_seq + 1, k * D_MODEL:(k + 1) * D_MODEL] for k in range(6)]
    mod_s = [jnp.repeat(mod[:n_seq, k * D_MODEL:(k + 1) * D_MODEL], n_step, axis=0) for k in range(6)]
    sh1_p, sc1_p, gt1_p, sh2_p, sc2_p, gt2_p = mod_p
    sh1_s, sc1_s, gt1_s, sh2_s, sc2_s, gt2_s = mod_s

    h1 = _modnorm(x_p, x_s, norm1_g[0].reshape(1, D_MODEL), sc1_p, sh1_p, sc1_s, sh1_s)
    rc, rs1, rs2 = _rope_tables(n_prompt, n_seq, n_step)
    proj = _inproj(h1, w_in[0], rc, rs1, rs2, q_norm_g[0].reshape(1, HEAD_DIM), k_norm_g[0].reshape(1, HEAD_DIM))

    pw_re, pw_im, bb_re, bb_im = _ssm_prep(ssm_a_re[0], ssm_a_im[0], ssm_log_dt[0], ssm_b_re[0], ssm_b_im[0])
    b_mat, c_mat = _ssm_block_matrices(bb_re, bb_im, ssm_c_re[0], ssm_c_im[0])
    d_skip = ssm_d[0].reshape(1, SSM_WIDTH)
    yg_p, fre_p, fim_p = _s5_prompt(proj, n_prompt, d_skip, pw_re, pw_im, b_mat, c_mat)
    yg_s, fre_s, fim_s = _s5_sample(proj, n_prompt, n_seq, n_step, d_skip, pw_re, pw_im, b_mat, c_mat,
                                    state_ssm_re[0].reshape(n_seq, SSM_FLAT), state_ssm_im[0].reshape(n_seq, SSM_FLAT))

    o_p = _attn_prompt(proj, n_prompt)
    o_s = _attn_sample(proj, n_prompt, n_seq, n_step, (cache_kv_w128[0], cache_kv_w512[0], cache_kv_w2048[0]))

    mixed = _mix(yg_p, yg_s, o_p, o_s, w_glu[0], w_attn_br[0], proj)

    w_router = jnp.concatenate([w_router_group[0], w_router_expert[0],
                                jnp.zeros((D_MODEL, LANES - N_EXPERT_GROUPS - N_EXPERTS), F32)], axis=1)
    b_router = jnp.concatenate([b_router_group[0], b_router_expert[0],
                                jnp.zeros((LANES - N_EXPERT_GROUPS - N_EXPERTS,), F32)]).reshape(1, LANES)
    x1, h2, route = _outproj(mixed, w_out[0].astype(BF16), x_p, x_s, norm2_g[0].reshape(1, D_MODEL),
                             w_router, b_router, gt1_p, sc2_p, sh2_p, gt1_s, sc2_s, sh2_s)
    y_p, y_s = _moe(h2, route, x1, w_expert_gate_up[0], w_expert_down[0], gt2_p, gt2_s, n_prompt, n_samp)

    kv_p, kv_s = [], []
    for g, (window, _) in enumerate(DILATION_PATTERNS):
        kc = slice(OFF_K + g * GROUP_WIDTH, OFF_K + (g + 1) * GROUP_WIDTH)
        vc = slice(OFF_V + g * GROUP_WIDTH, OFF_V + (g + 1) * GROUP_WIDTH)
        keep = min(window, n_prompt)
        rows_p = slice(n_prompt - keep, n_prompt)
        kp = proj[rows_p, kc].reshape(1, 1, keep, 1, HEADS_PER_GROUP, HEAD_DIM)
        vp = proj[rows_p, vc].reshape(1, 1, keep, 1, HEADS_PER_GROUP, HEAD_DIM)
        kv_p.append(jnp.concatenate([kp, vp], axis=3))
        ksm = proj[n_prompt:, kc].reshape(1, n_seq, n_step, 1, HEADS_PER_GROUP, HEAD_DIM)
        vsm = proj[n_prompt:, vc].reshape(1, n_seq, n_step, 1, HEADS_PER_GROUP, HEAD_DIM)
        kv_s.append(jnp.concatenate([ksm, vsm], axis=3))

    state_shape_p = (1, 1, SSM_GROUPS, SSM_STATE)
    state_shape_s = (1, n_seq, SSM_GROUPS, SSM_STATE)
    return (y_p.reshape(1, n_prompt, D_MODEL), y_s.reshape(n_seq, n_step, D_MODEL),
            kv_p[0], kv_p[1], kv_p[2], fre_p.reshape(state_shape_p), fim_p.reshape(state_shape_p),
            kv_s[0], kv_s[1], kv_s[2], fre_s.reshape(state_shape_s), fim_s.reshape(state_shape_s))
```

```python
import functools
import math

import numpy as np
import jax
import jax.numpy as jnp
from jax import lax
from jax.experimental import pallas as pl
from jax.experimental.pallas import tpu as pltpu

F32 = jnp.float32
BF16 = jnp.bfloat16

D_MODEL = 2048
PAST_LEN = 2048
SSM_WIDTH = D_MODEL // 2
SSM_GROUP = 16
SSM_GROUPS = SSM_WIDTH // SSM_GROUP
SSM_STATE = 64
SSM_FLAT = SSM_GROUPS * SSM_STATE
HEAD_DIM = 128
DILATION_PATTERNS = ((128, 1), (512, 4), (2048, 16))
N_PATTERNS = 3
HEADS_PER_GROUP = 4
GROUP_WIDTH = HEADS_PER_GROUP * HEAD_DIM
ATTN_WIDTH = N_PATTERNS * GROUP_WIDTH
ROT_DIM = HEAD_DIM // 4
ROPE_THETA = 500000.0
OFF_Q = SSM_WIDTH
OFF_K = OFF_Q + ATTN_WIDTH
OFF_V = OFF_K + ATTN_WIDTH
OFF_G = OFF_V + ATTN_WIDTH
IN_COLS = OFF_G + 2 * D_MODEL
N_EXPERT_GROUPS = 4
EXPERTS_PER_GROUP = 4
N_EXPERTS = 16
EXPERT_FF = D_MODEL // 4
EPS = 1e-6
NEG = -1e30

LANES = 128
SUBLANES = 8
VMEM_LIMIT = 56 * 1024 * 1024

TOK_TILE = 512
SMALL_TOK_TILE = 256
PROJ_TM = 544
PROJ_TN = 2432
SCAN_L = 32
SCAN_NC = 16
SCAN_TT = SCAN_L * SCAN_NC
SSM_BLK = 8
SSM_BLK_STATE = SSM_BLK * SSM_STATE
SSM_PAR = 4
MOE_TM = 512
ROUTE_E1, ROUTE_E2, ROUTE_W1, ROUTE_W2 = 0, 1, 2, 3
ATT_SB = 2048
BAND = 128
ATT_UNROLL = 16


def _cparams(sem, vmem=VMEM_LIMIT):
    return pltpu.CompilerParams(dimension_semantics=sem, vmem_limit_bytes=vmem)


def _sigmoid(x):
    return 0.5 * jnp.tanh(0.5 * x) + 0.5


def _row_halves(rows):
    half = rows // 2
    return (slice(0, half), slice(half, rows))


def _gelu_tanh(x):
    c = math.sqrt(2.0 / math.pi)
    return 0.5 * x * (1.0 + jnp.tanh(c * (x + 0.044715 * (x * x * x))))


def _ada_kernel(c_ref, w_ref, b_ref, o_ref):
    c = c_ref[...]
    cs = (c * _sigmoid(c)).astype(BF16)
    o_ref[...] = jnp.dot(cs, w_ref[...].astype(BF16), preferred_element_type=F32) + b_ref[...]


def _ada(c_all, w_ada, b_ada):
    rows = c_all.shape[0]
    n_out = w_ada.shape[1]
    tn = 1024
    return pl.pallas_call(
        _ada_kernel,
        grid=(n_out // tn,),
        in_specs=[pl.BlockSpec((rows, D_MODEL), lambda n: (0, 0)),
                  pl.BlockSpec((D_MODEL, tn), lambda n: (0, n)),
                  pl.BlockSpec((1, tn), lambda n: (0, n))],
        out_specs=pl.BlockSpec((rows, tn), lambda n: (0, n)),
        out_shape=jax.ShapeDtypeStruct((rows, n_out), F32),
        compiler_params=_cparams(("arbitrary",)),
        name="ada_mod",
    )(c_all, w_ada, b_ada.reshape(1, n_out))


def _modnorm_kernel(xp_ref, xs_ref, g_ref, scp_ref, shp_ref, scs_ref, shs_ref, o_ref, *, n_prompt_tiles):
    is_s = pl.program_id(0) >= n_prompt_tiles

    def norm(x_ref, sc_ref, sh_ref):
        x = x_ref[...]
        ms = jnp.mean(x * x, axis=-1, keepdims=True)
        y = x * lax.rsqrt(ms + EPS) * g_ref[...]
        o_ref[...] = (y * (1.0 + sc_ref[...]) + sh_ref[...]).astype(o_ref.dtype)

    pl.when(jnp.logical_not(is_s))(functools.partial(norm, xp_ref, scp_ref, shp_ref))
    pl.when(is_s)(functools.partial(norm, xs_ref, scs_ref, shs_ref))


def _modnorm(x_p, x_s, g, sc_p, sh_p, sc_s, sh_s):
    tp, ts = x_p.shape[0], x_s.shape[0]
    tm = TOK_TILE
    npt, nst = tp // tm, ts // tm
    row = lambda i: (jnp.minimum(i, npt - 1), 0)
    srow = lambda i: (jnp.maximum(i - npt, 0), 0)
    const = lambda i: (0, 0)
    return pl.pallas_call(
        functools.partial(_modnorm_kernel, n_prompt_tiles=npt),
        grid=(npt + nst,),
        in_specs=[pl.BlockSpec((tm, D_MODEL), row), pl.BlockSpec((tm, D_MODEL), srow),
                  pl.BlockSpec((1, D_MODEL), const), pl.BlockSpec((1, D_MODEL), const),
                  pl.BlockSpec((1, D_MODEL), const), pl.BlockSpec((tm, D_MODEL), srow),
                  pl.BlockSpec((tm, D_MODEL), srow)],
        out_specs=pl.BlockSpec((tm, D_MODEL), lambda i: (i, 0)),
        out_shape=jax.ShapeDtypeStruct((tp + ts, D_MODEL), BF16),
        compiler_params=_cparams(("arbitrary",)),
        name="modnorm1",
    )(x_p, x_s, g, sc_p, sh_p, sc_s, sh_s)


def _inproj_kernel(h_ref, w_ref, rc_ref, rs1_ref, rs2_ref, qg_ref, kg_ref, o_ref, wbf_ref, *, heads_per_tile):
    n = pl.program_id(0)

    @pl.when(pl.program_id(1) == 0)
    def _():
        wbf_ref[...] = w_ref[...].astype(BF16)

    half = ROT_DIM // 2
    q_heads = range(OFF_Q // HEAD_DIM, OFF_K // HEAD_DIM)
    k_heads = range(OFF_K // HEAD_DIM, OFF_V // HEAD_DIM)
    pair = 2 * HEAD_DIM

    def tile(col_tile):
        h = h_ref[...]
        for c0 in range(0, heads_per_tile * HEAD_DIM, pair):
            width = min(pair, heads_per_tile * HEAD_DIM - c0)
            acc = jnp.dot(h, wbf_ref[:, c0:c0 + width], preferred_element_type=F32)
            for c in range(c0, c0 + width, HEAD_DIM):
                slot = col_tile * heads_per_tile + c // HEAD_DIM
                x = acc[:, c - c0:c - c0 + HEAD_DIM]
                if slot in q_heads or slot in k_heads:
                    gain = qg_ref[...] if slot in q_heads else kg_ref[...]
                    ms = jnp.mean(x * x, axis=-1, keepdims=True)
                    y = x * lax.rsqrt(ms + EPS) * gain
                    up = pltpu.roll(y, HEAD_DIM - half, 1)
                    dn = pltpu.roll(y, half, 1)
                    x = y * rc_ref[...] + up * rs1_ref[...] + dn * rs2_ref[...]
                o_ref[:, c:c + HEAD_DIM] = x

    for col_tile in range(IN_COLS // (heads_per_tile * HEAD_DIM)):
        pl.when(n == col_tile)(functools.partial(tile, col_tile))


def _inproj(h, w_in, rc, rs1, rs2, qg, kg):
    n_tok = h.shape[0]
    tm, tn = PROJ_TM, PROJ_TN
    tab = pl.BlockSpec((tm, HEAD_DIM), lambda n, m: (m, 0))
    gain = pl.BlockSpec((1, HEAD_DIM), lambda n, m: (0, 0))
    return pl.pallas_call(
        functools.partial(_inproj_kernel, heads_per_tile=tn // HEAD_DIM),
        grid=(IN_COLS // tn, n_tok // tm),
        in_specs=[pl.BlockSpec((tm, D_MODEL), lambda n, m: (m, 0)),
                  pl.BlockSpec((D_MODEL, tn), lambda n, m: (0, n), pipeline_mode=pl.Buffered(1)),
                  tab, tab, tab, gain, gain],
        out_specs=pl.BlockSpec((tm, tn), lambda n, m: (m, n)),
        out_shape=jax.ShapeDtypeStruct((n_tok, IN_COLS), F32),
        scratch_shapes=[pltpu.VMEM((D_MODEL, tn), BF16)],
        compiler_params=_cparams(("arbitrary", "arbitrary")),
        name="in_proj",
    )(h, w_in, rc, rs1, rs2, qg, kg)


def _ssm_prep_kernel(are_ref, aim_ref, ldt_ref, arer_ref, aimr_ref, ldtr_ref, bre_ref, bim_ref,
                     pre_ref, pim_ref, bbre_ref, bbim_ref):
    def discretise(a_re, a_im, log_dt):
        dt = jnp.exp(log_dt)
        mag = jnp.exp(a_re * dt)
        return mag * jnp.cos(a_im * dt), mag * jnp.sin(a_im * dt)

    ab_re, ab_im = discretise(are_ref[...], aim_ref[...], ldt_ref[...])
    p_re, p_im = ab_re, ab_im
    for i in range(SCAN_L):
        pre_ref[i:i + 1, :] = p_re
        pim_ref[i:i + 1, :] = p_im
        p_re, p_im = p_re * ab_re - p_im * ab_im, p_re * ab_im + p_im * ab_re

    a_re, a_im = arer_ref[...], aimr_ref[...]
    r_re, r_im = discretise(a_re, a_im, ldtr_ref[...])
    nr, ni = r_re - 1.0, r_im
    den = a_re * a_re + a_im * a_im
    z_re = (nr * a_re + ni * a_im) / den
    z_im = (ni * a_re - nr * a_im) / den
    b_re, b_im = bre_ref[...], bim_ref[...]
    bbre_ref[...] = z_re * b_re - z_im * b_im
    bbim_ref[...] = z_re * b_im + z_im * b_re


def _ssm_prep(a_re, a_im, log_dt, b_re, b_im):
    g, p, n = b_re.shape
    flat = lambda x: x.reshape(1, g * p)
    rep = lambda x: jnp.repeat(x, n, axis=1)
    ldt_gp = jnp.broadcast_to(log_dt[:, None], (g, p))
    ldt_rep = jnp.broadcast_to(log_dt[:, None], (g, p * n))
    out_shape = [jax.ShapeDtypeStruct((SCAN_L, g * p), F32)] * 2 + [jax.ShapeDtypeStruct((g, p * n), F32)] * 2
    return pl.pallas_call(_ssm_prep_kernel, out_shape=out_shape, name="ssm_prep")(
        flat(a_re), flat(a_im), flat(ldt_gp), rep(a_re), rep(a_im), ldt_rep,
        b_re.reshape(g, p * n), b_im.reshape(g, p * n))


def _ssm_block_matrices(bb_re, bb_im, c_re, c_im):
    g, p, n = SSM_GROUPS, SSM_STATE, SSM_GROUP
    nb = g // SSM_BLK
    eye = jnp.eye(SSM_BLK, dtype=F32)

    def in_mat(bb):
        x = bb.reshape(nb, SSM_BLK, p, n)
        return jnp.einsum('bgpm,gh->bgmhp', x, eye).reshape(nb, SSM_BLK * n, SSM_BLK * p)

    def out_mat(c):
        x = c.reshape(nb, SSM_BLK, n, p)
        return jnp.einsum('bgnp,gh->bgphn', x, eye).reshape(nb, SSM_BLK * p, SSM_BLK * n)

    b_mat = jnp.concatenate([in_mat(bb_re), in_mat(bb_im)], axis=2).astype(BF16)
    c_mat = jnp.concatenate([out_mat(c_re), -out_mat(c_im)], axis=1).astype(BF16)
    return b_mat, c_mat


def _cmul_add(a_re, a_im, s_re, s_im, b_re, b_im):
    return a_re * s_re - a_im * s_im + b_re, a_re * s_im + a_im * s_re + b_im


def _s5_prompt_kernel(*refs):
    par = SSM_PAR
    u_refs = refs[:par]
    (d_ref, pre_ref, pim_ref, bm_ref, cm_ref, y_ref, fre_ref, fim_ref,
     up_scr, bu_scr, lhs_scr, in_re_scr, in_im_scr, car_re, car_im, yn_scr) = refs[par:]
    nc, ln, w = SCAN_NC, SCAN_L, SSM_BLK_STATE

    @pl.when(pl.program_id(1) == 0)
    def _():
        car_re[...] = jnp.zeros_like(car_re)
        car_im[...] = jnp.zeros_like(car_im)

    for b in range(par):
        lanes = slice(b * w, (b + 1) * w)
        rows = lambda i: slice(i * nc, (i + 1) * nc)
        for i in range(ln):
            up_scr[b, rows(i), :] = u_refs[b][pl.ds(i, nc, stride=ln), :]
        up = up_scr[b]
        bu_scr[b] = jnp.dot(up.astype(BF16), bm_ref[b], preferred_element_type=F32)

        a_re = jnp.broadcast_to(pre_ref[0:1, lanes], (nc, w))
        a_im = jnp.broadcast_to(pim_ref[0:1, lanes], (nc, w))
        s_re = s_im = jnp.zeros((nc, w), F32)
        for i in range(ln):
            s_re, s_im = _cmul_add(a_re, a_im, s_re, s_im, bu_scr[b, rows(i), 0:w], bu_scr[b, rows(i), w:2 * w])
            bu_scr[b, rows(i), 0:w] = s_re
            bu_scr[b, rows(i), w:2 * w] = s_im

        al_re, al_im = pre_ref[ln - 1:ln, lanes], pim_ref[ln - 1:ln, lanes]
        c_re, c_im = car_re[b], car_im[b]
        for c in range(nc):
            in_re_scr[b, c:c + 1, :] = c_re
            in_im_scr[b, c:c + 1, :] = c_im
            c_re, c_im = _cmul_add(al_re, al_im, c_re, c_im, s_re[c:c + 1, :], s_im[c:c + 1, :])
        car_re[b] = c_re
        car_im[b] = c_im
        fre_ref[:, lanes] = c_re
        fim_ref[:, lanes] = c_im
        in_re, in_im = in_re_scr[b], in_im_scr[b]

        for i in range(ln):
            p_re = jnp.broadcast_to(pre_ref[i:i + 1, lanes], (nc, w))
            p_im = jnp.broadcast_to(pim_ref[i:i + 1, lanes], (nc, w))
            f_re, f_im = _cmul_add(p_re, p_im, in_re, in_im, bu_scr[b, rows(i), 0:w], bu_scr[b, rows(i), w:2 * w])
            lhs_scr[b, rows(i), 0:w] = f_re.astype(BF16)
            lhs_scr[b, rows(i), w:2 * w] = f_im.astype(BF16)

        y = (jnp.dot(lhs_scr[b], cm_ref[b], preferred_element_type=F32)
             + d_ref[:, b * LANES:(b + 1) * LANES] * up)
        for i in range(ln):
            yn_scr[b, pl.ds(i, nc, stride=ln), :] = y[rows(i), :]
        y_ref[:, b * LANES:(b + 1) * LANES] = _gelu_tanh(yn_scr[b]).astype(y_ref.dtype)


def _s5_prompt(proj, n_prompt, d_skip, pw_re, pw_im, b_mat, c_mat):
    par = SSM_PAR
    nb = SSM_GROUPS // SSM_BLK // par
    tt, w = SCAN_TT, SSM_BLK_STATE
    u_spec = lambda b: pl.BlockSpec((tt, LANES), lambda j, i: (i, par * j + b))
    return pl.pallas_call(
        _s5_prompt_kernel,
        grid=(nb, n_prompt // tt),
        in_specs=[u_spec(b) for b in range(par)] + [
            pl.BlockSpec((1, par * LANES), lambda j, i: (0, j)),
            pl.BlockSpec((SCAN_L, par * w), lambda j, i: (0, j)),
            pl.BlockSpec((SCAN_L, par * w), lambda j, i: (0, j)),
            pl.BlockSpec((par, LANES, 2 * w), lambda j, i: (j, 0, 0)),
            pl.BlockSpec((par, 2 * w, LANES), lambda j, i: (j, 0, 0))],
        out_specs=[pl.BlockSpec((tt, par * LANES), lambda j, i: (i, j)),
                   pl.BlockSpec((1, par * w), lambda j, i: (0, j)),
                   pl.BlockSpec((1, par * w), lambda j, i: (0, j))],
        out_shape=[jax.ShapeDtypeStruct((n_prompt, SSM_WIDTH), BF16),
                   jax.ShapeDtypeStruct((1, SSM_FLAT), F32),
                   jax.ShapeDtypeStruct((1, SSM_FLAT), F32)],
        scratch_shapes=[pltpu.VMEM((par, tt, LANES), F32), pltpu.VMEM((par, tt, 2 * w), F32),
                        pltpu.VMEM((par, tt, 2 * w), BF16),
                        pltpu.VMEM((par, SCAN_NC, w), F32), pltpu.VMEM((par, SCAN_NC, w), F32),
                        pltpu.VMEM((par, 1, w), F32), pltpu.VMEM((par, 1, w), F32),
                        pltpu.VMEM((par, tt, LANES), F32)],
        compiler_params=_cparams(("arbitrary", "arbitrary")),
        name="s5_prompt",
    )(*([proj] * par), d_skip, pw_re, pw_im, b_mat, c_mat)


def _s5_sample_kernel(u_ref, d_ref, pre_ref, pim_ref, bm_ref, cm_ref, s0re_ref, s0im_ref,
                      y_ref, fre_ref, fim_ref, up_scr, bu_scr, lhs_scr, yn_scr, *, n_seq, n_step):
    w = SSM_BLK_STATE
    rb = 16
    for s in range(n_step):
        up_scr[s * n_seq:(s + 1) * n_seq, :] = u_ref[pl.ds(s, n_seq, stride=n_step), :]
    up = up_scr[...]
    bu_scr[...] = jnp.dot(up.astype(BF16), bm_ref[0], preferred_element_type=F32)
    a_re = jnp.broadcast_to(pre_ref[0:1, :], (rb, w))
    a_im = jnp.broadcast_to(pim_ref[0:1, :], (rb, w))

    def seq_block(b, _):
        r0 = pl.multiple_of(b * rb, rb)
        s_re, s_im = s0re_ref[pl.ds(r0, rb), :], s0im_ref[pl.ds(r0, rb), :]
        for s in range(n_step):
            rows = pl.ds(pl.multiple_of(s * n_seq + r0, rb), rb)
            s_re, s_im = _cmul_add(a_re, a_im, s_re, s_im, bu_scr[rows, 0:w], bu_scr[rows, w:2 * w])
            lhs_scr[rows, 0:w] = s_re.astype(BF16)
            lhs_scr[rows, w:2 * w] = s_im.astype(BF16)
        fre_ref[pl.ds(r0, rb), :] = s_re
        fim_ref[pl.ds(r0, rb), :] = s_im
        return 0

    lax.fori_loop(0, n_seq // rb, seq_block, 0)
    y = jnp.dot(lhs_scr[...], cm_ref[0], preferred_element_type=F32) + d_ref[...] * up
    for s in range(n_step):
        yn_scr[pl.ds(s, n_seq, stride=n_step), :] = y[s * n_seq:(s + 1) * n_seq, :]
    y_ref[...] = _gelu_tanh(yn_scr[...]).astype(y_ref.dtype)


def _s5_sample(proj, n_prompt, n_seq, n_step, d_skip, pw_re, pw_im, b_mat, c_mat, s0_re, s0_im):
    nb = SSM_GROUPS // SSM_BLK
    rows, w = n_seq * n_step, SSM_BLK_STATE
    rblk = n_prompt // rows
    return pl.pallas_call(
        functools.partial(_s5_sample_kernel, n_seq=n_seq, n_step=n_step),
        grid=(nb,),
        in_specs=[pl.BlockSpec((rows, LANES), lambda j: (rblk, j)),
                  pl.BlockSpec((1, LANES), lambda j: (0, j)),
                  pl.BlockSpec((SCAN_L, w), lambda j: (0, j)),
                  pl.BlockSpec((SCAN_L, w), lambda j: (0, j)),
                  pl.BlockSpec((1, LANES, 2 * w), lambda j: (j, 0, 0)),
                  pl.BlockSpec((1, 2 * w, LANES), lambda j: (j, 0, 0)),
                  pl.BlockSpec((n_seq, w), lambda j: (0, j)),
                  pl.BlockSpec((n_seq, w), lambda j: (0, j))],
        out_specs=[pl.BlockSpec((rows, LANES), lambda j: (0, j)),
                   pl.BlockSpec((n_seq, w), lambda j: (0, j)),
                   pl.BlockSpec((n_seq, w), lambda j: (0, j))],
        out_shape=[jax.ShapeDtypeStruct((rows, SSM_WIDTH), BF16),
                   jax.ShapeDtypeStruct((n_seq, SSM_FLAT), F32),
                   jax.ShapeDtypeStruct((n_seq, SSM_FLAT), F32)],
        scratch_shapes=[pltpu.VMEM((rows, LANES), F32), pltpu.VMEM((rows, 2 * w), F32),
                        pltpu.VMEM((rows, 2 * w), BF16), pltpu.VMEM((rows, LANES), F32)],
        compiler_params=_cparams(("arbitrary",)),
        name="s5_sample",
    )(proj, d_skip, pw_re, pw_im, b_mat, c_mat, s0_re, s0_im)


def _attn_prompt_kernel(*refs):
    ins, o_ref, scr = refs[:15], refs[15], refs[16:]
    sb = pl.program_id(0)
    scale = HEAD_DIM ** -0.5
    qi = lax.broadcasted_iota(jnp.int32, (BAND, 2 * BAND), 0)
    kj = lax.broadcasted_iota(jnp.int32, (BAND, 2 * BAND), 1)
    dist = qi + BAND - kj
    band_ok = (dist >= 0) & (dist <= BAND)

    for g, (_, dil) in enumerate(DILATION_PATTERNS):
        q_ref, k_ref, v_ref, kp_ref, vp_ref = ins[5 * g:5 * g + 5]
        kbuf, vbuf, o_scr, m_scr, l_scr = scr[5 * g:5 * g + 5]
        pre = BAND * dil
        kbuf[0:pre, :] = kp_ref[...]
        kbuf[pre:pre + ATT_SB, :] = k_ref[...]
        vbuf[0:pre, :] = vp_ref[...]
        vbuf[pre:pre + ATT_SB, :] = v_ref[...]
        nblk = ATT_SB // pre

        def block(idx, _, dil=dil, pre=pre, nblk=nblk, q_ref=q_ref, kbuf=kbuf, vbuf=vbuf,
                  o_scr=o_scr, m_scr=m_scr, l_scr=l_scr):
            r = idx // nblk
            b = idx - r * nblk
            row0 = r + b * pre
            if dil == 1:
                q_rows = pl.ds(pl.multiple_of(row0, BAND), BAND)
                kv_rows = pl.ds(pl.multiple_of(row0, BAND), 2 * BAND)
            else:
                q_rows = pl.ds(row0, BAND, stride=dil)
                kv_rows = pl.ds(row0, 2 * BAND, stride=dil)
            q = (q_ref[q_rows, :] * scale).astype(BF16)
            kw = kbuf[kv_rows, :].astype(BF16)
            vw = vbuf[kv_rows, :].astype(BF16)
            s = lax.dot_general(q, kw, (((1,), (1,)), ((), ())), preferred_element_type=F32)
            s = jnp.where(band_ok & ((kj >= BAND) | (sb > 0) | (b > 0)), s, NEG)
            m = jnp.max(s, axis=-1, keepdims=True)
            p = jnp.exp(s - m)
            l = jnp.sum(p, axis=-1, keepdims=True)
            o = jnp.dot(p.astype(BF16), vw, preferred_element_type=F32)
            o_scr[q_rows, :] = o
            m_scr[q_rows, :] = jnp.broadcast_to(m, (BAND, HEAD_DIM))
            l_scr[q_rows, :] = jnp.broadcast_to(l, (BAND, HEAD_DIM))
            return 0

        lax.fori_loop(0, ATT_SB // BAND, block, 0, unroll=ATT_UNROLL)

    ms = [scr[5 * g + 3][...] for g in range(N_PATTERNS)]
    mx = jnp.maximum(jnp.maximum(ms[0], ms[1]), ms[2])
    num = jnp.zeros((ATT_SB, HEAD_DIM), F32)
    den = jnp.zeros((ATT_SB, HEAD_DIM), F32)
    for g in range(N_PATTERNS):
        wgt = jnp.exp(ms[g] - mx)
        num = num + wgt * scr[5 * g + 2][...]
        den = den + wgt * scr[5 * g + 4][...]
    o_ref[...] = num / den


def _attn_prompt(proj, n_prompt):
    hcol = lambda off, g, j: (off + g * GROUP_WIDTH) // HEAD_DIM + j
    in_specs, scratch = [], []
    for g, (_, dil) in enumerate(DILATION_PATTERNS):
        pre = BAND * dil
        per = ATT_SB // pre
        cur = lambda off, g=g: pl.BlockSpec((ATT_SB, HEAD_DIM), lambda sb, j: (sb, hcol(off, g, j)))
        prev = lambda off, g=g, per=per, pre=pre: pl.BlockSpec(
            (pre, HEAD_DIM), lambda sb, j: (jnp.maximum(sb * per - 1, 0), hcol(off, g, j)))
        in_specs += [cur(OFF_Q), cur(OFF_K), cur(OFF_V), prev(OFF_K), prev(OFF_V)]
        scratch += [pltpu.VMEM((pre + ATT_SB, HEAD_DIM), F32), pltpu.VMEM((pre + ATT_SB, HEAD_DIM), F32),
                    pltpu.VMEM((ATT_SB, HEAD_DIM), F32), pltpu.VMEM((ATT_SB, HEAD_DIM), F32),
                    pltpu.VMEM((ATT_SB, HEAD_DIM), F32)]
    return pl.pallas_call(
        _attn_prompt_kernel,
        grid=(n_prompt // ATT_SB, HEADS_PER_GROUP),
        in_specs=in_specs,
        out_specs=pl.BlockSpec((ATT_SB, HEAD_DIM), lambda sb, j: (sb, j)),
        out_shape=jax.ShapeDtypeStruct((n_prompt, GROUP_WIDTH), F32),
        scratch_shapes=scratch,
        compiler_params=_cparams(("arbitrary", "arbitrary")),
        name="attn_prompt",
    )(*([proj] * 15))


SEQ_PER_STEP = 2
KV_PLANES = 2 * HEADS_PER_GROUP


def _compact_pitch(n_step):
    tiles = n_step * KV_PLANES // 8
    return 8 * (tiles + 1 - tiles % 2)


def _sample_bias(n_step):
    rows = HEADS_PER_GROUP * n_step
    step = np.arange(rows) % n_step
    cache_bias, new_bias = [], []
    for (window, dil) in DILATION_PATTERNS:
        wb = min(window, PAST_LEN)
        band = window // dil
        if dil > n_step:
            res, i = np.meshgrid(np.arange(n_step), np.arange(wb // dil), indexing='ij')
            c = (i * dil + res).reshape(-1)
        else:
            c = np.arange(wb)
        delta = wb + step[:, None] - c[None, :]
        ok = (delta >= 0) & (delta % dil == 0) & (delta // dil <= band)
        cache_bias.append(np.where(ok, 0.0, NEG).astype(np.float32))
        nb = np.full((SEQ_PER_STEP, rows, LANES), NEG, np.float32)
        for a in range(SEQ_PER_STEP):
            for sp in range(n_step):
                dl = step - sp
                okn = (dl >= 0) & (dl % dil == 0) & (dl // dil <= band)
                nb[a, :, a * n_step + sp] = np.where(okn, 0.0, NEG)
        new_bias.append(nb)
    return cache_bias, new_bias


def _attn_sample_kernel(*refs, n_step):
    (q0, k0, v0, q1, k1, v1, q2, k2, v2, c0, c1, c2, cb0, cb1, cb2, nb0, nb1, nb2, o_ref) = refs
    qs, ks, vs = (q0, q1, q2), (k0, k1, k2), (v0, v1, v2)
    caches, cbias, nbias = (c0, c1, c2), (cb0, cb1, cb2), (nb0, nb1, nb2)
    rows = HEADS_PER_GROUP * n_step
    gw = GROUP_WIDTH
    scale = HEAD_DIM ** -0.5
    row_head = lax.broadcasted_iota(jnp.int32, (rows, gw), 0) // n_step
    lane_head = lax.broadcasted_iota(jnp.int32, (rows, gw), 1) // HEAD_DIM
    own_head = row_head == lane_head
    nt = (((1,), (1,)), ((), ()))
    pad = jnp.zeros((LANES - SEQ_PER_STEP * n_step, gw), F32)

    def planes(load):
        k = jnp.concatenate([load(h) for h in range(HEADS_PER_GROUP)], axis=1)
        v = jnp.concatenate([load(HEADS_PER_GROUP + h) for h in range(HEADS_PER_GROUP)], axis=1)
        return k.astype(BF16), v.astype(BF16)

    for a in range(SEQ_PER_STEP):
        pieces = []
        for g, (window, dil) in enumerate(DILATION_PATTERNS):
            q = qs[g][a * n_step:(a + 1) * n_step, :] * scale
            qbd = jnp.where(own_head, jnp.concatenate([q] * HEADS_PER_GROUP, axis=0), 0.0).astype(BF16)
            cache = caches[g]
            per_seq = cache.shape[0] // SEQ_PER_STEP
            if dil > n_step:
                pitch = cache.shape[1]
                flat = cache.reshape(cache.shape[0] * pitch, HEAD_DIM)
                kvs = [planes(lambda p, r=r: flat[pl.ds(a * per_seq * pitch + r * KV_PLANES + p, per_seq,
                                                        stride=pitch), :])
                       for r in range(n_step)]
            else:
                kvs = [planes(lambda p: cache[pl.ds(a * per_seq + p, per_seq // KV_PLANES, stride=KV_PLANES), :])]
            sc = jnp.concatenate([lax.dot_general(qbd, k, nt, preferred_element_type=F32) for k, _ in kvs], axis=1)
            pieces.append((sc + cbias[g][...], [v for _, v in kvs]))
            k_new = jnp.concatenate([ks[g][...], pad], axis=0).astype(BF16)
            v_new = jnp.concatenate([vs[g][...], pad], axis=0).astype(BF16)
            sn = lax.dot_general(qbd, k_new, nt, preferred_element_type=F32) + nbias[g][a]
            pieces.append((sn, [v_new]))
        m = functools.reduce(jnp.maximum, [jnp.max(s, axis=-1, keepdims=True) for s, _ in pieces])
        l = jnp.zeros((rows, 1), F32)
        acc = jnp.zeros((rows, gw), F32)
        for s, vals in pieces:
            p = jnp.exp(s - m)
            l = l + jnp.sum(p, axis=-1, keepdims=True)
            pb = p.astype(BF16)
            nk = pb.shape[1] // len(vals)
            for r, v in enumerate(vals):
                acc = acc + jnp.dot(pb[:, r * nk:(r + 1) * nk], v, preferred_element_type=F32)
        acc = jnp.where(own_head, acc, 0.0)
        o16 = functools.reduce(lambda x, y: x + y,
                               [acc[:, h * HEAD_DIM:(h + 1) * HEAD_DIM] for h in range(HEADS_PER_GROUP)]) / l
        for h in range(HEADS_PER_GROUP):
            o_ref[a * n_step:(a + 1) * n_step, h * HEAD_DIM:(h + 1) * HEAD_DIM] = o16[h * n_step:(h + 1) * n_step, :]


def _attn_sample(proj, n_prompt, n_seq, n_step, caches):
    rows = SEQ_PER_STEP * n_step
    rblk = n_prompt // rows
    cache_bias, new_bias = _sample_bias(n_step)
    tok = lambda off, g: pl.BlockSpec((rows, GROUP_WIDTH), lambda i: (rblk + i, (off + g * GROUP_WIDTH) // GROUP_WIDTH))
    in_specs, args = [], []
    for g in range(N_PATTERNS):
        in_specs += [tok(OFF_Q, g), tok(OFF_K, g), tok(OFF_V, g)]
        args += [proj, proj, proj]
    for g, (window, dil) in enumerate(DILATION_PATTERNS):
        c = caches[g]
        wb = c.shape[1]
        if dil > n_step:
            pitch = _compact_pitch(n_step)
            assert pitch <= dil * KV_PLANES and wb % dil == 0
            c = c.reshape(n_seq * (wb // dil), dil * KV_PLANES, HEAD_DIM)
            in_specs.append(pl.BlockSpec((SEQ_PER_STEP * (wb // dil), pitch, HEAD_DIM), lambda i: (i, 0, 0)))
        else:
            c = c.reshape(n_seq * wb * KV_PLANES, HEAD_DIM)
            in_specs.append(pl.BlockSpec((SEQ_PER_STEP * wb * KV_PLANES, HEAD_DIM), lambda i: (i, 0)))
        args.append(c)
    for b in cache_bias:
        in_specs.append(pl.BlockSpec(b.shape, lambda i: (0, 0)))
        args.append(jnp.asarray(b))
    for b in new_bias:
        in_specs.append(pl.BlockSpec(b.shape, lambda i: (0, 0, 0)))
        args.append(jnp.asarray(b))
    return pl.pallas_call(
        functools.partial(_attn_sample_kernel, n_step=n_step),
        grid=(n_seq // SEQ_PER_STEP,),
        in_specs=in_specs,
        out_specs=pl.BlockSpec((rows, GROUP_WIDTH), lambda i: (i, 0)),
        out_shape=jax.ShapeDtypeStruct((n_seq * n_step, GROUP_WIDTH), F32),
        compiler_params=_cparams(("arbitrary",)),
        name="attn_sample",
    )(*args)


def _mix_kernel(yp_ref, ys_ref, op_ref, os_ref, wa_ref, wb_ref, wbr_ref, ga_ref, gb_ref, o_ref,
                wa_bf, wb_bf, wbr_bf, *, n_prompt_tiles):
    i = pl.program_id(1)

    @pl.when(i == 0)
    def _():
        wa_bf[...] = wa_ref[...].astype(BF16)
        wb_bf[...] = wb_ref[...].astype(BF16)
        wbr_bf[...] = wbr_ref[...].astype(BF16)

    is_s = i >= n_prompt_tiles
    for rows in _row_halves(o_ref.shape[0]):
        y = jnp.where(is_s, ys_ref[rows, :], yp_ref[rows, :])
        o = jnp.where(is_s, os_ref[rows, :], op_ref[rows, :]).astype(BF16)
        glu_a = jnp.dot(y, wa_bf[...], preferred_element_type=F32)
        glu_b = jnp.dot(y, wb_bf[...], preferred_element_type=F32)
        branch_a = glu_a * _sigmoid(glu_b)
        branch_b = jnp.dot(o, wbr_bf[...], preferred_element_type=F32)
        o_ref[rows, :] = (_sigmoid(ga_ref[rows, :]) * branch_a
                          + _sigmoid(gb_ref[rows, :]) * branch_b).astype(o_ref.dtype)


def _mix(y_p, y_s, o_p, o_s, w_glu, w_attn_br, proj):
    tp, ts = y_p.shape[0], y_s.shape[0]
    tm, tn = TOK_TILE, 512
    assert OFF_G % tn == 0
    npt, nst = tp // tm, ts // tm
    ncol = D_MODEL // tn
    prow = lambda n, i: (jnp.minimum(i, npt - 1), 0)
    srow = lambda n, i: (jnp.maximum(i - npt, 0), 0)
    return pl.pallas_call(
        functools.partial(_mix_kernel, n_prompt_tiles=npt),
        grid=(ncol, npt + nst),
        in_specs=[pl.BlockSpec((tm, SSM_WIDTH), prow), pl.BlockSpec((tm, SSM_WIDTH), srow),
                  pl.BlockSpec((tm, GROUP_WIDTH), prow), pl.BlockSpec((tm, GROUP_WIDTH), srow),
                  pl.BlockSpec((SSM_WIDTH, tn), lambda n, i: (0, n)),
                  pl.BlockSpec((SSM_WIDTH, tn), lambda n, i: (0, ncol + n)),
                  pl.BlockSpec((GROUP_WIDTH, tn), lambda n, i: (0, n)),
                  pl.BlockSpec((tm, tn), lambda n, i: (i, OFF_G // tn + n)),
                  pl.BlockSpec((tm, tn), lambda n, i: (i, OFF_G // tn + ncol + n))],
        out_specs=pl.BlockSpec((tm, tn), lambda n, i: (i, n)),
        out_shape=jax.ShapeDtypeStruct((tp + ts, D_MODEL), BF16),
        scratch_shapes=[pltpu.VMEM((SSM_WIDTH, tn), BF16), pltpu.VMEM((SSM_WIDTH, tn), BF16),
                        pltpu.VMEM((GROUP_WIDTH, tn), BF16)],
        compiler_params=_cparams(("arbitrary", "arbitrary")),
        name="glu_mix",
    )(y_p, y_s, o_p, o_s, w_glu, w_glu, w_attn_br, proj, proj)


def _route(logits):
    lane = lax.broadcasted_iota(jnp.int32, logits.shape, 1).astype(F32)
    big = 1000.0
    first = lambda cond: jnp.min(jnp.where(cond, lane, big), axis=-1, keepdims=True)
    is_g = lane < N_EXPERT_GROUPS
    lg = jnp.where(is_g, logits, NEG)
    mg = jnp.max(lg, axis=-1, keepdims=True)
    g_sel = first(lg == mg)
    p_group = 1.0 / jnp.sum(jnp.where(is_g, jnp.exp(lg - mg), 0.0), axis=-1, keepdims=True)
    e_lo = N_EXPERT_GROUPS + EXPERTS_PER_GROUP * g_sel
    le = jnp.where((lane >= e_lo) & (lane < e_lo + EXPERTS_PER_GROUP), logits, NEG)
    v1 = jnp.max(le, axis=-1, keepdims=True)
    i1 = first(le == v1)
    le2 = jnp.where(lane == i1, NEG, le)
    v2 = jnp.max(le2, axis=-1, keepdims=True)
    i2 = first(le2 == v2)
    e2 = jnp.exp(v2 - v1)
    w1 = p_group / (1.0 + e2)
    w2 = p_group * e2 / (1.0 + e2)
    pick = lambda k, val: jnp.where(lane == k, val, 0.0)
    return (pick(ROUTE_E1, i1 - N_EXPERT_GROUPS) + pick(ROUTE_E2, i2 - N_EXPERT_GROUPS)
            + pick(ROUTE_W1, w1) + pick(ROUTE_W2, w2))


def _outproj_kernel(mix_ref, w_ref, xp_ref, xs_ref, g_ref, wr_ref, br_ref,
                    gtp_ref, scp_ref, shp_ref, gts_ref, scs_ref, shs_ref,
                    x1_ref, h2_ref, route_ref, *, n_prompt_tiles):
    is_s = pl.program_id(0) >= n_prompt_tiles

    def split(v):
        high = v.astype(BF16)
        return high, (v - high.astype(F32)).astype(BF16)

    x = jnp.where(is_s, xs_ref[...], xp_ref[...])
    gt = jnp.where(is_s, gts_ref[...], gtp_ref[...])
    sc = jnp.where(is_s, scs_ref[...], scp_ref[...])
    sh = jnp.where(is_s, shs_ref[...], shp_ref[...])
    x1 = x + gt * jnp.dot(mix_ref[...], w_ref[...], preferred_element_type=F32)
    x1_ref[...] = x1
    ms = jnp.mean(x1 * x1, axis=-1, keepdims=True)
    h2 = (x1 * lax.rsqrt(ms + EPS) * g_ref[...]) * (1.0 + sc) + sh
    h2_ref[...] = h2.astype(h2_ref.dtype)
    r = jnp.dot(jnp.concatenate(split(h2), axis=0), jnp.concatenate(split(wr_ref[...]), axis=1),
                preferred_element_type=F32)
    n = h2.shape[0]
    logits = (r[:n, :LANES] + r[:n, LANES:]) + (r[n:, :LANES] + r[n:, LANES:]) + br_ref[...]
    route_ref[...] = _route(logits)


def _outproj(mixed, w_out_bf, x_p, x_s, g2, w_router, b_router, gt_p, sc_p, sh_p, gt_s, sc_s, sh_s):
    tp, ts = x_p.shape[0], x_s.shape[0]
    tm = SMALL_TOK_TILE
    npt, nst = tp // tm, ts // tm
    prow = lambda i: (jnp.minimum(i, npt - 1), 0)
    srow = lambda i: (jnp.maximum(i - npt, 0), 0)
    const = lambda i: (0, 0)
    vec = pl.BlockSpec((1, D_MODEL), const)
    svec = pl.BlockSpec((tm, D_MODEL), srow)
    full = lambda i: (i, 0)
    return pl.pallas_call(
        functools.partial(_outproj_kernel, n_prompt_tiles=npt),
        grid=(npt + nst,),
        in_specs=[pl.BlockSpec((tm, D_MODEL), full), pl.BlockSpec((D_MODEL, D_MODEL), const),
                  pl.BlockSpec((tm, D_MODEL), prow), pl.BlockSpec((tm, D_MODEL), srow),
                  vec, pl.BlockSpec((D_MODEL, LANES), const), pl.BlockSpec((1, LANES), const),
                  vec, vec, vec, svec, svec, svec],
        out_specs=[pl.BlockSpec((tm, D_MODEL), full), pl.BlockSpec((tm, D_MODEL), full),
                   pl.BlockSpec((tm, LANES), full)],
        out_shape=[jax.ShapeDtypeStruct((tp + ts, D_MODEL), F32),
                   jax.ShapeDtypeStruct((tp + ts, D_MODEL), F32),
                   jax.ShapeDtypeStruct((tp + ts, LANES), F32)],
        compiler_params=_cparams(("arbitrary",)),
        name="out_proj_norm2_router",
    )(mixed, w_out_bf, x_p, x_s, g2, w_router, b_router, gt_p, sc_p, sh_p, gt_s, sc_s, sh_s)


def _dispatch_plan(route, tm):
    e = route[:, ROUTE_E1:ROUTE_E2 + 1].astype(jnp.int32).reshape(-1)
    n_pairs = e.shape[0]
    onehot = (e[:, None] == jnp.arange(N_EXPERTS, dtype=jnp.int32)[None, :]).astype(jnp.int32)
    csum = jnp.cumsum(onehot, axis=0)
    rank = jnp.sum(onehot * csum, axis=1) - 1
    tiles_per_expert = (csum[-1] + tm - 1) // tm
    tile_end = jnp.cumsum(tiles_per_expert)
    tile_start = tile_end - tiles_per_expert
    dest = (tile_start[e] * tm + rank).astype(jnp.int32)
    max_tiles = n_pairs // tm + N_EXPERTS
    k = jnp.arange(max_tiles, dtype=jnp.int32)
    tile_expert = jnp.minimum(jnp.sum((k[:, None] >= tile_end[None, :]).astype(jnp.int32), axis=1), N_EXPERTS - 1)
    n_used = tile_end[-1].astype(jnp.int32)
    last_expert = jnp.take(tile_expert, n_used - 1)
    tile_expert = jnp.where(k < n_used, tile_expert, last_expert).astype(jnp.int32)
    pad_end = (jnp.concatenate([tile_end, tile_end[-1:]]) * tm).astype(jnp.int32)
    pad_len = (tiles_per_expert * tm - csum[-1]).astype(jnp.int32)
    return dest, pad_end, pad_len, tile_expert, n_used.reshape(1), max_tiles


def _start_pair_copies(dest_ref, tile, rows, make):
    def body(r, _):
        for k in range(2):
            make(r, k, dest_ref[(tile * rows + r) * 2 + k]).start()
        return 0
    lax.fori_loop(0, rows, body, 0, unroll=8)


def _pair_copies(dest_ref, tile, rows, make, make_all):
    _start_pair_copies(dest_ref, tile, rows, make)
    for k in range(2):
        make_all(k).wait()


def _dispatch_kernel(dest_ref, pad_end_ref, pad_len_ref, h_ref, xs_ref, zero_buf, sem, pad_sem):
    rows = h_ref.shape[0]

    @pl.when(pl.program_id(0) == 0)
    def _():
        zero_buf[...] = jnp.zeros_like(zero_buf)

        def pad_copies(fn):
            for e in range(N_EXPERTS):
                end, left = pad_end_ref[e], pad_len_ref[e]
                size = zero_buf.shape[0]
                while size >= SUBLANES:
                    take = (left & size) != 0
                    end = end - jnp.where(take, size, 0)

                    @pl.when(take)
                    def _(start=end, size=size):
                        fn(pltpu.make_async_copy(zero_buf.at[pl.ds(0, size)],
                                                 xs_ref.at[pl.ds(pl.multiple_of(start, size), size)], pad_sem))

                    size //= 2
                for r in range(1, SUBLANES):
                    @pl.when((left & (SUBLANES - 1)) >= r)
                    def _(row=end - r):
                        fn(pltpu.make_async_copy(zero_buf.at[pl.ds(0, 1)], xs_ref.at[pl.ds(row, 1)], pad_sem))
            size = zero_buf.shape[0]
            tail = pad_end_ref[N_EXPERTS]
            for t in range(N_EXPERTS * MOE_TM // size):
                @pl.when(tail + t * size < xs_ref.shape[0])
                def _(start=tail + t * size):
                    fn(pltpu.make_async_copy(zero_buf, xs_ref.at[pl.ds(pl.multiple_of(start, size), size)], pad_sem))

        pad_copies(lambda c: c.start())
        pad_copies(lambda c: c.wait())

    make = lambda r, k, d: pltpu.make_async_copy(h_ref.at[pl.ds(r, 1)], xs_ref.at[pl.ds(d, 1)], sem)
    make_all = lambda k: pltpu.make_async_copy(h_ref, xs_ref.at[pl.ds(0, rows)], sem)
    _pair_copies(dest_ref, pl.program_id(0), rows, make, make_all)


def _dispatch(dest, pad_end, pad_len, h2, n_slots):
    n_tok = h2.shape[0]
    tm = SMALL_TOK_TILE
    return pl.pallas_call(
        _dispatch_kernel,
        grid_spec=pltpu.PrefetchScalarGridSpec(
            num_scalar_prefetch=3, grid=(n_tok // tm,),
            in_specs=[pl.BlockSpec((tm, D_MODEL), lambda i, d, ps, pn: (i, 0))],
            out_specs=pl.BlockSpec(memory_space=pl.ANY),
            scratch_shapes=[pltpu.VMEM((MOE_TM // 2, D_MODEL), F32),
                            pltpu.SemaphoreType.DMA(()), pltpu.SemaphoreType.DMA(())]),
        out_shape=jax.ShapeDtypeStruct((n_slots, D_MODEL), F32),
        compiler_params=_cparams(("arbitrary",)),
        name="moe_dispatch",
    )(dest, pad_end, pad_len, h2)


def _experts_kernel(te_ref, used_ref, xs_ref, wgu_ref, wd_ref, y_ref, wgu_bf, wd_bf):
    k = pl.program_id(0)
    new_expert = (k == 0) | (te_ref[k] != te_ref[jnp.maximum(k - 1, 0)])

    @pl.when(new_expert)
    def _():
        wgu_bf[...] = wgu_ref[0].astype(BF16)
        wd_bf[...] = wd_ref[0].astype(BF16)

    @pl.when(k < used_ref[0])
    def _():
        gu = jnp.dot(xs_ref[...].astype(BF16), wgu_bf[...], preferred_element_type=F32)
        gate, up = gu[:, :EXPERT_FF], gu[:, EXPERT_FF:]
        act = (gate * _sigmoid(gate)) * up
        y_ref[...] = jnp.dot(act.astype(BF16), wd_bf[...], preferred_element_type=F32)

    @pl.when(k >= used_ref[0])
    def _():
        y_ref[...] = jnp.zeros_like(y_ref)


def _experts(tile_expert, n_used, xs, w_gu, w_down, max_tiles):
    tm = MOE_TM
    row = lambda k, te, nu: (jnp.minimum(k, nu[0] - 1), 0)
    return pl.pallas_call(
        _experts_kernel,
        grid_spec=pltpu.PrefetchScalarGridSpec(
            num_scalar_prefetch=2, grid=(max_tiles,),
            in_specs=[pl.BlockSpec((tm, D_MODEL), row),
                      pl.BlockSpec((1, D_MODEL, 2 * EXPERT_FF), lambda k, te, nu: (te[k], 0, 0)),
                      pl.BlockSpec((1, EXPERT_FF, D_MODEL), lambda k, te, nu: (te[k], 0, 0))],
            out_specs=pl.BlockSpec((tm, D_MODEL), lambda k, te, nu: (k, 0)),
            scratch_shapes=[pltpu.VMEM((D_MODEL, 2 * EXPERT_FF), BF16), pltpu.VMEM((EXPERT_FF, D_MODEL), BF16)]),
        out_shape=jax.ShapeDtypeStruct((max_tiles * tm, D_MODEL), F32),
        compiler_params=_cparams(("arbitrary",)),
        name="moe_experts",
    )(tile_expert, n_used, xs, w_gu, w_down)


def _combine_kernel(dest_ref, y_hbm, x1_ref, route_ref, gtp_ref, gts_ref, yp_ref, ys_ref, buf, sem, *, n_prompt_tiles):
    i = pl.program_id(0)
    rows = x1_ref.shape[0]

    def gather(tile, slot):
        make = lambda r, k, d: pltpu.make_async_copy(y_hbm.at[pl.ds(d, 1)], buf.at[slot, k, pl.ds(r, 1)],
                                                     sem.at[slot])
        _start_pair_copies(dest_ref, tile, rows, make)

    @pl.when(i == 0)
    def _():
        gather(0, 0)

    slot = lax.rem(i, 2)

    @pl.when(i + 1 < pl.num_programs(0))
    def _():
        gather(i + 1, 1 - slot)

    for k in range(2):
        pltpu.make_async_copy(y_hbm.at[pl.ds(0, rows)], buf.at[slot, k], sem.at[slot]).wait()
    route = route_ref[...]
    lane = lax.broadcasted_iota(jnp.int32, route.shape, 1)
    w1 = jnp.sum(jnp.where(lane == ROUTE_W1, route, 0.0), axis=-1, keepdims=True)
    w2 = jnp.sum(jnp.where(lane == ROUTE_W2, route, 0.0), axis=-1, keepdims=True)
    moe = w1 * buf[slot, 0] + w2 * buf[slot, 1]

    @pl.when(i < n_prompt_tiles)
    def _():
        yp_ref[...] = x1_ref[...] + gtp_ref[...] * moe

    @pl.when(i >= n_prompt_tiles)
    def _():
        ys_ref[...] = x1_ref[...] + gts_ref[...] * moe


def _combine(dest, y_slots, x1, route, gt_p, gt_s, tp, ts):
    tm = SMALL_TOK_TILE
    npt, nst = tp // tm, ts // tm
    row = lambda i, d: (i, 0)
    return pl.pallas_call(
        functools.partial(_combine_kernel, n_prompt_tiles=npt),
        grid_spec=pltpu.PrefetchScalarGridSpec(
            num_scalar_prefetch=1, grid=(npt + nst,),
            in_specs=[pl.BlockSpec(memory_space=pl.ANY),
                      pl.BlockSpec((tm, D_MODEL), row), pl.BlockSpec((tm, LANES), row),
                      pl.BlockSpec((1, D_MODEL), lambda i, d: (0, 0)),
                      pl.BlockSpec((tm, D_MODEL), lambda i, d: (jnp.maximum(i - npt, 0), 0))],
            out_specs=[pl.BlockSpec((tm, D_MODEL), lambda i, d: (jnp.minimum(i, npt - 1), 0)),
                       pl.BlockSpec((tm, D_MODEL), lambda i, d: (jnp.maximum(i - npt, 0), 0))],
            scratch_shapes=[pltpu.VMEM((2, 2, tm, D_MODEL), F32), pltpu.SemaphoreType.DMA((2,))]),
        out_shape=[jax.ShapeDtypeStruct((tp, D_MODEL), F32), jax.ShapeDtypeStruct((ts, D_MODEL), F32)],
        compiler_params=_cparams(("arbitrary",)),
        name="moe_combine",
    )(dest, y_slots, x1, route, gt_p, gt_s)


def _moe(h2, route, x1, w_gu, w_down, gt_p, gt_s, tp, ts):
    dest, pad_end, pad_len, tile_expert, n_used, max_tiles = _dispatch_plan(route, MOE_TM)
    xs = _dispatch(dest, pad_end, pad_len, h2, max_tiles * MOE_TM)
    y_slots = _experts(tile_expert, n_used, xs, w_gu, w_down, max_tiles)
    return _combine(dest, y_slots, x1, route, gt_p, gt_s, tp, ts)


def _rope_tables(n_prompt, n_seq, n_step):
    half = ROT_DIM // 2
    inv_freq = ROPE_THETA ** (-jnp.arange(half, dtype=F32) / half)
    pos = jnp.concatenate([jnp.arange(n_prompt, dtype=jnp.int32),
                           jnp.tile(PAST_LEN + jnp.arange(n_step, dtype=jnp.int32), n_seq)])
    ang = pos.astype(F32)[:, None] * inv_freq[None, :]
    cos, sin = jnp.cos(ang), jnp.sin(ang)
    n = pos.shape[0]
    one = jnp.ones((n, HEAD_DIM - ROT_DIM), F32)
    zero = jnp.zeros((n, HEAD_DIM - ROT_DIM), F32)
    zh = jnp.zeros((n, half), F32)
    rc = jnp.concatenate([cos, cos, one], axis=1)
    rs1 = jnp.concatenate([-sin, zh, zero], axis=1)
    rs2 = jnp.concatenate([zh, sin, zero], axis=1)
    return rc, rs1, rs2


def kernel(x_prompt, x_sample, cache_kv_w128, cache_kv_w512, cache_kv_w2048, state_ssm_re, state_ssm_im,
           c_prompt, c_sample, w_ada, b_ada, norm1_g, norm2_g, w_in, ssm_a_re, ssm_a_im, ssm_log_dt,
           ssm_b_re, ssm_b_im, ssm_c_re, ssm_c_im, ssm_d, w_glu, q_norm_g, k_norm_g, w_attn_br, w_out,
           w_router_group, b_router_group, w_router_expert, b_router_expert, w_expert_gate_up, w_expert_down):
    assert x_prompt.shape[0] == 1 and w_ada.shape[0] == 1
    n_prompt = x_prompt.shape[1]
    n_seq, n_step = x_sample.shape[0], x_sample.shape[1]
    n_samp = n_seq * n_step
    assert n_samp % TOK_TILE == 0 and n_prompt % ATT_SB == 0 and (n_prompt + n_samp) % PROJ_TM == 0
    x_p = x_prompt.reshape(n_prompt, D_MODEL)
    x_s = x_sample.reshape(n_samp, D_MODEL)

    pad = (-(n_seq + 1)) % 8
    c_all = jnp.concatenate([c_sample, c_prompt, jnp.zeros((pad, D_MODEL), F32)], axis=0)
    mod = _ada(c_all, w_ada[0], b_ada[0])
    mod_p = [mod[n_seq:n_seq + 1, k * D_MODEL:(k + 1) * D_MODEL] for k in range(6)]
    mod_s = [jnp.repeat(mod[:n_seq, k * D_MODEL:(k + 1) * D_MODEL], n_step, axis=0) for k in range(6)]
    sh1_p, sc1_p, gt1_p, sh2_p, sc2_p, gt2_p = mod_p
    sh1_s, sc1_s, gt1_s, sh2_s, sc2_s, gt2_s = mod_s

    h1 = _modnorm(x_p, x_s, norm1_g[0].reshape(1, D_MODEL), sc1_p, sh1_p, sc1_s, sh1_s)
    rc, rs1, rs2 = _rope_tables(n_prompt, n_seq, n_step)
    proj = _inproj(h1, w_in[0], rc, rs1, rs2, q_norm_g[0].reshape(1, HEAD_DIM), k_norm_g[0].reshape(1, HEAD_DIM))

    pw_re, pw_im, bb_re, bb_im = _ssm_prep(ssm_a_re[0], ssm_a_im[0], ssm_log_dt[0], ssm_b_re[0], ssm_b_im[0])
    b_mat, c_mat = _ssm_block_matrices(bb_re, bb_im, ssm_c_re[0], ssm_c_im[0])
    d_skip = ssm_d[0].reshape(1, SSM_WIDTH)
    yg_p, fre_p, fim_p = _s5_prompt(proj, n_prompt, d_skip, pw_re, pw_im, b_mat, c_mat)
    yg_s, fre_s, fim_s = _s5_sample(proj, n_prompt, n_seq, n_step, d_skip, pw_re, pw_im, b_mat, c_mat,
                                    state_ssm_re[0].reshape(n_seq, SSM_FLAT), state_ssm_im[0].reshape(n_seq, SSM_FLAT))

    o_p = _attn_prompt(proj, n_prompt)
    o_s = _attn_sample(proj, n_prompt, n_seq, n_step, (cache_kv_w128[0], cache_kv_w512[0], cache_kv_w2048[0]))

    mixed = _mix(yg_p, yg_s, o_p, o_s, w_glu[0], w_attn_br[0], proj)

    w_router = jnp.concatenate([w_router_group[0], w_router_expert[0],
                                jnp.zeros((D_MODEL, LANES - N_EXPERT_GROUPS - N_EXPERTS), F32)], axis=1)
    b_router = jnp.concatenate([b_router_group[0], b_router_expert[0],
                                jnp.zeros((LANES - N_EXPERT_GROUPS - N_EXPERTS,), F32)]).reshape(1, LANES)
    x1, h2, route = _outproj(mixed, w_out[0].astype(BF16), x_p, x_s, norm2_g[0].reshape(1, D_MODEL),
                             w_router, b_router, gt1_p, sc2_p, sh2_p, gt1_s, sc2_s, sh2_s)
    y_p, y_s = _moe(h2, route, x1, w_expert_gate_up[0], w_expert_down[0], gt2_p, gt2_s, n_prompt, n_samp)

    kv_p, kv_s = [], []
    for g, (window, _) in enumerate(DILATION_PATTERNS):
        kc = slice(OFF_K + g * GROUP_WIDTH, OFF_K + (g + 1) * GROUP_WIDTH)
        vc = slice(OFF_V + g * GROUP_WIDTH, OFF_V + (g + 1) * GROUP_WIDTH)
        keep = min(window, n_prompt)
        rows_p = slice(n_prompt - keep, n_prompt)
        kp = proj[rows_p, kc].reshape(1, 1, keep, 1, HEADS_PER_GROUP, HEAD_DIM)
        vp = proj[rows_p, vc].reshape(1, 1, keep, 1, HEADS_PER_GROUP, HEAD_DIM)
        kv_p.append(jnp.concatenate([kp, vp], axis=3))
        ksm = proj[n_prompt:, kc].reshape(1, n_seq, n_step, 1, HEADS_PER_GROUP, HEAD_DIM)
        vsm = proj[n_prompt:, vc].reshape(1, n_seq, n_step, 1, HEADS_PER_GROUP, HEAD_DIM)
        kv_s.append(jnp.concatenate([ksm, vsm], axis=3))

    state_shape_p = (1, 1, SSM_GROUPS, SSM_STATE)
    state_shape_s = (1, n_seq, SSM_GROUPS, SSM_STATE)
    return (y_p.reshape(1, n_prompt, D_MODEL), y_s.reshape(n_seq, n_step, D_MODEL),
            kv_p[0], kv_p[1], kv_p[2], fre_p.reshape(state_shape_p), fim_p.reshape(state_shape_p),
            kv_s[0], kv_s[1], kv_s[2], fre_s.reshape(state_shape_s), fim_s.reshape(state_shape_s))
```

```python
import functools
import math

import numpy as np
import jax
import jax.numpy as jnp
from jax import lax
from jax.experimental import pallas as pl
from jax.experimental.pallas import tpu as pltpu

F32 = jnp.float32
BF16 = jnp.bfloat16

D_MODEL = 2048
PAST_LEN = 2048
SSM_WIDTH = D_MODEL // 2
SSM_GROUP = 16
SSM_GROUPS = SSM_WIDTH // SSM_GROUP
SSM_STATE = 64
SSM_FLAT = SSM_GROUPS * SSM_STATE
HEAD_DIM = 128
DILATION_PATTERNS = ((128, 1), (512, 4), (2048, 16))
N_PATTERNS = 3
HEADS_PER_GROUP = 4
GROUP_WIDTH = HEADS_PER_GROUP * HEAD_DIM
ATTN_WIDTH = N_PATTERNS * GROUP_WIDTH
ROT_DIM = HEAD_DIM // 4
ROPE_THETA = 500000.0
OFF_Q = SSM_WIDTH
OFF_K = OFF_Q + ATTN_WIDTH
OFF_V = OFF_K + ATTN_WIDTH
OFF_G = OFF_V + ATTN_WIDTH
IN_COLS = OFF_G + 2 * D_MODEL
N_EXPERT_GROUPS = 4
EXPERTS_PER_GROUP = 4
N_EXPERTS = 16
EXPERT_FF = D_MODEL // 4
EPS = 1e-6
NEG = -1e30

LANES = 128
SUBLANES = 8
VMEM_LIMIT = 56 * 1024 * 1024

TOK_TILE = 512
SMALL_TOK_TILE = 256
PROJ_TM = 544
PROJ_TN = 2432
SCAN_L = 32
SCAN_NC = 16
SCAN_TT = SCAN_L * SCAN_NC
SSM_BLK = 8
SSM_BLK_STATE = SSM_BLK * SSM_STATE
SSM_PAR = 4
MOE_TM = 512
ROUTE_E1, ROUTE_E2, ROUTE_W1, ROUTE_W2 = 0, 1, 2, 3
MOD_SH1, MOD_SC1, MOD_GT1, MOD_SH2, MOD_SC2, MOD_GT2 = range(6)
ATT_SB = 2048
BAND = 128
ATT_UNROLL = 16


def _cparams(sem, vmem=VMEM_LIMIT):
    return pltpu.CompilerParams(dimension_semantics=sem, vmem_limit_bytes=vmem)


def _sigmoid(x):
    return 0.5 * jnp.tanh(0.5 * x) + 0.5


def _row_halves(rows):
    half = rows // 2
    return (slice(0, half), slice(half, rows))


def _gelu_tanh(x):
    c = math.sqrt(2.0 / math.pi)
    return 0.5 * x * (1.0 + jnp.tanh(c * (x + 0.044715 * (x * x * x))))


def _ada_kernel(c_ref, w_ref, b_ref, o_ref):
    c = c_ref[...]
    cs = (c * _sigmoid(c)).astype(BF16)
    o_ref[...] = jnp.dot(cs, w_ref[...].astype(BF16), preferred_element_type=F32) + b_ref[...]


def _ada(c_all, w_ada, b_ada):
    rows = c_all.shape[0]
    n_out = w_ada.shape[1]
    tn = 1024
    return pl.pallas_call(
        _ada_kernel,
        grid=(n_out // tn,),
        in_specs=[pl.BlockSpec((rows, D_MODEL), lambda n: (0, 0)),
                  pl.BlockSpec((D_MODEL, tn), lambda n: (0, n)),
                  pl.BlockSpec((1, tn), lambda n: (0, n))],
        out_specs=pl.BlockSpec((rows, tn), lambda n: (0, n)),
        out_shape=jax.ShapeDtypeStruct((rows, n_out), F32),
        compiler_params=_cparams(("arbitrary",)),
        name="ada_mod",
    )(c_all, w_ada, b_ada.reshape(1, n_out))


def _modnorm_kernel(xp_ref, xs_ref, g_ref, scp_ref, shp_ref, scs_ref, shs_ref, o_ref, *, n_prompt_tiles):
    is_s = pl.program_id(0) >= n_prompt_tiles

    def norm(x_ref, sc_ref, sh_ref, rows):
        x = x_ref[...]
        ms = jnp.mean(x * x, axis=-1, keepdims=True)
        y = x * lax.rsqrt(ms + EPS) * g_ref[...]
        o_ref[...] = (y * (1.0 + sc_ref[rows, :]) + sh_ref[rows, :]).astype(o_ref.dtype)

    pl.when(jnp.logical_not(is_s))(functools.partial(norm, xp_ref, scp_ref, shp_ref, slice(0, 1)))
    pl.when(is_s)(functools.partial(norm, xs_ref, scs_ref, shs_ref, slice(None)))


def _mod_specs(k, tm, n_samp, n_prompt_tiles):
    prompt = pl.BlockSpec((SUBLANES, D_MODEL), lambda i, *_: (n_samp // SUBLANES, k))
    sample = pl.BlockSpec((tm, D_MODEL), lambda i, *_: (jnp.maximum(i - n_prompt_tiles, 0), k))
    return prompt, sample


def _modnorm(x_p, x_s, g, mod):
    tp, ts = x_p.shape[0], x_s.shape[0]
    tm = TOK_TILE
    npt, nst = tp // tm, ts // tm
    row = lambda i: (jnp.minimum(i, npt - 1), 0)
    srow = lambda i: (jnp.maximum(i - npt, 0), 0)
    const = lambda i: (0, 0)
    sc_p, sc_s = _mod_specs(MOD_SC1, tm, ts, npt)
    sh_p, sh_s = _mod_specs(MOD_SH1, tm, ts, npt)
    return pl.pallas_call(
        functools.partial(_modnorm_kernel, n_prompt_tiles=npt),
        grid=(npt + nst,),
        in_specs=[pl.BlockSpec((tm, D_MODEL), row), pl.BlockSpec((tm, D_MODEL), srow),
                  pl.BlockSpec((1, D_MODEL), const), sc_p, sh_p, sc_s, sh_s],
        out_specs=pl.BlockSpec((tm, D_MODEL), lambda i: (i, 0)),
        out_shape=jax.ShapeDtypeStruct((tp + ts, D_MODEL), BF16),
        compiler_params=_cparams(("arbitrary",)),
        name="modnorm1",
    )(x_p, x_s, g, mod, mod, mod, mod)


def _inproj_kernel(h_ref, w_ref, rc_ref, rs1_ref, rs2_ref, qg_ref, kg_ref, o_ref, wbf_ref, *, heads_per_tile):
    n = pl.program_id(0)

    @pl.when(pl.program_id(1) == 0)
    def _():
        wbf_ref[...] = w_ref[...].astype(BF16)

    half = ROT_DIM // 2
    q_heads = range(OFF_Q // HEAD_DIM, OFF_K // HEAD_DIM)
    k_heads = range(OFF_K // HEAD_DIM, OFF_V // HEAD_DIM)
    pair = 2 * HEAD_DIM

    def tile(col_tile):
        h = h_ref[...]
        for c0 in range(0, heads_per_tile * HEAD_DIM, pair):
            width = min(pair, heads_per_tile * HEAD_DIM - c0)
            acc = jnp.dot(h, wbf_ref[:, c0:c0 + width], preferred_element_type=F32)
            for c in range(c0, c0 + width, HEAD_DIM):
                slot = col_tile * heads_per_tile + c // HEAD_DIM
                x = acc[:, c - c0:c - c0 + HEAD_DIM]
                if slot in q_heads or slot in k_heads:
                    gain = qg_ref[...] if slot in q_heads else kg_ref[...]
                    ms = jnp.mean(x * x, axis=-1, keepdims=True)
                    y = x * lax.rsqrt(ms + EPS) * gain
                    up = pltpu.roll(y, HEAD_DIM - half, 1)
                    dn = pltpu.roll(y, half, 1)
                    x = y * rc_ref[...] + up * rs1_ref[...] + dn * rs2_ref[...]
                o_ref[:, c:c + HEAD_DIM] = x

    for col_tile in range(IN_COLS // (heads_per_tile * HEAD_DIM)):
        pl.when(n == col_tile)(functools.partial(tile, col_tile))


def _inproj(h, w_in, rc, rs1, rs2, qg, kg):
    n_tok = h.shape[0]
    tm, tn = PROJ_TM, PROJ_TN
    tab = pl.BlockSpec((tm, HEAD_DIM), lambda n, m: (m, 0))
    gain = pl.BlockSpec((1, HEAD_DIM), lambda n, m: (0, 0))
    return pl.pallas_call(
        functools.partial(_inproj_kernel, heads_per_tile=tn // HEAD_DIM),
        grid=(IN_COLS // tn, n_tok // tm),
        in_specs=[pl.BlockSpec((tm, D_MODEL), lambda n, m: (m, 0)),
                  pl.BlockSpec((D_MODEL, tn), lambda n, m: (0, n), pipeline_mode=pl.Buffered(1)),
                  tab, tab, tab, gain, gain],
        out_specs=pl.BlockSpec((tm, tn), lambda n, m: (m, n)),
        out_shape=jax.ShapeDtypeStruct((n_tok, IN_COLS), F32),
        scratch_shapes=[pltpu.VMEM((D_MODEL, tn), BF16)],
        compiler_params=_cparams(("arbitrary", "arbitrary")),
        name="in_proj",
    )(h, w_in, rc, rs1, rs2, qg, kg)


def _ssm_prep_kernel(are_ref, aim_ref, ldt_ref, arer_ref, aimr_ref, ldtr_ref, bre_ref, bim_ref,
                     pre_ref, pim_ref, bbre_ref, bbim_ref):
    def discretise(a_re, a_im, log_dt):
        dt = jnp.exp(log_dt)
        mag = jnp.exp(a_re * dt)
        return mag * jnp.cos(a_im * dt), mag * jnp.sin(a_im * dt)

    ab_re, ab_im = discretise(are_ref[...], aim_ref[...], ldt_ref[...])
    p_re, p_im = ab_re, ab_im
    for i in range(SCAN_L):
        pre_ref[i:i + 1, :] = p_re
        pim_ref[i:i + 1, :] = p_im
        p_re, p_im = p_re * ab_re - p_im * ab_im, p_re * ab_im + p_im * ab_re

    a_re, a_im = arer_ref[...], aimr_ref[...]
    r_re, r_im = discretise(a_re, a_im, ldtr_ref[...])
    nr, ni = r_re - 1.0, r_im
    den = a_re * a_re + a_im * a_im
    z_re = (nr * a_re + ni * a_im) / den
    z_im = (ni * a_re - nr * a_im) / den
    b_re, b_im = bre_ref[...], bim_ref[...]
    bbre_ref[...] = z_re * b_re - z_im * b_im
    bbim_ref[...] = z_re * b_im + z_im * b_re


def _ssm_prep(a_re, a_im, log_dt, b_re, b_im):
    g, p, n = b_re.shape
    flat = lambda x: x.reshape(1, g * p)
    rep = lambda x: jnp.repeat(x, n, axis=1)
    ldt_gp = jnp.broadcast_to(log_dt[:, None], (g, p))
    ldt_rep = jnp.broadcast_to(log_dt[:, None], (g, p * n))
    out_shape = [jax.ShapeDtypeStruct((SCAN_L, g * p), F32)] * 2 + [jax.ShapeDtypeStruct((g, p * n), F32)] * 2
    return pl.pallas_call(_ssm_prep_kernel, out_shape=out_shape, name="ssm_prep")(
        flat(a_re), flat(a_im), flat(ldt_gp), rep(a_re), rep(a_im), ldt_rep,
        b_re.reshape(g, p * n), b_im.reshape(g, p * n))


def _ssm_block_matrices(bb_re, bb_im, c_re, c_im):
    g, p, n = SSM_GROUPS, SSM_STATE, SSM_GROUP
    nb = g // SSM_BLK
    eye = jnp.eye(SSM_BLK, dtype=F32)

    def in_mat(bb):
        x = bb.reshape(nb, SSM_BLK, p, n)
        return jnp.einsum('bgpm,gh->bgmhp', x, eye).reshape(nb, SSM_BLK * n, SSM_BLK * p)

    def out_mat(c):
        x = c.reshape(nb, SSM_BLK, n, p)
        return jnp.einsum('bgnp,gh->bgphn', x, eye).reshape(nb, SSM_BLK * p, SSM_BLK * n)

    b_mat = jnp.concatenate([in_mat(bb_re), in_mat(bb_im)], axis=2).astype(BF16)
    c_mat = jnp.concatenate([out_mat(c_re), -out_mat(c_im)], axis=1).astype(BF16)
    return b_mat, c_mat


def _cmul_add(a_re, a_im, s_re, s_im, b_re, b_im):
    return a_re * s_re - a_im * s_im + b_re, a_re * s_im + a_im * s_re + b_im


def _s5_prompt_kernel(*refs):
    par = SSM_PAR
    u_refs = refs[:par]
    (d_ref, pre_ref, pim_ref, bm_ref, cm_ref, y_ref, fre_ref, fim_ref,
     up_scr, bu_scr, lhs_scr, in_re_scr, in_im_scr, car_re, car_im, yn_scr) = refs[par:]
    nc, ln, w = SCAN_NC, SCAN_L, SSM_BLK_STATE

    @pl.when(pl.program_id(1) == 0)
    def _():
        car_re[...] = jnp.zeros_like(car_re)
        car_im[...] = jnp.zeros_like(car_im)

    for b in range(par):
        lanes = slice(b * w, (b + 1) * w)
        rows = lambda i: slice(i * nc, (i + 1) * nc)
        for i in range(ln):
            up_scr[b, rows(i), :] = u_refs[b][pl.ds(i, nc, stride=ln), :]
        up = up_scr[b]
        bu_scr[b] = jnp.dot(up.astype(BF16), bm_ref[b], preferred_element_type=F32)

        a_re = jnp.broadcast_to(pre_ref[0:1, lanes], (nc, w))
        a_im = jnp.broadcast_to(pim_ref[0:1, lanes], (nc, w))
        s_re = s_im = jnp.zeros((nc, w), F32)
        for i in range(ln):
            s_re, s_im = _cmul_add(a_re, a_im, s_re, s_im, bu_scr[b, rows(i), 0:w], bu_scr[b, rows(i), w:2 * w])
            bu_scr[b, rows(i), 0:w] = s_re
            bu_scr[b, rows(i), w:2 * w] = s_im

        al_re, al_im = pre_ref[ln - 1:ln, lanes], pim_ref[ln - 1:ln, lanes]
        c_re, c_im = car_re[b], car_im[b]
        for c in range(nc):
            in_re_scr[b, c:c + 1, :] = c_re
            in_im_scr[b, c:c + 1, :] = c_im
            c_re, c_im = _cmul_add(al_re, al_im, c_re, c_im, s_re[c:c + 1, :], s_im[c:c + 1, :])
        car_re[b] = c_re
        car_im[b] = c_im
        fre_ref[:, lanes] = c_re
        fim_ref[:, lanes] = c_im
        in_re, in_im = in_re_scr[b], in_im_scr[b]

        for i in range(ln):
            p_re = jnp.broadcast_to(pre_ref[i:i + 1, lanes], (nc, w))
            p_im = jnp.broadcast_to(pim_ref[i:i + 1, lanes], (nc, w))
            f_re, f_im = _cmul_add(p_re, p_im, in_re, in_im, bu_scr[b, rows(i), 0:w], bu_scr[b, rows(i), w:2 * w])
            lhs_scr[b, rows(i), 0:w] = f_re.astype(BF16)
            lhs_scr[b, rows(i), w:2 * w] = f_im.astype(BF16)

        y = (jnp.dot(lhs_scr[b], cm_ref[b], preferred_element_type=F32)
             + d_ref[:, b * LANES:(b + 1) * LANES] * up)
        for i in range(ln):
            yn_scr[b, pl.ds(i, nc, stride=ln), :] = y[rows(i), :]
        y_ref[:, b * LANES:(b + 1) * LANES] = _gelu_tanh(yn_scr[b]).astype(y_ref.dtype)


def _s5_prompt(proj, n_prompt, d_skip, pw_re, pw_im, b_mat, c_mat):
    par = SSM_PAR
    nb = SSM_GROUPS // SSM_BLK // par
    tt, w = SCAN_TT, SSM_BLK_STATE
    u_spec = lambda b: pl.BlockSpec((tt, LANES), lambda j, i: (i, par * j + b))
    return pl.pallas_call(
        _s5_prompt_kernel,
        grid=(nb, n_prompt // tt),
        in_specs=[u_spec(b) for b in range(par)] + [
            pl.BlockSpec((1, par * LANES), lambda j, i: (0, j)),
            pl.BlockSpec((SCAN_L, par * w), lambda j, i: (0, j)),
            pl.BlockSpec((SCAN_L, par * w), lambda j, i: (0, j)),
            pl.BlockSpec((par, LANES, 2 * w), lambda j, i: (j, 0, 0)),
            pl.BlockSpec((par, 2 * w, LANES), lambda j, i: (j, 0, 0))],
        out_specs=[pl.BlockSpec((tt, par * LANES), lambda j, i: (i, j)),
                   pl.BlockSpec((1, par * w), lambda j, i: (0, j)),
                   pl.BlockSpec((1, par * w), lambda j, i: (0, j))],
        out_shape=[jax.ShapeDtypeStruct((n_prompt, SSM_WIDTH), BF16),
                   jax.ShapeDtypeStruct((1, SSM_FLAT), F32),
                   jax.ShapeDtypeStruct((1, SSM_FLAT), F32)],
        scratch_shapes=[pltpu.VMEM((par, tt, LANES), F32), pltpu.VMEM((par, tt, 2 * w), F32),
                        pltpu.VMEM((par, tt, 2 * w), BF16),
                        pltpu.VMEM((par, SCAN_NC, w), F32), pltpu.VMEM((par, SCAN_NC, w), F32),
                        pltpu.VMEM((par, 1, w), F32), pltpu.VMEM((par, 1, w), F32),
                        pltpu.VMEM((par, tt, LANES), F32)],
        compiler_params=_cparams(("arbitrary", "arbitrary")),
        name="s5_prompt",
    )(*([proj] * par), d_skip, pw_re, pw_im, b_mat, c_mat)


def _s5_sample_kernel(u_ref, d_ref, pre_ref, pim_ref, bm_ref, cm_ref, s0re_ref, s0im_ref,
                      y_ref, fre_ref, fim_ref, up_scr, bu_scr, lhs_scr, yn_scr, *, n_seq, n_step):
    w = SSM_BLK_STATE
    rb = 16
    for s in range(n_step):
        up_scr[s * n_seq:(s + 1) * n_seq, :] = u_ref[pl.ds(s, n_seq, stride=n_step), :]
    up = up_scr[...]
    bu_scr[...] = jnp.dot(up.astype(BF16), bm_ref[0], preferred_element_type=F32)
    a_re = jnp.broadcast_to(pre_ref[0:1, :], (rb, w))
    a_im = jnp.broadcast_to(pim_ref[0:1, :], (rb, w))

    def seq_block(b, _):
        r0 = pl.multiple_of(b * rb, rb)
        s_re, s_im = s0re_ref[pl.ds(r0, rb), :], s0im_ref[pl.ds(r0, rb), :]
        for s in range(n_step):
            rows = pl.ds(pl.multiple_of(s * n_seq + r0, rb), rb)
            s_re, s_im = _cmul_add(a_re, a_im, s_re, s_im, bu_scr[rows, 0:w], bu_scr[rows, w:2 * w])
            lhs_scr[rows, 0:w] = s_re.astype(BF16)
            lhs_scr[rows, w:2 * w] = s_im.astype(BF16)
        fre_ref[pl.ds(r0, rb), :] = s_re
        fim_ref[pl.ds(r0, rb), :] = s_im
        return 0

    lax.fori_loop(0, n_seq // rb, seq_block, 0)
    y = jnp.dot(lhs_scr[...], cm_ref[0], preferred_element_type=F32) + d_ref[...] * up
    for s in range(n_step):
        yn_scr[pl.ds(s, n_seq, stride=n_step), :] = y[s * n_seq:(s + 1) * n_seq, :]
    y_ref[...] = _gelu_tanh(yn_scr[...]).astype(y_ref.dtype)


def _s5_sample(proj, n_prompt, n_seq, n_step, d_skip, pw_re, pw_im, b_mat, c_mat, s0_re, s0_im):
    nb = SSM_GROUPS // SSM_BLK
    rows, w = n_seq * n_step, SSM_BLK_STATE
    rblk = n_prompt // rows
    return pl.pallas_call(
        functools.partial(_s5_sample_kernel, n_seq=n_seq, n_step=n_step),
        grid=(nb,),
        in_specs=[pl.BlockSpec((rows, LANES), lambda j: (rblk, j)),
                  pl.BlockSpec((1, LANES), lambda j: (0, j)),
                  pl.BlockSpec((SCAN_L, w), lambda j: (0, j)),
                  pl.BlockSpec((SCAN_L, w), lambda j: (0, j)),
                  pl.BlockSpec((1, LANES, 2 * w), lambda j: (j, 0, 0)),
                  pl.BlockSpec((1, 2 * w, LANES), lambda j: (j, 0, 0)),
                  pl.BlockSpec((n_seq, w), lambda j: (0, j)),
                  pl.BlockSpec((n_seq, w), lambda j: (0, j))],
        out_specs=[pl.BlockSpec((rows, LANES), lambda j: (0, j)),
                   pl.BlockSpec((n_seq, w), lambda j: (0, j)),
                   pl.BlockSpec((n_seq, w), lambda j: (0, j))],
        out_shape=[jax.ShapeDtypeStruct((rows, SSM_WIDTH), BF16),
                   jax.ShapeDtypeStruct((n_seq, SSM_FLAT), F32),
                   jax.ShapeDtypeStruct((n_seq, SSM_FLAT), F32)],
        scratch_shapes=[pltpu.VMEM((rows, LANES), F32), pltpu.VMEM((rows, 2 * w), F32),
                        pltpu.VMEM((rows, 2 * w), BF16), pltpu.VMEM((rows, LANES), F32)],
        compiler_params=_cparams(("arbitrary",)),
        name="s5_sample",
    )(proj, d_skip, pw_re, pw_im, b_mat, c_mat, s0_re, s0_im)


def _attn_prompt_kernel(*refs):
    ins, o_ref, scr = refs[:15], refs[15], refs[16:]
    sb = pl.program_id(0)
    scale = HEAD_DIM ** -0.5
    qi = lax.broadcasted_iota(jnp.int32, (BAND, 2 * BAND), 0)
    kj = lax.broadcasted_iota(jnp.int32, (BAND, 2 * BAND), 1)
    dist = qi + BAND - kj
    band_ok = (dist >= 0) & (dist <= BAND)

    for g, (_, dil) in enumerate(DILATION_PATTERNS):
        q_ref, k_ref, v_ref, kp_ref, vp_ref = ins[5 * g:5 * g + 5]
        kbuf, vbuf, o_scr, m_scr, l_scr = scr[5 * g:5 * g + 5]
        pre = BAND * dil
        kbuf[0:pre, :] = kp_ref[...]
        kbuf[pre:pre + ATT_SB, :] = k_ref[...]
        vbuf[0:pre, :] = vp_ref[...]
        vbuf[pre:pre + ATT_SB, :] = v_ref[...]
        nblk = ATT_SB // pre

        def block(idx, _, dil=dil, pre=pre, nblk=nblk, q_ref=q_ref, kbuf=kbuf, vbuf=vbuf,
                  o_scr=o_scr, m_scr=m_scr, l_scr=l_scr):
            r = idx // nblk
            b = idx - r * nblk
            row0 = r + b * pre
            if dil == 1:
                q_rows = pl.ds(pl.multiple_of(row0, BAND), BAND)
                kv_rows = pl.ds(pl.multiple_of(row0, BAND), 2 * BAND)
            else:
                q_rows = pl.ds(row0, BAND, stride=dil)
                kv_rows = pl.ds(row0, 2 * BAND, stride=dil)
            q = (q_ref[q_rows, :] * scale).astype(BF16)
            kw = kbuf[kv_rows, :].astype(BF16)
            vw = vbuf[kv_rows, :].astype(BF16)
            s = lax.dot_general(q, kw, (((1,), (1,)), ((), ())), preferred_element_type=F32)
            s = jnp.where(band_ok & ((kj >= BAND) | (sb > 0) | (b > 0)), s, NEG)
            m = jnp.max(s, axis=-1, keepdims=True)
            p = jnp.exp(s - m)
            l = jnp.sum(p, axis=-1, keepdims=True)
            o = jnp.dot(p.astype(BF16), vw, preferred_element_type=F32)
            o_scr[q_rows, :] = o
            m_scr[q_rows, :] = jnp.broadcast_to(m, (BAND, HEAD_DIM))
            l_scr[q_rows, :] = jnp.broadcast_to(l, (BAND, HEAD_DIM))
            return 0

        lax.fori_loop(0, ATT_SB // BAND, block, 0, unroll=ATT_UNROLL)

    ms = [scr[5 * g + 3][...] for g in range(N_PATTERNS)]
    mx = jnp.maximum(jnp.maximum(ms[0], ms[1]), ms[2])
    num = jnp.zeros((ATT_SB, HEAD_DIM), F32)
    den = jnp.zeros((ATT_SB, HEAD_DIM), F32)
    for g in range(N_PATTERNS):
        wgt = jnp.exp(ms[g] - mx)
        num = num + wgt * scr[5 * g + 2][...]
        den = den + wgt * scr[5 * g + 4][...]
    o_ref[...] = num / den


def _attn_prompt(proj, n_prompt):
    hcol = lambda off, g, j: (off + g * GROUP_WIDTH) // HEAD_DIM + j
    in_specs, scratch = [], []
    for g, (_, dil) in enumerate(DILATION_PATTERNS):
        pre = BAND * dil
        per = ATT_SB // pre
        cur = lambda off, g=g: pl.BlockSpec((ATT_SB, HEAD_DIM), lambda sb, j: (sb, hcol(off, g, j)))
        prev = lambda off, g=g, per=per, pre=pre: pl.BlockSpec(
            (pre, HEAD_DIM), lambda sb, j: (jnp.maximum(sb * per - 1, 0), hcol(off, g, j)))
        in_specs += [cur(OFF_Q), cur(OFF_K), cur(OFF_V), prev(OFF_K), prev(OFF_V)]
        scratch += [pltpu.VMEM((pre + ATT_SB, HEAD_DIM), F32), pltpu.VMEM((pre + ATT_SB, HEAD_DIM), F32),
                    pltpu.VMEM((ATT_SB, HEAD_DIM), F32), pltpu.VMEM((ATT_SB, HEAD_DIM), F32),
                    pltpu.VMEM((ATT_SB, HEAD_DIM), F32)]
    return pl.pallas_call(
        _attn_prompt_kernel,
        grid=(n_prompt // ATT_SB, HEADS_PER_GROUP),
        in_specs=in_specs,
        out_specs=pl.BlockSpec((ATT_SB, HEAD_DIM), lambda sb, j: (sb, j)),
        out_shape=jax.ShapeDtypeStruct((n_prompt, GROUP_WIDTH), F32),
        scratch_shapes=scratch,
        compiler_params=_cparams(("arbitrary", "arbitrary")),
        name="attn_prompt",
    )(*([proj] * 15))


SEQ_PER_STEP = 2
KV_PLANES = 2 * HEADS_PER_GROUP


def _compact_pitch(n_step):
    tiles = n_step * KV_PLANES // 8
    return 8 * (tiles + 1 - tiles % 2)


def _sample_bias(n_step):
    rows = HEADS_PER_GROUP * n_step
    step = np.arange(rows) % n_step
    cache_bias, new_bias = [], []
    for (window, dil) in DILATION_PATTERNS:
        wb = min(window, PAST_LEN)
        band = window // dil
        if dil > n_step:
            res, i = np.meshgrid(np.arange(n_step), np.arange(wb // dil), indexing='ij')
            c = (i * dil + res).reshape(-1)
        else:
            c = np.arange(wb)
        delta = wb + step[:, None] - c[None, :]
        ok = (delta >= 0) & (delta % dil == 0) & (delta // dil <= band)
        cache_bias.append(np.where(ok, 0.0, NEG).astype(np.float32))
        nb = np.full((SEQ_PER_STEP, rows, LANES), NEG, np.float32)
        for a in range(SEQ_PER_STEP):
            for sp in range(n_step):
                dl = step - sp
                okn = (dl >= 0) & (dl % dil == 0) & (dl // dil <= band)
                nb[a, :, a * n_step + sp] = np.where(okn, 0.0, NEG)
        new_bias.append(nb)
    return cache_bias, new_bias


def _attn_sample_kernel(*refs, n_step):
    (q0, k0, v0, q1, k1, v1, q2, k2, v2, c0, c1, c2, cb0, cb1, cb2, nb0, nb1, nb2, o_ref) = refs
    qs, ks, vs = (q0, q1, q2), (k0, k1, k2), (v0, v1, v2)
    caches, cbias, nbias = (c0, c1, c2), (cb0, cb1, cb2), (nb0, nb1, nb2)
    rows = HEADS_PER_GROUP * n_step
    gw = GROUP_WIDTH
    scale = HEAD_DIM ** -0.5
    row_head = lax.broadcasted_iota(jnp.int32, (rows, gw), 0) // n_step
    lane_head = lax.broadcasted_iota(jnp.int32, (rows, gw), 1) // HEAD_DIM
    own_head = row_head == lane_head
    nt = (((1,), (1,)), ((), ()))
    pad = jnp.zeros((LANES - SEQ_PER_STEP * n_step, gw), F32)

    def planes(load):
        k = jnp.concatenate([load(h) for h in range(HEADS_PER_GROUP)], axis=1)
        v = jnp.concatenate([load(HEADS_PER_GROUP + h) for h in range(HEADS_PER_GROUP)], axis=1)
        return k.astype(BF16), v.astype(BF16)

    for a in range(SEQ_PER_STEP):
        pieces = []
        for g, (window, dil) in enumerate(DILATION_PATTERNS):
            q = qs[g][a * n_step:(a + 1) * n_step, :] * scale
            qbd = jnp.where(own_head, jnp.concatenate([q] * HEADS_PER_GROUP, axis=0), 0.0).astype(BF16)
            cache = caches[g]
            per_seq = cache.shape[0] // SEQ_PER_STEP
            if dil > n_step:
                pitch = cache.shape[1]
                flat = cache.reshape(cache.shape[0] * pitch, HEAD_DIM)
                kvs = [planes(lambda p, r=r: flat[pl.ds(a * per_seq * pitch + r * KV_PLANES + p, per_seq,
                                                        stride=pitch), :])
                       for r in range(n_step)]
            else:
                kvs = [planes(lambda p: cache[pl.ds(a * per_seq + p, per_seq // KV_PLANES, stride=KV_PLANES), :])]
            sc = jnp.concatenate([lax.dot_general(qbd, k, nt, preferred_element_type=F32) for k, _ in kvs], axis=1)
            pieces.append((sc + cbias[g][...], [v for _, v in kvs]))
            k_new = jnp.concatenate([ks[g][...], pad], axis=0).astype(BF16)
            v_new = jnp.concatenate([vs[g][...], pad], axis=0).astype(BF16)
            sn = lax.dot_general(qbd, k_new, nt, preferred_element_type=F32) + nbias[g][a]
            pieces.append((sn, [v_new]))
        m = functools.reduce(jnp.maximum, [jnp.max(s, axis=-1, keepdims=True) for s, _ in pieces])
        l = jnp.zeros((rows, 1), F32)
        acc = jnp.zeros((rows, gw), F32)
        for s, vals in pieces:
            p = jnp.exp(s - m)
            l = l + jnp.sum(p, axis=-1, keepdims=True)
            pb = p.astype(BF16)
            nk = pb.shape[1] // len(vals)
            for r, v in enumerate(vals):
                acc = acc + jnp.dot(pb[:, r * nk:(r + 1) * nk], v, preferred_element_type=F32)
        acc = jnp.where(own_head, acc, 0.0)
        o16 = functools.reduce(lambda x, y: x + y,
                               [acc[:, h * HEAD_DIM:(h + 1) * HEAD_DIM] for h in range(HEADS_PER_GROUP)]) / l
        for h in range(HEADS_PER_GROUP):
            o_ref[a * n_step:(a + 1) * n_step, h * HEAD_DIM:(h + 1) * HEAD_DIM] = o16[h * n_step:(h + 1) * n_step, :]


def _attn_sample(proj, n_prompt, n_seq, n_step, caches):
    rows = SEQ_PER_STEP * n_step
    rblk = n_prompt // rows
    cache_bias, new_bias = _sample_bias(n_step)
    tok = lambda off, g: pl.BlockSpec((rows, GROUP_WIDTH), lambda i: (rblk + i, (off + g * GROUP_WIDTH) // GROUP_WIDTH))
    in_specs, args = [], []
    for g in range(N_PATTERNS):
        in_specs += [tok(OFF_Q, g), tok(OFF_K, g), tok(OFF_V, g)]
        args += [proj, proj, proj]
    for g, (window, dil) in enumerate(DILATION_PATTERNS):
        c = caches[g]
        wb = c.shape[1]
        if dil > n_step:
            pitch = _compact_pitch(n_step)
            assert pitch <= dil * KV_PLANES and wb % dil == 0
            c = c.reshape(n_seq * (wb // dil), dil * KV_PLANES, HEAD_DIM)
            in_specs.append(pl.BlockSpec((SEQ_PER_STEP * (wb // dil), pitch, HEAD_DIM), lambda i: (i, 0, 0)))
        else:
            c = c.reshape(n_seq * wb * KV_PLANES, HEAD_DIM)
            in_specs.append(pl.BlockSpec((SEQ_PER_STEP * wb * KV_PLANES, HEAD_DIM), lambda i: (i, 0)))
        args.append(c)
    for b in cache_bias:
        in_specs.append(pl.BlockSpec(b.shape, lambda i: (0, 0)))
        args.append(jnp.asarray(b))
    for b in new_bias:
        in_specs.append(pl.BlockSpec(b.shape, lambda i: (0, 0, 0)))
        args.append(jnp.asarray(b))
    return pl.pallas_call(
        functools.partial(_attn_sample_kernel, n_step=n_step),
        grid=(n_seq // SEQ_PER_STEP,),
        in_specs=in_specs,
        out_specs=pl.BlockSpec((rows, GROUP_WIDTH), lambda i: (i, 0)),
        out_shape=jax.ShapeDtypeStruct((n_seq * n_step, GROUP_WIDTH), F32),
        compiler_params=_cparams(("arbitrary",)),
        name="attn_sample",
    )(*args)


def _mix_kernel(yp_ref, ys_ref, op_ref, os_ref, wa_ref, wb_ref, wbr_ref, ga_ref, gb_ref, o_ref,
                wa_bf, wb_bf, wbr_bf, *, n_prompt_tiles):
    i = pl.program_id(1)

    @pl.when(i == 0)
    def _():
        wa_bf[...] = wa_ref[...].astype(BF16)
        wb_bf[...] = wb_ref[...].astype(BF16)
        wbr_bf[...] = wbr_ref[...].astype(BF16)

    is_s = i >= n_prompt_tiles
    for rows in _row_halves(o_ref.shape[0]):
        y = jnp.where(is_s, ys_ref[rows, :], yp_ref[rows, :])
        o = jnp.where(is_s, os_ref[rows, :], op_ref[rows, :]).astype(BF16)
        glu_a = jnp.dot(y, wa_bf[...], preferred_element_type=F32)
        glu_b = jnp.dot(y, wb_bf[...], preferred_element_type=F32)
        branch_a = glu_a * _sigmoid(glu_b)
        branch_b = jnp.dot(o, wbr_bf[...], preferred_element_type=F32)
        o_ref[rows, :] = (_sigmoid(ga_ref[rows, :]) * branch_a
                          + _sigmoid(gb_ref[rows, :]) * branch_b).astype(o_ref.dtype)


def _mix(y_p, y_s, o_p, o_s, w_glu, w_attn_br, proj):
    tp, ts = y_p.shape[0], y_s.shape[0]
    tm, tn = TOK_TILE, 512
    assert OFF_G % tn == 0
    npt, nst = tp // tm, ts // tm
    ncol = D_MODEL // tn
    prow = lambda n, i: (jnp.minimum(i, npt - 1), 0)
    srow = lambda n, i: (jnp.maximum(i - npt, 0), 0)
    return pl.pallas_call(
        functools.partial(_mix_kernel, n_prompt_tiles=npt),
        grid=(ncol, npt + nst),
        in_specs=[pl.BlockSpec((tm, SSM_WIDTH), prow), pl.BlockSpec((tm, SSM_WIDTH), srow),
                  pl.BlockSpec((tm, GROUP_WIDTH), prow), pl.BlockSpec((tm, GROUP_WIDTH), srow),
                  pl.BlockSpec((SSM_WIDTH, tn), lambda n, i: (0, n)),
                  pl.BlockSpec((SSM_WIDTH, tn), lambda n, i: (0, ncol + n)),
                  pl.BlockSpec((GROUP_WIDTH, tn), lambda n, i: (0, n)),
                  pl.BlockSpec((tm, tn), lambda n, i: (i, OFF_G // tn + n)),
                  pl.BlockSpec((tm, tn), lambda n, i: (i, OFF_G // tn + ncol + n))],
        out_specs=pl.BlockSpec((tm, tn), lambda n, i: (i, n)),
        out_shape=jax.ShapeDtypeStruct((tp + ts, D_MODEL), BF16),
        scratch_shapes=[pltpu.VMEM((SSM_WIDTH, tn), BF16), pltpu.VMEM((SSM_WIDTH, tn), BF16),
                        pltpu.VMEM((GROUP_WIDTH, tn), BF16)],
        compiler_params=_cparams(("arbitrary", "arbitrary")),
        name="glu_mix",
    )(y_p, y_s, o_p, o_s, w_glu, w_glu, w_attn_br, proj, proj)


def _route(logits):
    lane = lax.broadcasted_iota(jnp.int32, logits.shape, 1).astype(F32)
    big = 1000.0
    first = lambda cond: jnp.min(jnp.where(cond, lane, big), axis=-1, keepdims=True)
    is_g = lane < N_EXPERT_GROUPS
    lg = jnp.where(is_g, logits, NEG)
    mg = jnp.max(lg, axis=-1, keepdims=True)
    g_sel = first(lg == mg)
    p_group = 1.0 / jnp.sum(jnp.where(is_g, jnp.exp(lg - mg), 0.0), axis=-1, keepdims=True)
    e_lo = N_EXPERT_GROUPS + EXPERTS_PER_GROUP * g_sel
    le = jnp.where((lane >= e_lo) & (lane < e_lo + EXPERTS_PER_GROUP), logits, NEG)
    v1 = jnp.max(le, axis=-1, keepdims=True)
    i1 = first(le == v1)
    le2 = jnp.where(lane == i1, NEG, le)
    v2 = jnp.max(le2, axis=-1, keepdims=True)
    i2 = first(le2 == v2)
    e2 = jnp.exp(v2 - v1)
    w1 = p_group / (1.0 + e2)
    w2 = p_group * e2 / (1.0 + e2)
    pick = lambda k, val: jnp.where(lane == k, val, 0.0)
    return (pick(ROUTE_E1, i1 - N_EXPERT_GROUPS) + pick(ROUTE_E2, i2 - N_EXPERT_GROUPS)
            + pick(ROUTE_W1, w1) + pick(ROUTE_W2, w2))


def _outproj_kernel(mix_ref, w_ref, xp_ref, xs_ref, g_ref, wr_ref, br_ref,
                    gtp_ref, scp_ref, shp_ref, gts_ref, scs_ref, shs_ref,
                    x1_ref, h2_ref, route_ref, *, n_prompt_tiles):
    is_s = pl.program_id(0) >= n_prompt_tiles

    def split(v):
        high = v.astype(BF16)
        return high, (v - high.astype(F32)).astype(BF16)

    x = jnp.where(is_s, xs_ref[...], xp_ref[...])
    gt = jnp.where(is_s, gts_ref[...], gtp_ref[0:1, :])
    sc = jnp.where(is_s, scs_ref[...], scp_ref[0:1, :])
    sh = jnp.where(is_s, shs_ref[...], shp_ref[0:1, :])
    x1 = x + gt * jnp.dot(mix_ref[...], w_ref[...], preferred_element_type=F32)
    x1_ref[...] = x1
    ms = jnp.mean(x1 * x1, axis=-1, keepdims=True)
    h2 = (x1 * lax.rsqrt(ms + EPS) * g_ref[...]) * (1.0 + sc) + sh
    h2_ref[...] = h2.astype(h2_ref.dtype)
    r = jnp.dot(jnp.concatenate(split(h2), axis=0), jnp.concatenate(split(wr_ref[...]), axis=1),
                preferred_element_type=F32)
    n = h2.shape[0]
    logits = (r[:n, :LANES] + r[:n, LANES:]) + (r[n:, :LANES] + r[n:, LANES:]) + br_ref[...]
    route_ref[...] = _route(logits)


def _outproj(mixed, w_out_bf, x_p, x_s, g2, w_router, b_router, mod):
    tp, ts = x_p.shape[0], x_s.shape[0]
    tm = SMALL_TOK_TILE
    npt, nst = tp // tm, ts // tm
    prow = lambda i: (jnp.minimum(i, npt - 1), 0)
    srow = lambda i: (jnp.maximum(i - npt, 0), 0)
    const = lambda i: (0, 0)
    vec = pl.BlockSpec((1, D_MODEL), const)
    gt_p, gt_s = _mod_specs(MOD_GT1, tm, ts, npt)
    sc_p, sc_s = _mod_specs(MOD_SC2, tm, ts, npt)
    sh_p, sh_s = _mod_specs(MOD_SH2, tm, ts, npt)
    full = lambda i: (i, 0)
    return pl.pallas_call(
        functools.partial(_outproj_kernel, n_prompt_tiles=npt),
        grid=(npt + nst,),
        in_specs=[pl.BlockSpec((tm, D_MODEL), full), pl.BlockSpec((D_MODEL, D_MODEL), const),
                  pl.BlockSpec((tm, D_MODEL), prow), pl.BlockSpec((tm, D_MODEL), srow),
                  vec, pl.BlockSpec((D_MODEL, LANES), const), pl.BlockSpec((1, LANES), const),
                  gt_p, sc_p, sh_p, gt_s, sc_s, sh_s],
        out_specs=[pl.BlockSpec((tm, D_MODEL), full), pl.BlockSpec((tm, D_MODEL), full),
                   pl.BlockSpec((tm, LANES), full)],
        out_shape=[jax.ShapeDtypeStruct((tp + ts, D_MODEL), F32),
                   jax.ShapeDtypeStruct((tp + ts, D_MODEL), F32),
                   jax.ShapeDtypeStruct((tp + ts, LANES), F32)],
        compiler_params=_cparams(("arbitrary",)),
        name="out_proj_norm2_router",
    )(mixed, w_out_bf, x_p, x_s, g2, w_router, b_router, mod, mod, mod, mod, mod, mod)


def _dispatch_plan(route, tm):
    e = route[:, ROUTE_E1:ROUTE_E2 + 1].astype(jnp.int32).reshape(-1)
    n_pairs = e.shape[0]
    onehot = (e[:, None] == jnp.arange(N_EXPERTS, dtype=jnp.int32)[None, :]).astype(jnp.int32)
    csum = jnp.cumsum(onehot, axis=0)
    rank = jnp.sum(onehot * csum, axis=1) - 1
    tiles_per_expert = (csum[-1] + tm - 1) // tm
    tile_end = jnp.cumsum(tiles_per_expert)
    tile_start = tile_end - tiles_per_expert
    dest = (tile_start[e] * tm + rank).astype(jnp.int32)
    max_tiles = n_pairs // tm + N_EXPERTS
    k = jnp.arange(max_tiles, dtype=jnp.int32)
    tile_expert = jnp.minimum(jnp.sum((k[:, None] >= tile_end[None, :]).astype(jnp.int32), axis=1), N_EXPERTS - 1)
    n_used = tile_end[-1].astype(jnp.int32)
    last_expert = jnp.take(tile_expert, n_used - 1)
    tile_expert = jnp.where(k < n_used, tile_expert, last_expert).astype(jnp.int32)
    pad_end = (jnp.concatenate([tile_end, tile_end[-1:]]) * tm).astype(jnp.int32)
    pad_len = (tiles_per_expert * tm - csum[-1]).astype(jnp.int32)
    return dest, pad_end, pad_len, tile_expert, n_used.reshape(1), max_tiles


def _start_pair_copies(dest_ref, tile, rows, make):
    def body(r, _):
        for k in range(2):
            make(r, k, dest_ref[(tile * rows + r) * 2 + k]).start()
        return 0
    lax.fori_loop(0, rows, body, 0, unroll=8)


def _pair_copies(dest_ref, tile, rows, make, make_all):
    _start_pair_copies(dest_ref, tile, rows, make)
    for k in range(2):
        make_all(k).wait()


def _dispatch_kernel(dest_ref, pad_end_ref, pad_len_ref, h_ref, xs_ref, zero_buf, sem, pad_sem):
    rows = h_ref.shape[0]

    @pl.when(pl.program_id(0) == 0)
    def _():
        zero_buf[...] = jnp.zeros_like(zero_buf)

        def pad_copies(fn):
            for e in range(N_EXPERTS):
                end, left = pad_end_ref[e], pad_len_ref[e]
                size = zero_buf.shape[0]
                while size >= SUBLANES:
                    take = (left & size) != 0
                    end = end - jnp.where(take, size, 0)

                    @pl.when(take)
                    def _(start=end, size=size):
                        fn(pltpu.make_async_copy(zero_buf.at[pl.ds(0, size)],
                                                 xs_ref.at[pl.ds(pl.multiple_of(start, size), size)], pad_sem))

                    size //= 2
                for r in range(1, SUBLANES):
                    @pl.when((left & (SUBLANES - 1)) >= r)
                    def _(row=end - r):
                        fn(pltpu.make_async_copy(zero_buf.at[pl.ds(0, 1)], xs_ref.at[pl.ds(row, 1)], pad_sem))
            size = zero_buf.shape[0]
            tail = pad_end_ref[N_EXPERTS]
            for t in range(N_EXPERTS * MOE_TM // size):
                @pl.when(tail + t * size < xs_ref.shape[0])
                def _(start=tail + t * size):
                    fn(pltpu.make_async_copy(zero_buf, xs_ref.at[pl.ds(pl.multiple_of(start, size), size)], pad_sem))

        pad_copies(lambda c: c.start())
        pad_copies(lambda c: c.wait())

    make = lambda r, k, d: pltpu.make_async_copy(h_ref.at[pl.ds(r, 1)], xs_ref.at[pl.ds(d, 1)], sem)
    make_all = lambda k: pltpu.make_async_copy(h_ref, xs_ref.at[pl.ds(0, rows)], sem)
    _pair_copies(dest_ref, pl.program_id(0), rows, make, make_all)


def _dispatch(dest, pad_end, pad_len, h2, n_slots):
    n_tok = h2.shape[0]
    tm = SMALL_TOK_TILE
    return pl.pallas_call(
        _dispatch_kernel,
        grid_spec=pltpu.PrefetchScalarGridSpec(
            num_scalar_prefetch=3, grid=(n_tok // tm,),
            in_specs=[pl.BlockSpec((tm, D_MODEL), lambda i, d, ps, pn: (i, 0))],
            out_specs=pl.BlockSpec(memory_space=pl.ANY),
            scratch_shapes=[pltpu.VMEM((MOE_TM // 2, D_MODEL), F32),
                            pltpu.SemaphoreType.DMA(()), pltpu.SemaphoreType.DMA(())]),
        out_shape=jax.ShapeDtypeStruct((n_slots, D_MODEL), F32),
        compiler_params=_cparams(("arbitrary",)),
        name="moe_dispatch",
    )(dest, pad_end, pad_len, h2)


def _experts_kernel(te_ref, used_ref, xs_ref, wgu_ref, wd_ref, y_ref, wgu_bf, wd_bf):
    k = pl.program_id(0)
    new_expert = (k == 0) | (te_ref[k] != te_ref[jnp.maximum(k - 1, 0)])

    @pl.when(new_expert)
    def _():
        wgu_bf[...] = wgu_ref[0].astype(BF16)
        wd_bf[...] = wd_ref[0].astype(BF16)

    @pl.when(k < used_ref[0])
    def _():
        gu = jnp.dot(xs_ref[...].astype(BF16), wgu_bf[...], preferred_element_type=F32)
        gate, up = gu[:, :EXPERT_FF], gu[:, EXPERT_FF:]
        act = (gate * _sigmoid(gate)) * up
        y_ref[...] = jnp.dot(act.astype(BF16), wd_bf[...], preferred_element_type=F32)

    @pl.when(k >= used_ref[0])
    def _():
        y_ref[...] = jnp.zeros_like(y_ref)


def _experts(tile_expert, n_used, xs, w_gu, w_down, max_tiles):
    tm = MOE_TM
    row = lambda k, te, nu: (jnp.minimum(k, nu[0] - 1), 0)
    return pl.pallas_call(
        _experts_kernel,
        grid_spec=pltpu.PrefetchScalarGridSpec(
            num_scalar_prefetch=2, grid=(max_tiles,),
            in_specs=[pl.BlockSpec((tm, D_MODEL), row),
                      pl.BlockSpec((1, D_MODEL, 2 * EXPERT_FF), lambda k, te, nu: (te[k], 0, 0)),
                      pl.BlockSpec((1, EXPERT_FF, D_MODEL), lambda k, te, nu: (te[k], 0, 0))],
            out_specs=pl.BlockSpec((tm, D_MODEL), lambda k, te, nu: (k, 0)),
            scratch_shapes=[pltpu.VMEM((D_MODEL, 2 * EXPERT_FF), BF16), pltpu.VMEM((EXPERT_FF, D_MODEL), BF16)]),
        out_shape=jax.ShapeDtypeStruct((max_tiles * tm, D_MODEL), F32),
        compiler_params=_cparams(("arbitrary",)),
        name="moe_experts",
    )(tile_expert, n_used, xs, w_gu, w_down)


def _combine_kernel(dest_ref, y_hbm, x1_ref, route_ref, gtp_ref, gts_ref, yp_ref, ys_ref, buf, sem, *, n_prompt_tiles):
    i = pl.program_id(0)
    rows = x1_ref.shape[0]

    def gather(tile, slot):
        make = lambda r, k, d: pltpu.make_async_copy(y_hbm.at[pl.ds(d, 1)], buf.at[slot, k, pl.ds(r, 1)],
                                                     sem.at[slot])
        _start_pair_copies(dest_ref, tile, rows, make)

    @pl.when(i == 0)
    def _():
        gather(0, 0)

    slot = lax.rem(i, 2)

    @pl.when(i + 1 < pl.num_programs(0))
    def _():
        gather(i + 1, 1 - slot)

    for k in range(2):
        pltpu.make_async_copy(y_hbm.at[pl.ds(0, rows)], buf.at[slot, k], sem.at[slot]).wait()
    route = route_ref[...]
    lane = lax.broadcasted_iota(jnp.int32, route.shape, 1)
    w1 = jnp.sum(jnp.where(lane == ROUTE_W1, route, 0.0), axis=-1, keepdims=True)
    w2 = jnp.sum(jnp.where(lane == ROUTE_W2, route, 0.0), axis=-1, keepdims=True)
    moe = w1 * buf[slot, 0] + w2 * buf[slot, 1]

    @pl.when(i < n_prompt_tiles)
    def _():
        yp_ref[...] = x1_ref[...] + gtp_ref[0:1, :] * moe

    @pl.when(i >= n_prompt_tiles)
    def _():
        ys_ref[...] = x1_ref[...] + gts_ref[...] * moe


def _combine(dest, y_slots, x1, route, mod, tp, ts):
    tm = SMALL_TOK_TILE
    npt, nst = tp // tm, ts // tm
    row = lambda i, d: (i, 0)
    gt_p, gt_s = _mod_specs(MOD_GT2, tm, ts, npt)
    return pl.pallas_call(
        functools.partial(_combine_kernel, n_prompt_tiles=npt),
        grid_spec=pltpu.PrefetchScalarGridSpec(
            num_scalar_prefetch=1, grid=(npt + nst,),
            in_specs=[pl.BlockSpec(memory_space=pl.ANY),
                      pl.BlockSpec((tm, D_MODEL), row), pl.BlockSpec((tm, LANES), row), gt_p, gt_s],
            out_specs=[pl.BlockSpec((tm, D_MODEL), lambda i, d: (jnp.minimum(i, npt - 1), 0)),
                       pl.BlockSpec((tm, D_MODEL), lambda i, d: (jnp.maximum(i - npt, 0), 0))],
            scratch_shapes=[pltpu.VMEM((2, 2, tm, D_MODEL), F32), pltpu.SemaphoreType.DMA((2,))]),
        out_shape=[jax.ShapeDtypeStruct((tp, D_MODEL), F32), jax.ShapeDtypeStruct((ts, D_MODEL), F32)],
        compiler_params=_cparams(("arbitrary",)),
        name="moe_combine",
    )(dest, y_slots, x1, route, mod, mod)


def _moe(h2, route, x1, w_gu, w_down, mod, tp, ts):
    dest, pad_end, pad_len, tile_expert, n_used, max_tiles = _dispatch_plan(route, MOE_TM)
    xs = _dispatch(dest, pad_end, pad_len, h2, max_tiles * MOE_TM)
    y_slots = _experts(tile_expert, n_used, xs, w_gu, w_down, max_tiles)
    return _combine(dest, y_slots, x1, route, mod, tp, ts)


def _kv_pack_kernel(k_ref, v_ref, o_ref):
    n = k_ref.shape[0]
    for h in range(HEADS_PER_GROUP):
        cols = slice(h * HEAD_DIM, (h + 1) * HEAD_DIM)
        o_ref[pl.ds(h, n, stride=KV_PLANES), :] = k_ref[:, cols]
        o_ref[pl.ds(HEADS_PER_GROUP + h, n, stride=KV_PLANES), :] = v_ref[:, cols]


def _kv_pack(proj, row0, rows, g):
    tr = min(rows, SMALL_TOK_TILE)
    assert rows % tr == 0 and row0 % tr == 0
    col = lambda off: (off + g * GROUP_WIDTH) // GROUP_WIDTH
    spec = lambda off: pl.BlockSpec((tr, GROUP_WIDTH), lambda i: (row0 // tr + i, col(off)))
    flat = pl.pallas_call(
        _kv_pack_kernel,
        grid=(rows // tr,),
        in_specs=[spec(OFF_K), spec(OFF_V)],
        out_specs=pl.BlockSpec((tr * KV_PLANES, HEAD_DIM), lambda i: (i, 0)),
        out_shape=jax.ShapeDtypeStruct((rows * KV_PLANES, HEAD_DIM), F32),
        compiler_params=_cparams(("arbitrary",)),
        name="kv_pack",
    )(proj, proj)
    return flat.reshape(rows, 2, HEADS_PER_GROUP, HEAD_DIM)


def _rope_tables(n_prompt, n_seq, n_step):
    half = ROT_DIM // 2
    inv_freq = ROPE_THETA ** (-jnp.arange(half, dtype=F32) / half)
    pos = jnp.concatenate([jnp.arange(n_prompt, dtype=jnp.int32),
                           jnp.tile(PAST_LEN + jnp.arange(n_step, dtype=jnp.int32), n_seq)])
    ang = pos.astype(F32)[:, None] * inv_freq[None, :]
    cos, sin = jnp.cos(ang), jnp.sin(ang)
    n = pos.shape[0]
    one = jnp.ones((n, HEAD_DIM - ROT_DIM), F32)
    zero = jnp.zeros((n, HEAD_DIM - ROT_DIM), F32)
    zh = jnp.zeros((n, half), F32)
    rc = jnp.concatenate([cos, cos, one], axis=1)
    rs1 = jnp.concatenate([-sin, zh, zero], axis=1)
    rs2 = jnp.concatenate([zh, sin, zero], axis=1)
    return rc, rs1, rs2


def kernel(x_prompt, x_sample, cache_kv_w128, cache_kv_w512, cache_kv_w2048, state_ssm_re, state_ssm_im,
           c_prompt, c_sample, w_ada, b_ada, norm1_g, norm2_g, w_in, ssm_a_re, ssm_a_im, ssm_log_dt,
           ssm_b_re, ssm_b_im, ssm_c_re, ssm_c_im, ssm_d, w_glu, q_norm_g, k_norm_g, w_attn_br, w_out,
           w_router_group, b_router_group, w_router_expert, b_router_expert, w_expert_gate_up, w_expert_down):
    assert x_prompt.shape[0] == 1 and w_ada.shape[0] == 1
    n_prompt = x_prompt.shape[1]
    n_seq, n_step = x_sample.shape[0], x_sample.shape[1]
    n_samp = n_seq * n_step
    assert n_samp % TOK_TILE == 0 and n_prompt % ATT_SB == 0 and (n_prompt + n_samp) % PROJ_TM == 0
    x_p = x_prompt.reshape(n_prompt, D_MODEL)
    x_s = x_sample.reshape(n_samp, D_MODEL)

    c_all = jnp.concatenate([jnp.repeat(c_sample, n_step, axis=0),
                             jnp.broadcast_to(c_prompt, (SUBLANES, D_MODEL))], axis=0)
    mod = _ada(c_all, w_ada[0], b_ada[0])

    h1 = _modnorm(x_p, x_s, norm1_g[0].reshape(1, D_MODEL), mod)
    rc, rs1, rs2 = _rope_tables(n_prompt, n_seq, n_step)
    proj = _inproj(h1, w_in[0], rc, rs1, rs2, q_norm_g[0].reshape(1, HEAD_DIM), k_norm_g[0].reshape(1, HEAD_DIM))

    pw_re, pw_im, bb_re, bb_im = _ssm_prep(ssm_a_re[0], ssm_a_im[0], ssm_log_dt[0], ssm_b_re[0], ssm_b_im[0])
    b_mat, c_mat = _ssm_block_matrices(bb_re, bb_im, ssm_c_re[0], ssm_c_im[0])
    d_skip = ssm_d[0].reshape(1, SSM_WIDTH)
    yg_p, fre_p, fim_p = _s5_prompt(proj, n_prompt, d_skip, pw_re, pw_im, b_mat, c_mat)
    yg_s, fre_s, fim_s = _s5_sample(proj, n_prompt, n_seq, n_step, d_skip, pw_re, pw_im, b_mat, c_mat,
                                    state_ssm_re[0].reshape(n_seq, SSM_FLAT), state_ssm_im[0].reshape(n_seq, SSM_FLAT))

    o_p = _attn_prompt(proj, n_prompt)
    o_s = _attn_sample(proj, n_prompt, n_seq, n_step, (cache_kv_w128[0], cache_kv_w512[0], cache_kv_w2048[0]))

    mixed = _mix(yg_p, yg_s, o_p, o_s, w_glu[0], w_attn_br[0], proj)

    w_router = jnp.concatenate([w_router_group[0], w_router_expert[0],
                                jnp.zeros((D_MODEL, LANES - N_EXPERT_GROUPS - N_EXPERTS), F32)], axis=1)
    b_router = jnp.concatenate([b_router_group[0], b_router_expert[0],
                                jnp.zeros((LANES - N_EXPERT_GROUPS - N_EXPERTS,), F32)]).reshape(1, LANES)
    x1, h2, route = _outproj(mixed, w_out[0].astype(BF16), x_p, x_s, norm2_g[0].reshape(1, D_MODEL),
                             w_router, b_router, mod)
    y_p, y_s = _moe(h2, route, x1, w_expert_gate_up[0], w_expert_down[0], mod, n_prompt, n_samp)

    kv_p, kv_s = [], []
    for g, (window, _) in enumerate(DILATION_PATTERNS):
        keep = min(window, n_prompt)
        kv_p.append(_kv_pack(proj, n_prompt - keep, keep, g).reshape(1, 1, keep, 2, HEADS_PER_GROUP, HEAD_DIM))
        kv_s.append(_kv_pack(proj, n_prompt, n_samp, g).reshape(1, n_seq, n_step, 2, HEADS_PER_GROUP, HEAD_DIM))

    state_shape_p = (1, 1, SSM_GROUPS, SSM_STATE)
    state_shape_s = (1, n_seq, SSM_GROUPS, SSM_STATE)
    return (y_p.reshape(1, n_prompt, D_MODEL), y_s.reshape(n_seq, n_step, D_MODEL),
            kv_p[0], kv_p[1], kv_p[2], fre_p.reshape(state_shape_p), fim_p.reshape(state_shape_p),
            kv_s[0], kv_s[1], kv_s[2], fre_s.reshape(state_shape_s), fim_s.reshape(state_shape_s))
```

```python
import functools
import math

import numpy as np
import jax
import jax.numpy as jnp
from jax import lax
from jax.experimental import pallas as pl
from jax.experimental.pallas import tpu as pltpu

F32 = jnp.float32
BF16 = jnp.bfloat16

D_MODEL = 2048
PAST_LEN = 2048
SSM_WIDTH = D_MODEL // 2
SSM_GROUP = 16
SSM_GROUPS = SSM_WIDTH // SSM_GROUP
SSM_STATE = 64
SSM_FLAT = SSM_GROUPS * SSM_STATE
HEAD_DIM = 128
DILATION_PATTERNS = ((128, 1), (512, 4), (2048, 16))
N_PATTERNS = 3
HEADS_PER_GROUP = 4
GROUP_WIDTH = HEADS_PER_GROUP * HEAD_DIM
ATTN_WIDTH = N_PATTERNS * GROUP_WIDTH
ROT_DIM = HEAD_DIM // 4
ROPE_THETA = 500000.0
OFF_Q = SSM_WIDTH
OFF_K = OFF_Q + ATTN_WIDTH
OFF_V = OFF_K + ATTN_WIDTH
OFF_G = OFF_V + ATTN_WIDTH
IN_COLS = OFF_G + 2 * D_MODEL
N_EXPERT_GROUPS = 4
EXPERTS_PER_GROUP = 4
N_EXPERTS = 16
EXPERT_FF = D_MODEL // 4
EPS = 1e-6
NEG = -1e30

LANES = 128
SUBLANES = 8
VMEM_LIMIT = 56 * 1024 * 1024

TOK_TILE = 512
SMALL_TOK_TILE = 256
PROJ_TM = 544
PROJ_TN = 2432
SCAN_L = 32
SCAN_NC = 16
SCAN_TT = SCAN_L * SCAN_NC
SSM_BLK = 8
SSM_BLK_STATE = SSM_BLK * SSM_STATE
SSM_PAR = 4
MOE_TM = 512
ROUTE_E1, ROUTE_E2, ROUTE_W1, ROUTE_W2 = 0, 1, 2, 3
MOD_SH1, MOD_SC1, MOD_GT1, MOD_SH2, MOD_SC2, MOD_GT2 = range(6)
ATT_SB = 2048
BAND = 128
ATT_UNROLL = 16


def _cparams(sem, vmem=VMEM_LIMIT):
    return pltpu.CompilerParams(dimension_semantics=sem, vmem_limit_bytes=vmem)


def _sigmoid(x):
    return 0.5 * jnp.tanh(0.5 * x) + 0.5


def _row_halves(rows):
    half = rows // 2
    return (slice(0, half), slice(half, rows))


def _gelu_tanh(x):
    c = math.sqrt(2.0 / math.pi)
    return 0.5 * x * (1.0 + jnp.tanh(c * (x + 0.044715 * (x * x * x))))


def _ada_kernel(c_ref, w_ref, b_ref, o_ref, cs_ref):
    @pl.when(pl.program_id(0) == 0)
    def _():
        c = c_ref[...]
        cs_ref[...] = (c * _sigmoid(c)).astype(BF16)

    o_ref[...] = jnp.dot(cs_ref[...], w_ref[...].astype(BF16), preferred_element_type=F32) + b_ref[...]


def _ada(c_all, w_ada, b_ada):
    rows = c_all.shape[0]
    n_out = w_ada.shape[1]
    tn = 1024
    return pl.pallas_call(
        _ada_kernel,
        grid=(n_out // tn,),
        in_specs=[pl.BlockSpec((rows, D_MODEL), lambda n: (0, 0)),
                  pl.BlockSpec((D_MODEL, tn), lambda n: (0, n)),
                  pl.BlockSpec((1, tn), lambda n: (0, n))],
        out_specs=pl.BlockSpec((rows, tn), lambda n: (0, n)),
        out_shape=jax.ShapeDtypeStruct((rows, n_out), F32),
        scratch_shapes=[pltpu.VMEM((rows, D_MODEL), BF16)],
        compiler_params=_cparams(("arbitrary",)),
        name="ada_mod",
    )(c_all, w_ada, b_ada.reshape(1, n_out))


def _modnorm_kernel(xp_ref, xs_ref, g_ref, scp_ref, shp_ref, scs_ref, shs_ref, o_ref, *, n_prompt_tiles):
    is_s = pl.program_id(0) >= n_prompt_tiles

    def norm(x_ref, sc_ref, sh_ref, rows):
        x = x_ref[...]
        ms = jnp.mean(x * x, axis=-1, keepdims=True)
        y = x * lax.rsqrt(ms + EPS) * g_ref[...]
        o_ref[...] = (y * (1.0 + sc_ref[rows, :]) + sh_ref[rows, :]).astype(o_ref.dtype)

    pl.when(jnp.logical_not(is_s))(functools.partial(norm, xp_ref, scp_ref, shp_ref, slice(0, 1)))
    pl.when(is_s)(functools.partial(norm, xs_ref, scs_ref, shs_ref, slice(None)))


def _mod_specs(k, tm, n_samp, n_prompt_tiles):
    prompt = pl.BlockSpec((SUBLANES, D_MODEL), lambda i, *_: (n_samp // SUBLANES, k))
    sample = pl.BlockSpec((tm, D_MODEL), lambda i, *_: (jnp.maximum(i - n_prompt_tiles, 0), k))
    return prompt, sample


def _modnorm(x_p, x_s, g, mod):
    tp, ts = x_p.shape[0], x_s.shape[0]
    tm = TOK_TILE
    npt, nst = tp // tm, ts // tm
    row = lambda i: (jnp.minimum(i, npt - 1), 0)
    srow = lambda i: (jnp.maximum(i - npt, 0), 0)
    const = lambda i: (0, 0)
    sc_p, sc_s = _mod_specs(MOD_SC1, tm, ts, npt)
    sh_p, sh_s = _mod_specs(MOD_SH1, tm, ts, npt)
    return pl.pallas_call(
        functools.partial(_modnorm_kernel, n_prompt_tiles=npt),
        grid=(npt + nst,),
        in_specs=[pl.BlockSpec((tm, D_MODEL), row), pl.BlockSpec((tm, D_MODEL), srow),
                  pl.BlockSpec((1, D_MODEL), const), sc_p, sh_p, sc_s, sh_s],
        out_specs=pl.BlockSpec((tm, D_MODEL), lambda i: (i, 0)),
        out_shape=jax.ShapeDtypeStruct((tp + ts, D_MODEL), BF16),
        compiler_params=_cparams(("arbitrary",)),
        name="modnorm1",
    )(x_p, x_s, g, mod, mod, mod, mod)


def _inproj_kernel(h_ref, w_ref, rc_ref, rs1_ref, rs2_ref, qg_ref, kg_ref, o_ref, wbf_ref, *, heads_per_tile):
    n = pl.program_id(0)

    @pl.when(pl.program_id(1) == 0)
    def _():
        wbf_ref[...] = w_ref[...].astype(BF16)

    half = ROT_DIM // 2
    q_heads = range(OFF_Q // HEAD_DIM, OFF_K // HEAD_DIM)
    k_heads = range(OFF_K // HEAD_DIM, OFF_V // HEAD_DIM)
    pair = 2 * HEAD_DIM

    def tile(col_tile):
        h = h_ref[...]
        for c0 in range(0, heads_per_tile * HEAD_DIM, pair):
            width = min(pair, heads_per_tile * HEAD_DIM - c0)
            acc = jnp.dot(h, wbf_ref[:, c0:c0 + width], preferred_element_type=F32)
            for c in range(c0, c0 + width, HEAD_DIM):
                slot = col_tile * heads_per_tile + c // HEAD_DIM
                x = acc[:, c - c0:c - c0 + HEAD_DIM]
                if slot in q_heads or slot in k_heads:
                    gain = qg_ref[...] if slot in q_heads else kg_ref[...]
                    ms = jnp.mean(x * x, axis=-1, keepdims=True)
                    y = x * lax.rsqrt(ms + EPS) * gain
                    up = pltpu.roll(y, HEAD_DIM - half, 1)
                    dn = pltpu.roll(y, half, 1)
                    x = y * rc_ref[...] + up * rs1_ref[...] + dn * rs2_ref[...]
                o_ref[:, c:c + HEAD_DIM] = x

    for col_tile in range(IN_COLS // (heads_per_tile * HEAD_DIM)):
        pl.when(n == col_tile)(functools.partial(tile, col_tile))


def _inproj(h, w_in, rc, rs1, rs2, qg, kg):
    n_tok = h.shape[0]
    tm, tn = PROJ_TM, PROJ_TN
    tab = pl.BlockSpec((tm, HEAD_DIM), lambda n, m: (m, 0))
    gain = pl.BlockSpec((1, HEAD_DIM), lambda n, m: (0, 0))
    return pl.pallas_call(
        functools.partial(_inproj_kernel, heads_per_tile=tn // HEAD_DIM),
        grid=(IN_COLS // tn, n_tok // tm),
        in_specs=[pl.BlockSpec((tm, D_MODEL), lambda n, m: (m, 0)),
                  pl.BlockSpec((D_MODEL, tn), lambda n, m: (0, n), pipeline_mode=pl.Buffered(1)),
                  tab, tab, tab, gain, gain],
        out_specs=pl.BlockSpec((tm, tn), lambda n, m: (m, n)),
        out_shape=jax.ShapeDtypeStruct((n_tok, IN_COLS), F32),
        scratch_shapes=[pltpu.VMEM((D_MODEL, tn), BF16)],
        compiler_params=_cparams(("arbitrary", "arbitrary")),
        name="in_proj",
    )(h, w_in, rc, rs1, rs2, qg, kg)


def _ssm_prep_kernel(are_ref, aim_ref, ldt_ref, arer_ref, aimr_ref, ldtr_ref, bre_ref, bim_ref,
                     pre_ref, pim_ref, bbre_ref, bbim_ref):
    def discretise(a_re, a_im, log_dt):
        dt = jnp.exp(log_dt)
        mag = jnp.exp(a_re * dt)
        return mag * jnp.cos(a_im * dt), mag * jnp.sin(a_im * dt)

    ab_re, ab_im = discretise(are_ref[...], aim_ref[...], ldt_ref[...])
    p_re, p_im = ab_re, ab_im
    for i in range(SCAN_L):
        pre_ref[i:i + 1, :] = p_re
        pim_ref[i:i + 1, :] = p_im
        p_re, p_im = p_re * ab_re - p_im * ab_im, p_re * ab_im + p_im * ab_re

    a_re, a_im = arer_ref[...], aimr_ref[...]
    r_re, r_im = discretise(a_re, a_im, ldtr_ref[...])
    nr, ni = r_re - 1.0, r_im
    den = a_re * a_re + a_im * a_im
    z_re = (nr * a_re + ni * a_im) / den
    z_im = (ni * a_re - nr * a_im) / den
    b_re, b_im = bre_ref[...], bim_ref[...]
    bbre_ref[...] = z_re * b_re - z_im * b_im
    bbim_ref[...] = z_re * b_im + z_im * b_re


def _ssm_prep(a_re, a_im, log_dt, b_re, b_im):
    g, p, n = b_re.shape
    flat = lambda x: x.reshape(1, g * p)
    rep = lambda x: jnp.repeat(x, n, axis=1)
    ldt_gp = jnp.broadcast_to(log_dt[:, None], (g, p))
    ldt_rep = jnp.broadcast_to(log_dt[:, None], (g, p * n))
    out_shape = [jax.ShapeDtypeStruct((SCAN_L, g * p), F32)] * 2 + [jax.ShapeDtypeStruct((g, p * n), F32)] * 2
    return pl.pallas_call(_ssm_prep_kernel, out_shape=out_shape, name="ssm_prep")(
        flat(a_re), flat(a_im), flat(ldt_gp), rep(a_re), rep(a_im), ldt_rep,
        b_re.reshape(g, p * n), b_im.reshape(g, p * n))


def _ssm_block_matrices(bb_re, bb_im, c_re, c_im):
    g, p, n = SSM_GROUPS, SSM_STATE, SSM_GROUP
    nb = g // SSM_BLK
    eye = jnp.eye(SSM_BLK, dtype=F32)

    def in_mat(bb):
        x = bb.reshape(nb, SSM_BLK, p, n)
        return jnp.einsum('bgpm,gh->bgmhp', x, eye).reshape(nb, SSM_BLK * n, SSM_BLK * p)

    def out_mat(c):
        x = c.reshape(nb, SSM_BLK, n, p)
        return jnp.einsum('bgnp,gh->bgphn', x, eye).reshape(nb, SSM_BLK * p, SSM_BLK * n)

    b_mat = jnp.concatenate([in_mat(bb_re), in_mat(bb_im)], axis=2).astype(BF16)
    c_mat = jnp.concatenate([out_mat(c_re), -out_mat(c_im)], axis=1).astype(BF16)
    return b_mat, c_mat


def _cmul_add(a_re, a_im, s_re, s_im, b_re, b_im):
    return a_re * s_re - a_im * s_im + b_re, a_re * s_im + a_im * s_re + b_im


def _s5_prompt_kernel(*refs):
    par = SSM_PAR
    u_refs = refs[:par]
    (d_ref, pre_ref, pim_ref, bm_ref, cm_ref, y_ref, fre_ref, fim_ref,
     up_scr, bu_scr, lhs_scr, in_re_scr, in_im_scr, car_re, car_im, yn_scr) = refs[par:]
    nc, ln, w = SCAN_NC, SCAN_L, SSM_BLK_STATE

    @pl.when(pl.program_id(1) == 0)
    def _():
        car_re[...] = jnp.zeros_like(car_re)
        car_im[...] = jnp.zeros_like(car_im)

    for b in range(par):
        lanes = slice(b * w, (b + 1) * w)
        rows = lambda i: slice(i * nc, (i + 1) * nc)
        for i in range(ln):
            up_scr[b, rows(i), :] = u_refs[b][pl.ds(i, nc, stride=ln), :]
        up = up_scr[b]
        bu_scr[b] = jnp.dot(up.astype(BF16), bm_ref[b], preferred_element_type=F32)

        a_re = jnp.broadcast_to(pre_ref[0:1, lanes], (nc, w))
        a_im = jnp.broadcast_to(pim_ref[0:1, lanes], (nc, w))
        s_re = s_im = jnp.zeros((nc, w), F32)
        for i in range(ln):
            s_re, s_im = _cmul_add(a_re, a_im, s_re, s_im, bu_scr[b, rows(i), 0:w], bu_scr[b, rows(i), w:2 * w])
            bu_scr[b, rows(i), 0:w] = s_re
            bu_scr[b, rows(i), w:2 * w] = s_im

        al_re, al_im = pre_ref[ln - 1:ln, lanes], pim_ref[ln - 1:ln, lanes]
        c_re, c_im = car_re[b], car_im[b]
        for c in range(nc):
            in_re_scr[b, c:c + 1, :] = c_re
            in_im_scr[b, c:c + 1, :] = c_im
            c_re, c_im = _cmul_add(al_re, al_im, c_re, c_im, s_re[c:c + 1, :], s_im[c:c + 1, :])
        car_re[b] = c_re
        car_im[b] = c_im
        fre_ref[:, lanes] = c_re
        fim_ref[:, lanes] = c_im
        in_re, in_im = in_re_scr[b], in_im_scr[b]

        for i in range(ln):
            p_re = jnp.broadcast_to(pre_ref[i:i + 1, lanes], (nc, w))
            p_im = jnp.broadcast_to(pim_ref[i:i + 1, lanes], (nc, w))
            f_re, f_im = _cmul_add(p_re, p_im, in_re, in_im, bu_scr[b, rows(i), 0:w], bu_scr[b, rows(i), w:2 * w])
            lhs_scr[b, rows(i), 0:w] = f_re.astype(BF16)
            lhs_scr[b, rows(i), w:2 * w] = f_im.astype(BF16)

        y = (jnp.dot(lhs_scr[b], cm_ref[b], preferred_element_type=F32)
             + d_ref[:, b * LANES:(b + 1) * LANES] * up)
        for i in range(ln):
            yn_scr[b, pl.ds(i, nc, stride=ln), :] = y[rows(i), :]
        y_ref[:, b * LANES:(b + 1) * LANES] = _gelu_tanh(yn_scr[b]).astype(y_ref.dtype)


def _s5_prompt(proj, n_prompt, d_skip, pw_re, pw_im, b_mat, c_mat):
    par = SSM_PAR
    nb = SSM_GROUPS // SSM_BLK // par
    tt, w = SCAN_TT, SSM_BLK_STATE
    u_spec = lambda b: pl.BlockSpec((tt, LANES), lambda j, i: (i, par * j + b))
    return pl.pallas_call(
        _s5_prompt_kernel,
        grid=(nb, n_prompt // tt),
        in_specs=[u_spec(b) for b in range(par)] + [
            pl.BlockSpec((1, par * LANES), lambda j, i: (0, j)),
            pl.BlockSpec((SCAN_L, par * w), lambda j, i: (0, j)),
            pl.BlockSpec((SCAN_L, par * w), lambda j, i: (0, j)),
            pl.BlockSpec((par, LANES, 2 * w), lambda j, i: (j, 0, 0)),
            pl.BlockSpec((par, 2 * w, LANES), lambda j, i: (j, 0, 0))],
        out_specs=[pl.BlockSpec((tt, par * LANES), lambda j, i: (i, j)),
                   pl.BlockSpec((1, par * w), lambda j, i: (0, j)),
                   pl.BlockSpec((1, par * w), lambda j, i: (0, j))],
        out_shape=[jax.ShapeDtypeStruct((n_prompt, SSM_WIDTH), BF16),
                   jax.ShapeDtypeStruct((1, SSM_FLAT), F32),
                   jax.ShapeDtypeStruct((1, SSM_FLAT), F32)],
        scratch_shapes=[pltpu.VMEM((par, tt, LANES), F32), pltpu.VMEM((par, tt, 2 * w), F32),
                        pltpu.VMEM((par, tt, 2 * w), BF16),
                        pltpu.VMEM((par, SCAN_NC, w), F32), pltpu.VMEM((par, SCAN_NC, w), F32),
                        pltpu.VMEM((par, 1, w), F32), pltpu.VMEM((par, 1, w), F32),
                        pltpu.VMEM((par, tt, LANES), F32)],
        compiler_params=_cparams(("arbitrary", "arbitrary")),
        name="s5_prompt",
    )(*([proj] * par), d_skip, pw_re, pw_im, b_mat, c_mat)


def _s5_sample_kernel(u_ref, d_ref, pre_ref, pim_ref, bm_ref, cm_ref, s0re_ref, s0im_ref,
                      y_ref, fre_ref, fim_ref, up_scr, bu_scr, lhs_scr, yn_scr, *, n_seq, n_step):
    w = SSM_BLK_STATE
    rb = 16
    for s in range(n_step):
        up_scr[s * n_seq:(s + 1) * n_seq, :] = u_ref[pl.ds(s, n_seq, stride=n_step), :]
    up = up_scr[...]
    bu_scr[...] = jnp.dot(up.astype(BF16), bm_ref[0], preferred_element_type=F32)
    a_re = jnp.broadcast_to(pre_ref[0:1, :], (rb, w))
    a_im = jnp.broadcast_to(pim_ref[0:1, :], (rb, w))

    def seq_block(b, _):
        r0 = pl.multiple_of(b * rb, rb)
        s_re, s_im = s0re_ref[pl.ds(r0, rb), :], s0im_ref[pl.ds(r0, rb), :]
        for s in range(n_step):
            rows = pl.ds(pl.multiple_of(s * n_seq + r0, rb), rb)
            s_re, s_im = _cmul_add(a_re, a_im, s_re, s_im, bu_scr[rows, 0:w], bu_scr[rows, w:2 * w])
            lhs_scr[rows, 0:w] = s_re.astype(BF16)
            lhs_scr[rows, w:2 * w] = s_im.astype(BF16)
        fre_ref[pl.ds(r0, rb), :] = s_re
        fim_ref[pl.ds(r0, rb), :] = s_im
        return 0

    lax.fori_loop(0, n_seq // rb, seq_block, 0)
    y = jnp.dot(lhs_scr[...], cm_ref[0], preferred_element_type=F32) + d_ref[...] * up
    for s in range(n_step):
        yn_scr[pl.ds(s, n_seq, stride=n_step), :] = y[s * n_seq:(s + 1) * n_seq, :]
    y_ref[...] = _gelu_tanh(yn_scr[...]).astype(y_ref.dtype)


def _s5_sample(proj, n_prompt, n_seq, n_step, d_skip, pw_re, pw_im, b_mat, c_mat, s0_re, s0_im):
    nb = SSM_GROUPS // SSM_BLK
    rows, w = n_seq * n_step, SSM_BLK_STATE
    rblk = n_prompt // rows
    return pl.pallas_call(
        functools.partial(_s5_sample_kernel, n_seq=n_seq, n_step=n_step),
        grid=(nb,),
        in_specs=[pl.BlockSpec((rows, LANES), lambda j: (rblk, j)),
                  pl.BlockSpec((1, LANES), lambda j: (0, j)),
                  pl.BlockSpec((SCAN_L, w), lambda j: (0, j)),
                  pl.BlockSpec((SCAN_L, w), lambda j: (0, j)),
                  pl.BlockSpec((1, LANES, 2 * w), lambda j: (j, 0, 0)),
                  pl.BlockSpec((1, 2 * w, LANES), lambda j: (j, 0, 0)),
                  pl.BlockSpec((n_seq, w), lambda j: (0, j)),
                  pl.BlockSpec((n_seq, w), lambda j: (0, j))],
        out_specs=[pl.BlockSpec((rows, LANES), lambda j: (0, j)),
                   pl.BlockSpec((n_seq, w), lambda j: (0, j)),
                   pl.BlockSpec((n_seq, w), lambda j: (0, j))],
        out_shape=[jax.ShapeDtypeStruct((rows, SSM_WIDTH), BF16),
                   jax.ShapeDtypeStruct((n_seq, SSM_FLAT), F32),
                   jax.ShapeDtypeStruct((n_seq, SSM_FLAT), F32)],
        scratch_shapes=[pltpu.VMEM((rows, LANES), F32), pltpu.VMEM((rows, 2 * w), F32),
                        pltpu.VMEM((rows, 2 * w), BF16), pltpu.VMEM((rows, LANES), F32)],
        compiler_params=_cparams(("arbitrary",)),
        name="s5_sample",
    )(proj, d_skip, pw_re, pw_im, b_mat, c_mat, s0_re, s0_im)


def _attn_prompt_kernel(*refs):
    ins, o_ref, scr = refs[:15], refs[15], refs[16:]
    sb = pl.program_id(0)
    scale = HEAD_DIM ** -0.5
    qi = lax.broadcasted_iota(jnp.int32, (BAND, 2 * BAND), 0)
    kj = lax.broadcasted_iota(jnp.int32, (BAND, 2 * BAND), 1)
    dist = qi + BAND - kj
    band_ok = (dist >= 0) & (dist <= BAND)

    for g, (_, dil) in enumerate(DILATION_PATTERNS):
        q_ref, k_ref, v_ref, kp_ref, vp_ref = ins[5 * g:5 * g + 5]
        kbuf, vbuf, o_scr, m_scr, l_scr = scr[5 * g:5 * g + 5]
        pre = BAND * dil
        kbuf[0:pre, :] = kp_ref[...]
        kbuf[pre:pre + ATT_SB, :] = k_ref[...]
        vbuf[0:pre, :] = vp_ref[...]
        vbuf[pre:pre + ATT_SB, :] = v_ref[...]
        nblk = ATT_SB // pre

        def block(idx, _, dil=dil, pre=pre, nblk=nblk, q_ref=q_ref, kbuf=kbuf, vbuf=vbuf,
                  o_scr=o_scr, m_scr=m_scr, l_scr=l_scr):
            r = idx // nblk
            b = idx - r * nblk
            row0 = r + b * pre
            if dil == 1:
                q_rows = pl.ds(pl.multiple_of(row0, BAND), BAND)
                kv_rows = pl.ds(pl.multiple_of(row0, BAND), 2 * BAND)
            else:
                q_rows = pl.ds(row0, BAND, stride=dil)
                kv_rows = pl.ds(row0, 2 * BAND, stride=dil)
            q = (q_ref[q_rows, :] * scale).astype(BF16)
            kw = kbuf[kv_rows, :].astype(BF16)
            vw = vbuf[kv_rows, :].astype(BF16)
            s = lax.dot_general(q, kw, (((1,), (1,)), ((), ())), preferred_element_type=F32)
            s = jnp.where(band_ok & ((kj >= BAND) | (sb > 0) | (b > 0)), s, NEG)
            m = jnp.max(s, axis=-1, keepdims=True)
            p = jnp.exp(s - m)
            l = jnp.sum(p, axis=-1, keepdims=True)
            o = jnp.dot(p.astype(BF16), vw, preferred_element_type=F32)
            o_scr[q_rows, :] = o
            m_scr[q_rows, :] = jnp.broadcast_to(m, (BAND, HEAD_DIM))
            l_scr[q_rows, :] = jnp.broadcast_to(l, (BAND, HEAD_DIM))
            return 0

        lax.fori_loop(0, ATT_SB // BAND, block, 0, unroll=ATT_UNROLL)

    ms = [scr[5 * g + 3][...] for g in range(N_PATTERNS)]
    mx = jnp.maximum(jnp.maximum(ms[0], ms[1]), ms[2])
    num = jnp.zeros((ATT_SB, HEAD_DIM), F32)
    den = jnp.zeros((ATT_SB, HEAD_DIM), F32)
    for g in range(N_PATTERNS):
        wgt = jnp.exp(ms[g] - mx)
        num = num + wgt * scr[5 * g + 2][...]
        den = den + wgt * scr[5 * g + 4][...]
    o_ref[...] = num / den


def _attn_prompt(proj, n_prompt):
    hcol = lambda off, g, j: (off + g * GROUP_WIDTH) // HEAD_DIM + j
    in_specs, scratch = [], []
    for g, (_, dil) in enumerate(DILATION_PATTERNS):
        pre = BAND * dil
        per = ATT_SB // pre
        cur = lambda off, g=g: pl.BlockSpec((ATT_SB, HEAD_DIM), lambda sb, j: (sb, hcol(off, g, j)))
        prev = lambda off, g=g, per=per, pre=pre: pl.BlockSpec(
            (pre, HEAD_DIM), lambda sb, j: (jnp.maximum(sb * per - 1, 0), hcol(off, g, j)))
        in_specs += [cur(OFF_Q), cur(OFF_K), cur(OFF_V), prev(OFF_K), prev(OFF_V)]
        scratch += [pltpu.VMEM((pre + ATT_SB, HEAD_DIM), F32), pltpu.VMEM((pre + ATT_SB, HEAD_DIM), F32),
                    pltpu.VMEM((ATT_SB, HEAD_DIM), F32), pltpu.VMEM((ATT_SB, HEAD_DIM), F32),
                    pltpu.VMEM((ATT_SB, HEAD_DIM), F32)]
    return pl.pallas_call(
        _attn_prompt_kernel,
        grid=(n_prompt // ATT_SB, HEADS_PER_GROUP),
        in_specs=in_specs,
        out_specs=pl.BlockSpec((ATT_SB, HEAD_DIM), lambda sb, j: (sb, j)),
        out_shape=jax.ShapeDtypeStruct((n_prompt, GROUP_WIDTH), F32),
        scratch_shapes=scratch,
        compiler_params=_cparams(("arbitrary", "arbitrary")),
        name="attn_prompt",
    )(*([proj] * 15))


SEQ_PER_STEP = 4
KV_PLANES = 2 * HEADS_PER_GROUP


def _compact_pitch(n_step):
    tiles = n_step * KV_PLANES // 8
    return 8 * (tiles + 1 - tiles % 2)


def _sample_bias(n_step):
    rows = HEADS_PER_GROUP * n_step
    step = np.arange(rows) % n_step
    cache_bias, new_bias = [], []
    for (window, dil) in DILATION_PATTERNS:
        wb = min(window, PAST_LEN)
        band = window // dil
        if dil > n_step:
            res, i = np.meshgrid(np.arange(n_step), np.arange(wb // dil), indexing='ij')
            c = (i * dil + res).reshape(-1)
        else:
            c = np.arange(wb)
        delta = wb + step[:, None] - c[None, :]
        ok = (delta >= 0) & (delta % dil == 0) & (delta // dil <= band)
        cache_bias.append(np.where(ok, 0.0, NEG).astype(np.float32))
        nb = np.full((SEQ_PER_STEP, rows, LANES), NEG, np.float32)
        for a in range(SEQ_PER_STEP):
            for sp in range(n_step):
                dl = step - sp
                okn = (dl >= 0) & (dl % dil == 0) & (dl // dil <= band)
                nb[a, :, a * n_step + sp] = np.where(okn, 0.0, NEG)
        new_bias.append(nb)
    return cache_bias, new_bias


def _attn_sample_kernel(*refs, n_step):
    (q0, k0, v0, q1, k1, v1, q2, k2, v2, c0, c1, c2, cb0, cb1, cb2, nb0, nb1, nb2, o_ref) = refs
    qs, ks, vs = (q0, q1, q2), (k0, k1, k2), (v0, v1, v2)
    caches, cbias, nbias = (c0, c1, c2), (cb0, cb1, cb2), (nb0, nb1, nb2)
    rows = HEADS_PER_GROUP * n_step
    gw = GROUP_WIDTH
    scale = HEAD_DIM ** -0.5
    row_head = lax.broadcasted_iota(jnp.int32, (rows, gw), 0) // n_step
    lane_head = lax.broadcasted_iota(jnp.int32, (rows, gw), 1) // HEAD_DIM
    own_head = row_head == lane_head
    nt = (((1,), (1,)), ((), ()))
    pad = jnp.zeros((LANES - SEQ_PER_STEP * n_step, gw), F32)

    def planes(load):
        k = jnp.concatenate([load(h) for h in range(HEADS_PER_GROUP)], axis=1)
        v = jnp.concatenate([load(HEADS_PER_GROUP + h) for h in range(HEADS_PER_GROUP)], axis=1)
        return k.astype(BF16), v.astype(BF16)

    for a in range(SEQ_PER_STEP):
        pieces = []
        for g, (window, dil) in enumerate(DILATION_PATTERNS):
            q = qs[g][a * n_step:(a + 1) * n_step, :] * scale
            qbd = jnp.where(own_head, jnp.concatenate([q] * HEADS_PER_GROUP, axis=0), 0.0).astype(BF16)
            cache = caches[g]
            per_seq = cache.shape[0] // SEQ_PER_STEP
            if dil > n_step:
                pitch = cache.shape[1]
                flat = cache.reshape(cache.shape[0] * pitch, HEAD_DIM)
                kvs = [planes(lambda p, r=r: flat[pl.ds(a * per_seq * pitch + r * KV_PLANES + p, per_seq,
                                                        stride=pitch), :])
                       for r in range(n_step)]
            else:
                kvs = [planes(lambda p: cache[pl.ds(a * per_seq + p, per_seq // KV_PLANES, stride=KV_PLANES), :])]
            sc = jnp.concatenate([lax.dot_general(qbd, k, nt, preferred_element_type=F32) for k, _ in kvs], axis=1)
            pieces.append((sc + cbias[g][...], [v for _, v in kvs]))
            k_new = jnp.concatenate([ks[g][...], pad], axis=0).astype(BF16)
            v_new = jnp.concatenate([vs[g][...], pad], axis=0).astype(BF16)
            sn = lax.dot_general(qbd, k_new, nt, preferred_element_type=F32) + nbias[g][a]
            pieces.append((sn, [v_new]))
        m = functools.reduce(jnp.maximum, [jnp.max(s, axis=-1, keepdims=True) for s, _ in pieces])
        l = jnp.zeros((rows, 1), F32)
        acc = jnp.zeros((rows, gw), F32)
        for s, vals in pieces:
            p = jnp.exp(s - m)
            l = l + jnp.sum(p, axis=-1, keepdims=True)
            pb = p.astype(BF16)
            nk = pb.shape[1] // len(vals)
            for r, v in enumerate(vals):
                acc = acc + jnp.dot(pb[:, r * nk:(r + 1) * nk], v, preferred_element_type=F32)
        acc = jnp.where(own_head, acc, 0.0)
        o16 = functools.reduce(lambda x, y: x + y,
                               [acc[:, h * HEAD_DIM:(h + 1) * HEAD_DIM] for h in range(HEADS_PER_GROUP)]) / l
        for h in range(HEADS_PER_GROUP):
            o_ref[a * n_step:(a + 1) * n_step, h * HEAD_DIM:(h + 1) * HEAD_DIM] = o16[h * n_step:(h + 1) * n_step, :]


def _attn_sample(proj, n_prompt, n_seq, n_step, caches):
    rows = SEQ_PER_STEP * n_step
    rblk = n_prompt // rows
    cache_bias, new_bias = _sample_bias(n_step)
    tok = lambda off, g: pl.BlockSpec((rows, GROUP_WIDTH), lambda i: (rblk + i, (off + g * GROUP_WIDTH) // GROUP_WIDTH))
    in_specs, args = [], []
    for g in range(N_PATTERNS):
        in_specs += [tok(OFF_Q, g), tok(OFF_K, g), tok(OFF_V, g)]
        args += [proj, proj, proj]
    for g, (window, dil) in enumerate(DILATION_PATTERNS):
        c = caches[g]
        wb = c.shape[1]
        if dil > n_step:
            pitch = _compact_pitch(n_step)
            assert pitch <= dil * KV_PLANES and wb % dil == 0
            c = c.reshape(n_seq * (wb // dil), dil * KV_PLANES, HEAD_DIM)
            in_specs.append(pl.BlockSpec((SEQ_PER_STEP * (wb // dil), pitch, HEAD_DIM), lambda i: (i, 0, 0)))
        else:
            c = c.reshape(n_seq * wb * KV_PLANES, HEAD_DIM)
            in_specs.append(pl.BlockSpec((SEQ_PER_STEP * wb * KV_PLANES, HEAD_DIM), lambda i: (i, 0)))
        args.append(c)
    for b in cache_bias:
        in_specs.append(pl.BlockSpec(b.shape, lambda i: (0, 0)))
        args.append(jnp.asarray(b))
    for b in new_bias:
        in_specs.append(pl.BlockSpec(b.shape, lambda i: (0, 0, 0)))
        args.append(jnp.asarray(b))
    return pl.pallas_call(
        functools.partial(_attn_sample_kernel, n_step=n_step),
        grid=(n_seq // SEQ_PER_STEP,),
        in_specs=in_specs,
        out_specs=pl.BlockSpec((rows, GROUP_WIDTH), lambda i: (i, 0)),
        out_shape=jax.ShapeDtypeStruct((n_seq * n_step, GROUP_WIDTH), F32),
        compiler_params=_cparams(("arbitrary",)),
        name="attn_sample",
    )(*args)


def _mix_kernel(yp_ref, ys_ref, op_ref, os_ref, wa_ref, wb_ref, wbr_ref, ga_ref, gb_ref, o_ref,
                wa_bf, wb_bf, wbr_bf, *, n_prompt_tiles):
    i = pl.program_id(1)

    @pl.when(i == 0)
    def _():
        wa_bf[...] = wa_ref[...].astype(BF16)
        wb_bf[...] = wb_ref[...].astype(BF16)
        wbr_bf[...] = wbr_ref[...].astype(BF16)

    is_s = i >= n_prompt_tiles
    for rows in _row_halves(o_ref.shape[0]):
        y = jnp.where(is_s, ys_ref[rows, :], yp_ref[rows, :])
        o = jnp.where(is_s, os_ref[rows, :], op_ref[rows, :]).astype(BF16)
        glu_a = jnp.dot(y, wa_bf[...], preferred_element_type=F32)
        glu_b = jnp.dot(y, wb_bf[...], preferred_element_type=F32)
        branch_a = glu_a * _sigmoid(glu_b)
        branch_b = jnp.dot(o, wbr_bf[...], preferred_element_type=F32)
        o_ref[rows, :] = (_sigmoid(ga_ref[rows, :]) * branch_a
                          + _sigmoid(gb_ref[rows, :]) * branch_b).astype(o_ref.dtype)


def _mix(y_p, y_s, o_p, o_s, w_glu, w_attn_br, proj):
    tp, ts = y_p.shape[0], y_s.shape[0]
    tm, tn = TOK_TILE, 512
    assert OFF_G % tn == 0
    npt, nst = tp // tm, ts // tm
    ncol = D_MODEL // tn
    prow = lambda n, i: (jnp.minimum(i, npt - 1), 0)
    srow = lambda n, i: (jnp.maximum(i - npt, 0), 0)
    return pl.pallas_call(
        functools.partial(_mix_kernel, n_prompt_tiles=npt),
        grid=(ncol, npt + nst),
        in_specs=[pl.BlockSpec((tm, SSM_WIDTH), prow), pl.BlockSpec((tm, SSM_WIDTH), srow),
                  pl.BlockSpec((tm, GROUP_WIDTH), prow), pl.BlockSpec((tm, GROUP_WIDTH), srow),
                  pl.BlockSpec((SSM_WIDTH, tn), lambda n, i: (0, n)),
                  pl.BlockSpec((SSM_WIDTH, tn), lambda n, i: (0, ncol + n)),
                  pl.BlockSpec((GROUP_WIDTH, tn), lambda n, i: (0, n)),
                  pl.BlockSpec((tm, tn), lambda n, i: (i, OFF_G // tn + n)),
                  pl.BlockSpec((tm, tn), lambda n, i: (i, OFF_G // tn + ncol + n))],
        out_specs=pl.BlockSpec((tm, tn), lambda n, i: (i, n)),
        out_shape=jax.ShapeDtypeStruct((tp + ts, D_MODEL), BF16),
        scratch_shapes=[pltpu.VMEM((SSM_WIDTH, tn), BF16), pltpu.VMEM((SSM_WIDTH, tn), BF16),
                        pltpu.VMEM((GROUP_WIDTH, tn), BF16)],
        compiler_params=_cparams(("arbitrary", "arbitrary")),
        name="glu_mix",
    )(y_p, y_s, o_p, o_s, w_glu, w_glu, w_attn_br, proj, proj)


def _route(logits):
    lane = lax.broadcasted_iota(jnp.int32, logits.shape, 1).astype(F32)
    big = 1000.0
    first = lambda cond: jnp.min(jnp.where(cond, lane, big), axis=-1, keepdims=True)
    is_g = lane < N_EXPERT_GROUPS
    lg = jnp.where(is_g, logits, NEG)
    mg = jnp.max(lg, axis=-1, keepdims=True)
    g_sel = first(lg == mg)
    p_group = 1.0 / jnp.sum(jnp.where(is_g, jnp.exp(lg - mg), 0.0), axis=-1, keepdims=True)
    e_lo = N_EXPERT_GROUPS + EXPERTS_PER_GROUP * g_sel
    le = jnp.where((lane >= e_lo) & (lane < e_lo + EXPERTS_PER_GROUP), logits, NEG)
    v1 = jnp.max(le, axis=-1, keepdims=True)
    i1 = first(le == v1)
    le2 = jnp.where(lane == i1, NEG, le)
    v2 = jnp.max(le2, axis=-1, keepdims=True)
    i2 = first(le2 == v2)
    e2 = jnp.exp(v2 - v1)
    w1 = p_group / (1.0 + e2)
    w2 = p_group * e2 / (1.0 + e2)
    pick = lambda k, val: jnp.where(lane == k, val, 0.0)
    return (pick(ROUTE_E1, i1 - N_EXPERT_GROUPS) + pick(ROUTE_E2, i2 - N_EXPERT_GROUPS)
            + pick(ROUTE_W1, w1) + pick(ROUTE_W2, w2))


def _outproj_kernel(mix_ref, w_ref, xp_ref, xs_ref, g_ref, wr_ref, br_ref,
                    gtp_ref, scp_ref, shp_ref, gts_ref, scs_ref, shs_ref,
                    x1_ref, h2_ref, route_ref, *, n_prompt_tiles):
    is_s = pl.program_id(0) >= n_prompt_tiles

    def split(v):
        high = v.astype(BF16)
        return high, (v - high.astype(F32)).astype(BF16)

    x = jnp.where(is_s, xs_ref[...], xp_ref[...])
    gt = jnp.where(is_s, gts_ref[...], gtp_ref[0:1, :])
    sc = jnp.where(is_s, scs_ref[...], scp_ref[0:1, :])
    sh = jnp.where(is_s, shs_ref[...], shp_ref[0:1, :])
    x1 = x + gt * jnp.dot(mix_ref[...], w_ref[...], preferred_element_type=F32)
    x1_ref[...] = x1
    ms = jnp.mean(x1 * x1, axis=-1, keepdims=True)
    h2 = (x1 * lax.rsqrt(ms + EPS) * g_ref[...]) * (1.0 + sc) + sh
    h2_ref[...] = h2.astype(h2_ref.dtype)
    r = jnp.dot(jnp.concatenate(split(h2), axis=0), jnp.concatenate(split(wr_ref[...]), axis=1),
                preferred_element_type=F32)
    n = h2.shape[0]
    logits = (r[:n, :LANES] + r[:n, LANES:]) + (r[n:, :LANES] + r[n:, LANES:]) + br_ref[...]
    route_ref[...] = _route(logits)


def _outproj(mixed, w_out_bf, x_p, x_s, g2, w_router, b_router, mod):
    tp, ts = x_p.shape[0], x_s.shape[0]
    tm = SMALL_TOK_TILE
    npt, nst = tp // tm, ts // tm
    prow = lambda i: (jnp.minimum(i, npt - 1), 0)
    srow = lambda i: (jnp.maximum(i - npt, 0), 0)
    const = lambda i: (0, 0)
    vec = pl.BlockSpec((1, D_MODEL), const)
    gt_p, gt_s = _mod_specs(MOD_GT1, tm, ts, npt)
    sc_p, sc_s = _mod_specs(MOD_SC2, tm, ts, npt)
    sh_p, sh_s = _mod_specs(MOD_SH2, tm, ts, npt)
    full = lambda i: (i, 0)
    return pl.pallas_call(
        functools.partial(_outproj_kernel, n_prompt_tiles=npt),
        grid=(npt + nst,),
        in_specs=[pl.BlockSpec((tm, D_MODEL), full), pl.BlockSpec((D_MODEL, D_MODEL), const),
                  pl.BlockSpec((tm, D_MODEL), prow), pl.BlockSpec((tm, D_MODEL), srow),
                  vec, pl.BlockSpec((D_MODEL, LANES), const), pl.BlockSpec((1, LANES), const),
                  gt_p, sc_p, sh_p, gt_s, sc_s, sh_s],
        out_specs=[pl.BlockSpec((tm, D_MODEL), full), pl.BlockSpec((tm, D_MODEL), full),
                   pl.BlockSpec((tm, LANES), full)],
        out_shape=[jax.ShapeDtypeStruct((tp + ts, D_MODEL), F32),
                   jax.ShapeDtypeStruct((tp + ts, D_MODEL), F32),
                   jax.ShapeDtypeStruct((tp + ts, LANES), F32)],
        compiler_params=_cparams(("arbitrary",)),
        name="out_proj_norm2_router",
    )(mixed, w_out_bf, x_p, x_s, g2, w_router, b_router, mod, mod, mod, mod, mod, mod)


def _dispatch_plan(route, tm):
    e = route[:, ROUTE_E1:ROUTE_E2 + 1].astype(jnp.int32).reshape(-1)
    n_pairs = e.shape[0]
    onehot = (e[:, None] == jnp.arange(N_EXPERTS, dtype=jnp.int32)[None, :]).astype(jnp.int32)
    csum = jnp.cumsum(onehot, axis=0)
    rank = jnp.sum(onehot * csum, axis=1) - 1
    tiles_per_expert = (csum[-1] + tm - 1) // tm
    tile_end = jnp.cumsum(tiles_per_expert)
    tile_start = tile_end - tiles_per_expert
    dest = (tile_start[e] * tm + rank).astype(jnp.int32)
    max_tiles = n_pairs // tm + N_EXPERTS
    k = jnp.arange(max_tiles, dtype=jnp.int32)
    tile_expert = jnp.minimum(jnp.sum((k[:, None] >= tile_end[None, :]).astype(jnp.int32), axis=1), N_EXPERTS - 1)
    n_used = tile_end[-1].astype(jnp.int32)
    last_expert = jnp.take(tile_expert, n_used - 1)
    tile_expert = jnp.where(k < n_used, tile_expert, last_expert).astype(jnp.int32)
    pad_end = (jnp.concatenate([tile_end, tile_end[-1:]]) * tm).astype(jnp.int32)
    pad_len = (tiles_per_expert * tm - csum[-1]).astype(jnp.int32)
    return dest, pad_end, pad_len, tile_expert, n_used.reshape(1), max_tiles


def _start_pair_copies(dest_ref, tile, rows, make):
    def body(r, _):
        for k in range(2):
            make(r, k, dest_ref[(tile * rows + r) * 2 + k]).start()
        return 0
    lax.fori_loop(0, rows, body, 0, unroll=8)


def _pair_copies(dest_ref, tile, rows, make, make_all):
    _start_pair_copies(dest_ref, tile, rows, make)
    for k in range(2):
        make_all(k).wait()


def _dispatch_kernel(dest_ref, pad_end_ref, pad_len_ref, h_ref, xs_ref, zero_buf, sem, pad_sem):
    rows = h_ref.shape[0]

    @pl.when(pl.program_id(0) == 0)
    def _():
        zero_buf[...] = jnp.zeros_like(zero_buf)

        def pad_copies(fn):
            for e in range(N_EXPERTS):
                end, left = pad_end_ref[e], pad_len_ref[e]
                size = zero_buf.shape[0]
                while size >= SUBLANES:
                    take = (left & size) != 0
                    end = end - jnp.where(take, size, 0)

                    @pl.when(take)
                    def _(start=end, size=size):
                        fn(pltpu.make_async_copy(zero_buf.at[pl.ds(0, size)],
                                                 xs_ref.at[pl.ds(pl.multiple_of(start, size), size)], pad_sem))

                    size //= 2
                for r in range(1, SUBLANES):
                    @pl.when((left & (SUBLANES - 1)) >= r)
                    def _(row=end - r):
                        fn(pltpu.make_async_copy(zero_buf.at[pl.ds(0, 1)], xs_ref.at[pl.ds(row, 1)], pad_sem))
            size = zero_buf.shape[0]
            tail = pad_end_ref[N_EXPERTS]
            for t in range(N_EXPERTS * MOE_TM // size):
                @pl.when(tail + t * size < xs_ref.shape[0])
                def _(start=tail + t * size):
                    fn(pltpu.make_async_copy(zero_buf, xs_ref.at[pl.ds(pl.multiple_of(start, size), size)], pad_sem))

        pad_copies(lambda c: c.start())
        pad_copies(lambda c: c.wait())

    make = lambda r, k, d: pltpu.make_async_copy(h_ref.at[pl.ds(r, 1)], xs_ref.at[pl.ds(d, 1)], sem)
    make_all = lambda k: pltpu.make_async_copy(h_ref, xs_ref.at[pl.ds(0, rows)], sem)
    _pair_copies(dest_ref, pl.program_id(0), rows, make, make_all)


def _dispatch(dest, pad_end, pad_len, h2, n_slots):
    n_tok = h2.shape[0]
    tm = SMALL_TOK_TILE
    return pl.pallas_call(
        _dispatch_kernel,
        grid_spec=pltpu.PrefetchScalarGridSpec(
            num_scalar_prefetch=3, grid=(n_tok // tm,),
            in_specs=[pl.BlockSpec((tm, D_MODEL), lambda i, d, ps, pn: (i, 0))],
            out_specs=pl.BlockSpec(memory_space=pl.ANY),
            scratch_shapes=[pltpu.VMEM((MOE_TM // 2, D_MODEL), F32),
                            pltpu.SemaphoreType.DMA(()), pltpu.SemaphoreType.DMA(())]),
        out_shape=jax.ShapeDtypeStruct((n_slots, D_MODEL), F32),
        compiler_params=_cparams(("arbitrary",)),
        name="moe_dispatch",
    )(dest, pad_end, pad_len, h2)


def _experts_kernel(te_ref, used_ref, xs_ref, wgu_ref, wd_ref, y_ref, wgu_bf, wd_bf):
    k = pl.program_id(0)
    new_expert = (k == 0) | (te_ref[k] != te_ref[jnp.maximum(k - 1, 0)])

    @pl.when(new_expert)
    def _():
        wgu_bf[...] = wgu_ref[0].astype(BF16)
        wd_bf[...] = wd_ref[0].astype(BF16)

    @pl.when(k < used_ref[0])
    def _():
        gu = jnp.dot(xs_ref[...].astype(BF16), wgu_bf[...], preferred_element_type=F32)
        gate, up = gu[:, :EXPERT_FF], gu[:, EXPERT_FF:]
        act = (gate * _sigmoid(gate)) * up
        y_ref[...] = jnp.dot(act.astype(BF16), wd_bf[...], preferred_element_type=F32)

    @pl.when(k >= used_ref[0])
    def _():
        y_ref[...] = jnp.zeros_like(y_ref)


def _experts(tile_expert, n_used, xs, w_gu, w_down, max_tiles):
    tm = MOE_TM
    row = lambda k, te, nu: (jnp.minimum(k, nu[0] - 1), 0)
    return pl.pallas_call(
        _experts_kernel,
        grid_spec=pltpu.PrefetchScalarGridSpec(
            num_scalar_prefetch=2, grid=(max_tiles,),
            in_specs=[pl.BlockSpec((tm, D_MODEL), row),
                      pl.BlockSpec((1, D_MODEL, 2 * EXPERT_FF), lambda k, te, nu: (te[k], 0, 0)),
                      pl.BlockSpec((1, EXPERT_FF, D_MODEL), lambda k, te, nu: (te[k], 0, 0))],
            out_specs=pl.BlockSpec((tm, D_MODEL), lambda k, te, nu: (k, 0)),
            scratch_shapes=[pltpu.VMEM((D_MODEL, 2 * EXPERT_FF), BF16), pltpu.VMEM((EXPERT_FF, D_MODEL), BF16)]),
        out_shape=jax.ShapeDtypeStruct((max_tiles * tm, D_MODEL), F32),
        compiler_params=_cparams(("arbitrary",)),
        name="moe_experts",
    )(tile_expert, n_used, xs, w_gu, w_down)


def _combine_kernel(dest_ref, y_hbm, x1_ref, route_ref, gtp_ref, gts_ref, yp_ref, ys_ref, buf, sem, *, n_prompt_tiles):
    i = pl.program_id(0)
    rows = x1_ref.shape[0]

    def gather(tile, slot):
        make = lambda r, k, d: pltpu.make_async_copy(y_hbm.at[pl.ds(d, 1)], buf.at[slot, k, pl.ds(r, 1)],
                                                     sem.at[slot])
        _start_pair_copies(dest_ref, tile, rows, make)

    @pl.when(i == 0)
    def _():
        gather(0, 0)

    slot = lax.rem(i, 2)

    @pl.when(i + 1 < pl.num_programs(0))
    def _():
        gather(i + 1, 1 - slot)

    for k in range(2):
        pltpu.make_async_copy(y_hbm.at[pl.ds(0, rows)], buf.at[slot, k], sem.at[slot]).wait()
    route = route_ref[...]
    lane = lax.broadcasted_iota(jnp.int32, route.shape, 1)
    w1 = jnp.sum(jnp.where(lane == ROUTE_W1, route, 0.0), axis=-1, keepdims=True)
    w2 = jnp.sum(jnp.where(lane == ROUTE_W2, route, 0.0), axis=-1, keepdims=True)
    moe = w1 * buf[slot, 0] + w2 * buf[slot, 1]

    @pl.when(i < n_prompt_tiles)
    def _():
        yp_ref[...] = x1_ref[...] + gtp_ref[0:1, :] * moe

    @pl.when(i >= n_prompt_tiles)
    def _():
        ys_ref[...] = x1_ref[...] + gts_ref[...] * moe


def _combine(dest, y_slots, x1, route, mod, tp, ts):
    tm = SMALL_TOK_TILE
    npt, nst = tp // tm, ts // tm
    row = lambda i, d: (i, 0)
    gt_p, gt_s = _mod_specs(MOD_GT2, tm, ts, npt)
    return pl.pallas_call(
        functools.partial(_combine_kernel, n_prompt_tiles=npt),
        grid_spec=pltpu.PrefetchScalarGridSpec(
            num_scalar_prefetch=1, grid=(npt + nst,),
            in_specs=[pl.BlockSpec(memory_space=pl.ANY),
                      pl.BlockSpec((tm, D_MODEL), row), pl.BlockSpec((tm, LANES), row), gt_p, gt_s],
            out_specs=[pl.BlockSpec((tm, D_MODEL), lambda i, d: (jnp.minimum(i, npt - 1), 0)),
                       pl.BlockSpec((tm, D_MODEL), lambda i, d: (jnp.maximum(i - npt, 0), 0))],
            scratch_shapes=[pltpu.VMEM((2, 2, tm, D_MODEL), F32), pltpu.SemaphoreType.DMA((2,))]),
        out_shape=[jax.ShapeDtypeStruct((tp, D_MODEL), F32), jax.ShapeDtypeStruct((ts, D_MODEL), F32)],
        compiler_params=_cparams(("arbitrary",)),
        name="moe_combine",
    )(dest, y_slots, x1, route, mod, mod)


def _moe(h2, route, x1, w_gu, w_down, mod, tp, ts):
    dest, pad_end, pad_len, tile_expert, n_used, max_tiles = _dispatch_plan(route, MOE_TM)
    xs = _dispatch(dest, pad_end, pad_len, h2, max_tiles * MOE_TM)
    y_slots = _experts(tile_expert, n_used, xs, w_gu, w_down, max_tiles)
    return _combine(dest, y_slots, x1, route, mod, tp, ts)


def _kv_pack_kernel(k_ref, v_ref, o_ref):
    n = k_ref.shape[0]
    for h in range(HEADS_PER_GROUP):
        cols = slice(h * HEAD_DIM, (h + 1) * HEAD_DIM)
        o_ref[pl.ds(h, n, stride=KV_PLANES), :] = k_ref[:, cols]
        o_ref[pl.ds(HEADS_PER_GROUP + h, n, stride=KV_PLANES), :] = v_ref[:, cols]


def _kv_pack(proj, row0, rows, g):
    tr = min(rows, SMALL_TOK_TILE)
    assert rows % tr == 0 and row0 % tr == 0
    col = lambda off: (off + g * GROUP_WIDTH) // GROUP_WIDTH
    spec = lambda off: pl.BlockSpec((tr, GROUP_WIDTH), lambda i: (row0 // tr + i, col(off)))
    flat = pl.pallas_call(
        _kv_pack_kernel,
        grid=(rows // tr,),
        in_specs=[spec(OFF_K), spec(OFF_V)],
        out_specs=pl.BlockSpec((tr * KV_PLANES, HEAD_DIM), lambda i: (i, 0)),
        out_shape=jax.ShapeDtypeStruct((rows * KV_PLANES, HEAD_DIM), F32),
        compiler_params=_cparams(("arbitrary",)),
        name="kv_pack",
    )(proj, proj)
    return flat.reshape(rows, 2, HEADS_PER_GROUP, HEAD_DIM)


def _rope_tables(n_prompt, n_seq, n_step):
    half = ROT_DIM // 2
    inv_freq = ROPE_THETA ** (-jnp.arange(half, dtype=F32) / half)
    angles = lambda pos: pos.astype(F32)[:, None] * inv_freq[None, :]
    fine = 128
    assert n_prompt % fine == 0
    coarse_ang = angles(jnp.arange(n_prompt // fine, dtype=jnp.int32) * fine)[:, None, :]
    fine_ang = angles(jnp.arange(fine, dtype=jnp.int32))[None, :, :]
    cc, sc, cf, sf = jnp.cos(coarse_ang), jnp.sin(coarse_ang), jnp.cos(fine_ang), jnp.sin(fine_ang)
    cos_p = (cc * cf - sc * sf).reshape(n_prompt, half)
    sin_p = (sc * cf + cc * sf).reshape(n_prompt, half)
    step_ang = angles(PAST_LEN + jnp.arange(n_step, dtype=jnp.int32))
    cos = jnp.concatenate([cos_p, jnp.tile(jnp.cos(step_ang), (n_seq, 1))], axis=0)
    sin = jnp.concatenate([sin_p, jnp.tile(jnp.sin(step_ang), (n_seq, 1))], axis=0)
    n = n_prompt + n_seq * n_step
    one = jnp.ones((n, HEAD_DIM - ROT_DIM), F32)
    zero = jnp.zeros((n, HEAD_DIM - ROT_DIM), F32)
    zh = jnp.zeros((n, half), F32)
    rc = jnp.concatenate([cos, cos, one], axis=1)
    rs1 = jnp.concatenate([-sin, zh, zero], axis=1)
    rs2 = jnp.concatenate([zh, sin, zero], axis=1)
    return rc, rs1, rs2


def kernel(x_prompt, x_sample, cache_kv_w128, cache_kv_w512, cache_kv_w2048, state_ssm_re, state_ssm_im,
           c_prompt, c_sample, w_ada, b_ada, norm1_g, norm2_g, w_in, ssm_a_re, ssm_a_im, ssm_log_dt,
           ssm_b_re, ssm_b_im, ssm_c_re, ssm_c_im, ssm_d, w_glu, q_norm_g, k_norm_g, w_attn_br, w_out,
           w_router_group, b_router_group, w_router_expert, b_router_expert, w_expert_gate_up, w_expert_down):
    assert x_prompt.shape[0] == 1 and w_ada.shape[0] == 1
    n_prompt = x_prompt.shape[1]
    n_seq, n_step = x_sample.shape[0], x_sample.shape[1]
    n_samp = n_seq * n_step
    assert n_samp % TOK_TILE == 0 and n_prompt % ATT_SB == 0 and (n_prompt + n_samp) % PROJ_TM == 0
    x_p = x_prompt.reshape(n_prompt, D_MODEL)
    x_s = x_sample.reshape(n_samp, D_MODEL)

    c_all = jnp.concatenate([jnp.repeat(c_sample, n_step, axis=0),
                             jnp.broadcast_to(c_prompt, (SUBLANES, D_MODEL))], axis=0)
    mod = _ada(c_all, w_ada[0], b_ada[0])

    h1 = _modnorm(x_p, x_s, norm1_g[0].reshape(1, D_MODEL), mod)
    rc, rs1, rs2 = _rope_tables(n_prompt, n_seq, n_step)
    proj = _inproj(h1, w_in[0], rc, rs1, rs2, q_norm_g[0].reshape(1, HEAD_DIM), k_norm_g[0].reshape(1, HEAD_DIM))

    pw_re, pw_im, bb_re, bb_im = _ssm_prep(ssm_a_re[0], ssm_a_im[0], ssm_log_dt[0], ssm_b_re[0], ssm_b_im[0])
    b_mat, c_mat = _ssm_block_matrices(bb_re, bb_im, ssm_c_re[0], ssm_c_im[0])
    d_skip = ssm_d[0].reshape(1, SSM_WIDTH)
    yg_p, fre_p, fim_p = _s5_prompt(proj, n_prompt, d_skip, pw_re, pw_im, b_mat, c_mat)
    yg_s, fre_s, fim_s = _s5_sample(proj, n_prompt, n_seq, n_step, d_skip, pw_re, pw_im, b_mat, c_mat,
                                    state_ssm_re[0].reshape(n_seq, SSM_FLAT), state_ssm_im[0].reshape(n_seq, SSM_FLAT))

    o_p = _attn_prompt(proj, n_prompt)
    o_s = _attn_sample(proj, n_prompt, n_seq, n_step, (cache_kv_w128[0], cache_kv_w512[0], cache_kv_w2048[0]))

    mixed = _mix(yg_p, yg_s, o_p, o_s, w_glu[0], w_attn_br[0], proj)

    w_router = jnp.concatenate([w_router_group[0], w_router_expert[0],
                                jnp.zeros((D_MODEL, LANES - N_EXPERT_GROUPS - N_EXPERTS), F32)], axis=1)
    b_router = jnp.concatenate([b_router_group[0], b_router_expert[0],
                                jnp.zeros((LANES - N_EXPERT_GROUPS - N_EXPERTS,), F32)]).reshape(1, LANES)
    x1, h2, route = _outproj(mixed, w_out[0].astype(BF16), x_p, x_s, norm2_g[0].reshape(1, D_MODEL),
                             w_router, b_router, mod)
    y_p, y_s = _moe(h2, route, x1, w_expert_gate_up[0], w_expert_down[0], mod, n_prompt, n_samp)

    kv_p, kv_s = [], []
    for g, (window, _) in enumerate(DILATION_PATTERNS):
        keep = min(window, n_prompt)
        kv_p.append(_kv_pack(proj, n_prompt - keep, keep, g).reshape(1, 1, keep, 2, HEADS_PER_GROUP, HEAD_DIM))
        kv_s.append(_kv_pack(proj, n_prompt, n_samp, g).reshape(1, n_seq, n_step, 2, HEADS_PER_GROUP, HEAD_DIM))

    state_shape_p = (1, 1, SSM_GROUPS, SSM_STATE)
    state_shape_s = (1, n_seq, SSM_GROUPS, SSM_STATE)
    return (y_p.reshape(1, n_prompt, D_MODEL), y_s.reshape(n_seq, n_step, D_MODEL),
            kv_p[0], kv_p[1], kv_p[2], fre_p.reshape(state_shape_p), fim_p.reshape(state_shape_p),
            kv_s[0], kv_s[1], kv_s[2], fre_s.reshape(state_shape_s), fim_s.reshape(state_shape_s))
```

```python
import functools
import math

import numpy as np
import jax
import jax.numpy as jnp
from jax import lax
from jax.experimental import pallas as pl
from jax.experimental.pallas import tpu as pltpu

F32 = jnp.float32
BF16 = jnp.bfloat16

D_MODEL = 2048
PAST_LEN = 2048
SSM_WIDTH = D_MODEL // 2
SSM_GROUP = 16
SSM_GROUPS = SSM_WIDTH // SSM_GROUP
SSM_STATE = 64
SSM_FLAT = SSM_GROUPS * SSM_STATE
HEAD_DIM = 128
DILATION_PATTERNS = ((128, 1), (512, 4), (2048, 16))
N_PATTERNS = 3
HEADS_PER_GROUP = 4
GROUP_WIDTH = HEADS_PER_GROUP * HEAD_DIM
ATTN_WIDTH = N_PATTERNS * GROUP_WIDTH
ROT_DIM = HEAD_DIM // 4
ROPE_THETA = 500000.0
OFF_Q = SSM_WIDTH
OFF_K = OFF_Q + ATTN_WIDTH
OFF_V = OFF_K + ATTN_WIDTH
OFF_G = OFF_V + ATTN_WIDTH
IN_COLS = OFF_G + 2 * D_MODEL
N_EXPERT_GROUPS = 4
EXPERTS_PER_GROUP = 4
N_EXPERTS = 16
EXPERT_FF = D_MODEL // 4
EPS = 1e-6
NEG = -1e30

LANES = 128
SUBLANES = 8
VMEM_LIMIT = 56 * 1024 * 1024

TOK_TILE = 512
SMALL_TOK_TILE = 256
PROJ_TM = 544
PROJ_TN = 2432
SCAN_L = 32
SCAN_NC = 16
SCAN_TT = SCAN_L * SCAN_NC
SSM_BLK = 8
SSM_BLK_STATE = SSM_BLK * SSM_STATE
SSM_PAR = 4
MOE_TM = 512
ROUTE_E1, ROUTE_E2, ROUTE_W1, ROUTE_W2 = 0, 1, 2, 3
MOD_SH1, MOD_SC1, MOD_GT1, MOD_SH2, MOD_SC2, MOD_GT2 = range(6)
ATT_SB = 2048
BAND = 128
ATT_UNROLL = 16


def _cparams(sem, vmem=VMEM_LIMIT):
    return pltpu.CompilerParams(dimension_semantics=sem, vmem_limit_bytes=vmem)


def _sigmoid(x):
    return 0.5 * jnp.tanh(0.5 * x) + 0.5


def _row_halves(rows):
    half = rows // 2
    return (slice(0, half), slice(half, rows))


def _gelu_tanh(x):
    c = math.sqrt(2.0 / math.pi)
    return 0.5 * x * (1.0 + jnp.tanh(c * (x + 0.044715 * (x * x * x))))


def _ada_kernel(c_ref, w_ref, b_ref, o_ref, cs_ref):
    @pl.when(pl.program_id(0) == 0)
    def _():
        c = c_ref[...]
        cs_ref[...] = (c * _sigmoid(c)).astype(BF16)

    o_ref[...] = jnp.dot(cs_ref[...], w_ref[...].astype(BF16), preferred_element_type=F32) + b_ref[...]


def _ada(c_all, w_ada, b_ada):
    rows = c_all.shape[0]
    n_out = w_ada.shape[1]
    tn = 1024
    return pl.pallas_call(
        _ada_kernel,
        grid=(n_out // tn,),
        in_specs=[pl.BlockSpec((rows, D_MODEL), lambda n: (0, 0)),
                  pl.BlockSpec((D_MODEL, tn), lambda n: (0, n)),
                  pl.BlockSpec((1, tn), lambda n: (0, n))],
        out_specs=pl.BlockSpec((rows, tn), lambda n: (0, n)),
        out_shape=jax.ShapeDtypeStruct((rows, n_out), F32),
        scratch_shapes=[pltpu.VMEM((rows, D_MODEL), BF16)],
        compiler_params=_cparams(("arbitrary",)),
        name="ada_mod",
    )(c_all, w_ada, b_ada.reshape(1, n_out))


def _modnorm_kernel(xp_ref, xs_ref, g_ref, scp_ref, shp_ref, scs_ref, shs_ref, o_ref, *, n_prompt_tiles):
    is_s = pl.program_id(0) >= n_prompt_tiles

    def norm(x_ref, sc_ref, sh_ref, rows):
        x = x_ref[...]
        ms = jnp.mean(x * x, axis=-1, keepdims=True)
        y = x * lax.rsqrt(ms + EPS) * g_ref[...]
        o_ref[...] = (y * (1.0 + sc_ref[rows, :]) + sh_ref[rows, :]).astype(o_ref.dtype)

    pl.when(jnp.logical_not(is_s))(functools.partial(norm, xp_ref, scp_ref, shp_ref, slice(0, 1)))
    pl.when(is_s)(functools.partial(norm, xs_ref, scs_ref, shs_ref, slice(None)))


def _mod_specs(k, tm, n_samp, n_prompt_tiles):
    prompt = pl.BlockSpec((SUBLANES, D_MODEL), lambda i, *_: (n_samp // SUBLANES, k))
    sample = pl.BlockSpec((tm, D_MODEL), lambda i, *_: (jnp.maximum(i - n_prompt_tiles, 0), k))
    return prompt, sample


def _modnorm(x_p, x_s, g, mod):
    tp, ts = x_p.shape[0], x_s.shape[0]
    tm = TOK_TILE
    npt, nst = tp // tm, ts // tm
    row = lambda i: (jnp.minimum(i, npt - 1), 0)
    srow = lambda i: (jnp.maximum(i - npt, 0), 0)
    const = lambda i: (0, 0)
    sc_p, sc_s = _mod_specs(MOD_SC1, tm, ts, npt)
    sh_p, sh_s = _mod_specs(MOD_SH1, tm, ts, npt)
    return pl.pallas_call(
        functools.partial(_modnorm_kernel, n_prompt_tiles=npt),
        grid=(npt + nst,),
        in_specs=[pl.BlockSpec((tm, D_MODEL), row), pl.BlockSpec((tm, D_MODEL), srow),
                  pl.BlockSpec((1, D_MODEL), const), sc_p, sh_p, sc_s, sh_s],
        out_specs=pl.BlockSpec((tm, D_MODEL), lambda i: (i, 0)),
        out_shape=jax.ShapeDtypeStruct((tp + ts, D_MODEL), BF16),
        compiler_params=_cparams(("arbitrary",)),
        name="modnorm1",
    )(x_p, x_s, g, mod, mod, mod, mod)


def _inproj_kernel(h_ref, w_ref, rc_ref, rs1_ref, rs2_ref, qg_ref, kg_ref, o_ref, wbf_ref, *, heads_per_tile):
    n = pl.program_id(0)

    @pl.when(pl.program_id(1) == 0)
    def _():
        wbf_ref[...] = w_ref[...].astype(BF16)

    half = ROT_DIM // 2
    q_heads = range(OFF_Q // HEAD_DIM, OFF_K // HEAD_DIM)
    k_heads = range(OFF_K // HEAD_DIM, OFF_V // HEAD_DIM)
    pair = 2 * HEAD_DIM

    def tile(col_tile):
        h = h_ref[...]
        for c0 in range(0, heads_per_tile * HEAD_DIM, pair):
            width = min(pair, heads_per_tile * HEAD_DIM - c0)
            acc = jnp.dot(h, wbf_ref[:, c0:c0 + width], preferred_element_type=F32)
            for c in range(c0, c0 + width, HEAD_DIM):
                slot = col_tile * heads_per_tile + c // HEAD_DIM
                x = acc[:, c - c0:c - c0 + HEAD_DIM]
                if slot in q_heads or slot in k_heads:
                    gain = qg_ref[...] if slot in q_heads else kg_ref[...]
                    ms = jnp.mean(x * x, axis=-1, keepdims=True)
                    y = x * lax.rsqrt(ms + EPS) * gain
                    up = pltpu.roll(y, HEAD_DIM - half, 1)
                    dn = pltpu.roll(y, half, 1)
                    x = y * rc_ref[...] + up * rs1_ref[...] + dn * rs2_ref[...]
                o_ref[:, c:c + HEAD_DIM] = x

    for col_tile in range(IN_COLS // (heads_per_tile * HEAD_DIM)):
        pl.when(n == col_tile)(functools.partial(tile, col_tile))


def _inproj(h, w_in, rc, rs1, rs2, qg, kg):
    n_tok = h.shape[0]
    tm, tn = PROJ_TM, PROJ_TN
    tab = pl.BlockSpec((tm, HEAD_DIM), lambda n, m: (m, 0))
    gain = pl.BlockSpec((1, HEAD_DIM), lambda n, m: (0, 0))
    return pl.pallas_call(
        functools.partial(_inproj_kernel, heads_per_tile=tn // HEAD_DIM),
        grid=(IN_COLS // tn, n_tok // tm),
        in_specs=[pl.BlockSpec((tm, D_MODEL), lambda n, m: (m, 0)),
                  pl.BlockSpec((D_MODEL, tn), lambda n, m: (0, n), pipeline_mode=pl.Buffered(1)),
                  tab, tab, tab, gain, gain],
        out_specs=pl.BlockSpec((tm, tn), lambda n, m: (m, n)),
        out_shape=jax.ShapeDtypeStruct((n_tok, IN_COLS), F32),
        scratch_shapes=[pltpu.VMEM((D_MODEL, tn), BF16)],
        compiler_params=_cparams(("arbitrary", "arbitrary")),
        name="in_proj",
    )(h, w_in, rc, rs1, rs2, qg, kg)


def _ssm_prep_kernel(are_ref, aim_ref, ldt_ref, arer_ref, aimr_ref, ldtr_ref, bre_ref, bim_ref,
                     pre_ref, pim_ref, bbre_ref, bbim_ref):
    def discretise(a_re, a_im, log_dt):
        dt = jnp.exp(log_dt)
        mag = jnp.exp(a_re * dt)
        return mag * jnp.cos(a_im * dt), mag * jnp.sin(a_im * dt)

    ab_re, ab_im = discretise(are_ref[...], aim_ref[...], ldt_ref[...])
    p_re, p_im = ab_re, ab_im
    for i in range(SCAN_L):
        pre_ref[i:i + 1, :] = p_re
        pim_ref[i:i + 1, :] = p_im
        p_re, p_im = p_re * ab_re - p_im * ab_im, p_re * ab_im + p_im * ab_re

    a_re, a_im = arer_ref[...], aimr_ref[...]
    r_re, r_im = discretise(a_re, a_im, ldtr_ref[...])
    nr, ni = r_re - 1.0, r_im
    den = a_re * a_re + a_im * a_im
    z_re = (nr * a_re + ni * a_im) / den
    z_im = (ni * a_re - nr * a_im) / den
    b_re, b_im = bre_ref[...], bim_ref[...]
    bbre_ref[...] = z_re * b_re - z_im * b_im
    bbim_ref[...] = z_re * b_im + z_im * b_re


def _ssm_prep(a_re, a_im, log_dt, b_re, b_im):
    g, p, n = b_re.shape
    flat = lambda x: x.reshape(1, g * p)
    rep = lambda x: jnp.repeat(x, n, axis=1)
    ldt_gp = jnp.broadcast_to(log_dt[:, None], (g, p))
    ldt_rep = jnp.broadcast_to(log_dt[:, None], (g, p * n))
    out_shape = [jax.ShapeDtypeStruct((SCAN_L, g * p), F32)] * 2 + [jax.ShapeDtypeStruct((g, p * n), F32)] * 2
    return pl.pallas_call(_ssm_prep_kernel, out_shape=out_shape, name="ssm_prep")(
        flat(a_re), flat(a_im), flat(ldt_gp), rep(a_re), rep(a_im), ldt_rep,
        b_re.reshape(g, p * n), b_im.reshape(g, p * n))


def _ssm_block_matrices(bb_re, bb_im, c_re, c_im):
    g, p, n = SSM_GROUPS, SSM_STATE, SSM_GROUP
    nb = g // SSM_BLK
    eye = jnp.eye(SSM_BLK, dtype=F32)

    def in_mat(bb):
        x = bb.reshape(nb, SSM_BLK, p, n)
        return jnp.einsum('bgpm,gh->bgmhp', x, eye).reshape(nb, SSM_BLK * n, SSM_BLK * p)

    def out_mat(c):
        x = c.reshape(nb, SSM_BLK, n, p)
        return jnp.einsum('bgnp,gh->bgphn', x, eye).reshape(nb, SSM_BLK * p, SSM_BLK * n)

    b_mat = jnp.concatenate([in_mat(bb_re), in_mat(bb_im)], axis=2).astype(BF16)
    c_mat = jnp.concatenate([out_mat(c_re), -out_mat(c_im)], axis=1).astype(BF16)
    return b_mat, c_mat


def _cmul_add(a_re, a_im, s_re, s_im, b_re, b_im):
    return a_re * s_re - a_im * s_im + b_re, a_re * s_im + a_im * s_re + b_im


def _s5_prompt_kernel(*refs):
    par = SSM_PAR
    u_refs = refs[:par]
    (d_ref, pre_ref, pim_ref, bm_ref, cm_ref, y_ref, fre_ref, fim_ref,
     up_scr, bu_scr, lhs_scr, in_re_scr, in_im_scr, car_re, car_im, yn_scr) = refs[par:]
    nc, ln, w = SCAN_NC, SCAN_L, SSM_BLK_STATE

    @pl.when(pl.program_id(1) == 0)
    def _():
        car_re[...] = jnp.zeros_like(car_re)
        car_im[...] = jnp.zeros_like(car_im)

    for b in range(par):
        lanes = slice(b * w, (b + 1) * w)
        rows = lambda i: slice(i * nc, (i + 1) * nc)
        for i in range(ln):
            up_scr[b, rows(i), :] = u_refs[b][pl.ds(i, nc, stride=ln), :]
        up = up_scr[b]
        bu_scr[b] = jnp.dot(up.astype(BF16), bm_ref[b], preferred_element_type=F32)

        a_re = jnp.broadcast_to(pre_ref[0:1, lanes], (nc, w))
        a_im = jnp.broadcast_to(pim_ref[0:1, lanes], (nc, w))
        s_re = s_im = jnp.zeros((nc, w), F32)
        for i in range(ln):
            s_re, s_im = _cmul_add(a_re, a_im, s_re, s_im, bu_scr[b, rows(i), 0:w], bu_scr[b, rows(i), w:2 * w])
            bu_scr[b, rows(i), 0:w] = s_re
            bu_scr[b, rows(i), w:2 * w] = s_im

        al_re, al_im = pre_ref[ln - 1:ln, lanes], pim_ref[ln - 1:ln, lanes]
        c_re, c_im = car_re[b], car_im[b]
        for c in range(nc):
            in_re_scr[b, c:c + 1, :] = c_re
            in_im_scr[b, c:c + 1, :] = c_im
            c_re, c_im = _cmul_add(al_re, al_im, c_re, c_im, s_re[c:c + 1, :], s_im[c:c + 1, :])
        car_re[b] = c_re
        car_im[b] = c_im
        fre_ref[:, lanes] = c_re
        fim_ref[:, lanes] = c_im
        in_re, in_im = in_re_scr[b], in_im_scr[b]

        for i in range(ln):
            p_re = jnp.broadcast_to(pre_ref[i:i + 1, lanes], (nc, w))
            p_im = jnp.broadcast_to(pim_ref[i:i + 1, lanes], (nc, w))
            f_re, f_im = _cmul_add(p_re, p_im, in_re, in_im, bu_scr[b, rows(i), 0:w], bu_scr[b, rows(i), w:2 * w])
            lhs_scr[b, rows(i), 0:w] = f_re.astype(BF16)
            lhs_scr[b, rows(i), w:2 * w] = f_im.astype(BF16)

        y = (jnp.dot(lhs_scr[b], cm_ref[b], preferred_element_type=F32)
             + d_ref[:, b * LANES:(b + 1) * LANES] * up)
        for i in range(ln):
            yn_scr[b, pl.ds(i, nc, stride=ln), :] = y[rows(i), :]
        y_ref[:, b * LANES:(b + 1) * LANES] = _gelu_tanh(yn_scr[b]).astype(y_ref.dtype)


def _s5_prompt(proj, n_prompt, d_skip, pw_re, pw_im, b_mat, c_mat):
    par = SSM_PAR
    nb = SSM_GROUPS // SSM_BLK // par
    tt, w = SCAN_TT, SSM_BLK_STATE
    u_spec = lambda b: pl.BlockSpec((tt, LANES), lambda j, i: (i, par * j + b))
    return pl.pallas_call(
        _s5_prompt_kernel,
        grid=(nb, n_prompt // tt),
        in_specs=[u_spec(b) for b in range(par)] + [
            pl.BlockSpec((1, par * LANES), lambda j, i: (0, j)),
            pl.BlockSpec((SCAN_L, par * w), lambda j, i: (0, j)),
            pl.BlockSpec((SCAN_L, par * w), lambda j, i: (0, j)),
            pl.BlockSpec((par, LANES, 2 * w), lambda j, i: (j, 0, 0)),
            pl.BlockSpec((par, 2 * w, LANES), lambda j, i: (j, 0, 0))],
        out_specs=[pl.BlockSpec((tt, par * LANES), lambda j, i: (i, j)),
                   pl.BlockSpec((1, par * w), lambda j, i: (0, j)),
                   pl.BlockSpec((1, par * w), lambda j, i: (0, j))],
        out_shape=[jax.ShapeDtypeStruct((n_prompt, SSM_WIDTH), BF16),
                   jax.ShapeDtypeStruct((1, SSM_FLAT), F32),
                   jax.ShapeDtypeStruct((1, SSM_FLAT), F32)],
        scratch_shapes=[pltpu.VMEM((par, tt, LANES), F32), pltpu.VMEM((par, tt, 2 * w), F32),
                        pltpu.VMEM((par, tt, 2 * w), BF16),
                        pltpu.VMEM((par, SCAN_NC, w), F32), pltpu.VMEM((par, SCAN_NC, w), F32),
                        pltpu.VMEM((par, 1, w), F32), pltpu.VMEM((par, 1, w), F32),
                        pltpu.VMEM((par, tt, LANES), F32)],
        compiler_params=_cparams(("arbitrary", "arbitrary")),
        name="s5_prompt",
    )(*([proj] * par), d_skip, pw_re, pw_im, b_mat, c_mat)


def _s5_sample_kernel(u_ref, d_ref, pre_ref, pim_ref, bm_ref, cm_ref, s0re_ref, s0im_ref,
                      y_ref, fre_ref, fim_ref, up_scr, bu_scr, lhs_scr, yn_scr, *, n_seq, n_step):
    w = SSM_BLK_STATE
    rb = 16
    for s in range(n_step):
        up_scr[s * n_seq:(s + 1) * n_seq, :] = u_ref[pl.ds(s, n_seq, stride=n_step), :]
    up = up_scr[...]
    bu_scr[...] = jnp.dot(up.astype(BF16), bm_ref[0], preferred_element_type=F32)
    a_re = jnp.broadcast_to(pre_ref[0:1, :], (rb, w))
    a_im = jnp.broadcast_to(pim_ref[0:1, :], (rb, w))

    def seq_block(b, _):
        r0 = pl.multiple_of(b * rb, rb)
        s_re, s_im = s0re_ref[pl.ds(r0, rb), :], s0im_ref[pl.ds(r0, rb), :]
        for s in range(n_step):
            rows = pl.ds(pl.multiple_of(s * n_seq + r0, rb), rb)
            s_re, s_im = _cmul_add(a_re, a_im, s_re, s_im, bu_scr[rows, 0:w], bu_scr[rows, w:2 * w])
            lhs_scr[rows, 0:w] = s_re.astype(BF16)
            lhs_scr[rows, w:2 * w] = s_im.astype(BF16)
        fre_ref[pl.ds(r0, rb), :] = s_re
        fim_ref[pl.ds(r0, rb), :] = s_im
        return 0

    lax.fori_loop(0, n_seq // rb, seq_block, 0)
    y = jnp.dot(lhs_scr[...], cm_ref[0], preferred_element_type=F32) + d_ref[...] * up
    for s in range(n_step):
        yn_scr[pl.ds(s, n_seq, stride=n_step), :] = y[s * n_seq:(s + 1) * n_seq, :]
    y_ref[...] = _gelu_tanh(yn_scr[...]).astype(y_ref.dtype)


def _s5_sample(proj, n_prompt, n_seq, n_step, d_skip, pw_re, pw_im, b_mat, c_mat, s0_re, s0_im):
    nb = SSM_GROUPS // SSM_BLK
    rows, w = n_seq * n_step, SSM_BLK_STATE
    rblk = n_prompt // rows
    return pl.pallas_call(
        functools.partial(_s5_sample_kernel, n_seq=n_seq, n_step=n_step),
        grid=(nb,),
        in_specs=[pl.BlockSpec((rows, LANES), lambda j: (rblk, j)),
                  pl.BlockSpec((1, LANES), lambda j: (0, j)),
                  pl.BlockSpec((SCAN_L, w), lambda j: (0, j)),
                  pl.BlockSpec((SCAN_L, w), lambda j: (0, j)),
                  pl.BlockSpec((1, LANES, 2 * w), lambda j: (j, 0, 0)),
                  pl.BlockSpec((1, 2 * w, LANES), lambda j: (j, 0, 0)),
                  pl.BlockSpec((n_seq, w), lambda j: (0, j)),
                  pl.BlockSpec((n_seq, w), lambda j: (0, j))],
        out_specs=[pl.BlockSpec((rows, LANES), lambda j: (0, j)),
                   pl.BlockSpec((n_seq, w), lambda j: (0, j)),
                   pl.BlockSpec((n_seq, w), lambda j: (0, j))],
        out_shape=[jax.ShapeDtypeStruct((rows, SSM_WIDTH), BF16),
                   jax.ShapeDtypeStruct((n_seq, SSM_FLAT), F32),
                   jax.ShapeDtypeStruct((n_seq, SSM_FLAT), F32)],
        scratch_shapes=[pltpu.VMEM((rows, LANES), F32), pltpu.VMEM((rows, 2 * w), F32),
                        pltpu.VMEM((rows, 2 * w), BF16), pltpu.VMEM((rows, LANES), F32)],
        compiler_params=_cparams(("arbitrary",)),
        name="s5_sample",
    )(proj, d_skip, pw_re, pw_im, b_mat, c_mat, s0_re, s0_im)


def _attn_prompt_kernel(*refs):
    ins, o_ref, scr = refs[:15], refs[15], refs[16:]
    sb = pl.program_id(0)
    scale = HEAD_DIM ** -0.5
    qi = lax.broadcasted_iota(jnp.int32, (BAND, 2 * BAND), 0)
    kj = lax.broadcasted_iota(jnp.int32, (BAND, 2 * BAND), 1)
    dist = qi + BAND - kj
    band_ok = (dist >= 0) & (dist <= BAND)

    for g, (_, dil) in enumerate(DILATION_PATTERNS):
        q_ref, k_ref, v_ref, kp_ref, vp_ref = ins[5 * g:5 * g + 5]
        kbuf, vbuf, o_scr, m_scr, l_scr = scr[5 * g:5 * g + 5]
        pre = BAND * dil
        kbuf[0:pre, :] = kp_ref[...]
        kbuf[pre:pre + ATT_SB, :] = k_ref[...]
        vbuf[0:pre, :] = vp_ref[...]
        vbuf[pre:pre + ATT_SB, :] = v_ref[...]
        nblk = ATT_SB // pre

        def block(idx, _, dil=dil, pre=pre, nblk=nblk, q_ref=q_ref, kbuf=kbuf, vbuf=vbuf,
                  o_scr=o_scr, m_scr=m_scr, l_scr=l_scr):
            r = idx // nblk
            b = idx - r * nblk
            row0 = r + b * pre
            if dil == 1:
                q_rows = pl.ds(pl.multiple_of(row0, BAND), BAND)
                kv_rows = pl.ds(pl.multiple_of(row0, BAND), 2 * BAND)
            else:
                q_rows = pl.ds(row0, BAND, stride=dil)
                kv_rows = pl.ds(row0, 2 * BAND, stride=dil)
            q = (q_ref[q_rows, :] * scale).astype(BF16)
            kw = kbuf[kv_rows, :].astype(BF16)
            vw = vbuf[kv_rows, :].astype(BF16)
            s = lax.dot_general(q, kw, (((1,), (1,)), ((), ())), preferred_element_type=F32)
            s = jnp.where(band_ok & ((kj >= BAND) | (sb > 0) | (b > 0)), s, NEG)
            m = jnp.max(s, axis=-1, keepdims=True)
            p = jnp.exp(s - m)
            l = jnp.sum(p, axis=-1, keepdims=True)
            o = jnp.dot(p.astype(BF16), vw, preferred_element_type=F32)
            o_scr[q_rows, :] = o
            m_scr[q_rows, :] = jnp.broadcast_to(m, (BAND, HEAD_DIM))
            l_scr[q_rows, :] = jnp.broadcast_to(l, (BAND, HEAD_DIM))
            return 0

        lax.fori_loop(0, ATT_SB // BAND, block, 0, unroll=ATT_UNROLL)

    ms = [scr[5 * g + 3][...] for g in range(N_PATTERNS)]
    mx = jnp.maximum(jnp.maximum(ms[0], ms[1]), ms[2])
    num = jnp.zeros((ATT_SB, HEAD_DIM), F32)
    den = jnp.zeros((ATT_SB, HEAD_DIM), F32)
    for g in range(N_PATTERNS):
        wgt = jnp.exp(ms[g] - mx)
        num = num + wgt * scr[5 * g + 2][...]
        den = den + wgt * scr[5 * g + 4][...]
    o_ref[...] = num / den


def _attn_prompt(proj, n_prompt):
    hcol = lambda off, g, j: (off + g * GROUP_WIDTH) // HEAD_DIM + j
    in_specs, scratch = [], []
    for g, (_, dil) in enumerate(DILATION_PATTERNS):
        pre = BAND * dil
        per = ATT_SB // pre
        cur = lambda off, g=g: pl.BlockSpec((ATT_SB, HEAD_DIM), lambda sb, j: (sb, hcol(off, g, j)))
        prev = lambda off, g=g, per=per, pre=pre: pl.BlockSpec(
            (pre, HEAD_DIM), lambda sb, j: (jnp.maximum(sb * per - 1, 0), hcol(off, g, j)))
        in_specs += [cur(OFF_Q), cur(OFF_K), cur(OFF_V), prev(OFF_K), prev(OFF_V)]
        scratch += [pltpu.VMEM((pre + ATT_SB, HEAD_DIM), F32), pltpu.VMEM((pre + ATT_SB, HEAD_DIM), F32),
                    pltpu.VMEM((ATT_SB, HEAD_DIM), F32), pltpu.VMEM((ATT_SB, HEAD_DIM), F32),
                    pltpu.VMEM((ATT_SB, HEAD_DIM), F32)]
    return pl.pallas_call(
        _attn_prompt_kernel,
        grid=(n_prompt // ATT_SB, HEADS_PER_GROUP),
        in_specs=in_specs,
        out_specs=pl.BlockSpec((ATT_SB, HEAD_DIM), lambda sb, j: (sb, j)),
        out_shape=jax.ShapeDtypeStruct((n_prompt, GROUP_WIDTH), F32),
        scratch_shapes=scratch,
        compiler_params=_cparams(("arbitrary", "arbitrary")),
        name="attn_prompt",
    )(*([proj] * 15))


SEQ_PER_STEP = 2
KV_PLANES = 2 * HEADS_PER_GROUP


def _sample_bias(n_step):
    rows = HEADS_PER_GROUP * n_step
    step = np.arange(rows) % n_step
    cache_bias, new_bias = [], []
    for (window, dil) in DILATION_PATTERNS:
        wb = min(window, PAST_LEN)
        band = window // dil
        if dil > n_step:
            res, i = np.meshgrid(np.arange(n_step), np.arange(wb // dil), indexing='ij')
            c = (i * dil + res).reshape(-1)
        else:
            c = np.arange(wb)
        delta = wb + step[:, None] - c[None, :]
        ok = (delta >= 0) & (delta % dil == 0) & (delta // dil <= band)
        cache_bias.append(np.where(ok, 0.0, NEG).astype(np.float32))
        nb = np.full((SEQ_PER_STEP, rows, LANES), NEG, np.float32)
        for a in range(SEQ_PER_STEP):
            for sp in range(n_step):
                dl = step - sp
                okn = (dl >= 0) & (dl % dil == 0) & (dl // dil <= band)
                nb[a, :, a * n_step + sp] = np.where(okn, 0.0, NEG)
        new_bias.append(nb)
    return cache_bias, new_bias


def _attn_sample_kernel(*refs, n_step, cache_pieces):
    (q0, k0, v0, q1, k1, v1, q2, k2, v2) = refs[:9]
    qs, ks, vs = (q0, q1, q2), (k0, k1, k2), (v0, v1, v2)
    refs = list(refs[9:])
    caches = [[refs.pop(0) for _ in range(n)] for n in cache_pieces]
    (cb0, cb1, cb2, nb0, nb1, nb2, o_ref) = refs
    cbias, nbias = (cb0, cb1, cb2), (nb0, nb1, nb2)
    rows = HEADS_PER_GROUP * n_step
    gw = GROUP_WIDTH
    scale = HEAD_DIM ** -0.5
    row_head = lax.broadcasted_iota(jnp.int32, (rows, gw), 0) // n_step
    lane_head = lax.broadcasted_iota(jnp.int32, (rows, gw), 1) // HEAD_DIM
    own_head = row_head == lane_head
    nt = (((1,), (1,)), ((), ()))
    pad = jnp.zeros((LANES - SEQ_PER_STEP * n_step, gw), F32)

    def planes(load):
        k = jnp.concatenate([load(h) for h in range(HEADS_PER_GROUP)], axis=1)
        v = jnp.concatenate([load(HEADS_PER_GROUP + h) for h in range(HEADS_PER_GROUP)], axis=1)
        return k.astype(BF16), v.astype(BF16)

    for a in range(SEQ_PER_STEP):
        pieces = []
        for g, (window, dil) in enumerate(DILATION_PATTERNS):
            q = qs[g][a * n_step:(a + 1) * n_step, :] * scale
            qbd = jnp.where(own_head, jnp.concatenate([q] * HEADS_PER_GROUP, axis=0), 0.0).astype(BF16)
            kvs = []
            for cache in caches[g]:
                if len(cache.shape) == 3:
                    cache = cache.reshape(cache.shape[0] * KV_PLANES, HEAD_DIM)
                per_seq = cache.shape[0] // SEQ_PER_STEP
                kvs.append(planes(lambda p, cache=cache, per_seq=per_seq: cache[
                    pl.ds(a * per_seq + p, per_seq // KV_PLANES, stride=KV_PLANES), :]))
            sc = jnp.concatenate([lax.dot_general(qbd, k, nt, preferred_element_type=F32) for k, _ in kvs], axis=1)
            pieces.append((sc + cbias[g][...], [v for _, v in kvs]))
            k_new = jnp.concatenate([ks[g][...], pad], axis=0).astype(BF16)
            v_new = jnp.concatenate([vs[g][...], pad], axis=0).astype(BF16)
            sn = lax.dot_general(qbd, k_new, nt, preferred_element_type=F32) + nbias[g][a]
            pieces.append((sn, [v_new]))
        m = functools.reduce(jnp.maximum, [jnp.max(s, axis=-1, keepdims=True) for s, _ in pieces])
        l = jnp.zeros((rows, 1), F32)
        acc = jnp.zeros((rows, gw), F32)
        for s, vals in pieces:
            p = jnp.exp(s - m)
            l = l + jnp.sum(p, axis=-1, keepdims=True)
            pb = p.astype(BF16)
            nk = pb.shape[1] // len(vals)
            for r, v in enumerate(vals):
                acc = acc + jnp.dot(pb[:, r * nk:(r + 1) * nk], v, preferred_element_type=F32)
        acc = jnp.where(own_head, acc, 0.0)
        o16 = functools.reduce(lambda x, y: x + y,
                               [acc[:, h * HEAD_DIM:(h + 1) * HEAD_DIM] for h in range(HEADS_PER_GROUP)]) / l
        for h in range(HEADS_PER_GROUP):
            o_ref[a * n_step:(a + 1) * n_step, h * HEAD_DIM:(h + 1) * HEAD_DIM] = o16[h * n_step:(h + 1) * n_step, :]


def _attn_sample(proj, n_prompt, n_seq, n_step, caches):
    rows = SEQ_PER_STEP * n_step
    rblk = n_prompt // rows
    cache_bias, new_bias = _sample_bias(n_step)
    tok = lambda off, g: pl.BlockSpec((rows, GROUP_WIDTH), lambda i: (rblk + i, (off + g * GROUP_WIDTH) // GROUP_WIDTH))
    in_specs, args, cache_pieces = [], [], []
    for g in range(N_PATTERNS):
        in_specs += [tok(OFF_Q, g), tok(OFF_K, g), tok(OFF_V, g)]
        args += [proj, proj, proj]
    for g, (window, dil) in enumerate(DILATION_PATTERNS):
        c = caches[g]
        wb = c.shape[1]
        if dil > n_step:
            assert wb % dil == 0
            c = c.reshape(n_seq * (wb // dil), dil * KV_PLANES, HEAD_DIM)
            for r in range(n_step):
                in_specs.append(pl.BlockSpec((SEQ_PER_STEP * (wb // dil), KV_PLANES, HEAD_DIM),
                                             lambda i, r=r: (i, r, 0)))
                args.append(c)
            cache_pieces.append(n_step)
        else:
            c = c.reshape(n_seq * wb * KV_PLANES, HEAD_DIM)
            in_specs.append(pl.BlockSpec((SEQ_PER_STEP * wb * KV_PLANES, HEAD_DIM), lambda i: (i, 0)))
            args.append(c)
            cache_pieces.append(1)
    for b in cache_bias:
        in_specs.append(pl.BlockSpec(b.shape, lambda i: (0, 0)))
        args.append(jnp.asarray(b))
    for b in new_bias:
        in_specs.append(pl.BlockSpec(b.shape, lambda i: (0, 0, 0)))
        args.append(jnp.asarray(b))
    return pl.pallas_call(
        functools.partial(_attn_sample_kernel, n_step=n_step, cache_pieces=tuple(cache_pieces)),
        grid=(n_seq // SEQ_PER_STEP,),
        in_specs=in_specs,
        out_specs=pl.BlockSpec((rows, GROUP_WIDTH), lambda i: (i, 0)),
        out_shape=jax.ShapeDtypeStruct((n_seq * n_step, GROUP_WIDTH), F32),
        compiler_params=_cparams(("arbitrary",)),
        name="attn_sample",
    )(*args)


def _mix_kernel(yp_ref, ys_ref, op_ref, os_ref, wa_ref, wb_ref, wbr_ref, ga_ref, gb_ref, o_ref,
                wa_bf, wb_bf, wbr_bf, *, n_prompt_tiles):
    i = pl.program_id(1)

    @pl.when(i == 0)
    def _():
        wa_bf[...] = wa_ref[...].astype(BF16)
        wb_bf[...] = wb_ref[...].astype(BF16)
        wbr_bf[...] = wbr_ref[...].astype(BF16)

    is_s = i >= n_prompt_tiles
    for rows in _row_halves(o_ref.shape[0]):
        y = jnp.where(is_s, ys_ref[rows, :], yp_ref[rows, :])
        o = jnp.where(is_s, os_ref[rows, :], op_ref[rows, :]).astype(BF16)
        glu_a = jnp.dot(y, wa_bf[...], preferred_element_type=F32)
        glu_b = jnp.dot(y, wb_bf[...], preferred_element_type=F32)
        branch_a = glu_a * _sigmoid(glu_b)
        branch_b = jnp.dot(o, wbr_bf[...], preferred_element_type=F32)
        o_ref[rows, :] = (_sigmoid(ga_ref[rows, :]) * branch_a
                          + _sigmoid(gb_ref[rows, :]) * branch_b).astype(o_ref.dtype)


def _mix(y_p, y_s, o_p, o_s, w_glu, w_attn_br, proj):
    tp, ts = y_p.shape[0], y_s.shape[0]
    tm, tn = TOK_TILE, 512
    assert OFF_G % tn == 0
    npt, nst = tp // tm, ts // tm
    ncol = D_MODEL // tn
    prow = lambda n, i: (jnp.minimum(i, npt - 1), 0)
    srow = lambda n, i: (jnp.maximum(i - npt, 0), 0)
    return pl.pallas_call(
        functools.partial(_mix_kernel, n_prompt_tiles=npt),
        grid=(ncol, npt + nst),
        in_specs=[pl.BlockSpec((tm, SSM_WIDTH), prow), pl.BlockSpec((tm, SSM_WIDTH), srow),
                  pl.BlockSpec((tm, GROUP_WIDTH), prow), pl.BlockSpec((tm, GROUP_WIDTH), srow),
                  pl.BlockSpec((SSM_WIDTH, tn), lambda n, i: (0, n)),
                  pl.BlockSpec((SSM_WIDTH, tn), lambda n, i: (0, ncol + n)),
                  pl.BlockSpec((GROUP_WIDTH, tn), lambda n, i: (0, n)),
                  pl.BlockSpec((tm, tn), lambda n, i: (i, OFF_G // tn + n)),
                  pl.BlockSpec((tm, tn), lambda n, i: (i, OFF_G // tn + ncol + n))],
        out_specs=pl.BlockSpec((tm, tn), lambda n, i: (i, n)),
        out_shape=jax.ShapeDtypeStruct((tp + ts, D_MODEL), BF16),
        scratch_shapes=[pltpu.VMEM((SSM_WIDTH, tn), BF16), pltpu.VMEM((SSM_WIDTH, tn), BF16),
                        pltpu.VMEM((GROUP_WIDTH, tn), BF16)],
        compiler_params=_cparams(("arbitrary", "arbitrary")),
        name="glu_mix",
    )(y_p, y_s, o_p, o_s, w_glu, w_glu, w_attn_br, proj, proj)


def _route(logits):
    lane = lax.broadcasted_iota(jnp.int32, logits.shape, 1).astype(F32)
    big = 1000.0
    first = lambda cond: jnp.min(jnp.where(cond, lane, big), axis=-1, keepdims=True)
    is_g = lane < N_EXPERT_GROUPS
    lg = jnp.where(is_g, logits, NEG)
    mg = jnp.max(lg, axis=-1, keepdims=True)
    g_sel = first(lg == mg)
    p_group = 1.0 / jnp.sum(jnp.where(is_g, jnp.exp(lg - mg), 0.0), axis=-1, keepdims=True)
    e_lo = N_EXPERT_GROUPS + EXPERTS_PER_GROUP * g_sel
    le = jnp.where((lane >= e_lo) & (lane < e_lo + EXPERTS_PER_GROUP), logits, NEG)
    v1 = jnp.max(le, axis=-1, keepdims=True)
    i1 = first(le == v1)
    le2 = jnp.where(lane == i1, NEG, le)
    v2 = jnp.max(le2, axis=-1, keepdims=True)
    i2 = first(le2 == v2)
    e2 = jnp.exp(v2 - v1)
    w1 = p_group / (1.0 + e2)
    w2 = p_group * e2 / (1.0 + e2)
    pick = lambda k, val: jnp.where(lane == k, val, 0.0)
    return (pick(ROUTE_E1, i1 - N_EXPERT_GROUPS) + pick(ROUTE_E2, i2 - N_EXPERT_GROUPS)
            + pick(ROUTE_W1, w1) + pick(ROUTE_W2, w2))


def _outproj_kernel(mix_ref, w_ref, xp_ref, xs_ref, g_ref, wr_ref, br_ref,
                    gtp_ref, scp_ref, shp_ref, gts_ref, scs_ref, shs_ref,
                    x1_ref, h2_ref, route_ref, *, n_prompt_tiles):
    is_s = pl.program_id(0) >= n_prompt_tiles

    def split(v):
        high = v.astype(BF16)
        return high, (v - high.astype(F32)).astype(BF16)

    x = jnp.where(is_s, xs_ref[...], xp_ref[...])
    gt = jnp.where(is_s, gts_ref[...], gtp_ref[0:1, :])
    sc = jnp.where(is_s, scs_ref[...], scp_ref[0:1, :])
    sh = jnp.where(is_s, shs_ref[...], shp_ref[0:1, :])
    x1 = x + gt * jnp.dot(mix_ref[...], w_ref[...], preferred_element_type=F32)
    x1_ref[...] = x1
    ms = jnp.mean(x1 * x1, axis=-1, keepdims=True)
    h2 = (x1 * lax.rsqrt(ms + EPS) * g_ref[...]) * (1.0 + sc) + sh
    h2_ref[...] = h2.astype(h2_ref.dtype)
    r = jnp.dot(jnp.concatenate(split(h2), axis=0), jnp.concatenate(split(wr_ref[...]), axis=1),
                preferred_element_type=F32)
    n = h2.shape[0]
    logits = (r[:n, :LANES] + r[:n, LANES:]) + (r[n:, :LANES] + r[n:, LANES:]) + br_ref[...]
    route_ref[...] = _route(logits)


def _outproj(mixed, w_out_bf, x_p, x_s, g2, w_router, b_router, mod):
    tp, ts = x_p.shape[0], x_s.shape[0]
    tm = SMALL_TOK_TILE
    npt, nst = tp // tm, ts // tm
    prow = lambda i: (jnp.minimum(i, npt - 1), 0)
    srow = lambda i: (jnp.maximum(i - npt, 0), 0)
    const = lambda i: (0, 0)
    vec = pl.BlockSpec((1, D_MODEL), const)
    gt_p, gt_s = _mod_specs(MOD_GT1, tm, ts, npt)
    sc_p, sc_s = _mod_specs(MOD_SC2, tm, ts, npt)
    sh_p, sh_s = _mod_specs(MOD_SH2, tm, ts, npt)
    full = lambda i: (i, 0)
    return pl.pallas_call(
        functools.partial(_outproj_kernel, n_prompt_tiles=npt),
        grid=(npt + nst,),
        in_specs=[pl.BlockSpec((tm, D_MODEL), full), pl.BlockSpec((D_MODEL, D_MODEL), const),
                  pl.BlockSpec((tm, D_MODEL), prow), pl.BlockSpec((tm, D_MODEL), srow),
                  vec, pl.BlockSpec((D_MODEL, LANES), const), pl.BlockSpec((1, LANES), const),
                  gt_p, sc_p, sh_p, gt_s, sc_s, sh_s],
        out_specs=[pl.BlockSpec((tm, D_MODEL), full), pl.BlockSpec((tm, D_MODEL), full),
                   pl.BlockSpec((tm, LANES), full)],
        out_shape=[jax.ShapeDtypeStruct((tp + ts, D_MODEL), F32),
                   jax.ShapeDtypeStruct((tp + ts, D_MODEL), F32),
                   jax.ShapeDtypeStruct((tp + ts, LANES), F32)],
        compiler_params=_cparams(("arbitrary",)),
        name="out_proj_norm2_router",
    )(mixed, w_out_bf, x_p, x_s, g2, w_router, b_router, mod, mod, mod, mod, mod, mod)


def _dispatch_plan(route, tm):
    e = route[:, ROUTE_E1:ROUTE_E2 + 1].astype(jnp.int32).reshape(-1)
    n_pairs = e.shape[0]
    onehot = (e[:, None] == jnp.arange(N_EXPERTS, dtype=jnp.int32)[None, :]).astype(jnp.int32)
    csum = jnp.cumsum(onehot, axis=0)
    rank = jnp.sum(onehot * csum, axis=1) - 1
    tiles_per_expert = (csum[-1] + tm - 1) // tm
    tile_end = jnp.cumsum(tiles_per_expert)
    tile_start = tile_end - tiles_per_expert
    dest = (tile_start[e] * tm + rank).astype(jnp.int32)
    max_tiles = n_pairs // tm + N_EXPERTS
    k = jnp.arange(max_tiles, dtype=jnp.int32)
    tile_expert = jnp.minimum(jnp.sum((k[:, None] >= tile_end[None, :]).astype(jnp.int32), axis=1), N_EXPERTS - 1)
    n_used = tile_end[-1].astype(jnp.int32)
    last_expert = jnp.take(tile_expert, n_used - 1)
    tile_expert = jnp.where(k < n_used, tile_expert, last_expert).astype(jnp.int32)
    pad_end = (jnp.concatenate([tile_end, tile_end[-1:]]) * tm).astype(jnp.int32)
    pad_len = (tiles_per_expert * tm - csum[-1]).astype(jnp.int32)
    return dest, pad_end, pad_len, tile_expert, n_used.reshape(1), max_tiles


def _start_pair_copies(dest_ref, tile, rows, make):
    def body(r, _):
        for k in range(2):
            make(r, k, dest_ref[(tile * rows + r) * 2 + k]).start()
        return 0
    lax.fori_loop(0, rows, body, 0, unroll=8)


def _pair_copies(dest_ref, tile, rows, make, make_all):
    _start_pair_copies(dest_ref, tile, rows, make)
    for k in range(2):
        make_all(k).wait()


def _dispatch_kernel(dest_ref, pad_end_ref, pad_len_ref, h_ref, xs_ref, zero_buf, sem, pad_sem):
    rows = h_ref.shape[0]

    @pl.when(pl.program_id(0) == 0)
    def _():
        zero_buf[...] = jnp.zeros_like(zero_buf)

        def pad_copies(fn):
            for e in range(N_EXPERTS):
                end, left = pad_end_ref[e], pad_len_ref[e]
                size = zero_buf.shape[0]
                while size >= SUBLANES:
                    take = (left & size) != 0
                    end = end - jnp.where(take, size, 0)

                    @pl.when(take)
                    def _(start=end, size=size):
                        fn(pltpu.make_async_copy(zero_buf.at[pl.ds(0, size)],
                                                 xs_ref.at[pl.ds(pl.multiple_of(start, size), size)], pad_sem))

                    size //= 2
                for r in range(1, SUBLANES):
                    @pl.when((left & (SUBLANES - 1)) >= r)
                    def _(row=end - r):
                        fn(pltpu.make_async_copy(zero_buf.at[pl.ds(0, 1)], xs_ref.at[pl.ds(row, 1)], pad_sem))
            size = zero_buf.shape[0]
            tail = pad_end_ref[N_EXPERTS]
            for t in range(N_EXPERTS * MOE_TM // size):
                @pl.when(tail + t * size < xs_ref.shape[0])
                def _(start=tail + t * size):
                    fn(pltpu.make_async_copy(zero_buf, xs_ref.at[pl.ds(pl.multiple_of(start, size), size)], pad_sem))

        pad_copies(lambda c: c.start())
        pad_copies(lambda c: c.wait())

    make = lambda r, k, d: pltpu.make_async_copy(h_ref.at[pl.ds(r, 1)], xs_ref.at[pl.ds(d, 1)], sem)
    make_all = lambda k: pltpu.make_async_copy(h_ref, xs_ref.at[pl.ds(0, rows)], sem)
    _pair_copies(dest_ref, pl.program_id(0), rows, make, make_all)


def _dispatch(dest, pad_end, pad_len, h2, n_slots):
    n_tok = h2.shape[0]
    tm = SMALL_TOK_TILE
    return pl.pallas_call(
        _dispatch_kernel,
        grid_spec=pltpu.PrefetchScalarGridSpec(
            num_scalar_prefetch=3, grid=(n_tok // tm,),
            in_specs=[pl.BlockSpec((tm, D_MODEL), lambda i, d, ps, pn: (i, 0))],
            out_specs=pl.BlockSpec(memory_space=pl.ANY),
            scratch_shapes=[pltpu.VMEM((MOE_TM // 2, D_MODEL), F32),
                            pltpu.SemaphoreType.DMA(()), pltpu.SemaphoreType.DMA(())]),
        out_shape=jax.ShapeDtypeStruct((n_slots, D_MODEL), F32),
        compiler_params=_cparams(("arbitrary",)),
        name="moe_dispatch",
    )(dest, pad_end, pad_len, h2)


def _experts_kernel(te_ref, used_ref, xs_ref, wgu_ref, wd_ref, y_ref, wgu_bf, wd_bf):
    k = pl.program_id(0)
    new_expert = (k == 0) | (te_ref[k] != te_ref[jnp.maximum(k - 1, 0)])

    @pl.when(new_expert)
    def _():
        wgu_bf[...] = wgu_ref[0].astype(BF16)
        wd_bf[...] = wd_ref[0].astype(BF16)

    @pl.when(k < used_ref[0])
    def _():
        gu = jnp.dot(xs_ref[...].astype(BF16), wgu_bf[...], preferred_element_type=F32)
        gate, up = gu[:, :EXPERT_FF], gu[:, EXPERT_FF:]
        act = (gate * _sigmoid(gate)) * up
        y_ref[...] = jnp.dot(act.astype(BF16), wd_bf[...], preferred_element_type=F32)

    @pl.when(k >= used_ref[0])
    def _():
        y_ref[...] = jnp.zeros_like(y_ref)


def _experts(tile_expert, n_used, xs, w_gu, w_down, max_tiles):
    tm = MOE_TM
    row = lambda k, te, nu: (jnp.minimum(k, nu[0] - 1), 0)
    return pl.pallas_call(
        _experts_kernel,
        grid_spec=pltpu.PrefetchScalarGridSpec(
            num_scalar_prefetch=2, grid=(max_tiles,),
            in_specs=[pl.BlockSpec((tm, D_MODEL), row),
                      pl.BlockSpec((1, D_MODEL, 2 * EXPERT_FF), lambda k, te, nu: (te[k], 0, 0)),
                      pl.BlockSpec((1, EXPERT_FF, D_MODEL), lambda k, te, nu: (te[k], 0, 0))],
            out_specs=pl.BlockSpec((tm, D_MODEL), lambda k, te, nu: (k, 0)),
            scratch_shapes=[pltpu.VMEM((D_MODEL, 2 * EXPERT_FF), BF16), pltpu.VMEM((EXPERT_FF, D_MODEL), BF16)]),
        out_shape=jax.ShapeDtypeStruct((max_tiles * tm, D_MODEL), F32),
        compiler_params=_cparams(("arbitrary",)),
        name="moe_experts",
    )(tile_expert, n_used, xs, w_gu, w_down)


def _combine_kernel(dest_ref, y_hbm, x1_ref, route_ref, gtp_ref, gts_ref, yp_ref, ys_ref, buf, sem, *, n_prompt_tiles):
    i = pl.program_id(0)
    rows = x1_ref.shape[0]

    def gather(tile, slot):
        make = lambda r, k, d: pltpu.make_async_copy(y_hbm.at[pl.ds(d, 1)], buf.at[slot, k, pl.ds(r, 1)],
                                                     sem.at[slot])
        _start_pair_copies(dest_ref, tile, rows, make)

    @pl.when(i == 0)
    def _():
        gather(0, 0)

    slot = lax.rem(i, 2)

    @pl.when(i + 1 < pl.num_programs(0))
    def _():
        gather(i + 1, 1 - slot)

    for k in range(2):
        pltpu.make_async_copy(y_hbm.at[pl.ds(0, rows)], buf.at[slot, k], sem.at[slot]).wait()
    route = route_ref[...]
    lane = lax.broadcasted_iota(jnp.int32, route.shape, 1)
    w1 = jnp.sum(jnp.where(lane == ROUTE_W1, route, 0.0), axis=-1, keepdims=True)
    w2 = jnp.sum(jnp.where(lane == ROUTE_W2, route, 0.0), axis=-1, keepdims=True)
    moe = w1 * buf[slot, 0] + w2 * buf[slot, 1]

    @pl.when(i < n_prompt_tiles)
    def _():
        yp_ref[...] = x1_ref[...] + gtp_ref[0:1, :] * moe

    @pl.when(i >= n_prompt_tiles)
    def _():
        ys_ref[...] = x1_ref[...] + gts_ref[...] * moe


def _combine(dest, y_slots, x1, route, mod, tp, ts):
    tm = SMALL_TOK_TILE
    npt, nst = tp // tm, ts // tm
    row = lambda i, d: (i, 0)
    gt_p, gt_s = _mod_specs(MOD_GT2, tm, ts, npt)
    return pl.pallas_call(
        functools.partial(_combine_kernel, n_prompt_tiles=npt),
        grid_spec=pltpu.PrefetchScalarGridSpec(
            num_scalar_prefetch=1, grid=(npt + nst,),
            in_specs=[pl.BlockSpec(memory_space=pl.ANY),
                      pl.BlockSpec((tm, D_MODEL), row), pl.BlockSpec((tm, LANES), row), gt_p, gt_s],
            out_specs=[pl.BlockSpec((tm, D_MODEL), lambda i, d: (jnp.minimum(i, npt - 1), 0)),
                       pl.BlockSpec((tm, D_MODEL), lambda i, d: (jnp.maximum(i - npt, 0), 0))],
            scratch_shapes=[pltpu.VMEM((2, 2, tm, D_MODEL), F32), pltpu.SemaphoreType.DMA((2,))]),
        out_shape=[jax.ShapeDtypeStruct((tp, D_MODEL), F32), jax.ShapeDtypeStruct((ts, D_MODEL), F32)],
        compiler_params=_cparams(("arbitrary",)),
        name="moe_combine",
    )(dest, y_slots, x1, route, mod, mod)


def _moe(h2, route, x1, w_gu, w_down, mod, tp, ts):
    dest, pad_end, pad_len, tile_expert, n_used, max_tiles = _dispatch_plan(route, MOE_TM)
    xs = _dispatch(dest, pad_end, pad_len, h2, max_tiles * MOE_TM)
    y_slots = _experts(tile_expert, n_used, xs, w_gu, w_down, max_tiles)
    return _combine(dest, y_slots, x1, route, mod, tp, ts)


def _kv_pack_kernel(k_ref, v_ref, o_ref):
    n = k_ref.shape[0]
    for h in range(HEADS_PER_GROUP):
        cols = slice(h * HEAD_DIM, (h + 1) * HEAD_DIM)
        o_ref[pl.ds(h, n, stride=KV_PLANES), :] = k_ref[:, cols]
        o_ref[pl.ds(HEADS_PER_GROUP + h, n, stride=KV_PLANES), :] = v_ref[:, cols]


def _kv_pack(proj, row0, rows, g):
    tr = min(rows, SMALL_TOK_TILE)
    assert rows % tr == 0 and row0 % tr == 0
    col = lambda off: (off + g * GROUP_WIDTH) // GROUP_WIDTH
    spec = lambda off: pl.BlockSpec((tr, GROUP_WIDTH), lambda i: (row0 // tr + i, col(off)))
    flat = pl.pallas_call(
        _kv_pack_kernel,
        grid=(rows // tr,),
        in_specs=[spec(OFF_K), spec(OFF_V)],
        out_specs=pl.BlockSpec((tr * KV_PLANES, HEAD_DIM), lambda i: (i, 0)),
        out_shape=jax.ShapeDtypeStruct((rows * KV_PLANES, HEAD_DIM), F32),
        compiler_params=_cparams(("arbitrary",)),
        name="kv_pack",
    )(proj, proj)
    return flat.reshape(rows, 2, HEADS_PER_GROUP, HEAD_DIM)


def _rope_tables(n_prompt, n_seq, n_step):
    half = ROT_DIM // 2
    inv_freq = ROPE_THETA ** (-jnp.arange(half, dtype=F32) / half)
    angles = lambda pos: pos.astype(F32)[:, None] * inv_freq[None, :]
    fine = 128
    assert n_prompt % fine == 0
    coarse_ang = angles(jnp.arange(n_prompt // fine, dtype=jnp.int32) * fine)[:, None, :]
    fine_ang = angles(jnp.arange(fine, dtype=jnp.int32))[None, :, :]
    cc, sc, cf, sf = jnp.cos(coarse_ang), jnp.sin(coarse_ang), jnp.cos(fine_ang), jnp.sin(fine_ang)
    cos_p = (cc * cf - sc * sf).reshape(n_prompt, half)
    sin_p = (sc * cf + cc * sf).reshape(n_prompt, half)
    step_ang = angles(PAST_LEN + jnp.arange(n_step, dtype=jnp.int32))
    cos = jnp.concatenate([cos_p, jnp.tile(jnp.cos(step_ang), (n_seq, 1))], axis=0)
    sin = jnp.concatenate([sin_p, jnp.tile(jnp.sin(step_ang), (n_seq, 1))], axis=0)
    n = n_prompt + n_seq * n_step
    one = jnp.ones((n, HEAD_DIM - ROT_DIM), F32)
    zero = jnp.zeros((n, HEAD_DIM - ROT_DIM), F32)
    zh = jnp.zeros((n, half), F32)
    rc = jnp.concatenate([cos, cos, one], axis=1)
    rs1 = jnp.concatenate([-sin, zh, zero], axis=1)
    rs2 = jnp.concatenate([zh, sin, zero], axis=1)
    return rc, rs1, rs2


def kernel(x_prompt, x_sample, cache_kv_w128, cache_kv_w512, cache_kv_w2048, state_ssm_re, state_ssm_im,
           c_prompt, c_sample, w_ada, b_ada, norm1_g, norm2_g, w_in, ssm_a_re, ssm_a_im, ssm_log_dt,
           ssm_b_re, ssm_b_im, ssm_c_re, ssm_c_im, ssm_d, w_glu, q_norm_g, k_norm_g, w_attn_br, w_out,
           w_router_group, b_router_group, w_router_expert, b_router_expert, w_expert_gate_up, w_expert_down):
    assert x_prompt.shape[0] == 1 and w_ada.shape[0] == 1
    n_prompt = x_prompt.shape[1]
    n_seq, n_step = x_sample.shape[0], x_sample.shape[1]
    n_samp = n_seq * n_step
    assert n_samp % TOK_TILE == 0 and n_prompt % ATT_SB == 0 and (n_prompt + n_samp) % PROJ_TM == 0
    x_p = x_prompt.reshape(n_prompt, D_MODEL)
    x_s = x_sample.reshape(n_samp, D_MODEL)

    c_all = jnp.concatenate([jnp.repeat(c_sample, n_step, axis=0),
                             jnp.broadcast_to(c_prompt, (SUBLANES, D_MODEL))], axis=0)
    mod = _ada(c_all, w_ada[0], b_ada[0])

    h1 = _modnorm(x_p, x_s, norm1_g[0].reshape(1, D_MODEL), mod)
    rc, rs1, rs2 = _rope_tables(n_prompt, n_seq, n_step)
    proj = _inproj(h1, w_in[0], rc, rs1, rs2, q_norm_g[0].reshape(1, HEAD_DIM), k_norm_g[0].reshape(1, HEAD_DIM))

    pw_re, pw_im, bb_re, bb_im = _ssm_prep(ssm_a_re[0], ssm_a_im[0], ssm_log_dt[0], ssm_b_re[0], ssm_b_im[0])
    b_mat, c_mat = _ssm_block_matrices(bb_re, bb_im, ssm_c_re[0], ssm_c_im[0])
    d_skip = ssm_d[0].reshape(1, SSM_WIDTH)
    yg_p, fre_p, fim_p = _s5_prompt(proj, n_prompt, d_skip, pw_re, pw_im, b_mat, c_mat)
    yg_s, fre_s, fim_s = _s5_sample(proj, n_prompt, n_seq, n_step, d_skip, pw_re, pw_im, b_mat, c_mat,
                                    state_ssm_re[0].reshape(n_seq, SSM_FLAT), state_ssm_im[0].reshape(n_seq, SSM_FLAT))

    o_p = _attn_prompt(proj, n_prompt)
    o_s = _attn_sample(proj, n_prompt, n_seq, n_step, (cache_kv_w128[0], cache_kv_w512[0], cache_kv_w2048[0]))

    mixed = _mix(yg_p, yg_s, o_p, o_s, w_glu[0], w_attn_br[0], proj)

    w_router = jnp.concatenate([w_router_group[0], w_router_expert[0],
                                jnp.zeros((D_MODEL, LANES - N_EXPERT_GROUPS - N_EXPERTS), F32)], axis=1)
    b_router = jnp.concatenate([b_router_group[0], b_router_expert[0],
                                jnp.zeros((LANES - N_EXPERT_GROUPS - N_EXPERTS,), F32)]).reshape(1, LANES)
    x1, h2, route = _outproj(mixed, w_out[0].astype(BF16), x_p, x_s, norm2_g[0].reshape(1, D_MODEL),
                             w_router, b_router, mod)
    y_p, y_s = _moe(h2, route, x1, w_expert_gate_up[0], w_expert_down[0], mod, n_prompt, n_samp)

    kv_p, kv_s = [], []
    for g, (window, _) in enumerate(DILATION_PATTERNS):
        keep = min(window, n_prompt)
        kv_p.append(_kv_pack(proj, n_prompt - keep, keep, g).reshape(1, 1, keep, 2, HEADS_PER_GROUP, HEAD_DIM))
        kv_s.append(_kv_pack(proj, n_prompt, n_samp, g).reshape(1, n_seq, n_step, 2, HEADS_PER_GROUP, HEAD_DIM))

    state_shape_p = (1, 1, SSM_GROUPS, SSM_STATE)
    state_shape_s = (1, n_seq, SSM_GROUPS, SSM_STATE)
    return (y_p.reshape(1, n_prompt, D_MODEL), y_s.reshape(n_seq, n_step, D_MODEL),
            kv_p[0], kv_p[1], kv_p[2], fre_p.reshape(state_shape_p), fim_p.reshape(state_shape_p),
            kv_s[0], kv_s[1], kv_s[2], fre_s.reshape(state_shape_s), fim_s.reshape(state_shape_s))
```

```python
import functools
import math

import numpy as np
import jax
import jax.numpy as jnp
from jax import lax
from jax.experimental import pallas as pl
from jax.experimental.pallas import tpu as pltpu

F32 = jnp.float32
BF16 = jnp.bfloat16

D_MODEL = 2048
PAST_LEN = 2048
SSM_WIDTH = D_MODEL // 2
SSM_GROUP = 16
SSM_GROUPS = SSM_WIDTH // SSM_GROUP
SSM_STATE = 64
SSM_FLAT = SSM_GROUPS * SSM_STATE
HEAD_DIM = 128
DILATION_PATTERNS = ((128, 1), (512, 4), (2048, 16))
N_PATTERNS = 3
HEADS_PER_GROUP = 4
GROUP_WIDTH = HEADS_PER_GROUP * HEAD_DIM
ATTN_WIDTH = N_PATTERNS * GROUP_WIDTH
ROT_DIM = HEAD_DIM // 4
ROPE_THETA = 500000.0
OFF_Q = SSM_WIDTH
OFF_K = OFF_Q + ATTN_WIDTH
OFF_V = OFF_K + ATTN_WIDTH
OFF_G = OFF_V + ATTN_WIDTH
IN_COLS = OFF_G + 2 * D_MODEL
N_EXPERT_GROUPS = 4
EXPERTS_PER_GROUP = 4
N_EXPERTS = 16
EXPERT_FF = D_MODEL // 4
EPS = 1e-6
NEG = -1e30

LANES = 128
SUBLANES = 8
VMEM_LIMIT = 56 * 1024 * 1024

TOK_TILE = 512
SMALL_TOK_TILE = 256
PROJ_TM = 544
PROJ_TN = 2432
SCAN_L = 32
SCAN_NC = 16
SCAN_TT = SCAN_L * SCAN_NC
SSM_BLK = 8
SSM_BLK_STATE = SSM_BLK * SSM_STATE
SSM_PAR = 4
MOE_TM = 512
ROUTE_E1, ROUTE_E2, ROUTE_W1, ROUTE_W2 = 0, 1, 2, 3
MOD_SH1, MOD_SC1, MOD_GT1, MOD_SH2, MOD_SC2, MOD_GT2 = range(6)
ATT_SB = 2048
BAND = 128
ATT_UNROLL = 16


def _cparams(sem, vmem=VMEM_LIMIT):
    return pltpu.CompilerParams(dimension_semantics=sem, vmem_limit_bytes=vmem)


def _sigmoid(x):
    return 0.5 * jnp.tanh(0.5 * x) + 0.5


def _row_halves(rows):
    half = rows // 2
    return (slice(0, half), slice(half, rows))


def _gelu_tanh(x):
    c = math.sqrt(2.0 / math.pi)
    return 0.5 * x * (1.0 + jnp.tanh(c * (x + 0.044715 * (x * x * x))))


def _ada_kernel(c_ref, w_ref, b_ref, o_ref, cs_ref):
    @pl.when(pl.program_id(0) == 0)
    def _():
        c = c_ref[...]
        cs_ref[...] = (c * _sigmoid(c)).astype(BF16)

    o_ref[...] = jnp.dot(cs_ref[...], w_ref[...].astype(BF16), preferred_element_type=F32) + b_ref[...]


def _ada(c_all, w_ada, b_ada):
    rows = c_all.shape[0]
    n_out = w_ada.shape[1]
    tn = 1024
    return pl.pallas_call(
        _ada_kernel,
        grid=(n_out // tn,),
        in_specs=[pl.BlockSpec((rows, D_MODEL), lambda n: (0, 0)),
                  pl.BlockSpec((D_MODEL, tn), lambda n: (0, n)),
                  pl.BlockSpec((1, tn), lambda n: (0, n))],
        out_specs=pl.BlockSpec((rows, tn), lambda n: (0, n)),
        out_shape=jax.ShapeDtypeStruct((rows, n_out), F32),
        scratch_shapes=[pltpu.VMEM((rows, D_MODEL), BF16)],
        compiler_params=_cparams(("arbitrary",)),
        name="ada_mod",
    )(c_all, w_ada, b_ada.reshape(1, n_out))


def _modnorm_kernel(xp_ref, xs_ref, g_ref, scp_ref, shp_ref, scs_ref, shs_ref, o_ref, *, n_prompt_tiles):
    is_s = pl.program_id(0) >= n_prompt_tiles

    def norm(x_ref, sc_ref, sh_ref, rows):
        x = x_ref[...]
        ms = jnp.mean(x * x, axis=-1, keepdims=True)
        y = x * lax.rsqrt(ms + EPS) * g_ref[...]
        o_ref[...] = (y * (1.0 + sc_ref[rows, :]) + sh_ref[rows, :]).astype(o_ref.dtype)

    pl.when(jnp.logical_not(is_s))(functools.partial(norm, xp_ref, scp_ref, shp_ref, slice(0, 1)))
    pl.when(is_s)(functools.partial(norm, xs_ref, scs_ref, shs_ref, slice(None)))


def _mod_specs(k, tm, n_samp, n_prompt_tiles):
    prompt = pl.BlockSpec((SUBLANES, D_MODEL), lambda i, *_: (n_samp // SUBLANES, k))
    sample = pl.BlockSpec((tm, D_MODEL), lambda i, *_: (jnp.maximum(i - n_prompt_tiles, 0), k))
    return prompt, sample


def _modnorm(x_p, x_s, g, mod):
    tp, ts = x_p.shape[0], x_s.shape[0]
    tm = TOK_TILE
    npt, nst = tp // tm, ts // tm
    row = lambda i: (jnp.minimum(i, npt - 1), 0)
    srow = lambda i: (jnp.maximum(i - npt, 0), 0)
    const = lambda i: (0, 0)
    sc_p, sc_s = _mod_specs(MOD_SC1, tm, ts, npt)
    sh_p, sh_s = _mod_specs(MOD_SH1, tm, ts, npt)
    return pl.pallas_call(
        functools.partial(_modnorm_kernel, n_prompt_tiles=npt),
        grid=(npt + nst,),
        in_specs=[pl.BlockSpec((tm, D_MODEL), row), pl.BlockSpec((tm, D_MODEL), srow),
                  pl.BlockSpec((1, D_MODEL), const), sc_p, sh_p, sc_s, sh_s],
        out_specs=pl.BlockSpec((tm, D_MODEL), lambda i: (i, 0)),
        out_shape=jax.ShapeDtypeStruct((tp + ts, D_MODEL), BF16),
        compiler_params=_cparams(("arbitrary",)),
        name="modnorm1",
    )(x_p, x_s, g, mod, mod, mod, mod)


def _inproj_kernel(h_ref, w_hbm, rc_ref, rs1_ref, rs2_ref, qg_ref, kg_ref, o_ref, wf32_ref, wbf_ref, sem, *,
                   heads_per_tile):
    n, m = pl.program_id(0), pl.program_id(1)
    tn = heads_per_tile * HEAD_DIM
    fetch = lambda t: pltpu.make_async_copy(w_hbm.at[:, pl.ds(pl.multiple_of(t * tn, LANES), tn)], wf32_ref, sem)

    @pl.when((m == 0) & (n == 0))
    def _():
        fetch(0).start()

    @pl.when(m == 0)
    def _():
        fetch(n).wait()
        wbf_ref[...] = wf32_ref[...].astype(BF16)

    @pl.when((m == 0) & (n + 1 < pl.num_programs(0)))
    def _():
        fetch(n + 1).start()

    half = ROT_DIM // 2
    q_heads = range(OFF_Q // HEAD_DIM, OFF_K // HEAD_DIM)
    k_heads = range(OFF_K // HEAD_DIM, OFF_V // HEAD_DIM)
    pair = 2 * HEAD_DIM

    def tile(col_tile):
        h = h_ref[...]
        for c0 in range(0, heads_per_tile * HEAD_DIM, pair):
            width = min(pair, heads_per_tile * HEAD_DIM - c0)
            acc = jnp.dot(h, wbf_ref[:, c0:c0 + width], preferred_element_type=F32)
            for c in range(c0, c0 + width, HEAD_DIM):
                slot = col_tile * heads_per_tile + c // HEAD_DIM
                x = acc[:, c - c0:c - c0 + HEAD_DIM]
                if slot in q_heads or slot in k_heads:
                    gain = qg_ref[...] if slot in q_heads else kg_ref[...]
                    ms = jnp.mean(x * x, axis=-1, keepdims=True)
                    y = x * lax.rsqrt(ms + EPS) * gain
                    up = pltpu.roll(y, HEAD_DIM - half, 1)
                    dn = pltpu.roll(y, half, 1)
                    x = y * rc_ref[...] + up * rs1_ref[...] + dn * rs2_ref[...]
                o_ref[:, c:c + HEAD_DIM] = x

    for col_tile in range(IN_COLS // (heads_per_tile * HEAD_DIM)):
        pl.when(n == col_tile)(functools.partial(tile, col_tile))


def _inproj(h, w_in, rc, rs1, rs2, qg, kg):
    n_tok = h.shape[0]
    tm, tn = PROJ_TM, PROJ_TN
    tab = pl.BlockSpec((tm, HEAD_DIM), lambda n, m: (m, 0))
    gain = pl.BlockSpec((1, HEAD_DIM), lambda n, m: (0, 0))
    return pl.pallas_call(
        functools.partial(_inproj_kernel, heads_per_tile=tn // HEAD_DIM),
        grid=(IN_COLS // tn, n_tok // tm),
        in_specs=[pl.BlockSpec((tm, D_MODEL), lambda n, m: (m, 0)),
                  pl.BlockSpec(memory_space=pl.ANY),
                  tab, tab, tab, gain, gain],
        out_specs=pl.BlockSpec((tm, tn), lambda n, m: (m, n)),
        out_shape=jax.ShapeDtypeStruct((n_tok, IN_COLS), F32),
        scratch_shapes=[pltpu.VMEM((D_MODEL, tn), F32), pltpu.VMEM((D_MODEL, tn), BF16),
                        pltpu.SemaphoreType.DMA(())],
        compiler_params=_cparams(("arbitrary", "arbitrary")),
        name="in_proj",
    )(h, w_in, rc, rs1, rs2, qg, kg)


def _ssm_prep_kernel(are_ref, aim_ref, ldt_ref, arer_ref, aimr_ref, ldtr_ref, bre_ref, bim_ref,
                     pre_ref, pim_ref, bbre_ref, bbim_ref):
    def discretise(a_re, a_im, log_dt):
        dt = jnp.exp(log_dt)
        mag = jnp.exp(a_re * dt)
        return mag * jnp.cos(a_im * dt), mag * jnp.sin(a_im * dt)

    ab_re, ab_im = discretise(are_ref[...], aim_ref[...], ldt_ref[...])
    p_re, p_im = ab_re, ab_im
    for i in range(SCAN_L):
        pre_ref[i:i + 1, :] = p_re
        pim_ref[i:i + 1, :] = p_im
        p_re, p_im = p_re * ab_re - p_im * ab_im, p_re * ab_im + p_im * ab_re

    a_re, a_im = arer_ref[...], aimr_ref[...]
    r_re, r_im = discretise(a_re, a_im, ldtr_ref[...])
    nr, ni = r_re - 1.0, r_im
    den = a_re * a_re + a_im * a_im
    z_re = (nr * a_re + ni * a_im) / den
    z_im = (ni * a_re - nr * a_im) / den
    b_re, b_im = bre_ref[...], bim_ref[...]
    bbre_ref[...] = z_re * b_re - z_im * b_im
    bbim_ref[...] = z_re * b_im + z_im * b_re


def _ssm_prep(a_re, a_im, log_dt, b_re, b_im):
    g, p, n = b_re.shape
    flat = lambda x: x.reshape(1, g * p)
    rep = lambda x: jnp.repeat(x, n, axis=1)
    ldt_gp = jnp.broadcast_to(log_dt[:, None], (g, p))
    ldt_rep = jnp.broadcast_to(log_dt[:, None], (g, p * n))
    out_shape = [jax.ShapeDtypeStruct((SCAN_L, g * p), F32)] * 2 + [jax.ShapeDtypeStruct((g, p * n), F32)] * 2
    return pl.pallas_call(_ssm_prep_kernel, out_shape=out_shape, name="ssm_prep")(
        flat(a_re), flat(a_im), flat(ldt_gp), rep(a_re), rep(a_im), ldt_rep,
        b_re.reshape(g, p * n), b_im.reshape(g, p * n))


def _ssm_block_matrices(bb_re, bb_im, c_re, c_im):
    g, p, n = SSM_GROUPS, SSM_STATE, SSM_GROUP
    nb = g // SSM_BLK
    eye = jnp.eye(SSM_BLK, dtype=F32)

    def in_mat(bb):
        x = bb.reshape(nb, SSM_BLK, p, n)
        return jnp.einsum('bgpm,gh->bgmhp', x, eye).reshape(nb, SSM_BLK * n, SSM_BLK * p)

    def out_mat(c):
        x = c.reshape(nb, SSM_BLK, n, p)
        return jnp.einsum('bgnp,gh->bgphn', x, eye).reshape(nb, SSM_BLK * p, SSM_BLK * n)

    b_mat = jnp.concatenate([in_mat(bb_re), in_mat(bb_im)], axis=2).astype(BF16)
    c_mat = jnp.concatenate([out_mat(c_re), -out_mat(c_im)], axis=1).astype(BF16)
    return b_mat, c_mat


def _cmul_add(a_re, a_im, s_re, s_im, b_re, b_im):
    return a_re * s_re - a_im * s_im + b_re, a_re * s_im + a_im * s_re + b_im


def _s5_prompt_kernel(*refs):
    par = SSM_PAR
    u_refs = refs[:par]
    (d_ref, pre_ref, pim_ref, bm_ref, cm_ref, y_ref, fre_ref, fim_ref,
     up_scr, bu_scr, lhs_scr, in_re_scr, in_im_scr, car_re, car_im, yn_scr) = refs[par:]
    nc, ln, w = SCAN_NC, SCAN_L, SSM_BLK_STATE

    @pl.when(pl.program_id(1) == 0)
    def _():
        car_re[...] = jnp.zeros_like(car_re)
        car_im[...] = jnp.zeros_like(car_im)

    for b in range(par):
        lanes = slice(b * w, (b + 1) * w)
        rows = lambda i: slice(i * nc, (i + 1) * nc)
        for i in range(ln):
            up_scr[b, rows(i), :] = u_refs[b][pl.ds(i, nc, stride=ln), :]
        up = up_scr[b]
        bu_scr[b] = jnp.dot(up.astype(BF16), bm_ref[b], preferred_element_type=F32)

        a_re = jnp.broadcast_to(pre_ref[0:1, lanes], (nc, w))
        a_im = jnp.broadcast_to(pim_ref[0:1, lanes], (nc, w))
        s_re = s_im = jnp.zeros((nc, w), F32)
        for i in range(ln):
            s_re, s_im = _cmul_add(a_re, a_im, s_re, s_im, bu_scr[b, rows(i), 0:w], bu_scr[b, rows(i), w:2 * w])
            bu_scr[b, rows(i), 0:w] = s_re
            bu_scr[b, rows(i), w:2 * w] = s_im

        al_re, al_im = pre_ref[ln - 1:ln, lanes], pim_ref[ln - 1:ln, lanes]
        c_re, c_im = car_re[b], car_im[b]
        for c in range(nc):
            in_re_scr[b, c:c + 1, :] = c_re
            in_im_scr[b, c:c + 1, :] = c_im
            c_re, c_im = _cmul_add(al_re, al_im, c_re, c_im, s_re[c:c + 1, :], s_im[c:c + 1, :])
        car_re[b] = c_re
        car_im[b] = c_im
        fre_ref[:, lanes] = c_re
        fim_ref[:, lanes] = c_im
        in_re, in_im = in_re_scr[b], in_im_scr[b]

        for i in range(ln):
            p_re = jnp.broadcast_to(pre_ref[i:i + 1, lanes], (nc, w))
            p_im = jnp.broadcast_to(pim_ref[i:i + 1, lanes], (nc, w))
            f_re, f_im = _cmul_add(p_re, p_im, in_re, in_im, bu_scr[b, rows(i), 0:w], bu_scr[b, rows(i), w:2 * w])
            lhs_scr[b, rows(i), 0:w] = f_re.astype(BF16)
            lhs_scr[b, rows(i), w:2 * w] = f_im.astype(BF16)

        y = (jnp.dot(lhs_scr[b], cm_ref[b], preferred_element_type=F32)
             + d_ref[:, b * LANES:(b + 1) * LANES] * up)
        for i in range(ln):
            yn_scr[b, pl.ds(i, nc, stride=ln), :] = y[rows(i), :]
        y_ref[:, b * LANES:(b + 1) * LANES] = _gelu_tanh(yn_scr[b]).astype(y_ref.dtype)


def _s5_prompt(proj, n_prompt, d_skip, pw_re, pw_im, b_mat, c_mat):
    par = SSM_PAR
    nb = SSM_GROUPS // SSM_BLK // par
    tt, w = SCAN_TT, SSM_BLK_STATE
    u_spec = lambda b: pl.BlockSpec((tt, LANES), lambda j, i: (i, par * j + b))
    return pl.pallas_call(
        _s5_prompt_kernel,
        grid=(nb, n_prompt // tt),
        in_specs=[u_spec(b) for b in range(par)] + [
            pl.BlockSpec((1, par * LANES), lambda j, i: (0, j)),
            pl.BlockSpec((SCAN_L, par * w), lambda j, i: (0, j)),
            pl.BlockSpec((SCAN_L, par * w), lambda j, i: (0, j)),
            pl.BlockSpec((par, LANES, 2 * w), lambda j, i: (j, 0, 0)),
            pl.BlockSpec((par, 2 * w, LANES), lambda j, i: (j, 0, 0))],
        out_specs=[pl.BlockSpec((tt, par * LANES), lambda j, i: (i, j)),
                   pl.BlockSpec((1, par * w), lambda j, i: (0, j)),
                   pl.BlockSpec((1, par * w), lambda j, i: (0, j))],
        out_shape=[jax.ShapeDtypeStruct((n_prompt, SSM_WIDTH), BF16),
                   jax.ShapeDtypeStruct((1, SSM_FLAT), F32),
                   jax.ShapeDtypeStruct((1, SSM_FLAT), F32)],
        scratch_shapes=[pltpu.VMEM((par, tt, LANES), F32), pltpu.VMEM((par, tt, 2 * w), F32),
                        pltpu.VMEM((par, tt, 2 * w), BF16),
                        pltpu.VMEM((par, SCAN_NC, w), F32), pltpu.VMEM((par, SCAN_NC, w), F32),
                        pltpu.VMEM((par, 1, w), F32), pltpu.VMEM((par, 1, w), F32),
                        pltpu.VMEM((par, tt, LANES), F32)],
        compiler_params=_cparams(("arbitrary", "arbitrary")),
        name="s5_prompt",
    )(*([proj] * par), d_skip, pw_re, pw_im, b_mat, c_mat)


def _s5_sample_kernel(u_ref, d_ref, pre_ref, pim_ref, bm_ref, cm_ref, s0re_ref, s0im_ref,
                      y_ref, fre_ref, fim_ref, up_scr, bu_scr, lhs_scr, yn_scr, *, n_seq, n_step):
    w = SSM_BLK_STATE
    rb = 16
    for s in range(n_step):
        up_scr[s * n_seq:(s + 1) * n_seq, :] = u_ref[pl.ds(s, n_seq, stride=n_step), :]
    up = up_scr[...]
    bu_scr[...] = jnp.dot(up.astype(BF16), bm_ref[0], preferred_element_type=F32)
    a_re = jnp.broadcast_to(pre_ref[0:1, :], (rb, w))
    a_im = jnp.broadcast_to(pim_ref[0:1, :], (rb, w))

    def seq_block(b, _):
        r0 = pl.multiple_of(b * rb, rb)
        s_re, s_im = s0re_ref[pl.ds(r0, rb), :], s0im_ref[pl.ds(r0, rb), :]
        for s in range(n_step):
            rows = pl.ds(pl.multiple_of(s * n_seq + r0, rb), rb)
            s_re, s_im = _cmul_add(a_re, a_im, s_re, s_im, bu_scr[rows, 0:w], bu_scr[rows, w:2 * w])
            lhs_scr[rows, 0:w] = s_re.astype(BF16)
            lhs_scr[rows, w:2 * w] = s_im.astype(BF16)
        fre_ref[pl.ds(r0, rb), :] = s_re
        fim_ref[pl.ds(r0, rb), :] = s_im
        return 0

    lax.fori_loop(0, n_seq // rb, seq_block, 0)
    y = jnp.dot(lhs_scr[...], cm_ref[0], preferred_element_type=F32) + d_ref[...] * up
    for s in range(n_step):
        yn_scr[pl.ds(s, n_seq, stride=n_step), :] = y[s * n_seq:(s + 1) * n_seq, :]
    y_ref[...] = _gelu_tanh(yn_scr[...]).astype(y_ref.dtype)


def _s5_sample(proj, n_prompt, n_seq, n_step, d_skip, pw_re, pw_im, b_mat, c_mat, s0_re, s0_im):
    nb = SSM_GROUPS // SSM_BLK
    rows, w = n_seq * n_step, SSM_BLK_STATE
    rblk = n_prompt // rows
    return pl.pallas_call(
        functools.partial(_s5_sample_kernel, n_seq=n_seq, n_step=n_step),
        grid=(nb,),
        in_specs=[pl.BlockSpec((rows, LANES), lambda j: (rblk, j)),
                  pl.BlockSpec((1, LANES), lambda j: (0, j)),
                  pl.BlockSpec((SCAN_L, w), lambda j: (0, j)),
                  pl.BlockSpec((SCAN_L, w), lambda j: (0, j)),
                  pl.BlockSpec((1, LANES, 2 * w), lambda j: (j, 0, 0)),
                  pl.BlockSpec((1, 2 * w, LANES), lambda j: (j, 0, 0)),
                  pl.BlockSpec((n_seq, w), lambda j: (0, j)),
                  pl.BlockSpec((n_seq, w), lambda j: (0, j))],
        out_specs=[pl.BlockSpec((rows, LANES), lambda j: (0, j)),
                   pl.BlockSpec((n_seq, w), lambda j: (0, j)),
                   pl.BlockSpec((n_seq, w), lambda j: (0, j))],
        out_shape=[jax.ShapeDtypeStruct((rows, SSM_WIDTH), BF16),
                   jax.ShapeDtypeStruct((n_seq, SSM_FLAT), F32),
                   jax.ShapeDtypeStruct((n_seq, SSM_FLAT), F32)],
        scratch_shapes=[pltpu.VMEM((rows, LANES), F32), pltpu.VMEM((rows, 2 * w), F32),
                        pltpu.VMEM((rows, 2 * w), BF16), pltpu.VMEM((rows, LANES), F32)],
        compiler_params=_cparams(("arbitrary",)),
        name="s5_sample",
    )(proj, d_skip, pw_re, pw_im, b_mat, c_mat, s0_re, s0_im)


def _attn_prompt_kernel(*refs):
    ins, o_ref, scr = refs[:15], refs[15], refs[16:]
    sb = pl.program_id(0)
    scale = HEAD_DIM ** -0.5
    qi = lax.broadcasted_iota(jnp.int32, (BAND, 2 * BAND), 0)
    kj = lax.broadcasted_iota(jnp.int32, (BAND, 2 * BAND), 1)
    dist = qi + BAND - kj
    band_ok = (dist >= 0) & (dist <= BAND)

    for g, (_, dil) in enumerate(DILATION_PATTERNS):
        q_ref, k_ref, v_ref, kp_ref, vp_ref = ins[5 * g:5 * g + 5]
        kbuf, vbuf, o_scr, m_scr, l_scr = scr[5 * g:5 * g + 5]
        pre = BAND * dil
        kbuf[0:pre, :] = kp_ref[...]
        kbuf[pre:pre + ATT_SB, :] = k_ref[...]
        vbuf[0:pre, :] = vp_ref[...]
        vbuf[pre:pre + ATT_SB, :] = v_ref[...]
        nblk = ATT_SB // pre

        def block(idx, _, dil=dil, pre=pre, nblk=nblk, q_ref=q_ref, kbuf=kbuf, vbuf=vbuf,
                  o_scr=o_scr, m_scr=m_scr, l_scr=l_scr):
            r = idx // nblk
            b = idx - r * nblk
            row0 = r + b * pre
            if dil == 1:
                q_rows = pl.ds(pl.multiple_of(row0, BAND), BAND)
                kv_rows = pl.ds(pl.multiple_of(row0, BAND), 2 * BAND)
            else:
                q_rows = pl.ds(row0, BAND, stride=dil)
                kv_rows = pl.ds(row0, 2 * BAND, stride=dil)
            q = (q_ref[q_rows, :] * scale).astype(BF16)
            kw = kbuf[kv_rows, :].astype(BF16)
            vw = vbuf[kv_rows, :].astype(BF16)
            s = lax.dot_general(q, kw, (((1,), (1,)), ((), ())), preferred_element_type=F32)
            s = jnp.where(band_ok & ((kj >= BAND) | (sb > 0) | (b > 0)), s, NEG)
            m = jnp.max(s, axis=-1, keepdims=True)
            p = jnp.exp(s - m)
            l = jnp.sum(p, axis=-1, keepdims=True)
            o = jnp.dot(p.astype(BF16), vw, preferred_element_type=F32)
            o_scr[q_rows, :] = o
            m_scr[q_rows, :] = jnp.broadcast_to(m, (BAND, HEAD_DIM))
            l_scr[q_rows, :] = jnp.broadcast_to(l, (BAND, HEAD_DIM))
            return 0

        lax.fori_loop(0, ATT_SB // BAND, block, 0, unroll=ATT_UNROLL)

    ms = [scr[5 * g + 3][...] for g in range(N_PATTERNS)]
    mx = jnp.maximum(jnp.maximum(ms[0], ms[1]), ms[2])
    num = jnp.zeros((ATT_SB, HEAD_DIM), F32)
    den = jnp.zeros((ATT_SB, HEAD_DIM), F32)
    for g in range(N_PATTERNS):
        wgt = jnp.exp(ms[g] - mx)
        num = num + wgt * scr[5 * g + 2][...]
        den = den + wgt * scr[5 * g + 4][...]
    o_ref[...] = num / den


def _attn_prompt(proj, n_prompt):
    hcol = lambda off, g, j: (off + g * GROUP_WIDTH) // HEAD_DIM + j
    in_specs, scratch = [], []
    for g, (_, dil) in enumerate(DILATION_PATTERNS):
        pre = BAND * dil
        per = ATT_SB // pre
        cur = lambda off, g=g: pl.BlockSpec((ATT_SB, HEAD_DIM), lambda sb, j: (sb, hcol(off, g, j)))
        prev = lambda off, g=g, per=per, pre=pre: pl.BlockSpec(
            (pre, HEAD_DIM), lambda sb, j: (jnp.maximum(sb * per - 1, 0), hcol(off, g, j)))
        in_specs += [cur(OFF_Q), cur(OFF_K), cur(OFF_V), prev(OFF_K), prev(OFF_V)]
        scratch += [pltpu.VMEM((pre + ATT_SB, HEAD_DIM), F32), pltpu.VMEM((pre + ATT_SB, HEAD_DIM), F32),
                    pltpu.VMEM((ATT_SB, HEAD_DIM), F32), pltpu.VMEM((ATT_SB, HEAD_DIM), F32),
                    pltpu.VMEM((ATT_SB, HEAD_DIM), F32)]
    return pl.pallas_call(
        _attn_prompt_kernel,
        grid=(n_prompt // ATT_SB, HEADS_PER_GROUP),
        in_specs=in_specs,
        out_specs=pl.BlockSpec((ATT_SB, HEAD_DIM), lambda sb, j: (sb, j)),
        out_shape=jax.ShapeDtypeStruct((n_prompt, GROUP_WIDTH), F32),
        scratch_shapes=scratch,
        compiler_params=_cparams(("arbitrary", "arbitrary")),
        name="attn_prompt",
    )(*([proj] * 15))


SEQ_PER_STEP = 2
KV_PLANES = 2 * HEADS_PER_GROUP


def _sample_bias(n_step):
    rows = HEADS_PER_GROUP * n_step
    step = np.arange(rows) % n_step
    cache_bias, new_bias = [], []
    for (window, dil) in DILATION_PATTERNS:
        wb = min(window, PAST_LEN)
        band = window // dil
        if dil > n_step:
            res, i = np.meshgrid(np.arange(n_step), np.arange(wb // dil), indexing='ij')
            c = (i * dil + res).reshape(-1)
        else:
            c = np.arange(wb)
        delta = wb + step[:, None] - c[None, :]
        ok = (delta >= 0) & (delta % dil == 0) & (delta // dil <= band)
        cache_bias.append(np.where(ok, 0.0, NEG).astype(np.float32))
        nb = np.full((SEQ_PER_STEP, rows, LANES), NEG, np.float32)
        for a in range(SEQ_PER_STEP):
            for sp in range(n_step):
                dl = step - sp
                okn = (dl >= 0) & (dl % dil == 0) & (dl // dil <= band)
                nb[a, :, a * n_step + sp] = np.where(okn, 0.0, NEG)
        new_bias.append(nb)
    return cache_bias, new_bias


def _attn_sample_kernel(*refs, n_step, cache_pieces):
    (q0, k0, v0, q1, k1, v1, q2, k2, v2) = refs[:9]
    qs, ks, vs = (q0, q1, q2), (k0, k1, k2), (v0, v1, v2)
    refs = list(refs[9:])
    caches = [[refs.pop(0) for _ in range(n)] for n in cache_pieces]
    (cb0, cb1, cb2, nb0, nb1, nb2, o_ref) = refs
    cbias, nbias = (cb0, cb1, cb2), (nb0, nb1, nb2)
    rows = HEADS_PER_GROUP * n_step
    gw = GROUP_WIDTH
    scale = HEAD_DIM ** -0.5
    row_head = lax.broadcasted_iota(jnp.int32, (rows, gw), 0) // n_step
    lane_head = lax.broadcasted_iota(jnp.int32, (rows, gw), 1) // HEAD_DIM
    own_head = row_head == lane_head
    nt = (((1,), (1,)), ((), ()))
    pad = jnp.zeros((LANES - SEQ_PER_STEP * n_step, gw), F32)

    def planes(load):
        k = jnp.concatenate([load(h) for h in range(HEADS_PER_GROUP)], axis=1)
        v = jnp.concatenate([load(HEADS_PER_GROUP + h) for h in range(HEADS_PER_GROUP)], axis=1)
        return k.astype(BF16), v.astype(BF16)

    for a in range(SEQ_PER_STEP):
        pieces = []
        for g, (window, dil) in enumerate(DILATION_PATTERNS):
            q = qs[g][a * n_step:(a + 1) * n_step, :] * scale
            qbd = jnp.where(own_head, jnp.concatenate([q] * HEADS_PER_GROUP, axis=0), 0.0).astype(BF16)
            kvs = []
            for cache in caches[g]:
                if len(cache.shape) == 3:
                    cache = cache.reshape(cache.shape[0] * KV_PLANES, HEAD_DIM)
                per_seq = cache.shape[0] // SEQ_PER_STEP
                kvs.append(planes(lambda p, cache=cache, per_seq=per_seq: cache[
                    pl.ds(a * per_seq + p, per_seq // KV_PLANES, stride=KV_PLANES), :]))
            sc = jnp.concatenate([lax.dot_general(qbd, k, nt, preferred_element_type=F32) for k, _ in kvs], axis=1)
            pieces.append((sc + cbias[g][...], [v for _, v in kvs]))
            k_new = jnp.concatenate([ks[g][...], pad], axis=0).astype(BF16)
            v_new = jnp.concatenate([vs[g][...], pad], axis=0).astype(BF16)
            sn = lax.dot_general(qbd, k_new, nt, preferred_element_type=F32) + nbias[g][a]
            pieces.append((sn, [v_new]))
        m = functools.reduce(jnp.maximum, [jnp.max(s, axis=-1, keepdims=True) for s, _ in pieces])
        l = jnp.zeros((rows, 1), F32)
        acc = jnp.zeros((rows, gw), F32)
        for s, vals in pieces:
            p = jnp.exp(s - m)
            l = l + jnp.sum(p, axis=-1, keepdims=True)
            pb = p.astype(BF16)
            nk = pb.shape[1] // len(vals)
            for r, v in enumerate(vals):
                acc = acc + jnp.dot(pb[:, r * nk:(r + 1) * nk], v, preferred_element_type=F32)
        acc = jnp.where(own_head, acc, 0.0)
        o16 = functools.reduce(lambda x, y: x + y,
                               [acc[:, h * HEAD_DIM:(h + 1) * HEAD_DIM] for h in range(HEADS_PER_GROUP)]) / l
        for h in range(HEADS_PER_GROUP):
            o_ref[a * n_step:(a + 1) * n_step, h * HEAD_DIM:(h + 1) * HEAD_DIM] = o16[h * n_step:(h + 1) * n_step, :]


def _attn_sample(proj, n_prompt, n_seq, n_step, caches):
    rows = SEQ_PER_STEP * n_step
    rblk = n_prompt // rows
    cache_bias, new_bias = _sample_bias(n_step)
    tok = lambda off, g: pl.BlockSpec((rows, GROUP_WIDTH), lambda i: (rblk + i, (off + g * GROUP_WIDTH) // GROUP_WIDTH))
    in_specs, args, cache_pieces = [], [], []
    for g in range(N_PATTERNS):
        in_specs += [tok(OFF_Q, g), tok(OFF_K, g), tok(OFF_V, g)]
        args += [proj, proj, proj]
    for g, (window, dil) in enumerate(DILATION_PATTERNS):
        c = caches[g]
        wb = c.shape[1]
        if dil > n_step:
            assert wb % dil == 0
            c = c.reshape(n_seq * (wb // dil), dil * KV_PLANES, HEAD_DIM)
            for r in range(n_step):
                in_specs.append(pl.BlockSpec((SEQ_PER_STEP * (wb // dil), KV_PLANES, HEAD_DIM),
                                             lambda i, r=r: (i, r, 0)))
                args.append(c)
            cache_pieces.append(n_step)
        else:
            c = c.reshape(n_seq * wb * KV_PLANES, HEAD_DIM)
            in_specs.append(pl.BlockSpec((SEQ_PER_STEP * wb * KV_PLANES, HEAD_DIM), lambda i: (i, 0)))
            args.append(c)
            cache_pieces.append(1)
    for b in cache_bias:
        in_specs.append(pl.BlockSpec(b.shape, lambda i: (0, 0)))
        args.append(jnp.asarray(b))
    for b in new_bias:
        in_specs.append(pl.BlockSpec(b.shape, lambda i: (0, 0, 0)))
        args.append(jnp.asarray(b))
    return pl.pallas_call(
        functools.partial(_attn_sample_kernel, n_step=n_step, cache_pieces=tuple(cache_pieces)),
        grid=(n_seq // SEQ_PER_STEP,),
        in_specs=in_specs,
        out_specs=pl.BlockSpec((rows, GROUP_WIDTH), lambda i: (i, 0)),
        out_shape=jax.ShapeDtypeStruct((n_seq * n_step, GROUP_WIDTH), F32),
        compiler_params=_cparams(("arbitrary",)),
        name="attn_sample",
    )(*args)


def _mix_kernel(yp_ref, ys_ref, op_ref, os_ref, wa_ref, wb_ref, wbr_ref, ga_ref, gb_ref, o_ref,
                wa_bf, wb_bf, wbr_bf, *, n_prompt_tiles):
    i = pl.program_id(1)

    @pl.when(i == 0)
    def _():
        wa_bf[...] = wa_ref[...].astype(BF16)
        wb_bf[...] = wb_ref[...].astype(BF16)
        wbr_bf[...] = wbr_ref[...].astype(BF16)

    is_s = i >= n_prompt_tiles
    for rows in _row_halves(o_ref.shape[0]):
        y = jnp.where(is_s, ys_ref[rows, :], yp_ref[rows, :])
        o = jnp.where(is_s, os_ref[rows, :], op_ref[rows, :]).astype(BF16)
        glu_a = jnp.dot(y, wa_bf[...], preferred_element_type=F32)
        glu_b = jnp.dot(y, wb_bf[...], preferred_element_type=F32)
        branch_a = glu_a * _sigmoid(glu_b)
        branch_b = jnp.dot(o, wbr_bf[...], preferred_element_type=F32)
        o_ref[rows, :] = (_sigmoid(ga_ref[rows, :]) * branch_a
                          + _sigmoid(gb_ref[rows, :]) * branch_b).astype(o_ref.dtype)


def _mix(y_p, y_s, o_p, o_s, w_glu, w_attn_br, proj):
    tp, ts = y_p.shape[0], y_s.shape[0]
    tm, tn = TOK_TILE, 512
    assert OFF_G % tn == 0
    npt, nst = tp // tm, ts // tm
    ncol = D_MODEL // tn
    prow = lambda n, i: (jnp.minimum(i, npt - 1), 0)
    srow = lambda n, i: (jnp.maximum(i - npt, 0), 0)
    return pl.pallas_call(
        functools.partial(_mix_kernel, n_prompt_tiles=npt),
        grid=(ncol, npt + nst),
        in_specs=[pl.BlockSpec((tm, SSM_WIDTH), prow), pl.BlockSpec((tm, SSM_WIDTH), srow),
                  pl.BlockSpec((tm, GROUP_WIDTH), prow), pl.BlockSpec((tm, GROUP_WIDTH), srow),
                  pl.BlockSpec((SSM_WIDTH, tn), lambda n, i: (0, n)),
                  pl.BlockSpec((SSM_WIDTH, tn), lambda n, i: (0, ncol + n)),
                  pl.BlockSpec((GROUP_WIDTH, tn), lambda n, i: (0, n)),
                  pl.BlockSpec((tm, tn), lambda n, i: (i, OFF_G // tn + n)),
                  pl.BlockSpec((tm, tn), lambda n, i: (i, OFF_G // tn + ncol + n))],
        out_specs=pl.BlockSpec((tm, tn), lambda n, i: (i, n)),
        out_shape=jax.ShapeDtypeStruct((tp + ts, D_MODEL), BF16),
        scratch_shapes=[pltpu.VMEM((SSM_WIDTH, tn), BF16), pltpu.VMEM((SSM_WIDTH, tn), BF16),
                        pltpu.VMEM((GROUP_WIDTH, tn), BF16)],
        compiler_params=_cparams(("arbitrary", "arbitrary")),
        name="glu_mix",
    )(y_p, y_s, o_p, o_s, w_glu, w_glu, w_attn_br, proj, proj)


def _route(logits):
    lane = lax.broadcasted_iota(jnp.int32, logits.shape, 1).astype(F32)
    big = 1000.0
    first = lambda cond: jnp.min(jnp.where(cond, lane, big), axis=-1, keepdims=True)
    is_g = lane < N_EXPERT_GROUPS
    lg = jnp.where(is_g, logits, NEG)
    mg = jnp.max(lg, axis=-1, keepdims=True)
    g_sel = first(lg == mg)
    p_group = 1.0 / jnp.sum(jnp.where(is_g, jnp.exp(lg - mg), 0.0), axis=-1, keepdims=True)
    e_lo = N_EXPERT_GROUPS + EXPERTS_PER_GROUP * g_sel
    le = jnp.where((lane >= e_lo) & (lane < e_lo + EXPERTS_PER_GROUP), logits, NEG)
    v1 = jnp.max(le, axis=-1, keepdims=True)
    i1 = first(le == v1)
    le2 = jnp.where(lane == i1, NEG, le)
    v2 = jnp.max(le2, axis=-1, keepdims=True)
    i2 = first(le2 == v2)
    e2 = jnp.exp(v2 - v1)
    w1 = p_group / (1.0 + e2)
    w2 = p_group * e2 / (1.0 + e2)
    pick = lambda k, val: jnp.where(lane == k, val, 0.0)
    return (pick(ROUTE_E1, i1 - N_EXPERT_GROUPS) + pick(ROUTE_E2, i2 - N_EXPERT_GROUPS)
            + pick(ROUTE_W1, w1) + pick(ROUTE_W2, w2))


def _outproj_kernel(mix_ref, w_ref, xp_ref, xs_ref, g_ref, wr_ref, br_ref,
                    gtp_ref, scp_ref, shp_ref, gts_ref, scs_ref, shs_ref,
                    x1_ref, h2_ref, route_ref, *, n_prompt_tiles):
    is_s = pl.program_id(0) >= n_prompt_tiles

    def split(v):
        high = v.astype(BF16)
        return high, (v - high.astype(F32)).astype(BF16)

    x = jnp.where(is_s, xs_ref[...], xp_ref[...])
    gt = jnp.where(is_s, gts_ref[...], gtp_ref[0:1, :])
    sc = jnp.where(is_s, scs_ref[...], scp_ref[0:1, :])
    sh = jnp.where(is_s, shs_ref[...], shp_ref[0:1, :])
    x1 = x + gt * jnp.dot(mix_ref[...], w_ref[...], preferred_element_type=F32)
    x1_ref[...] = x1
    ms = jnp.mean(x1 * x1, axis=-1, keepdims=True)
    h2 = (x1 * lax.rsqrt(ms + EPS) * g_ref[...]) * (1.0 + sc) + sh
    h2_ref[...] = h2.astype(h2_ref.dtype)
    r = jnp.dot(jnp.concatenate(split(h2), axis=0), jnp.concatenate(split(wr_ref[...]), axis=1),
                preferred_element_type=F32)
    n = h2.shape[0]
    logits = (r[:n, :LANES] + r[:n, LANES:]) + (r[n:, :LANES] + r[n:, LANES:]) + br_ref[...]
    route_ref[...] = _route(logits)


def _outproj(mixed, w_out_bf, x_p, x_s, g2, w_router, b_router, mod):
    tp, ts = x_p.shape[0], x_s.shape[0]
    tm = SMALL_TOK_TILE
    npt, nst = tp // tm, ts // tm
    prow = lambda i: (jnp.minimum(i, npt - 1), 0)
    srow = lambda i: (jnp.maximum(i - npt, 0), 0)
    const = lambda i: (0, 0)
    vec = pl.BlockSpec((1, D_MODEL), const)
    gt_p, gt_s = _mod_specs(MOD_GT1, tm, ts, npt)
    sc_p, sc_s = _mod_specs(MOD_SC2, tm, ts, npt)
    sh_p, sh_s = _mod_specs(MOD_SH2, tm, ts, npt)
    full = lambda i: (i, 0)
    return pl.pallas_call(
        functools.partial(_outproj_kernel, n_prompt_tiles=npt),
        grid=(npt + nst,),
        in_specs=[pl.BlockSpec((tm, D_MODEL), full), pl.BlockSpec((D_MODEL, D_MODEL), const),
                  pl.BlockSpec((tm, D_MODEL), prow), pl.BlockSpec((tm, D_MODEL), srow),
                  vec, pl.BlockSpec((D_MODEL, LANES), const), pl.BlockSpec((1, LANES), const),
                  gt_p, sc_p, sh_p, gt_s, sc_s, sh_s],
        out_specs=[pl.BlockSpec((tm, D_MODEL), full), pl.BlockSpec((tm, D_MODEL), full),
                   pl.BlockSpec((tm, LANES), full)],
        out_shape=[jax.ShapeDtypeStruct((tp + ts, D_MODEL), F32),
                   jax.ShapeDtypeStruct((tp + ts, D_MODEL), F32),
                   jax.ShapeDtypeStruct((tp + ts, LANES), F32)],
        compiler_params=_cparams(("arbitrary",)),
        name="out_proj_norm2_router",
    )(mixed, w_out_bf, x_p, x_s, g2, w_router, b_router, mod, mod, mod, mod, mod, mod)


def _dispatch_plan(route, tm):
    e = route[:, ROUTE_E1:ROUTE_E2 + 1].astype(jnp.int32).reshape(-1)
    n_pairs = e.shape[0]
    onehot = (e[:, None] == jnp.arange(N_EXPERTS, dtype=jnp.int32)[None, :]).astype(jnp.int32)
    csum = jnp.cumsum(onehot, axis=0)
    rank = jnp.sum(onehot * csum, axis=1) - 1
    tiles_per_expert = (csum[-1] + tm - 1) // tm
    tile_end = jnp.cumsum(tiles_per_expert)
    tile_start = tile_end - tiles_per_expert
    dest = (tile_start[e] * tm + rank).astype(jnp.int32)
    max_tiles = n_pairs // tm + N_EXPERTS
    k = jnp.arange(max_tiles, dtype=jnp.int32)
    tile_expert = jnp.minimum(jnp.sum((k[:, None] >= tile_end[None, :]).astype(jnp.int32), axis=1), N_EXPERTS - 1)
    n_used = tile_end[-1].astype(jnp.int32)
    last_expert = jnp.take(tile_expert, n_used - 1)
    tile_expert = jnp.where(k < n_used, tile_expert, last_expert).astype(jnp.int32)
    pad_end = (jnp.concatenate([tile_end, tile_end[-1:]]) * tm).astype(jnp.int32)
    pad_len = (tiles_per_expert * tm - csum[-1]).astype(jnp.int32)
    return dest, pad_end, pad_len, tile_expert, n_used.reshape(1), max_tiles


def _start_pair_copies(dest_ref, tile, rows, make):
    def body(r, _):
        for k in range(2):
            make(r, k, dest_ref[(tile * rows + r) * 2 + k]).start()
        return 0
    lax.fori_loop(0, rows, body, 0, unroll=8)


def _pair_copies(dest_ref, tile, rows, make, make_all):
    _start_pair_copies(dest_ref, tile, rows, make)
    for k in range(2):
        make_all(k).wait()


def _dispatch_kernel(dest_ref, pad_end_ref, pad_len_ref, h_ref, xs_ref, zero_buf, sem, pad_sem):
    rows = h_ref.shape[0]

    @pl.when(pl.program_id(0) == 0)
    def _():
        zero_buf[...] = jnp.zeros_like(zero_buf)

        def pad_copies(fn):
            for e in range(N_EXPERTS):
                end, left = pad_end_ref[e], pad_len_ref[e]
                size = zero_buf.shape[0]
                while size >= SUBLANES:
                    take = (left & size) != 0
                    end = end - jnp.where(take, size, 0)

                    @pl.when(take)
                    def _(start=end, size=size):
                        fn(pltpu.make_async_copy(zero_buf.at[pl.ds(0, size)],
                                                 xs_ref.at[pl.ds(pl.multiple_of(start, size), size)], pad_sem))

                    size //= 2
                for r in range(1, SUBLANES):
                    @pl.when((left & (SUBLANES - 1)) >= r)
                    def _(row=end - r):
                        fn(pltpu.make_async_copy(zero_buf.at[pl.ds(0, 1)], xs_ref.at[pl.ds(row, 1)], pad_sem))
            size = zero_buf.shape[0]
            tail = pad_end_ref[N_EXPERTS]
            for t in range(N_EXPERTS * MOE_TM // size):
                @pl.when(tail + t * size < xs_ref.shape[0])
                def _(start=tail + t * size):
                    fn(pltpu.make_async_copy(zero_buf, xs_ref.at[pl.ds(pl.multiple_of(start, size), size)], pad_sem))

        pad_copies(lambda c: c.start())
        pad_copies(lambda c: c.wait())

    make = lambda r, k, d: pltpu.make_async_copy(h_ref.at[pl.ds(r, 1)], xs_ref.at[pl.ds(d, 1)], sem)
    make_all = lambda k: pltpu.make_async_copy(h_ref, xs_ref.at[pl.ds(0, rows)], sem)
    _pair_copies(dest_ref, pl.program_id(0), rows, make, make_all)


def _dispatch(dest, pad_end, pad_len, h2, n_slots):
    n_tok = h2.shape[0]
    tm = SMALL_TOK_TILE
    return pl.pallas_call(
        _dispatch_kernel,
        grid_spec=pltpu.PrefetchScalarGridSpec(
            num_scalar_prefetch=3, grid=(n_tok // tm,),
            in_specs=[pl.BlockSpec((tm, D_MODEL), lambda i, d, ps, pn: (i, 0))],
            out_specs=pl.BlockSpec(memory_space=pl.ANY),
            scratch_shapes=[pltpu.VMEM((MOE_TM // 2, D_MODEL), F32),
                            pltpu.SemaphoreType.DMA(()), pltpu.SemaphoreType.DMA(())]),
        out_shape=jax.ShapeDtypeStruct((n_slots, D_MODEL), F32),
        compiler_params=_cparams(("arbitrary",)),
        name="moe_dispatch",
    )(dest, pad_end, pad_len, h2)


def _experts_kernel(te_ref, used_ref, xs_ref, wgu_ref, wd_ref, y_ref, wgu_bf, wd_bf):
    k = pl.program_id(0)
    new_expert = (k == 0) | (te_ref[k] != te_ref[jnp.maximum(k - 1, 0)])

    @pl.when(new_expert)
    def _():
        wgu_bf[...] = wgu_ref[0].astype(BF16)
        wd_bf[...] = wd_ref[0].astype(BF16)

    @pl.when(k < used_ref[0])
    def _():
        gu = jnp.dot(xs_ref[...].astype(BF16), wgu_bf[...], preferred_element_type=F32)
        gate, up = gu[:, :EXPERT_FF], gu[:, EXPERT_FF:]
        act = (gate * _sigmoid(gate)) * up
        y_ref[...] = jnp.dot(act.astype(BF16), wd_bf[...], preferred_element_type=F32)

    @pl.when(k >= used_ref[0])
    def _():
        y_ref[...] = jnp.zeros_like(y_ref)


def _experts(tile_expert, n_used, xs, w_gu, w_down, max_tiles):
    tm = MOE_TM
    row = lambda k, te, nu: (jnp.minimum(k, nu[0] - 1), 0)
    return pl.pallas_call(
        _experts_kernel,
        grid_spec=pltpu.PrefetchScalarGridSpec(
            num_scalar_prefetch=2, grid=(max_tiles,),
            in_specs=[pl.BlockSpec((tm, D_MODEL), row),
                      pl.BlockSpec((1, D_MODEL, 2 * EXPERT_FF), lambda k, te, nu: (te[k], 0, 0)),
                      pl.BlockSpec((1, EXPERT_FF, D_MODEL), lambda k, te, nu: (te[k], 0, 0))],
            out_specs=pl.BlockSpec((tm, D_MODEL), lambda k, te, nu: (k, 0)),
            scratch_shapes=[pltpu.VMEM((D_MODEL, 2 * EXPERT_FF), BF16), pltpu.VMEM((EXPERT_FF, D_MODEL), BF16)]),
        out_shape=jax.ShapeDtypeStruct((max_tiles * tm, D_MODEL), F32),
        compiler_params=_cparams(("arbitrary",)),
        name="moe_experts",
    )(tile_expert, n_used, xs, w_gu, w_down)


def _combine_kernel(dest_ref, y_hbm, x1_ref, route_ref, gtp_ref, gts_ref, yp_ref, ys_ref, buf, sem, *, n_prompt_tiles):
    i = pl.program_id(0)
    rows = x1_ref.shape[0]

    def gather(tile, slot):
        make = lambda r, k, d: pltpu.make_async_copy(y_hbm.at[pl.ds(d, 1)], buf.at[slot, k, pl.ds(r, 1)],
                                                     sem.at[slot])
        _start_pair_copies(dest_ref, tile, rows, make)

    @pl.when(i == 0)
    def _():
        gather(0, 0)

    slot = lax.rem(i, 2)

    @pl.when(i + 1 < pl.num_programs(0))
    def _():
        gather(i + 1, 1 - slot)

    for k in range(2):
        pltpu.make_async_copy(y_hbm.at[pl.ds(0, rows)], buf.at[slot, k], sem.at[slot]).wait()
    route = route_ref[...]
    lane = lax.broadcasted_iota(jnp.int32, route.shape, 1)
    w1 = jnp.sum(jnp.where(lane == ROUTE_W1, route, 0.0), axis=-1, keepdims=True)
    w2 = jnp.sum(jnp.where(lane == ROUTE_W2, route, 0.0), axis=-1, keepdims=True)
    moe = w1 * buf[slot, 0] + w2 * buf[slot, 1]

    @pl.when(i < n_prompt_tiles)
    def _():
        yp_ref[...] = x1_ref[...] + gtp_ref[0:1, :] * moe

    @pl.when(i >= n_prompt_tiles)
    def _():
        ys_ref[...] = x1_ref[...] + gts_ref[...] * moe


def _combine(dest, y_slots, x1, route, mod, tp, ts):
    tm = SMALL_TOK_TILE
    npt, nst = tp // tm, ts // tm
    row = lambda i, d: (i, 0)
    gt_p, gt_s = _mod_specs(MOD_GT2, tm, ts, npt)
    return pl.pallas_call(
        functools.partial(_combine_kernel, n_prompt_tiles=npt),
        grid_spec=pltpu.PrefetchScalarGridSpec(
            num_scalar_prefetch=1, grid=(npt + nst,),
            in_specs=[pl.BlockSpec(memory_space=pl.ANY),
                      pl.BlockSpec((tm, D_MODEL), row), pl.BlockSpec((tm, LANES), row), gt_p, gt_s],
            out_specs=[pl.BlockSpec((tm, D_MODEL), lambda i, d: (jnp.minimum(i, npt - 1), 0)),
                       pl.BlockSpec((tm, D_MODEL), lambda i, d: (jnp.maximum(i - npt, 0), 0))],
            scratch_shapes=[pltpu.VMEM((2, 2, tm, D_MODEL), F32), pltpu.SemaphoreType.DMA((2,))]),
        out_shape=[jax.ShapeDtypeStruct((tp, D_MODEL), F32), jax.ShapeDtypeStruct((ts, D_MODEL), F32)],
        compiler_params=_cparams(("arbitrary",)),
        name="moe_combine",
    )(dest, y_slots, x1, route, mod, mod)


def _moe(h2, route, x1, w_gu, w_down, mod, tp, ts):
    dest, pad_end, pad_len, tile_expert, n_used, max_tiles = _dispatch_plan(route, MOE_TM)
    xs = _dispatch(dest, pad_end, pad_len, h2, max_tiles * MOE_TM)
    y_slots = _experts(tile_expert, n_used, xs, w_gu, w_down, max_tiles)
    return _combine(dest, y_slots, x1, route, mod, tp, ts)


def _kv_pack_kernel(k_ref, v_ref, o_ref):
    n = k_ref.shape[0]
    for h in range(HEADS_PER_GROUP):
        cols = slice(h * HEAD_DIM, (h + 1) * HEAD_DIM)
        o_ref[pl.ds(h, n, stride=KV_PLANES), :] = k_ref[:, cols]
        o_ref[pl.ds(HEADS_PER_GROUP + h, n, stride=KV_PLANES), :] = v_ref[:, cols]


def _kv_pack(proj, row0, rows, g):
    tr = min(rows, SMALL_TOK_TILE)
    assert rows % tr == 0 and row0 % tr == 0
    col = lambda off: (off + g * GROUP_WIDTH) // GROUP_WIDTH
    spec = lambda off: pl.BlockSpec((tr, GROUP_WIDTH), lambda i: (row0 // tr + i, col(off)))
    flat = pl.pallas_call(
        _kv_pack_kernel,
        grid=(rows // tr,),
        in_specs=[spec(OFF_K), spec(OFF_V)],
        out_specs=pl.BlockSpec((tr * KV_PLANES, HEAD_DIM), lambda i: (i, 0)),
        out_shape=jax.ShapeDtypeStruct((rows * KV_PLANES, HEAD_DIM), F32),
        compiler_params=_cparams(("arbitrary",)),
        name="kv_pack",
    )(proj, proj)
    return flat.reshape(rows, 2, HEADS_PER_GROUP, HEAD_DIM)


def _rope_tables(n_prompt, n_seq, n_step):
    half = ROT_DIM // 2
    inv_freq = ROPE_THETA ** (-jnp.arange(half, dtype=F32) / half)
    angles = lambda pos: pos.astype(F32)[:, None] * inv_freq[None, :]
    fine = 128
    assert n_prompt % fine == 0
    coarse_ang = angles(jnp.arange(n_prompt // fine, dtype=jnp.int32) * fine)[:, None, :]
    fine_ang = angles(jnp.arange(fine, dtype=jnp.int32))[None, :, :]
    cc, sc, cf, sf = jnp.cos(coarse_ang), jnp.sin(coarse_ang), jnp.cos(fine_ang), jnp.sin(fine_ang)
    cos_p = (cc * cf - sc * sf).reshape(n_prompt, half)
    sin_p = (sc * cf + cc * sf).reshape(n_prompt, half)
    step_ang = angles(PAST_LEN + jnp.arange(n_step, dtype=jnp.int32))
    cos = jnp.concatenate([cos_p, jnp.tile(jnp.cos(step_ang), (n_seq, 1))], axis=0)
    sin = jnp.concatenate([sin_p, jnp.tile(jnp.sin(step_ang), (n_seq, 1))], axis=0)
    n = n_prompt + n_seq * n_step
    one = jnp.ones((n, HEAD_DIM - ROT_DIM), F32)
    zero = jnp.zeros((n, HEAD_DIM - ROT_DIM), F32)
    zh = jnp.zeros((n, half), F32)
    rc = jnp.concatenate([cos, cos, one], axis=1)
    rs1 = jnp.concatenate([-sin, zh, zero], axis=1)
    rs2 = jnp.concatenate([zh, sin, zero], axis=1)
    return rc, rs1, rs2


def kernel(x_prompt, x_sample, cache_kv_w128, cache_kv_w512, cache_kv_w2048, state_ssm_re, state_ssm_im,
           c_prompt, c_sample, w_ada, b_ada, norm1_g, norm2_g, w_in, ssm_a_re, ssm_a_im, ssm_log_dt,
           ssm_b_re, ssm_b_im, ssm_c_re, ssm_c_im, ssm_d, w_glu, q_norm_g, k_norm_g, w_attn_br, w_out,
           w_router_group, b_router_group, w_router_expert, b_router_expert, w_expert_gate_up, w_expert_down):
    assert x_prompt.shape[0] == 1 and w_ada.shape[0] == 1
    n_prompt = x_prompt.shape[1]
    n_seq, n_step = x_sample.shape[0], x_sample.shape[1]
    n_samp = n_seq * n_step
    assert n_samp % TOK_TILE == 0 and n_prompt % ATT_SB == 0 and (n_prompt + n_samp) % PROJ_TM == 0
    x_p = x_prompt.reshape(n_prompt, D_MODEL)
    x_s = x_sample.reshape(n_samp, D_MODEL)

    c_all = jnp.concatenate([jnp.repeat(c_sample, n_step, axis=0),
                             jnp.broadcast_to(c_prompt, (SUBLANES, D_MODEL))], axis=0)
    mod = _ada(c_all, w_ada[0], b_ada[0])

    h1 = _modnorm(x_p, x_s, norm1_g[0].reshape(1, D_MODEL), mod)
    rc, rs1, rs2 = _rope_tables(n_prompt, n_seq, n_step)
    proj = _inproj(h1, w_in[0], rc, rs1, rs2, q_norm_g[0].reshape(1, HEAD_DIM), k_norm_g[0].reshape(1, HEAD_DIM))

    pw_re, pw_im, bb_re, bb_im = _ssm_prep(ssm_a_re[0], ssm_a_im[0], ssm_log_dt[0], ssm_b_re[0], ssm_b_im[0])
    b_mat, c_mat = _ssm_block_matrices(bb_re, bb_im, ssm_c_re[0], ssm_c_im[0])
    d_skip = ssm_d[0].reshape(1, SSM_WIDTH)
    yg_p, fre_p, fim_p = _s5_prompt(proj, n_prompt, d_skip, pw_re, pw_im, b_mat, c_mat)
    yg_s, fre_s, fim_s = _s5_sample(proj, n_prompt, n_seq, n_step, d_skip, pw_re, pw_im, b_mat, c_mat,
                                    state_ssm_re[0].reshape(n_seq, SSM_FLAT), state_ssm_im[0].reshape(n_seq, SSM_FLAT))

    o_p = _attn_prompt(proj, n_prompt)
    o_s = _attn_sample(proj, n_prompt, n_seq, n_step, (cache_kv_w128[0], cache_kv_w512[0], cache_kv_w2048[0]))

    mixed = _mix(yg_p, yg_s, o_p, o_s, w_glu[0], w_attn_br[0], proj)

    w_router = jnp.concatenate([w_router_group[0], w_router_expert[0],
                                jnp.zeros((D_MODEL, LANES - N_EXPERT_GROUPS - N_EXPERTS), F32)], axis=1)
    b_router = jnp.concatenate([b_router_group[0], b_router_expert[0],
                                jnp.zeros((LANES - N_EXPERT_GROUPS - N_EXPERTS,), F32)]).reshape(1, LANES)
    x1, h2, route = _outproj(mixed, w_out[0].astype(BF16), x_p, x_s, norm2_g[0].reshape(1, D_MODEL),
                             w_router, b_router, mod)
    y_p, y_s = _moe(h2, route, x1, w_expert_gate_up[0], w_expert_down[0], mod, n_prompt, n_samp)

    kv_p, kv_s = [], []
    for g, (window, _) in enumerate(DILATION_PATTERNS):
        keep = min(window, n_prompt)
        kv_p.append(_kv_pack(proj, n_prompt - keep, keep, g).reshape(1, 1, keep, 2, HEADS_PER_GROUP, HEAD_DIM))
        kv_s.append(_kv_pack(proj, n_prompt, n_samp, g).reshape(1, n_seq, n_step, 2, HEADS_PER_GROUP, HEAD_DIM))

    state_shape_p = (1, 1, SSM_GROUPS, SSM_STATE)
    state_shape_s = (1, n_seq, SSM_GROUPS, SSM_STATE)
    return (y_p.reshape(1, n_prompt, D_MODEL), y_s.reshape(n_seq, n_step, D_MODEL),
            kv_p[0], kv_p[1], kv_p[2], fre_p.reshape(state_shape_p), fim_p.reshape(state_shape_p),
            kv_s[0], kv_s[1], kv_s[2], fre_s.reshape(state_shape_s), fim_s.reshape(state_shape_s))
```

```python
import functools
import math

import numpy as np
import jax
import jax.numpy as jnp
from jax import lax
from jax.experimental import pallas as pl
from jax.experimental.pallas import tpu as pltpu

F32 = jnp.float32
BF16 = jnp.bfloat16

D_MODEL = 2048
PAST_LEN = 2048
SSM_WIDTH = D_MODEL // 2
SSM_GROUP = 16
SSM_GROUPS = SSM_WIDTH // SSM_GROUP
SSM_STATE = 64
SSM_FLAT = SSM_GROUPS * SSM_STATE
HEAD_DIM = 128
DILATION_PATTERNS = ((128, 1), (512, 4), (2048, 16))
N_PATTERNS = 3
HEADS_PER_GROUP = 4
GROUP_WIDTH = HEADS_PER_GROUP * HEAD_DIM
ATTN_WIDTH = N_PATTERNS * GROUP_WIDTH
ROT_DIM = HEAD_DIM // 4
ROPE_THETA = 500000.0
OFF_Q = SSM_WIDTH
OFF_K = OFF_Q + ATTN_WIDTH
OFF_V = OFF_K + ATTN_WIDTH
OFF_G = OFF_V + ATTN_WIDTH
IN_COLS = OFF_G + 2 * D_MODEL
N_EXPERT_GROUPS = 4
EXPERTS_PER_GROUP = 4
N_EXPERTS = 16
EXPERT_FF = D_MODEL // 4
EPS = 1e-6
NEG = -1e30

LANES = 128
SUBLANES = 8
VMEM_LIMIT = 56 * 1024 * 1024

TOK_TILE = 512
SMALL_TOK_TILE = 256
PROJ_TM = 544
PROJ_TN = 2432
SCAN_L = 32
SCAN_NC = 16
SCAN_TT = SCAN_L * SCAN_NC
SSM_BLK = 8
SSM_BLK_STATE = SSM_BLK * SSM_STATE
SSM_PAR = 4
MOE_TM = 512
ROUTE_E1, ROUTE_E2, ROUTE_W1, ROUTE_W2 = 0, 1, 2, 3
MOD_SH1, MOD_SC1, MOD_GT1, MOD_SH2, MOD_SC2, MOD_GT2 = range(6)
ATT_SB = 2048
BAND = 128
ATT_UNROLL = 16


def _cparams(sem, vmem=VMEM_LIMIT):
    return pltpu.CompilerParams(dimension_semantics=sem, vmem_limit_bytes=vmem)


def _sigmoid(x):
    return 0.5 * jnp.tanh(0.5 * x) + 0.5


def _row_halves(rows):
    half = rows // 2
    return (slice(0, half), slice(half, rows))


def _gelu_tanh(x):
    c = math.sqrt(2.0 / math.pi)
    return 0.5 * x * (1.0 + jnp.tanh(c * (x + 0.044715 * (x * x * x))))


def _ada_kernel(c_ref, w_ref, b_ref, o_ref, cs_ref):
    @pl.when(pl.program_id(0) == 0)
    def _():
        c = c_ref[...]
        cs_ref[...] = (c * _sigmoid(c)).astype(BF16)

    o_ref[...] = jnp.dot(cs_ref[...], w_ref[...].astype(BF16), preferred_element_type=F32) + b_ref[...]


def _ada(c_all, w_ada, b_ada):
    rows = c_all.shape[0]
    n_out = w_ada.shape[1]
    tn = 1024
    return pl.pallas_call(
        _ada_kernel,
        grid=(n_out // tn,),
        in_specs=[pl.BlockSpec((rows, D_MODEL), lambda n: (0, 0)),
                  pl.BlockSpec((D_MODEL, tn), lambda n: (0, n)),
                  pl.BlockSpec((1, tn), lambda n: (0, n))],
        out_specs=pl.BlockSpec((rows, tn), lambda n: (0, n)),
        out_shape=jax.ShapeDtypeStruct((rows, n_out), F32),
        scratch_shapes=[pltpu.VMEM((rows, D_MODEL), BF16)],
        compiler_params=_cparams(("arbitrary",)),
        name="ada_mod",
    )(c_all, w_ada, b_ada.reshape(1, n_out))


def _modnorm_kernel(xp_ref, xs_ref, g_ref, scp_ref, shp_ref, scs_ref, shs_ref, o_ref, *, n_prompt_tiles):
    is_s = pl.program_id(0) >= n_prompt_tiles

    def norm(x_ref, sc_ref, sh_ref, rows):
        x = x_ref[...]
        ms = jnp.mean(x * x, axis=-1, keepdims=True)
        y = x * lax.rsqrt(ms + EPS) * g_ref[...]
        o_ref[...] = (y * (1.0 + sc_ref[rows, :]) + sh_ref[rows, :]).astype(o_ref.dtype)

    pl.when(jnp.logical_not(is_s))(functools.partial(norm, xp_ref, scp_ref, shp_ref, slice(0, 1)))
    pl.when(is_s)(functools.partial(norm, xs_ref, scs_ref, shs_ref, slice(None)))


def _mod_specs(k, tm, n_samp, n_prompt_tiles):
    prompt = pl.BlockSpec((SUBLANES, D_MODEL), lambda i, *_: (n_samp // SUBLANES, k))
    sample = pl.BlockSpec((tm, D_MODEL), lambda i, *_: (jnp.maximum(i - n_prompt_tiles, 0), k))
    return prompt, sample


def _modnorm(x_p, x_s, g, mod):
    tp, ts = x_p.shape[0], x_s.shape[0]
    tm = TOK_TILE
    npt, nst = tp // tm, ts // tm
    row = lambda i: (jnp.minimum(i, npt - 1), 0)
    srow = lambda i: (jnp.maximum(i - npt, 0), 0)
    const = lambda i: (0, 0)
    sc_p, sc_s = _mod_specs(MOD_SC1, tm, ts, npt)
    sh_p, sh_s = _mod_specs(MOD_SH1, tm, ts, npt)
    return pl.pallas_call(
        functools.partial(_modnorm_kernel, n_prompt_tiles=npt),
        grid=(npt + nst,),
        in_specs=[pl.BlockSpec((tm, D_MODEL), row), pl.BlockSpec((tm, D_MODEL), srow),
                  pl.BlockSpec((1, D_MODEL), const), sc_p, sh_p, sc_s, sh_s],
        out_specs=pl.BlockSpec((tm, D_MODEL), lambda i: (i, 0)),
        out_shape=jax.ShapeDtypeStruct((tp + ts, D_MODEL), BF16),
        compiler_params=_cparams(("arbitrary",)),
        name="modnorm1",
    )(x_p, x_s, g, mod, mod, mod, mod)


def _inproj_kernel(h_ref, w_hbm, rc_ref, rs1_ref, rs2_ref, qg_ref, kg_ref, o_ref, wf32_ref, wbf_ref, sem, *,
                   heads_per_tile):
    n, m = pl.program_id(0), pl.program_id(1)
    tn = heads_per_tile * HEAD_DIM
    fetch = lambda t: pltpu.make_async_copy(w_hbm.at[:, pl.ds(pl.multiple_of(t * tn, LANES), tn)], wf32_ref, sem)

    @pl.when((m == 0) & (n == 0))
    def _():
        fetch(0).start()

    @pl.when(m == 0)
    def _():
        fetch(n).wait()
        wbf_ref[...] = wf32_ref[...].astype(BF16)

    @pl.when((m == 0) & (n + 1 < pl.num_programs(0)))
    def _():
        fetch(n + 1).start()

    half = ROT_DIM // 2
    q_heads = range(OFF_Q // HEAD_DIM, OFF_K // HEAD_DIM)
    k_heads = range(OFF_K // HEAD_DIM, OFF_V // HEAD_DIM)
    pair = 2 * HEAD_DIM

    def tile(col_tile):
        h = h_ref[...]
        for c0 in range(0, heads_per_tile * HEAD_DIM, pair):
            width = min(pair, heads_per_tile * HEAD_DIM - c0)
            acc = jnp.dot(h, wbf_ref[:, c0:c0 + width], preferred_element_type=F32)
            for c in range(c0, c0 + width, HEAD_DIM):
                slot = col_tile * heads_per_tile + c // HEAD_DIM
                x = acc[:, c - c0:c - c0 + HEAD_DIM]
                if slot in q_heads or slot in k_heads:
                    gain = qg_ref[...] if slot in q_heads else kg_ref[...]
                    ms = jnp.mean(x * x, axis=-1, keepdims=True)
                    y = x * lax.rsqrt(ms + EPS) * gain
                    up = pltpu.roll(y, HEAD_DIM - half, 1)
                    dn = pltpu.roll(y, half, 1)
                    x = y * rc_ref[...] + up * rs1_ref[...] + dn * rs2_ref[...]
                o_ref[:, c:c + HEAD_DIM] = x

    for col_tile in range(IN_COLS // (heads_per_tile * HEAD_DIM)):
        pl.when(n == col_tile)(functools.partial(tile, col_tile))


def _inproj(h, w_in, rc, rs1, rs2, qg, kg):
    n_tok = h.shape[0]
    tm, tn = PROJ_TM, PROJ_TN
    tab = pl.BlockSpec((tm, HEAD_DIM), lambda n, m: (m, 0))
    gain = pl.BlockSpec((1, HEAD_DIM), lambda n, m: (0, 0))
    return pl.pallas_call(
        functools.partial(_inproj_kernel, heads_per_tile=tn // HEAD_DIM),
        grid=(IN_COLS // tn, n_tok // tm),
        in_specs=[pl.BlockSpec((tm, D_MODEL), lambda n, m: (m, 0)),
                  pl.BlockSpec(memory_space=pl.ANY),
                  tab, tab, tab, gain, gain],
        out_specs=pl.BlockSpec((tm, tn), lambda n, m: (m, n)),
        out_shape=jax.ShapeDtypeStruct((n_tok, IN_COLS), F32),
        scratch_shapes=[pltpu.VMEM((D_MODEL, tn), F32), pltpu.VMEM((D_MODEL, tn), BF16),
                        pltpu.SemaphoreType.DMA(())],
        compiler_params=_cparams(("arbitrary", "arbitrary")),
        name="in_proj",
    )(h, w_in, rc, rs1, rs2, qg, kg)


def _ssm_prep_kernel(are_ref, aim_ref, ldt_ref, arer_ref, aimr_ref, ldtr_ref, bre_ref, bim_ref,
                     pre_ref, pim_ref, bbre_ref, bbim_ref):
    def discretise(a_re, a_im, log_dt):
        dt = jnp.exp(log_dt)
        mag = jnp.exp(a_re * dt)
        return mag * jnp.cos(a_im * dt), mag * jnp.sin(a_im * dt)

    ab_re, ab_im = discretise(are_ref[...], aim_ref[...], ldt_ref[...])
    p_re, p_im = ab_re, ab_im
    for i in range(SCAN_L):
        pre_ref[i:i + 1, :] = p_re
        pim_ref[i:i + 1, :] = p_im
        p_re, p_im = p_re * ab_re - p_im * ab_im, p_re * ab_im + p_im * ab_re

    a_re, a_im = arer_ref[...], aimr_ref[...]
    r_re, r_im = discretise(a_re, a_im, ldtr_ref[...])
    nr, ni = r_re - 1.0, r_im
    den = a_re * a_re + a_im * a_im
    z_re = (nr * a_re + ni * a_im) / den
    z_im = (ni * a_re - nr * a_im) / den
    b_re, b_im = bre_ref[...], bim_ref[...]
    bbre_ref[...] = z_re * b_re - z_im * b_im
    bbim_ref[...] = z_re * b_im + z_im * b_re


def _ssm_prep(a_re, a_im, log_dt, b_re, b_im):
    g, p, n = b_re.shape
    flat = lambda x: x.reshape(1, g * p)
    rep = lambda x: jnp.repeat(x, n, axis=1)
    ldt_gp = jnp.broadcast_to(log_dt[:, None], (g, p))
    ldt_rep = jnp.broadcast_to(log_dt[:, None], (g, p * n))
    out_shape = [jax.ShapeDtypeStruct((SCAN_L, g * p), F32)] * 2 + [jax.ShapeDtypeStruct((g, p * n), F32)] * 2
    return pl.pallas_call(_ssm_prep_kernel, out_shape=out_shape, name="ssm_prep")(
        flat(a_re), flat(a_im), flat(ldt_gp), rep(a_re), rep(a_im), ldt_rep,
        b_re.reshape(g, p * n), b_im.reshape(g, p * n))


def _ssm_block_matrices(bb_re, bb_im, c_re, c_im):
    g, p, n = SSM_GROUPS, SSM_STATE, SSM_GROUP
    nb = g // SSM_BLK
    eye = jnp.eye(SSM_BLK, dtype=F32)

    def in_mat(bb):
        x = bb.reshape(nb, SSM_BLK, p, n)
        return jnp.einsum('bgpm,gh->bgmhp', x, eye).reshape(nb, SSM_BLK * n, SSM_BLK * p)

    def out_mat(c):
        x = c.reshape(nb, SSM_BLK, n, p)
        return jnp.einsum('bgnp,gh->bgphn', x, eye).reshape(nb, SSM_BLK * p, SSM_BLK * n)

    b_mat = jnp.concatenate([in_mat(bb_re), in_mat(bb_im)], axis=2).astype(BF16)
    c_mat = jnp.concatenate([out_mat(c_re), -out_mat(c_im)], axis=1).astype(BF16)
    return b_mat, c_mat


def _cmul_add(a_re, a_im, s_re, s_im, b_re, b_im):
    return a_re * s_re - a_im * s_im + b_re, a_re * s_im + a_im * s_re + b_im


def _s5_prompt_kernel(*refs):
    par = SSM_PAR
    u_refs = refs[:par]
    (d_ref, pre_ref, pim_ref, bm_ref, cm_ref, y_ref, fre_ref, fim_ref,
     up_scr, bu_scr, lhs_scr, in_re_scr, in_im_scr, car_re, car_im, yn_scr) = refs[par:]
    nc, ln, w = SCAN_NC, SCAN_L, SSM_BLK_STATE

    @pl.when(pl.program_id(1) == 0)
    def _():
        car_re[...] = jnp.zeros_like(car_re)
        car_im[...] = jnp.zeros_like(car_im)

    for b in range(par):
        lanes = slice(b * w, (b + 1) * w)
        rows = lambda i: slice(i * nc, (i + 1) * nc)
        for i in range(ln):
            up_scr[b, rows(i), :] = u_refs[b][pl.ds(i, nc, stride=ln), :]
        up = up_scr[b]
        bu_scr[b] = jnp.dot(up.astype(BF16), bm_ref[b], preferred_element_type=F32)

        a_re = jnp.broadcast_to(pre_ref[0:1, lanes], (nc, w))
        a_im = jnp.broadcast_to(pim_ref[0:1, lanes], (nc, w))
        s_re = s_im = jnp.zeros((nc, w), F32)
        for i in range(ln):
            s_re, s_im = _cmul_add(a_re, a_im, s_re, s_im, bu_scr[b, rows(i), 0:w], bu_scr[b, rows(i), w:2 * w])
            bu_scr[b, rows(i), 0:w] = s_re
            bu_scr[b, rows(i), w:2 * w] = s_im

        al_re, al_im = pre_ref[ln - 1:ln, lanes], pim_ref[ln - 1:ln, lanes]
        c_re, c_im = car_re[b], car_im[b]
        for c in range(nc):
            in_re_scr[b, c:c + 1, :] = c_re
            in_im_scr[b, c:c + 1, :] = c_im
            c_re, c_im = _cmul_add(al_re, al_im, c_re, c_im, s_re[c:c + 1, :], s_im[c:c + 1, :])
        car_re[b] = c_re
        car_im[b] = c_im
        fre_ref[:, lanes] = c_re
        fim_ref[:, lanes] = c_im
        in_re, in_im = in_re_scr[b], in_im_scr[b]

        for i in range(ln):
            p_re = jnp.broadcast_to(pre_ref[i:i + 1, lanes], (nc, w))
            p_im = jnp.broadcast_to(pim_ref[i:i + 1, lanes], (nc, w))
            f_re, f_im = _cmul_add(p_re, p_im, in_re, in_im, bu_scr[b, rows(i), 0:w], bu_scr[b, rows(i), w:2 * w])
            lhs_scr[b, rows(i), 0:w] = f_re.astype(BF16)
            lhs_scr[b, rows(i), w:2 * w] = f_im.astype(BF16)

        y = (jnp.dot(lhs_scr[b], cm_ref[b], preferred_element_type=F32)
             + d_ref[:, b * LANES:(b + 1) * LANES] * up)
        for i in range(ln):
            yn_scr[b, pl.ds(i, nc, stride=ln), :] = y[rows(i), :]
        y_ref[:, b * LANES:(b + 1) * LANES] = _gelu_tanh(yn_scr[b]).astype(y_ref.dtype)


def _s5_prompt(proj, n_prompt, d_skip, pw_re, pw_im, b_mat, c_mat):
    par = SSM_PAR
    nb = SSM_GROUPS // SSM_BLK // par
    tt, w = SCAN_TT, SSM_BLK_STATE
    u_spec = lambda b: pl.BlockSpec((tt, LANES), lambda j, i: (i, par * j + b))
    return pl.pallas_call(
        _s5_prompt_kernel,
        grid=(nb, n_prompt // tt),
        in_specs=[u_spec(b) for b in range(par)] + [
            pl.BlockSpec((1, par * LANES), lambda j, i: (0, j)),
            pl.BlockSpec((SCAN_L, par * w), lambda j, i: (0, j)),
            pl.BlockSpec((SCAN_L, par * w), lambda j, i: (0, j)),
            pl.BlockSpec((par, LANES, 2 * w), lambda j, i: (j, 0, 0)),
            pl.BlockSpec((par, 2 * w, LANES), lambda j, i: (j, 0, 0))],
        out_specs=[pl.BlockSpec((tt, par * LANES), lambda j, i: (i, j)),
                   pl.BlockSpec((1, par * w), lambda j, i: (0, j)),
                   pl.BlockSpec((1, par * w), lambda j, i: (0, j))],
        out_shape=[jax.ShapeDtypeStruct((n_prompt, SSM_WIDTH), BF16),
                   jax.ShapeDtypeStruct((1, SSM_FLAT), F32),
                   jax.ShapeDtypeStruct((1, SSM_FLAT), F32)],
        scratch_shapes=[pltpu.VMEM((par, tt, LANES), F32), pltpu.VMEM((par, tt, 2 * w), F32),
                        pltpu.VMEM((par, tt, 2 * w), BF16),
                        pltpu.VMEM((par, SCAN_NC, w), F32), pltpu.VMEM((par, SCAN_NC, w), F32),
                        pltpu.VMEM((par, 1, w), F32), pltpu.VMEM((par, 1, w), F32),
                        pltpu.VMEM((par, tt, LANES), F32)],
        compiler_params=_cparams(("arbitrary", "arbitrary")),
        name="s5_prompt",
    )(*([proj] * par), d_skip, pw_re, pw_im, b_mat, c_mat)


def _s5_sample_kernel(u_ref, d_ref, pre_ref, pim_ref, bm_ref, cm_ref, s0re_ref, s0im_ref,
                      y_ref, fre_ref, fim_ref, up_scr, bu_scr, lhs_scr, yn_scr, *, n_seq, n_step):
    w = SSM_BLK_STATE
    rb = 16
    for s in range(n_step):
        up_scr[s * n_seq:(s + 1) * n_seq, :] = u_ref[pl.ds(s, n_seq, stride=n_step), :]
    up = up_scr[...]
    bu_scr[...] = jnp.dot(up.astype(BF16), bm_ref[0], preferred_element_type=F32)
    a_re = jnp.broadcast_to(pre_ref[0:1, :], (rb, w))
    a_im = jnp.broadcast_to(pim_ref[0:1, :], (rb, w))

    def seq_block(b, _):
        r0 = pl.multiple_of(b * rb, rb)
        s_re, s_im = s0re_ref[pl.ds(r0, rb), :], s0im_ref[pl.ds(r0, rb), :]
        for s in range(n_step):
            rows = pl.ds(pl.multiple_of(s * n_seq + r0, rb), rb)
            s_re, s_im = _cmul_add(a_re, a_im, s_re, s_im, bu_scr[rows, 0:w], bu_scr[rows, w:2 * w])
            lhs_scr[rows, 0:w] = s_re.astype(BF16)
            lhs_scr[rows, w:2 * w] = s_im.astype(BF16)
        fre_ref[pl.ds(r0, rb), :] = s_re
        fim_ref[pl.ds(r0, rb), :] = s_im
        return 0

    lax.fori_loop(0, n_seq // rb, seq_block, 0)
    y = jnp.dot(lhs_scr[...], cm_ref[0], preferred_element_type=F32) + d_ref[...] * up
    for s in range(n_step):
        yn_scr[pl.ds(s, n_seq, stride=n_step), :] = y[s * n_seq:(s + 1) * n_seq, :]
    y_ref[...] = _gelu_tanh(yn_scr[...]).astype(y_ref.dtype)


def _s5_sample(proj, n_prompt, n_seq, n_step, d_skip, pw_re, pw_im, b_mat, c_mat, s0_re, s0_im):
    nb = SSM_GROUPS // SSM_BLK
    rows, w = n_seq * n_step, SSM_BLK_STATE
    rblk = n_prompt // rows
    return pl.pallas_call(
        functools.partial(_s5_sample_kernel, n_seq=n_seq, n_step=n_step),
        grid=(nb,),
        in_specs=[pl.BlockSpec((rows, LANES), lambda j: (rblk, j)),
                  pl.BlockSpec((1, LANES), lambda j: (0, j)),
                  pl.BlockSpec((SCAN_L, w), lambda j: (0, j)),
                  pl.BlockSpec((SCAN_L, w), lambda j: (0, j)),
                  pl.BlockSpec((1, LANES, 2 * w), lambda j: (j, 0, 0)),
                  pl.BlockSpec((1, 2 * w, LANES), lambda j: (j, 0, 0)),
                  pl.BlockSpec((n_seq, w), lambda j: (0, j)),
                  pl.BlockSpec((n_seq, w), lambda j: (0, j))],
        out_specs=[pl.BlockSpec((rows, LANES), lambda j: (0, j)),
                   pl.BlockSpec((n_seq, w), lambda j: (0, j)),
                   pl.BlockSpec((n_seq, w), lambda j: (0, j))],
        out_shape=[jax.ShapeDtypeStruct((rows, SSM_WIDTH), BF16),
                   jax.ShapeDtypeStruct((n_seq, SSM_FLAT), F32),
                   jax.ShapeDtypeStruct((n_seq, SSM_FLAT), F32)],
        scratch_shapes=[pltpu.VMEM((rows, LANES), F32), pltpu.VMEM((rows, 2 * w), F32),
                        pltpu.VMEM((rows, 2 * w), BF16), pltpu.VMEM((rows, LANES), F32)],
        compiler_params=_cparams(("arbitrary",)),
        name="s5_sample",
    )(proj, d_skip, pw_re, pw_im, b_mat, c_mat, s0_re, s0_im)


def _attn_prompt_kernel(*refs):
    ins, o_ref, scr = refs[:15], refs[15], refs[16:]
    sb = pl.program_id(0)
    scale = HEAD_DIM ** -0.5
    qi = lax.broadcasted_iota(jnp.int32, (BAND, 2 * BAND), 0)
    kj = lax.broadcasted_iota(jnp.int32, (BAND, 2 * BAND), 1)
    dist = qi + BAND - kj
    band_ok = (dist >= 0) & (dist <= BAND)

    for g, (_, dil) in enumerate(DILATION_PATTERNS):
        q_ref, k_ref, v_ref, kp_ref, vp_ref = ins[5 * g:5 * g + 5]
        kbuf, vbuf, o_scr, m_scr, l_scr = scr[5 * g:5 * g + 5]
        pre = BAND * dil
        kbuf[0:pre, :] = kp_ref[...]
        kbuf[pre:pre + ATT_SB, :] = k_ref[...]
        vbuf[0:pre, :] = vp_ref[...]
        vbuf[pre:pre + ATT_SB, :] = v_ref[...]
        nblk = ATT_SB // pre

        def block(idx, _, dil=dil, pre=pre, nblk=nblk, q_ref=q_ref, kbuf=kbuf, vbuf=vbuf,
                  o_scr=o_scr, m_scr=m_scr, l_scr=l_scr):
            r = idx // nblk
            b = idx - r * nblk
            row0 = r + b * pre
            if dil == 1:
                q_rows = pl.ds(pl.multiple_of(row0, BAND), BAND)
                kv_rows = pl.ds(pl.multiple_of(row0, BAND), 2 * BAND)
            else:
                q_rows = pl.ds(row0, BAND, stride=dil)
                kv_rows = pl.ds(row0, 2 * BAND, stride=dil)
            q = (q_ref[q_rows, :] * scale).astype(BF16)
            kw = kbuf[kv_rows, :].astype(BF16)
            vw = vbuf[kv_rows, :].astype(BF16)
            s = lax.dot_general(q, kw, (((1,), (1,)), ((), ())), preferred_element_type=F32)
            s = jnp.where(band_ok & ((kj >= BAND) | (sb > 0) | (b > 0)), s, NEG)
            m = jnp.max(s, axis=-1, keepdims=True)
            p = jnp.exp(s - m)
            l = jnp.sum(p, axis=-1, keepdims=True)
            o = jnp.dot(p.astype(BF16), vw, preferred_element_type=F32)
            o_scr[q_rows, :] = o
            m_scr[q_rows, :] = jnp.broadcast_to(m, (BAND, HEAD_DIM))
            l_scr[q_rows, :] = jnp.broadcast_to(l, (BAND, HEAD_DIM))
            return 0

        lax.fori_loop(0, ATT_SB // BAND, block, 0, unroll=ATT_UNROLL)

    ms = [scr[5 * g + 3][...] for g in range(N_PATTERNS)]
    mx = jnp.maximum(jnp.maximum(ms[0], ms[1]), ms[2])
    num = jnp.zeros((ATT_SB, HEAD_DIM), F32)
    den = jnp.zeros((ATT_SB, HEAD_DIM), F32)
    for g in range(N_PATTERNS):
        wgt = jnp.exp(ms[g] - mx)
        num = num + wgt * scr[5 * g + 2][...]
        den = den + wgt * scr[5 * g + 4][...]
    o_ref[...] = num / den


def _attn_prompt(proj, n_prompt):
    hcol = lambda off, g, j: (off + g * GROUP_WIDTH) // HEAD_DIM + j
    in_specs, scratch = [], []
    for g, (_, dil) in enumerate(DILATION_PATTERNS):
        pre = BAND * dil
        per = ATT_SB // pre
        cur = lambda off, g=g: pl.BlockSpec((ATT_SB, HEAD_DIM), lambda sb, j: (sb, hcol(off, g, j)))
        prev = lambda off, g=g, per=per, pre=pre: pl.BlockSpec(
            (pre, HEAD_DIM), lambda sb, j: (jnp.maximum(sb * per - 1, 0), hcol(off, g, j)))
        in_specs += [cur(OFF_Q), cur(OFF_K), cur(OFF_V), prev(OFF_K), prev(OFF_V)]
        scratch += [pltpu.VMEM((pre + ATT_SB, HEAD_DIM), F32), pltpu.VMEM((pre + ATT_SB, HEAD_DIM), F32),
                    pltpu.VMEM((ATT_SB, HEAD_DIM), F32), pltpu.VMEM((ATT_SB, HEAD_DIM), F32),
                    pltpu.VMEM((ATT_SB, HEAD_DIM), F32)]
    return pl.pallas_call(
        _attn_prompt_kernel,
        grid=(n_prompt // ATT_SB, HEADS_PER_GROUP),
        in_specs=in_specs,
        out_specs=pl.BlockSpec((ATT_SB, HEAD_DIM), lambda sb, j: (sb, j)),
        out_shape=jax.ShapeDtypeStruct((n_prompt, GROUP_WIDTH), F32),
        scratch_shapes=scratch,
        compiler_params=_cparams(("arbitrary", "arbitrary")),
        name="attn_prompt",
    )(*([proj] * 15))


SEQ_PER_STEP = 2
KV_PLANES = 2 * HEADS_PER_GROUP


def _sample_bias(n_step):
    rows = HEADS_PER_GROUP * n_step
    step = np.arange(rows) % n_step
    cache_bias, new_bias = [], []
    for (window, dil) in DILATION_PATTERNS:
        wb = min(window, PAST_LEN)
        band = window // dil
        if dil > n_step:
            res, i = np.meshgrid(np.arange(n_step), np.arange(wb // dil), indexing='ij')
            c = (i * dil + res).reshape(-1)
        else:
            c = np.arange(wb)
        delta = wb + step[:, None] - c[None, :]
        ok = (delta >= 0) & (delta % dil == 0) & (delta // dil <= band)
        cache_bias.append(np.where(ok, 0.0, NEG).astype(np.float32))
        nb = np.full((SEQ_PER_STEP, rows, LANES), NEG, np.float32)
        for a in range(SEQ_PER_STEP):
            for sp in range(n_step):
                dl = step - sp
                okn = (dl >= 0) & (dl % dil == 0) & (dl // dil <= band)
                nb[a, :, a * n_step + sp] = np.where(okn, 0.0, NEG)
        new_bias.append(nb)
    return cache_bias, new_bias


def _attn_sample_kernel(*refs, n_step, cache_pieces):
    (q0, k0, v0, q1, k1, v1, q2, k2, v2) = refs[:9]
    qs, ks, vs = (q0, q1, q2), (k0, k1, k2), (v0, v1, v2)
    refs = list(refs[9:])
    caches = [[refs.pop(0) for _ in range(n)] for n in cache_pieces]
    (cb0, cb1, cb2, nb0, nb1, nb2, o_ref) = refs
    cbias, nbias = (cb0, cb1, cb2), (nb0, nb1, nb2)
    rows = HEADS_PER_GROUP * n_step
    gw = GROUP_WIDTH
    scale = HEAD_DIM ** -0.5
    row_head = lax.broadcasted_iota(jnp.int32, (rows, gw), 0) // n_step
    lane_head = lax.broadcasted_iota(jnp.int32, (rows, gw), 1) // HEAD_DIM
    own_head = row_head == lane_head
    nt = (((1,), (1,)), ((), ()))
    pad = jnp.zeros((LANES - SEQ_PER_STEP * n_step, gw), F32)

    def planes(load):
        k = jnp.concatenate([load(h) for h in range(HEADS_PER_GROUP)], axis=1)
        v = jnp.concatenate([load(HEADS_PER_GROUP + h) for h in range(HEADS_PER_GROUP)], axis=1)
        return k.astype(BF16), v.astype(BF16)

    for a in range(SEQ_PER_STEP):
        pieces = []
        for g, (window, dil) in enumerate(DILATION_PATTERNS):
            q = qs[g][a * n_step:(a + 1) * n_step, :] * scale
            qbd = jnp.where(own_head, jnp.concatenate([q] * HEADS_PER_GROUP, axis=0), 0.0).astype(BF16)
            kvs = []
            for cache in caches[g]:
                if len(cache.shape) == 3:
                    cache = cache.reshape(cache.shape[0] * KV_PLANES, HEAD_DIM)
                per_seq = cache.shape[0] // SEQ_PER_STEP
                kvs.append(planes(lambda p, cache=cache, per_seq=per_seq: cache[
                    pl.ds(a * per_seq + p, per_seq // KV_PLANES, stride=KV_PLANES), :]))
            sc = jnp.concatenate([lax.dot_general(qbd, k, nt, preferred_element_type=F32) for k, _ in kvs], axis=1)
            pieces.append((sc + cbias[g][...], [v for _, v in kvs]))
            k_new = jnp.concatenate([ks[g][...], pad], axis=0).astype(BF16)
            v_new = jnp.concatenate([vs[g][...], pad], axis=0).astype(BF16)
            sn = lax.dot_general(qbd, k_new, nt, preferred_element_type=F32) + nbias[g][a]
            pieces.append((sn, [v_new]))
        m = functools.reduce(jnp.maximum, [jnp.max(s, axis=-1, keepdims=True) for s, _ in pieces])
        l = jnp.zeros((rows, 1), F32)
        acc = jnp.zeros((rows, gw), F32)
        for s, vals in pieces:
            p = jnp.exp(s - m)
            l = l + jnp.sum(p, axis=-1, keepdims=True)
            pb = p.astype(BF16)
            nk = pb.shape[1] // len(vals)
            for r, v in enumerate(vals):
                acc = acc + jnp.dot(pb[:, r * nk:(r + 1) * nk], v, preferred_element_type=F32)
        acc = jnp.where(own_head, acc, 0.0)
        o16 = functools.reduce(lambda x, y: x + y,
                               [acc[:, h * HEAD_DIM:(h + 1) * HEAD_DIM] for h in range(HEADS_PER_GROUP)]) / l
        for h in range(HEADS_PER_GROUP):
            o_ref[a * n_step:(a + 1) * n_step, h * HEAD_DIM:(h + 1) * HEAD_DIM] = o16[h * n_step:(h + 1) * n_step, :]


def _attn_sample(proj, n_prompt, n_seq, n_step, caches):
    rows = SEQ_PER_STEP * n_step
    rblk = n_prompt // rows
    cache_bias, new_bias = _sample_bias(n_step)
    tok = lambda off, g: pl.BlockSpec((rows, GROUP_WIDTH), lambda i: (rblk + i, (off + g * GROUP_WIDTH) // GROUP_WIDTH))
    in_specs, args, cache_pieces = [], [], []
    for g in range(N_PATTERNS):
        in_specs += [tok(OFF_Q, g), tok(OFF_K, g), tok(OFF_V, g)]
        args += [proj, proj, proj]
    for g, (window, dil) in enumerate(DILATION_PATTERNS):
        c = caches[g]
        wb = c.shape[1]
        if dil > n_step:
            assert wb % dil == 0
            c = c.reshape(n_seq * (wb // dil), dil * KV_PLANES, HEAD_DIM)
            for r in range(n_step):
                in_specs.append(pl.BlockSpec((SEQ_PER_STEP * (wb // dil), KV_PLANES, HEAD_DIM),
                                             lambda i, r=r: (i, r, 0)))
                args.append(c)
            cache_pieces.append(n_step)
        else:
            c = c.reshape(n_seq * wb * KV_PLANES, HEAD_DIM)
            in_specs.append(pl.BlockSpec((SEQ_PER_STEP * wb * KV_PLANES, HEAD_DIM), lambda i: (i, 0)))
            args.append(c)
            cache_pieces.append(1)
    for b in cache_bias:
        in_specs.append(pl.BlockSpec(b.shape, lambda i: (0, 0)))
        args.append(jnp.asarray(b))
    for b in new_bias:
        in_specs.append(pl.BlockSpec(b.shape, lambda i: (0, 0, 0)))
        args.append(jnp.asarray(b))
    return pl.pallas_call(
        functools.partial(_attn_sample_kernel, n_step=n_step, cache_pieces=tuple(cache_pieces)),
        grid=(n_seq // SEQ_PER_STEP,),
        in_specs=in_specs,
        out_specs=pl.BlockSpec((rows, GROUP_WIDTH), lambda i: (i, 0)),
        out_shape=jax.ShapeDtypeStruct((n_seq * n_step, GROUP_WIDTH), F32),
        compiler_params=_cparams(("arbitrary",)),
        name="attn_sample",
    )(*args)


def _mix_kernel(yp_ref, ys_ref, op_ref, os_ref, wa_ref, wb_ref, wbr_ref, ga_ref, gb_ref, o_ref,
                wa_bf, wb_bf, wbr_bf, *, n_prompt_tiles):
    i = pl.program_id(1)

    @pl.when(i == 0)
    def _():
        wa_bf[...] = wa_ref[...].astype(BF16)
        wb_bf[...] = wb_ref[...].astype(BF16)
        wbr_bf[...] = wbr_ref[...].astype(BF16)

    is_s = i >= n_prompt_tiles
    for rows in _row_halves(o_ref.shape[0]):
        y = jnp.where(is_s, ys_ref[rows, :], yp_ref[rows, :])
        o = jnp.where(is_s, os_ref[rows, :], op_ref[rows, :]).astype(BF16)
        glu_a = jnp.dot(y, wa_bf[...], preferred_element_type=F32)
        glu_b = jnp.dot(y, wb_bf[...], preferred_element_type=F32)
        branch_a = glu_a * _sigmoid(glu_b)
        branch_b = jnp.dot(o, wbr_bf[...], preferred_element_type=F32)
        o_ref[rows, :] = (_sigmoid(ga_ref[rows, :]) * branch_a
                          + _sigmoid(gb_ref[rows, :]) * branch_b).astype(o_ref.dtype)


def _mix(y_p, y_s, o_p, o_s, w_glu, w_attn_br, proj):
    tp, ts = y_p.shape[0], y_s.shape[0]
    tm, tn = TOK_TILE, 512
    assert OFF_G % tn == 0
    npt, nst = tp // tm, ts // tm
    ncol = D_MODEL // tn
    prow = lambda n, i: (jnp.minimum(i, npt - 1), 0)
    srow = lambda n, i: (jnp.maximum(i - npt, 0), 0)
    return pl.pallas_call(
        functools.partial(_mix_kernel, n_prompt_tiles=npt),
        grid=(ncol, npt + nst),
        in_specs=[pl.BlockSpec((tm, SSM_WIDTH), prow), pl.BlockSpec((tm, SSM_WIDTH), srow),
                  pl.BlockSpec((tm, GROUP_WIDTH), prow), pl.BlockSpec((tm, GROUP_WIDTH), srow),
                  pl.BlockSpec((SSM_WIDTH, tn), lambda n, i: (0, n)),
                  pl.BlockSpec((SSM_WIDTH, tn), lambda n, i: (0, ncol + n)),
                  pl.BlockSpec((GROUP_WIDTH, tn), lambda n, i: (0, n)),
                  pl.BlockSpec((tm, tn), lambda n, i: (i, OFF_G // tn + n)),
                  pl.BlockSpec((tm, tn), lambda n, i: (i, OFF_G // tn + ncol + n))],
        out_specs=pl.BlockSpec((tm, tn), lambda n, i: (i, n)),
        out_shape=jax.ShapeDtypeStruct((tp + ts, D_MODEL), BF16),
        scratch_shapes=[pltpu.VMEM((SSM_WIDTH, tn), BF16), pltpu.VMEM((SSM_WIDTH, tn), BF16),
                        pltpu.VMEM((GROUP_WIDTH, tn), BF16)],
        compiler_params=_cparams(("arbitrary", "arbitrary")),
        name="glu_mix",
    )(y_p, y_s, o_p, o_s, w_glu, w_glu, w_attn_br, proj, proj)


def _route(logits):
    lane = lax.broadcasted_iota(jnp.int32, logits.shape, 1).astype(F32)
    big = 1000.0
    first = lambda cond: jnp.min(jnp.where(cond, lane, big), axis=-1, keepdims=True)
    is_g = lane < N_EXPERT_GROUPS
    lg = jnp.where(is_g, logits, NEG)
    mg = jnp.max(lg, axis=-1, keepdims=True)
    g_sel = first(lg == mg)
    p_group = 1.0 / jnp.sum(jnp.where(is_g, jnp.exp(lg - mg), 0.0), axis=-1, keepdims=True)
    e_lo = N_EXPERT_GROUPS + EXPERTS_PER_GROUP * g_sel
    le = jnp.where((lane >= e_lo) & (lane < e_lo + EXPERTS_PER_GROUP), logits, NEG)
    v1 = jnp.max(le, axis=-1, keepdims=True)
    i1 = first(le == v1)
    le2 = jnp.where(lane == i1, NEG, le)
    v2 = jnp.max(le2, axis=-1, keepdims=True)
    i2 = first(le2 == v2)
    e2 = jnp.exp(v2 - v1)
    w1 = p_group / (1.0 + e2)
    w2 = p_group * e2 / (1.0 + e2)
    pick = lambda k, val: jnp.where(lane == k, val, 0.0)
    return (pick(ROUTE_E1, i1 - N_EXPERT_GROUPS) + pick(ROUTE_E2, i2 - N_EXPERT_GROUPS)
            + pick(ROUTE_W1, w1) + pick(ROUTE_W2, w2))


def _outproj_kernel(mix_ref, w_ref, xp_ref, xs_ref, g_ref, wr_ref, br_ref,
                    gtp_ref, scp_ref, shp_ref, gts_ref, scs_ref, shs_ref,
                    x1_ref, h2_ref, route_ref, *, n_prompt_tiles):
    is_s = pl.program_id(0) >= n_prompt_tiles

    def split(v):
        high = v.astype(BF16)
        return high, (v - high.astype(F32)).astype(BF16)

    x = jnp.where(is_s, xs_ref[...], xp_ref[...])
    gt = jnp.where(is_s, gts_ref[...], gtp_ref[0:1, :])
    sc = jnp.where(is_s, scs_ref[...], scp_ref[0:1, :])
    sh = jnp.where(is_s, shs_ref[...], shp_ref[0:1, :])
    x1 = x + gt * jnp.dot(mix_ref[...], w_ref[...], preferred_element_type=F32)
    x1_ref[...] = x1
    ms = jnp.mean(x1 * x1, axis=-1, keepdims=True)
    h2 = (x1 * lax.rsqrt(ms + EPS) * g_ref[...]) * (1.0 + sc) + sh
    h2_ref[...] = h2.astype(h2_ref.dtype)
    r = jnp.dot(jnp.concatenate(split(h2), axis=0), jnp.concatenate(split(wr_ref[...]), axis=1),
                preferred_element_type=F32)
    n = h2.shape[0]
    logits = (r[:n, :LANES] + r[:n, LANES:]) + (r[n:, :LANES] + r[n:, LANES:]) + br_ref[...]
    route_ref[...] = _route(logits)


def _outproj(mixed, w_out_bf, x_p, x_s, g2, w_router, b_router, mod):
    tp, ts = x_p.shape[0], x_s.shape[0]
    tm = SMALL_TOK_TILE
    npt, nst = tp // tm, ts // tm
    prow = lambda i: (jnp.minimum(i, npt - 1), 0)
    srow = lambda i: (jnp.maximum(i - npt, 0), 0)
    const = lambda i: (0, 0)
    vec = pl.BlockSpec((1, D_MODEL), const)
    gt_p, gt_s = _mod_specs(MOD_GT1, tm, ts, npt)
    sc_p, sc_s = _mod_specs(MOD_SC2, tm, ts, npt)
    sh_p, sh_s = _mod_specs(MOD_SH2, tm, ts, npt)
    full = lambda i: (i, 0)
    return pl.pallas_call(
        functools.partial(_outproj_kernel, n_prompt_tiles=npt),
        grid=(npt + nst,),
        in_specs=[pl.BlockSpec((tm, D_MODEL), full), pl.BlockSpec((D_MODEL, D_MODEL), const),
                  pl.BlockSpec((tm, D_MODEL), prow), pl.BlockSpec((tm, D_MODEL), srow),
                  vec, pl.BlockSpec((D_MODEL, LANES), const), pl.BlockSpec((1, LANES), const),
                  gt_p, sc_p, sh_p, gt_s, sc_s, sh_s],
        out_specs=[pl.BlockSpec((tm, D_MODEL), full), pl.BlockSpec((tm, D_MODEL), full),
                   pl.BlockSpec((tm, LANES), full)],
        out_shape=[jax.ShapeDtypeStruct((tp + ts, D_MODEL), F32),
                   jax.ShapeDtypeStruct((tp + ts, D_MODEL), F32),
                   jax.ShapeDtypeStruct((tp + ts, LANES), F32)],
        compiler_params=_cparams(("arbitrary",)),
        name="out_proj_norm2_router",
    )(mixed, w_out_bf, x_p, x_s, g2, w_router, b_router, mod, mod, mod, mod, mod, mod)


def _dispatch_plan(route, tm):
    e = route[:, ROUTE_E1:ROUTE_E2 + 1].astype(jnp.int32).reshape(-1)
    n_pairs = e.shape[0]
    onehot = (e[:, None] == jnp.arange(N_EXPERTS, dtype=jnp.int32)[None, :]).astype(jnp.int32)
    csum = jnp.cumsum(onehot, axis=0)
    rank = jnp.sum(onehot * csum, axis=1) - 1
    tiles_per_expert = (csum[-1] + tm - 1) // tm
    tile_end = jnp.cumsum(tiles_per_expert)
    tile_start = tile_end - tiles_per_expert
    dest = (tile_start[e] * tm + rank).astype(jnp.int32)
    max_tiles = n_pairs // tm + N_EXPERTS
    k = jnp.arange(max_tiles, dtype=jnp.int32)
    tile_expert = jnp.minimum(jnp.sum((k[:, None] >= tile_end[None, :]).astype(jnp.int32), axis=1), N_EXPERTS - 1)
    n_used = tile_end[-1].astype(jnp.int32)
    last_expert = jnp.take(tile_expert, n_used - 1)
    tile_expert = jnp.where(k < n_used, tile_expert, last_expert).astype(jnp.int32)
    pad_end = (jnp.concatenate([tile_end, tile_end[-1:]]) * tm).astype(jnp.int32)
    pad_len = (tiles_per_expert * tm - csum[-1]).astype(jnp.int32)
    ids = jnp.arange(N_EXPERTS, dtype=jnp.int32)
    has_tiles = jnp.where(tiles_per_expert > 0, ids, N_EXPERTS)
    following = jnp.min(jnp.where(ids[None, :] > ids[:, None], has_tiles[None, :], N_EXPERTS), axis=1)
    next_expert = jnp.where(following < N_EXPERTS, following, -1)[tile_expert].astype(jnp.int32)
    return dest, pad_end, pad_len, tile_expert, n_used.reshape(1), next_expert, max_tiles


def _start_pair_copies(dest_ref, tile, rows, make):
    def body(r, _):
        for k in range(2):
            make(r, k, dest_ref[(tile * rows + r) * 2 + k]).start()
        return 0
    lax.fori_loop(0, rows, body, 0, unroll=8)


def _pair_copies(dest_ref, tile, rows, make, make_all):
    _start_pair_copies(dest_ref, tile, rows, make)
    for k in range(2):
        make_all(k).wait()


def _dispatch_kernel(dest_ref, pad_end_ref, pad_len_ref, h_ref, xs_ref, zero_buf, sem, pad_sem):
    rows = h_ref.shape[0]

    @pl.when(pl.program_id(0) == 0)
    def _():
        zero_buf[...] = jnp.zeros_like(zero_buf)

        def pad_copies(fn):
            for e in range(N_EXPERTS):
                end, left = pad_end_ref[e], pad_len_ref[e]
                size = zero_buf.shape[0]
                while size >= SUBLANES:
                    take = (left & size) != 0
                    end = end - jnp.where(take, size, 0)

                    @pl.when(take)
                    def _(start=end, size=size):
                        fn(pltpu.make_async_copy(zero_buf.at[pl.ds(0, size)],
                                                 xs_ref.at[pl.ds(pl.multiple_of(start, size), size)], pad_sem))

                    size //= 2
                for r in range(1, SUBLANES):
                    @pl.when((left & (SUBLANES - 1)) >= r)
                    def _(row=end - r):
                        fn(pltpu.make_async_copy(zero_buf.at[pl.ds(0, 1)], xs_ref.at[pl.ds(row, 1)], pad_sem))
            size = zero_buf.shape[0]
            tail = pad_end_ref[N_EXPERTS]
            for t in range(N_EXPERTS * MOE_TM // size):
                @pl.when(tail + t * size < xs_ref.shape[0])
                def _(start=tail + t * size):
                    fn(pltpu.make_async_copy(zero_buf, xs_ref.at[pl.ds(pl.multiple_of(start, size), size)], pad_sem))

        pad_copies(lambda c: c.start())
        pad_copies(lambda c: c.wait())

    make = lambda r, k, d: pltpu.make_async_copy(h_ref.at[pl.ds(r, 1)], xs_ref.at[pl.ds(d, 1)], sem)
    make_all = lambda k: pltpu.make_async_copy(h_ref, xs_ref.at[pl.ds(0, rows)], sem)
    _pair_copies(dest_ref, pl.program_id(0), rows, make, make_all)


def _dispatch(dest, pad_end, pad_len, h2, n_slots):
    n_tok = h2.shape[0]
    tm = SMALL_TOK_TILE
    return pl.pallas_call(
        _dispatch_kernel,
        grid_spec=pltpu.PrefetchScalarGridSpec(
            num_scalar_prefetch=3, grid=(n_tok // tm,),
            in_specs=[pl.BlockSpec((tm, D_MODEL), lambda i, d, ps, pn: (i, 0))],
            out_specs=pl.BlockSpec(memory_space=pl.ANY),
            scratch_shapes=[pltpu.VMEM((MOE_TM // 2, D_MODEL), F32),
                            pltpu.SemaphoreType.DMA(()), pltpu.SemaphoreType.DMA(())]),
        out_shape=jax.ShapeDtypeStruct((n_slots, D_MODEL), F32),
        compiler_params=_cparams(("arbitrary",)),
        name="moe_dispatch",
    )(dest, pad_end, pad_len, h2)


def _experts_kernel(te_ref, used_ref, next_ref, xs_ref, wgu_hbm, wd_hbm, y_ref, wgu_f32, wd_f32, wgu_bf, wd_bf, sem):
    k = pl.program_id(0)
    new_expert = (k == 0) | (te_ref[k] != te_ref[jnp.maximum(k - 1, 0)])

    def fetch(e):
        return (pltpu.make_async_copy(wgu_hbm.at[e], wgu_f32, sem.at[0]),
                pltpu.make_async_copy(wd_hbm.at[e], wd_f32, sem.at[1]))

    @pl.when(k == 0)
    def _():
        for c in fetch(te_ref[0]):
            c.start()

    @pl.when(new_expert)
    def _():
        for c in fetch(te_ref[k]):
            c.wait()
        wgu_bf[...] = wgu_f32[...].astype(BF16)
        wd_bf[...] = wd_f32[...].astype(BF16)

    @pl.when(new_expert & (next_ref[k] >= 0))
    def _():
        for c in fetch(next_ref[k]):
            c.start()

    @pl.when(k < used_ref[0])
    def _():
        gu = jnp.dot(xs_ref[...].astype(BF16), wgu_bf[...], preferred_element_type=F32)
        gate, up = gu[:, :EXPERT_FF], gu[:, EXPERT_FF:]
        act = (gate * _sigmoid(gate)) * up
        y_ref[...] = jnp.dot(act.astype(BF16), wd_bf[...], preferred_element_type=F32)

    @pl.when(k >= used_ref[0])
    def _():
        y_ref[...] = jnp.zeros_like(y_ref)


def _experts(tile_expert, n_used, next_expert, xs, w_gu, w_down, max_tiles):
    tm = MOE_TM
    row = lambda k, te, nu, nx: (jnp.minimum(k, nu[0] - 1), 0)
    return pl.pallas_call(
        _experts_kernel,
        grid_spec=pltpu.PrefetchScalarGridSpec(
            num_scalar_prefetch=3, grid=(max_tiles,),
            in_specs=[pl.BlockSpec((tm, D_MODEL), row),
                      pl.BlockSpec(memory_space=pl.ANY), pl.BlockSpec(memory_space=pl.ANY)],
            out_specs=pl.BlockSpec((tm, D_MODEL), lambda k, te, nu, nx: (k, 0)),
            scratch_shapes=[pltpu.VMEM((D_MODEL, 2 * EXPERT_FF), F32), pltpu.VMEM((EXPERT_FF, D_MODEL), F32),
                            pltpu.VMEM((D_MODEL, 2 * EXPERT_FF), BF16), pltpu.VMEM((EXPERT_FF, D_MODEL), BF16),
                            pltpu.SemaphoreType.DMA((2,))]),
        out_shape=jax.ShapeDtypeStruct((max_tiles * tm, D_MODEL), F32),
        compiler_params=_cparams(("arbitrary",)),
        name="moe_experts",
    )(tile_expert, n_used, next_expert, xs, w_gu, w_down)


def _combine_kernel(dest_ref, y_hbm, x1_ref, route_ref, gtp_ref, gts_ref, yp_ref, ys_ref, buf, sem, *, n_prompt_tiles):
    i = pl.program_id(0)
    rows = x1_ref.shape[0]

    def gather(tile, slot):
        make = lambda r, k, d: pltpu.make_async_copy(y_hbm.at[pl.ds(d, 1)], buf.at[slot, k, pl.ds(r, 1)],
                                                     sem.at[slot])
        _start_pair_copies(dest_ref, tile, rows, make)

    @pl.when(i == 0)
    def _():
        gather(0, 0)

    slot = lax.rem(i, 2)

    @pl.when(i + 1 < pl.num_programs(0))
    def _():
        gather(i + 1, 1 - slot)

    for k in range(2):
        pltpu.make_async_copy(y_hbm.at[pl.ds(0, rows)], buf.at[slot, k], sem.at[slot]).wait()
    route = route_ref[...]
    lane = lax.broadcasted_iota(jnp.int32, route.shape, 1)
    w1 = jnp.sum(jnp.where(lane == ROUTE_W1, route, 0.0), axis=-1, keepdims=True)
    w2 = jnp.sum(jnp.where(lane == ROUTE_W2, route, 0.0), axis=-1, keepdims=True)
    moe = w1 * buf[slot, 0] + w2 * buf[slot, 1]

    @pl.when(i < n_prompt_tiles)
    def _():
        yp_ref[...] = x1_ref[...] + gtp_ref[0:1, :] * moe

    @pl.when(i >= n_prompt_tiles)
    def _():
        ys_ref[...] = x1_ref[...] + gts_ref[...] * moe


def _combine(dest, y_slots, x1, route, mod, tp, ts):
    tm = SMALL_TOK_TILE
    npt, nst = tp // tm, ts // tm
    row = lambda i, d: (i, 0)
    gt_p, gt_s = _mod_specs(MOD_GT2, tm, ts, npt)
    return pl.pallas_call(
        functools.partial(_combine_kernel, n_prompt_tiles=npt),
        grid_spec=pltpu.PrefetchScalarGridSpec(
            num_scalar_prefetch=1, grid=(npt + nst,),
            in_specs=[pl.BlockSpec(memory_space=pl.ANY),
                      pl.BlockSpec((tm, D_MODEL), row), pl.BlockSpec((tm, LANES), row), gt_p, gt_s],
            out_specs=[pl.BlockSpec((tm, D_MODEL), lambda i, d: (jnp.minimum(i, npt - 1), 0)),
                       pl.BlockSpec((tm, D_MODEL), lambda i, d: (jnp.maximum(i - npt, 0), 0))],
            scratch_shapes=[pltpu.VMEM((2, 2, tm, D_MODEL), F32), pltpu.SemaphoreType.DMA((2,))]),
        out_shape=[jax.ShapeDtypeStruct((tp, D_MODEL), F32), jax.ShapeDtypeStruct((ts, D_MODEL), F32)],
        compiler_params=_cparams(("arbitrary",)),
        name="moe_combine",
    )(dest, y_slots, x1, route, mod, mod)


def _moe(h2, route, x1, w_gu, w_down, mod, tp, ts):
    dest, pad_end, pad_len, tile_expert, n_used, next_expert, max_tiles = _dispatch_plan(route, MOE_TM)
    xs = _dispatch(dest, pad_end, pad_len, h2, max_tiles * MOE_TM)
    y_slots = _experts(tile_expert, n_used, next_expert, xs, w_gu, w_down, max_tiles)
    return _combine(dest, y_slots, x1, route, mod, tp, ts)


def _kv_pack_kernel(k_ref, v_ref, o_ref):
    n = k_ref.shape[0]
    for h in range(HEADS_PER_GROUP):
        cols = slice(h * HEAD_DIM, (h + 1) * HEAD_DIM)
        o_ref[pl.ds(h, n, stride=KV_PLANES), :] = k_ref[:, cols]
        o_ref[pl.ds(HEADS_PER_GROUP + h, n, stride=KV_PLANES), :] = v_ref[:, cols]


def _kv_pack(proj, row0, rows, g):
    tr = min(rows, SMALL_TOK_TILE)
    assert rows % tr == 0 and row0 % tr == 0
    col = lambda off: (off + g * GROUP_WIDTH) // GROUP_WIDTH
    spec = lambda off: pl.BlockSpec((tr, GROUP_WIDTH), lambda i: (row0 // tr + i, col(off)))
    flat = pl.pallas_call(
        _kv_pack_kernel,
        grid=(rows // tr,),
        in_specs=[spec(OFF_K), spec(OFF_V)],
        out_specs=pl.BlockSpec((tr * KV_PLANES, HEAD_DIM), lambda i: (i, 0)),
        out_shape=jax.ShapeDtypeStruct((rows * KV_PLANES, HEAD_DIM), F32),
        compiler_params=_cparams(("arbitrary",)),
        name="kv_pack",
    )(proj, proj)
    return flat.reshape(rows, 2, HEADS_PER_GROUP, HEAD_DIM)


def _rope_tables(n_prompt, n_seq, n_step):
    half = ROT_DIM // 2
    inv_freq = ROPE_THETA ** (-jnp.arange(half, dtype=F32) / half)
    angles = lambda pos: pos.astype(F32)[:, None] * inv_freq[None, :]
    fine = 128
    assert n_prompt % fine == 0
    coarse_ang = angles(jnp.arange(n_prompt // fine, dtype=jnp.int32) * fine)[:, None, :]
    fine_ang = angles(jnp.arange(fine, dtype=jnp.int32))[None, :, :]
    cc, sc, cf, sf = jnp.cos(coarse_ang), jnp.sin(coarse_ang), jnp.cos(fine_ang), jnp.sin(fine_ang)
    cos_p = (cc * cf - sc * sf).reshape(n_prompt, half)
    sin_p = (sc * cf + cc * sf).reshape(n_prompt, half)
    step_ang = angles(PAST_LEN + jnp.arange(n_step, dtype=jnp.int32))
    cos = jnp.concatenate([cos_p, jnp.tile(jnp.cos(step_ang), (n_seq, 1))], axis=0)
    sin = jnp.concatenate([sin_p, jnp.tile(jnp.sin(step_ang), (n_seq, 1))], axis=0)
    n = n_prompt + n_seq * n_step
    one = jnp.ones((n, HEAD_DIM - ROT_DIM), F32)
    zero = jnp.zeros((n, HEAD_DIM - ROT_DIM), F32)
    zh = jnp.zeros((n, half), F32)
    rc = jnp.concatenate([cos, cos, one], axis=1)
    rs1 = jnp.concatenate([-sin, zh, zero], axis=1)
    rs2 = jnp.concatenate([zh, sin, zero], axis=1)
    return rc, rs1, rs2


def kernel(x_prompt, x_sample, cache_kv_w128, cache_kv_w512, cache_kv_w2048, state_ssm_re, state_ssm_im,
           c_prompt, c_sample, w_ada, b_ada, norm1_g, norm2_g, w_in, ssm_a_re, ssm_a_im, ssm_log_dt,
           ssm_b_re, ssm_b_im, ssm_c_re, ssm_c_im, ssm_d, w_glu, q_norm_g, k_norm_g, w_attn_br, w_out,
           w_router_group, b_router_group, w_router_expert, b_router_expert, w_expert_gate_up, w_expert_down):
    assert x_prompt.shape[0] == 1 and w_ada.shape[0] == 1
    n_prompt = x_prompt.shape[1]
    n_seq, n_step = x_sample.shape[0], x_sample.shape[1]
    n_samp = n_seq * n_step
    assert n_samp % TOK_TILE == 0 and n_prompt % ATT_SB == 0 and (n_prompt + n_samp) % PROJ_TM == 0
    x_p = x_prompt.reshape(n_prompt, D_MODEL)
    x_s = x_sample.reshape(n_samp, D_MODEL)

    c_all = jnp.concatenate([jnp.repeat(c_sample, n_step, axis=0),
                             jnp.broadcast_to(c_prompt, (SUBLANES, D_MODEL))], axis=0)
    mod = _ada(c_all, w_ada[0], b_ada[0])

    h1 = _modnorm(x_p, x_s, norm1_g[0].reshape(1, D_MODEL), mod)
    rc, rs1, rs2 = _rope_tables(n_prompt, n_seq, n_step)
    proj = _inproj(h1, w_in[0], rc, rs1, rs2, q_norm_g[0].reshape(1, HEAD_DIM), k_norm_g[0].reshape(1, HEAD_DIM))

    pw_re, pw_im, bb_re, bb_im = _ssm_prep(ssm_a_re[0], ssm_a_im[0], ssm_log_dt[0], ssm_b_re[0], ssm_b_im[0])
    b_mat, c_mat = _ssm_block_matrices(bb_re, bb_im, ssm_c_re[0], ssm_c_im[0])
    d_skip = ssm_d[0].reshape(1, SSM_WIDTH)
    yg_p, fre_p, fim_p = _s5_prompt(proj, n_prompt, d_skip, pw_re, pw_im, b_mat, c_mat)
    yg_s, fre_s, fim_s = _s5_sample(proj, n_prompt, n_seq, n_step, d_skip, pw_re, pw_im, b_mat, c_mat,
                                    state_ssm_re[0].reshape(n_seq, SSM_FLAT), state_ssm_im[0].reshape(n_seq, SSM_FLAT))

    o_p = _attn_prompt(proj, n_prompt)
    o_s = _attn_sample(proj, n_prompt, n_seq, n_step, (cache_kv_w128[0], cache_kv_w512[0], cache_kv_w2048[0]))

    mixed = _mix(yg_p, yg_s, o_p, o_s, w_glu[0], w_attn_br[0], proj)

    w_router = jnp.concatenate([w_router_group[0], w_router_expert[0],
                                jnp.zeros((D_MODEL, LANES - N_EXPERT_GROUPS - N_EXPERTS), F32)], axis=1)
    b_router = jnp.concatenate([b_router_group[0], b_router_expert[0],
                                jnp.zeros((LANES - N_EXPERT_GROUPS - N_EXPERTS,), F32)]).reshape(1, LANES)
    x1, h2, route = _outproj(mixed, w_out[0].astype(BF16), x_p, x_s, norm2_g[0].reshape(1, D_MODEL),
                             w_router, b_router, mod)
    y_p, y_s = _moe(h2, route, x1, w_expert_gate_up[0], w_expert_down[0], mod, n_prompt, n_samp)

    kv_p, kv_s = [], []
    for g, (window, _) in enumerate(DILATION_PATTERNS):
        keep = min(window, n_prompt)
        kv_p.append(_kv_pack(proj, n_prompt - keep, keep, g).reshape(1, 1, keep, 2, HEADS_PER_GROUP, HEAD_DIM))
        kv_s.append(_kv_pack(proj, n_prompt, n_samp, g).reshape(1, n_seq, n_step, 2, HEADS_PER_GROUP, HEAD_DIM))

    state_shape_p = (1, 1, SSM_GROUPS, SSM_STATE)
    state_shape_s = (1, n_seq, SSM_GROUPS, SSM_STATE)
    return (y_p.reshape(1, n_prompt, D_MODEL), y_s.reshape(n_seq, n_step, D_MODEL),
            kv_p[0], kv_p[1], kv_p[2], fre_p.reshape(state_shape_p), fim_p.reshape(state_shape_p),
            kv_s[0], kv_s[1], kv_s[2], fre_s.reshape(state_shape_s), fim_s.reshape(state_shape_s))
```

```python
import functools
import math

import numpy as np
import jax
import jax.numpy as jnp
from jax import lax
from jax.experimental import pallas as pl
from jax.experimental.pallas import tpu as pltpu

F32 = jnp.float32
BF16 = jnp.bfloat16

D_MODEL = 2048
PAST_LEN = 2048
SSM_WIDTH = D_MODEL // 2
SSM_GROUP = 16
SSM_GROUPS = SSM_WIDTH // SSM_GROUP
SSM_STATE = 64
SSM_FLAT = SSM_GROUPS * SSM_STATE
HEAD_DIM = 128
DILATION_PATTERNS = ((128, 1), (512, 4), (2048, 16))
N_PATTERNS = 3
HEADS_PER_GROUP = 4
GROUP_WIDTH = HEADS_PER_GROUP * HEAD_DIM
ATTN_WIDTH = N_PATTERNS * GROUP_WIDTH
ROT_DIM = HEAD_DIM // 4
ROPE_THETA = 500000.0
OFF_Q = SSM_WIDTH
OFF_K = OFF_Q + ATTN_WIDTH
OFF_V = OFF_K + ATTN_WIDTH
OFF_G = OFF_V + ATTN_WIDTH
IN_COLS = OFF_G + 2 * D_MODEL
N_EXPERT_GROUPS = 4
EXPERTS_PER_GROUP = 4
N_EXPERTS = 16
EXPERT_FF = D_MODEL // 4
EPS = 1e-6
NEG = -1e30

LANES = 128
SUBLANES = 8
VMEM_LIMIT = 56 * 1024 * 1024

TOK_TILE = 512
SMALL_TOK_TILE = 256
PROJ_TM = 544
PROJ_TN = 2432
SCAN_L = 32
SCAN_NC = 16
SCAN_TT = SCAN_L * SCAN_NC
SSM_BLK = 8
SSM_BLK_STATE = SSM_BLK * SSM_STATE
SSM_PAR = 4
MOE_TM = 256
ROUTE_E1, ROUTE_E2, ROUTE_W1, ROUTE_W2 = 0, 1, 2, 3
MOD_SH1, MOD_SC1, MOD_GT1, MOD_SH2, MOD_SC2, MOD_GT2 = range(6)
ATT_SB = 2048
BAND = 128
ATT_UNROLL = 16


def _cparams(sem, vmem=VMEM_LIMIT):
    return pltpu.CompilerParams(dimension_semantics=sem, vmem_limit_bytes=vmem)


def _sigmoid(x):
    return 0.5 * jnp.tanh(0.5 * x) + 0.5


def _row_halves(rows):
    half = rows // 2
    return (slice(0, half), slice(half, rows))


def _gelu_tanh(x):
    c = math.sqrt(2.0 / math.pi)
    return 0.5 * x * (1.0 + jnp.tanh(c * (x + 0.044715 * (x * x * x))))


def _ada_kernel(c_ref, w_ref, b_ref, o_ref, cs_ref):
    @pl.when(pl.program_id(0) == 0)
    def _():
        c = c_ref[...]
        cs_ref[...] = (c * _sigmoid(c)).astype(BF16)

    o_ref[...] = jnp.dot(cs_ref[...], w_ref[...].astype(BF16), preferred_element_type=F32) + b_ref[...]


def _ada(c_all, w_ada, b_ada):
    rows = c_all.shape[0]
    n_out = w_ada.shape[1]
    tn = 1024
    return pl.pallas_call(
        _ada_kernel,
        grid=(n_out // tn,),
        in_specs=[pl.BlockSpec((rows, D_MODEL), lambda n: (0, 0)),
                  pl.BlockSpec((D_MODEL, tn), lambda n: (0, n)),
                  pl.BlockSpec((1, tn), lambda n: (0, n))],
        out_specs=pl.BlockSpec((rows, tn), lambda n: (0, n)),
        out_shape=jax.ShapeDtypeStruct((rows, n_out), F32),
        scratch_shapes=[pltpu.VMEM((rows, D_MODEL), BF16)],
        compiler_params=_cparams(("arbitrary",)),
        name="ada_mod",
    )(c_all, w_ada, b_ada.reshape(1, n_out))


def _modnorm_kernel(xp_ref, xs_ref, g_ref, scp_ref, shp_ref, scs_ref, shs_ref, o_ref, *, n_prompt_tiles):
    is_s = pl.program_id(0) >= n_prompt_tiles

    def norm(x_ref, sc_ref, sh_ref, rows):
        x = x_ref[...]
        ms = jnp.mean(x * x, axis=-1, keepdims=True)
        y = x * lax.rsqrt(ms + EPS) * g_ref[...]
        o_ref[...] = (y * (1.0 + sc_ref[rows, :]) + sh_ref[rows, :]).astype(o_ref.dtype)

    pl.when(jnp.logical_not(is_s))(functools.partial(norm, xp_ref, scp_ref, shp_ref, slice(0, 1)))
    pl.when(is_s)(functools.partial(norm, xs_ref, scs_ref, shs_ref, slice(None)))


def _mod_specs(k, tm, n_samp, n_prompt_tiles):
    prompt = pl.BlockSpec((SUBLANES, D_MODEL), lambda i, *_: (n_samp // SUBLANES, k))
    sample = pl.BlockSpec((tm, D_MODEL), lambda i, *_: (jnp.maximum(i - n_prompt_tiles, 0), k))
    return prompt, sample


def _modnorm(x_p, x_s, g, mod):
    tp, ts = x_p.shape[0], x_s.shape[0]
    tm = TOK_TILE
    npt, nst = tp // tm, ts // tm
    row = lambda i: (jnp.minimum(i, npt - 1), 0)
    srow = lambda i: (jnp.maximum(i - npt, 0), 0)
    const = lambda i: (0, 0)
    sc_p, sc_s = _mod_specs(MOD_SC1, tm, ts, npt)
    sh_p, sh_s = _mod_specs(MOD_SH1, tm, ts, npt)
    return pl.pallas_call(
        functools.partial(_modnorm_kernel, n_prompt_tiles=npt),
        grid=(npt + nst,),
        in_specs=[pl.BlockSpec((tm, D_MODEL), row), pl.BlockSpec((tm, D_MODEL), srow),
                  pl.BlockSpec((1, D_MODEL), const), sc_p, sh_p, sc_s, sh_s],
        out_specs=pl.BlockSpec((tm, D_MODEL), lambda i: (i, 0)),
        out_shape=jax.ShapeDtypeStruct((tp + ts, D_MODEL), BF16),
        compiler_params=_cparams(("arbitrary",)),
        name="modnorm1",
    )(x_p, x_s, g, mod, mod, mod, mod)


def _inproj_kernel(h_ref, w_hbm, rc_ref, rs1_ref, rs2_ref, qg_ref, kg_ref, o_ref, wf32_ref, wbf_ref, sem, *,
                   heads_per_tile):
    n, m = pl.program_id(0), pl.program_id(1)
    tn = heads_per_tile * HEAD_DIM
    fetch = lambda t: pltpu.make_async_copy(w_hbm.at[:, pl.ds(pl.multiple_of(t * tn, LANES), tn)], wf32_ref, sem)

    @pl.when((m == 0) & (n == 0))
    def _():
        fetch(0).start()

    @pl.when(m == 0)
    def _():
        fetch(n).wait()
        wbf_ref[...] = wf32_ref[...].astype(BF16)

    @pl.when((m == 0) & (n + 1 < pl.num_programs(0)))
    def _():
        fetch(n + 1).start()

    half = ROT_DIM // 2
    q_heads = range(OFF_Q // HEAD_DIM, OFF_K // HEAD_DIM)
    k_heads = range(OFF_K // HEAD_DIM, OFF_V // HEAD_DIM)
    pair = 2 * HEAD_DIM

    def tile(col_tile):
        h = h_ref[...]
        for c0 in range(0, heads_per_tile * HEAD_DIM, pair):
            width = min(pair, heads_per_tile * HEAD_DIM - c0)
            acc = jnp.dot(h, wbf_ref[:, c0:c0 + width], preferred_element_type=F32)
            for c in range(c0, c0 + width, HEAD_DIM):
                slot = col_tile * heads_per_tile + c // HEAD_DIM
                x = acc[:, c - c0:c - c0 + HEAD_DIM]
                if slot in q_heads or slot in k_heads:
                    gain = qg_ref[...] if slot in q_heads else kg_ref[...]
                    ms = jnp.mean(x * x, axis=-1, keepdims=True)
                    y = x * lax.rsqrt(ms + EPS) * gain
                    up = pltpu.roll(y, HEAD_DIM - half, 1)
                    dn = pltpu.roll(y, half, 1)
                    x = y * rc_ref[...] + up * rs1_ref[...] + dn * rs2_ref[...]
                o_ref[:, c:c + HEAD_DIM] = x

    for col_tile in range(IN_COLS // (heads_per_tile * HEAD_DIM)):
        pl.when(n == col_tile)(functools.partial(tile, col_tile))


def _inproj(h, w_in, rc, rs1, rs2, qg, kg):
    n_tok = h.shape[0]
    tm, tn = PROJ_TM, PROJ_TN
    tab = pl.BlockSpec((tm, HEAD_DIM), lambda n, m: (m, 0))
    gain = pl.BlockSpec((1, HEAD_DIM), lambda n, m: (0, 0))
    return pl.pallas_call(
        functools.partial(_inproj_kernel, heads_per_tile=tn // HEAD_DIM),
        grid=(IN_COLS // tn, n_tok // tm),
        in_specs=[pl.BlockSpec((tm, D_MODEL), lambda n, m: (m, 0)),
                  pl.BlockSpec(memory_space=pl.ANY),
                  tab, tab, tab, gain, gain],
        out_specs=pl.BlockSpec((tm, tn), lambda n, m: (m, n)),
        out_shape=jax.ShapeDtypeStruct((n_tok, IN_COLS), F32),
        scratch_shapes=[pltpu.VMEM((D_MODEL, tn), F32), pltpu.VMEM((D_MODEL, tn), BF16),
                        pltpu.SemaphoreType.DMA(())],
        compiler_params=_cparams(("arbitrary", "arbitrary")),
        name="in_proj",
    )(h, w_in, rc, rs1, rs2, qg, kg)


def _ssm_prep_kernel(are_ref, aim_ref, ldt_ref, arer_ref, aimr_ref, ldtr_ref, bre_ref, bim_ref,
                     pre_ref, pim_ref, bbre_ref, bbim_ref):
    def discretise(a_re, a_im, log_dt):
        dt = jnp.exp(log_dt)
        mag = jnp.exp(a_re * dt)
        return mag * jnp.cos(a_im * dt), mag * jnp.sin(a_im * dt)

    ab_re, ab_im = discretise(are_ref[...], aim_ref[...], ldt_ref[...])
    p_re, p_im = ab_re, ab_im
    for i in range(SCAN_L):
        pre_ref[i:i + 1, :] = p_re
        pim_ref[i:i + 1, :] = p_im
        p_re, p_im = p_re * ab_re - p_im * ab_im, p_re * ab_im + p_im * ab_re

    a_re, a_im = arer_ref[...], aimr_ref[...]
    r_re, r_im = discretise(a_re, a_im, ldtr_ref[...])
    nr, ni = r_re - 1.0, r_im
    den = a_re * a_re + a_im * a_im
    z_re = (nr * a_re + ni * a_im) / den
    z_im = (ni * a_re - nr * a_im) / den
    b_re, b_im = bre_ref[...], bim_ref[...]
    bbre_ref[...] = z_re * b_re - z_im * b_im
    bbim_ref[...] = z_re * b_im + z_im * b_re


def _ssm_prep(a_re, a_im, log_dt, b_re, b_im):
    g, p, n = b_re.shape
    flat = lambda x: x.reshape(1, g * p)
    rep = lambda x: jnp.repeat(x, n, axis=1)
    ldt_gp = jnp.broadcast_to(log_dt[:, None], (g, p))
    ldt_rep = jnp.broadcast_to(log_dt[:, None], (g, p * n))
    out_shape = [jax.ShapeDtypeStruct((SCAN_L, g * p), F32)] * 2 + [jax.ShapeDtypeStruct((g, p * n), F32)] * 2
    return pl.pallas_call(_ssm_prep_kernel, out_shape=out_shape, name="ssm_prep")(
        flat(a_re), flat(a_im), flat(ldt_gp), rep(a_re), rep(a_im), ldt_rep,
        b_re.reshape(g, p * n), b_im.reshape(g, p * n))


def _ssm_block_matrices(bb_re, bb_im, c_re, c_im):
    g, p, n = SSM_GROUPS, SSM_STATE, SSM_GROUP
    nb = g // SSM_BLK
    eye = jnp.eye(SSM_BLK, dtype=F32)

    def in_mat(bb):
        x = bb.reshape(nb, SSM_BLK, p, n)
        return jnp.einsum('bgpm,gh->bgmhp', x, eye).reshape(nb, SSM_BLK * n, SSM_BLK * p)

    def out_mat(c):
        x = c.reshape(nb, SSM_BLK, n, p)
        return jnp.einsum('bgnp,gh->bgphn', x, eye).reshape(nb, SSM_BLK * p, SSM_BLK * n)

    b_mat = jnp.concatenate([in_mat(bb_re), in_mat(bb_im)], axis=2).astype(BF16)
    c_mat = jnp.concatenate([out_mat(c_re), -out_mat(c_im)], axis=1).astype(BF16)
    return b_mat, c_mat


def _cmul_add(a_re, a_im, s_re, s_im, b_re, b_im):
    return a_re * s_re - a_im * s_im + b_re, a_re * s_im + a_im * s_re + b_im


def _s5_prompt_kernel(*refs):
    par = SSM_PAR
    u_refs = refs[:par]
    (d_ref, pre_ref, pim_ref, bm_ref, cm_ref, y_ref, fre_ref, fim_ref,
     up_scr, bu_scr, lhs_scr, in_re_scr, in_im_scr, car_re, car_im, yn_scr) = refs[par:]
    nc, ln, w = SCAN_NC, SCAN_L, SSM_BLK_STATE

    @pl.when(pl.program_id(1) == 0)
    def _():
        car_re[...] = jnp.zeros_like(car_re)
        car_im[...] = jnp.zeros_like(car_im)

    for b in range(par):
        lanes = slice(b * w, (b + 1) * w)
        rows = lambda i: slice(i * nc, (i + 1) * nc)
        for i in range(ln):
            up_scr[b, rows(i), :] = u_refs[b][pl.ds(i, nc, stride=ln), :]
        up = up_scr[b]
        bu_scr[b] = jnp.dot(up.astype(BF16), bm_ref[b], preferred_element_type=F32)

        a_re = jnp.broadcast_to(pre_ref[0:1, lanes], (nc, w))
        a_im = jnp.broadcast_to(pim_ref[0:1, lanes], (nc, w))
        s_re = s_im = jnp.zeros((nc, w), F32)
        for i in range(ln):
            s_re, s_im = _cmul_add(a_re, a_im, s_re, s_im, bu_scr[b, rows(i), 0:w], bu_scr[b, rows(i), w:2 * w])
            bu_scr[b, rows(i), 0:w] = s_re
            bu_scr[b, rows(i), w:2 * w] = s_im

        al_re, al_im = pre_ref[ln - 1:ln, lanes], pim_ref[ln - 1:ln, lanes]
        c_re, c_im = car_re[b], car_im[b]
        for c in range(nc):
            in_re_scr[b, c:c + 1, :] = c_re
            in_im_scr[b, c:c + 1, :] = c_im
            c_re, c_im = _cmul_add(al_re, al_im, c_re, c_im, s_re[c:c + 1, :], s_im[c:c + 1, :])
        car_re[b] = c_re
        car_im[b] = c_im
        fre_ref[:, lanes] = c_re
        fim_ref[:, lanes] = c_im
        in_re, in_im = in_re_scr[b], in_im_scr[b]

        for i in range(ln):
            p_re = jnp.broadcast_to(pre_ref[i:i + 1, lanes], (nc, w))
            p_im = jnp.broadcast_to(pim_ref[i:i + 1, lanes], (nc, w))
            f_re, f_im = _cmul_add(p_re, p_im, in_re, in_im, bu_scr[b, rows(i), 0:w], bu_scr[b, rows(i), w:2 * w])
            lhs_scr[b, rows(i), 0:w] = f_re.astype(BF16)
            lhs_scr[b, rows(i), w:2 * w] = f_im.astype(BF16)

        y = (jnp.dot(lhs_scr[b], cm_ref[b], preferred_element_type=F32)
             + d_ref[:, b * LANES:(b + 1) * LANES] * up)
        for i in range(ln):
            yn_scr[b, pl.ds(i, nc, stride=ln), :] = y[rows(i), :]
        y_ref[:, b * LANES:(b + 1) * LANES] = _gelu_tanh(yn_scr[b]).astype(y_ref.dtype)


def _s5_prompt(proj, n_prompt, d_skip, pw_re, pw_im, b_mat, c_mat):
    par = SSM_PAR
    nb = SSM_GROUPS // SSM_BLK // par
    tt, w = SCAN_TT, SSM_BLK_STATE
    u_spec = lambda b: pl.BlockSpec((tt, LANES), lambda j, i: (i, par * j + b))
    return pl.pallas_call(
        _s5_prompt_kernel,
        grid=(nb, n_prompt // tt),
        in_specs=[u_spec(b) for b in range(par)] + [
            pl.BlockSpec((1, par * LANES), lambda j, i: (0, j)),
            pl.BlockSpec((SCAN_L, par * w), lambda j, i: (0, j)),
            pl.BlockSpec((SCAN_L, par * w), lambda j, i: (0, j)),
            pl.BlockSpec((par, LANES, 2 * w), lambda j, i: (j, 0, 0)),
            pl.BlockSpec((par, 2 * w, LANES), lambda j, i: (j, 0, 0))],
        out_specs=[pl.BlockSpec((tt, par * LANES), lambda j, i: (i, j)),
                   pl.BlockSpec((1, par * w), lambda j, i: (0, j)),
                   pl.BlockSpec((1, par * w), lambda j, i: (0, j))],
        out_shape=[jax.ShapeDtypeStruct((n_prompt, SSM_WIDTH), BF16),
                   jax.ShapeDtypeStruct((1, SSM_FLAT), F32),
                   jax.ShapeDtypeStruct((1, SSM_FLAT), F32)],
        scratch_shapes=[pltpu.VMEM((par, tt, LANES), F32), pltpu.VMEM((par, tt, 2 * w), F32),
                        pltpu.VMEM((par, tt, 2 * w), BF16),
                        pltpu.VMEM((par, SCAN_NC, w), F32), pltpu.VMEM((par, SCAN_NC, w), F32),
                        pltpu.VMEM((par, 1, w), F32), pltpu.VMEM((par, 1, w), F32),
                        pltpu.VMEM((par, tt, LANES), F32)],
        compiler_params=_cparams(("arbitrary", "arbitrary")),
        name="s5_prompt",
    )(*([proj] * par), d_skip, pw_re, pw_im, b_mat, c_mat)


def _s5_sample_kernel(u_ref, d_ref, pre_ref, pim_ref, bm_ref, cm_ref, s0re_ref, s0im_ref,
                      y_ref, fre_ref, fim_ref, up_scr, bu_scr, lhs_scr, yn_scr, *, n_seq, n_step):
    w = SSM_BLK_STATE
    rb = 16
    for s in range(n_step):
        up_scr[s * n_seq:(s + 1) * n_seq, :] = u_ref[pl.ds(s, n_seq, stride=n_step), :]
    up = up_scr[...]
    bu_scr[...] = jnp.dot(up.astype(BF16), bm_ref[0], preferred_element_type=F32)
    a_re = jnp.broadcast_to(pre_ref[0:1, :], (rb, w))
    a_im = jnp.broadcast_to(pim_ref[0:1, :], (rb, w))

    def seq_block(b, _):
        r0 = pl.multiple_of(b * rb, rb)
        s_re, s_im = s0re_ref[pl.ds(r0, rb), :], s0im_ref[pl.ds(r0, rb), :]
        for s in range(n_step):
            rows = pl.ds(pl.multiple_of(s * n_seq + r0, rb), rb)
            s_re, s_im = _cmul_add(a_re, a_im, s_re, s_im, bu_scr[rows, 0:w], bu_scr[rows, w:2 * w])
            lhs_scr[rows, 0:w] = s_re.astype(BF16)
            lhs_scr[rows, w:2 * w] = s_im.astype(BF16)
        fre_ref[pl.ds(r0, rb), :] = s_re
        fim_ref[pl.ds(r0, rb), :] = s_im
        return 0

    lax.fori_loop(0, n_seq // rb, seq_block, 0)
    y = jnp.dot(lhs_scr[...], cm_ref[0], preferred_element_type=F32) + d_ref[...] * up
    for s in range(n_step):
        yn_scr[pl.ds(s, n_seq, stride=n_step), :] = y[s * n_seq:(s + 1) * n_seq, :]
    y_ref[...] = _gelu_tanh(yn_scr[...]).astype(y_ref.dtype)


def _s5_sample(proj, n_prompt, n_seq, n_step, d_skip, pw_re, pw_im, b_mat, c_mat, s0_re, s0_im):
    nb = SSM_GROUPS // SSM_BLK
    rows, w = n_seq * n_step, SSM_BLK_STATE
    rblk = n_prompt // rows
    return pl.pallas_call(
        functools.partial(_s5_sample_kernel, n_seq=n_seq, n_step=n_step),
        grid=(nb,),
        in_specs=[pl.BlockSpec((rows, LANES), lambda j: (rblk, j)),
                  pl.BlockSpec((1, LANES), lambda j: (0, j)),
                  pl.BlockSpec((SCAN_L, w), lambda j: (0, j)),
                  pl.BlockSpec((SCAN_L, w), lambda j: (0, j)),
                  pl.BlockSpec((1, LANES, 2 * w), lambda j: (j, 0, 0)),
                  pl.BlockSpec((1, 2 * w, LANES), lambda j: (j, 0, 0)),
                  pl.BlockSpec((n_seq, w), lambda j: (0, j)),
                  pl.BlockSpec((n_seq, w), lambda j: (0, j))],
        out_specs=[pl.BlockSpec((rows, LANES), lambda j: (0, j)),
                   pl.BlockSpec((n_seq, w), lambda j: (0, j)),
                   pl.BlockSpec((n_seq, w), lambda j: (0, j))],
        out_shape=[jax.ShapeDtypeStruct((rows, SSM_WIDTH), BF16),
                   jax.ShapeDtypeStruct((n_seq, SSM_FLAT), F32),
                   jax.ShapeDtypeStruct((n_seq, SSM_FLAT), F32)],
        scratch_shapes=[pltpu.VMEM((rows, LANES), F32), pltpu.VMEM((rows, 2 * w), F32),
                        pltpu.VMEM((rows, 2 * w), BF16), pltpu.VMEM((rows, LANES), F32)],
        compiler_params=_cparams(("arbitrary",)),
        name="s5_sample",
    )(proj, d_skip, pw_re, pw_im, b_mat, c_mat, s0_re, s0_im)


def _attn_prompt_kernel(*refs):
    ins, o_ref, scr = refs[:15], refs[15], refs[16:]
    sb = pl.program_id(0)
    scale = HEAD_DIM ** -0.5
    qi = lax.broadcasted_iota(jnp.int32, (BAND, 2 * BAND), 0)
    kj = lax.broadcasted_iota(jnp.int32, (BAND, 2 * BAND), 1)
    dist = qi + BAND - kj
    band_ok = (dist >= 0) & (dist <= BAND)

    for g, (_, dil) in enumerate(DILATION_PATTERNS):
        q_ref, k_ref, v_ref, kp_ref, vp_ref = ins[5 * g:5 * g + 5]
        kbuf, vbuf, o_scr, m_scr, l_scr = scr[5 * g:5 * g + 5]
        pre = BAND * dil
        kbuf[0:pre, :] = kp_ref[...]
        kbuf[pre:pre + ATT_SB, :] = k_ref[...]
        vbuf[0:pre, :] = vp_ref[...]
        vbuf[pre:pre + ATT_SB, :] = v_ref[...]
        nblk = ATT_SB // pre

        def block(idx, _, dil=dil, pre=pre, nblk=nblk, q_ref=q_ref, kbuf=kbuf, vbuf=vbuf,
                  o_scr=o_scr, m_scr=m_scr, l_scr=l_scr):
            r = idx // nblk
            b = idx - r * nblk
            row0 = r + b * pre
            if dil == 1:
                q_rows = pl.ds(pl.multiple_of(row0, BAND), BAND)
                kv_rows = pl.ds(pl.multiple_of(row0, BAND), 2 * BAND)
            else:
                q_rows = pl.ds(row0, BAND, stride=dil)
                kv_rows = pl.ds(row0, 2 * BAND, stride=dil)
            q = (q_ref[q_rows, :] * scale).astype(BF16)
            kw = kbuf[kv_rows, :].astype(BF16)
            vw = vbuf[kv_rows, :].astype(BF16)
            s = lax.dot_general(q, kw, (((1,), (1,)), ((), ())), preferred_element_type=F32)
            s = jnp.where(band_ok & ((kj >= BAND) | (sb > 0) | (b > 0)), s, NEG)
            m = jnp.max(s, axis=-1, keepdims=True)
            p = jnp.exp(s - m)
            l = jnp.sum(p, axis=-1, keepdims=True)
            o = jnp.dot(p.astype(BF16), vw, preferred_element_type=F32)
            o_scr[q_rows, :] = o
            m_scr[q_rows, :] = jnp.broadcast_to(m, (BAND, HEAD_DIM))
            l_scr[q_rows, :] = jnp.broadcast_to(l, (BAND, HEAD_DIM))
            return 0

        lax.fori_loop(0, ATT_SB // BAND, block, 0, unroll=ATT_UNROLL)

    ms = [scr[5 * g + 3][...] for g in range(N_PATTERNS)]
    mx = jnp.maximum(jnp.maximum(ms[0], ms[1]), ms[2])
    num = jnp.zeros((ATT_SB, HEAD_DIM), F32)
    den = jnp.zeros((ATT_SB, HEAD_DIM), F32)
    for g in range(N_PATTERNS):
        wgt = jnp.exp(ms[g] - mx)
        num = num + wgt * scr[5 * g + 2][...]
        den = den + wgt * scr[5 * g + 4][...]
    o_ref[...] = num / den


def _attn_prompt(proj, n_prompt):
    hcol = lambda off, g, j: (off + g * GROUP_WIDTH) // HEAD_DIM + j
    in_specs, scratch = [], []
    for g, (_, dil) in enumerate(DILATION_PATTERNS):
        pre = BAND * dil
        per = ATT_SB // pre
        cur = lambda off, g=g: pl.BlockSpec((ATT_SB, HEAD_DIM), lambda sb, j: (sb, hcol(off, g, j)))
        prev = lambda off, g=g, per=per, pre=pre: pl.BlockSpec(
            (pre, HEAD_DIM), lambda sb, j: (jnp.maximum(sb * per - 1, 0), hcol(off, g, j)))
        in_specs += [cur(OFF_Q), cur(OFF_K), cur(OFF_V), prev(OFF_K), prev(OFF_V)]
        scratch += [pltpu.VMEM((pre + ATT_SB, HEAD_DIM), F32), pltpu.VMEM((pre + ATT_SB, HEAD_DIM), F32),
                    pltpu.VMEM((ATT_SB, HEAD_DIM), F32), pltpu.VMEM((ATT_SB, HEAD_DIM), F32),
                    pltpu.VMEM((ATT_SB, HEAD_DIM), F32)]
    return pl.pallas_call(
        _attn_prompt_kernel,
        grid=(n_prompt // ATT_SB, HEADS_PER_GROUP),
        in_specs=in_specs,
        out_specs=pl.BlockSpec((ATT_SB, HEAD_DIM), lambda sb, j: (sb, j)),
        out_shape=jax.ShapeDtypeStruct((n_prompt, GROUP_WIDTH), F32),
        scratch_shapes=scratch,
        compiler_params=_cparams(("arbitrary", "arbitrary")),
        name="attn_prompt",
    )(*([proj] * 15))


SEQ_PER_STEP = 2
KV_PLANES = 2 * HEADS_PER_GROUP


def _sample_bias(n_step):
    rows = HEADS_PER_GROUP * n_step
    step = np.arange(rows) % n_step
    cache_bias, new_bias = [], []
    for (window, dil) in DILATION_PATTERNS:
        wb = min(window, PAST_LEN)
        band = window // dil
        if dil > n_step:
            res, i = np.meshgrid(np.arange(n_step), np.arange(wb // dil), indexing='ij')
            c = (i * dil + res).reshape(-1)
        else:
            c = np.arange(wb)
        delta = wb + step[:, None] - c[None, :]
        ok = (delta >= 0) & (delta % dil == 0) & (delta // dil <= band)
        cache_bias.append(np.where(ok, 0.0, NEG).astype(np.float32))
        nb = np.full((SEQ_PER_STEP, rows, LANES), NEG, np.float32)
        for a in range(SEQ_PER_STEP):
            for sp in range(n_step):
                dl = step - sp
                okn = (dl >= 0) & (dl % dil == 0) & (dl // dil <= band)
                nb[a, :, a * n_step + sp] = np.where(okn, 0.0, NEG)
        new_bias.append(nb)
    return cache_bias, new_bias


def _attn_sample_kernel(*refs, n_step, cache_pieces):
    (q0, k0, v0, q1, k1, v1, q2, k2, v2) = refs[:9]
    qs, ks, vs = (q0, q1, q2), (k0, k1, k2), (v0, v1, v2)
    refs = list(refs[9:])
    caches = [[refs.pop(0) for _ in range(n)] for n in cache_pieces]
    (cb0, cb1, cb2, nb0, nb1, nb2, o_ref) = refs
    cbias, nbias = (cb0, cb1, cb2), (nb0, nb1, nb2)
    rows = HEADS_PER_GROUP * n_step
    gw = GROUP_WIDTH
    scale = HEAD_DIM ** -0.5
    row_head = lax.broadcasted_iota(jnp.int32, (rows, gw), 0) // n_step
    lane_head = lax.broadcasted_iota(jnp.int32, (rows, gw), 1) // HEAD_DIM
    own_head = row_head == lane_head
    nt = (((1,), (1,)), ((), ()))
    pad = jnp.zeros((LANES - SEQ_PER_STEP * n_step, gw), F32)

    def planes(load):
        k = jnp.concatenate([load(h) for h in range(HEADS_PER_GROUP)], axis=1)
        v = jnp.concatenate([load(HEADS_PER_GROUP + h) for h in range(HEADS_PER_GROUP)], axis=1)
        return k.astype(BF16), v.astype(BF16)

    for a in range(SEQ_PER_STEP):
        pieces = []
        for g, (window, dil) in enumerate(DILATION_PATTERNS):
            q = qs[g][a * n_step:(a + 1) * n_step, :] * scale
            qbd = jnp.where(own_head, jnp.concatenate([q] * HEADS_PER_GROUP, axis=0), 0.0).astype(BF16)
            kvs = []
            for cache in caches[g]:
                if len(cache.shape) == 3:
                    cache = cache.reshape(cache.shape[0] * KV_PLANES, HEAD_DIM)
                per_seq = cache.shape[0] // SEQ_PER_STEP
                kvs.append(planes(lambda p, cache=cache, per_seq=per_seq: cache[
                    pl.ds(a * per_seq + p, per_seq // KV_PLANES, stride=KV_PLANES), :]))
            sc = jnp.concatenate([lax.dot_general(qbd, k, nt, preferred_element_type=F32) for k, _ in kvs], axis=1)
            pieces.append((sc + cbias[g][...], [v for _, v in kvs]))
            k_new = jnp.concatenate([ks[g][...], pad], axis=0).astype(BF16)
            v_new = jnp.concatenate([vs[g][...], pad], axis=0).astype(BF16)
            sn = lax.dot_general(qbd, k_new, nt, preferred_element_type=F32) + nbias[g][a]
            pieces.append((sn, [v_new]))
        m = functools.reduce(jnp.maximum, [jnp.max(s, axis=-1, keepdims=True) for s, _ in pieces])
        l = jnp.zeros((rows, 1), F32)
        acc = jnp.zeros((rows, gw), F32)
        for s, vals in pieces:
            p = jnp.exp(s - m)
            l = l + jnp.sum(p, axis=-1, keepdims=True)
            pb = p.astype(BF16)
            nk = pb.shape[1] // len(vals)
            for r, v in enumerate(vals):
                acc = acc + jnp.dot(pb[:, r * nk:(r + 1) * nk], v, preferred_element_type=F32)
        acc = jnp.where(own_head, acc, 0.0)
        o16 = functools.reduce(lambda x, y: x + y,
                               [acc[:, h * HEAD_DIM:(h + 1) * HEAD_DIM] for h in range(HEADS_PER_GROUP)]) / l
        for h in range(HEADS_PER_GROUP):
            o_ref[a * n_step:(a + 1) * n_step, h * HEAD_DIM:(h + 1) * HEAD_DIM] = o16[h * n_step:(h + 1) * n_step, :]


def _attn_sample(proj, n_prompt, n_seq, n_step, caches):
    rows = SEQ_PER_STEP * n_step
    rblk = n_prompt // rows
    cache_bias, new_bias = _sample_bias(n_step)
    tok = lambda off, g: pl.BlockSpec((rows, GROUP_WIDTH), lambda i: (rblk + i, (off + g * GROUP_WIDTH) // GROUP_WIDTH))
    in_specs, args, cache_pieces = [], [], []
    for g in range(N_PATTERNS):
        in_specs += [tok(OFF_Q, g), tok(OFF_K, g), tok(OFF_V, g)]
        args += [proj, proj, proj]
    for g, (window, dil) in enumerate(DILATION_PATTERNS):
        c = caches[g]
        wb = c.shape[1]
        if dil > n_step:
            assert wb % dil == 0
            c = c.reshape(n_seq * (wb // dil), dil * KV_PLANES, HEAD_DIM)
            for r in range(n_step):
                in_specs.append(pl.BlockSpec((SEQ_PER_STEP * (wb // dil), KV_PLANES, HEAD_DIM),
                                             lambda i, r=r: (i, r, 0)))
                args.append(c)
            cache_pieces.append(n_step)
        else:
            c = c.reshape(n_seq * wb * KV_PLANES, HEAD_DIM)
            in_specs.append(pl.BlockSpec((SEQ_PER_STEP * wb * KV_PLANES, HEAD_DIM), lambda i: (i, 0)))
            args.append(c)
            cache_pieces.append(1)
    for b in cache_bias:
        in_specs.append(pl.BlockSpec(b.shape, lambda i: (0, 0)))
        args.append(jnp.asarray(b))
    for b in new_bias:
        in_specs.append(pl.BlockSpec(b.shape, lambda i: (0, 0, 0)))
        args.append(jnp.asarray(b))
    return pl.pallas_call(
        functools.partial(_attn_sample_kernel, n_step=n_step, cache_pieces=tuple(cache_pieces)),
        grid=(n_seq // SEQ_PER_STEP,),
        in_specs=in_specs,
        out_specs=pl.BlockSpec((rows, GROUP_WIDTH), lambda i: (i, 0)),
        out_shape=jax.ShapeDtypeStruct((n_seq * n_step, GROUP_WIDTH), F32),
        compiler_params=_cparams(("arbitrary",)),
        name="attn_sample",
    )(*args)


def _mix_kernel(yp_ref, ys_ref, op_ref, os_ref, wa_ref, wb_ref, wbr_ref, ga_ref, gb_ref, o_ref,
                wa_bf, wb_bf, wbr_bf, *, n_prompt_tiles):
    i = pl.program_id(1)

    @pl.when(i == 0)
    def _():
        wa_bf[...] = wa_ref[...].astype(BF16)
        wb_bf[...] = wb_ref[...].astype(BF16)
        wbr_bf[...] = wbr_ref[...].astype(BF16)

    is_s = i >= n_prompt_tiles
    for rows in _row_halves(o_ref.shape[0]):
        y = jnp.where(is_s, ys_ref[rows, :], yp_ref[rows, :])
        o = jnp.where(is_s, os_ref[rows, :], op_ref[rows, :]).astype(BF16)
        glu_a = jnp.dot(y, wa_bf[...], preferred_element_type=F32)
        glu_b = jnp.dot(y, wb_bf[...], preferred_element_type=F32)
        branch_a = glu_a * _sigmoid(glu_b)
        branch_b = jnp.dot(o, wbr_bf[...], preferred_element_type=F32)
        o_ref[rows, :] = (_sigmoid(ga_ref[rows, :]) * branch_a
                          + _sigmoid(gb_ref[rows, :]) * branch_b).astype(o_ref.dtype)


def _mix(y_p, y_s, o_p, o_s, w_glu, w_attn_br, proj):
    tp, ts = y_p.shape[0], y_s.shape[0]
    tm, tn = TOK_TILE, 512
    assert OFF_G % tn == 0
    npt, nst = tp // tm, ts // tm
    ncol = D_MODEL // tn
    prow = lambda n, i: (jnp.minimum(i, npt - 1), 0)
    srow = lambda n, i: (jnp.maximum(i - npt, 0), 0)
    return pl.pallas_call(
        functools.partial(_mix_kernel, n_prompt_tiles=npt),
        grid=(ncol, npt + nst),
        in_specs=[pl.BlockSpec((tm, SSM_WIDTH), prow), pl.BlockSpec((tm, SSM_WIDTH), srow),
                  pl.BlockSpec((tm, GROUP_WIDTH), prow), pl.BlockSpec((tm, GROUP_WIDTH), srow),
                  pl.BlockSpec((SSM_WIDTH, tn), lambda n, i: (0, n)),
                  pl.BlockSpec((SSM_WIDTH, tn), lambda n, i: (0, ncol + n)),
                  pl.BlockSpec((GROUP_WIDTH, tn), lambda n, i: (0, n)),
                  pl.BlockSpec((tm, tn), lambda n, i: (i, OFF_G // tn + n)),
                  pl.BlockSpec((tm, tn), lambda n, i: (i, OFF_G // tn + ncol + n))],
        out_specs=pl.BlockSpec((tm, tn), lambda n, i: (i, n)),
        out_shape=jax.ShapeDtypeStruct((tp + ts, D_MODEL), BF16),
        scratch_shapes=[pltpu.VMEM((SSM_WIDTH, tn), BF16), pltpu.VMEM((SSM_WIDTH, tn), BF16),
                        pltpu.VMEM((GROUP_WIDTH, tn), BF16)],
        compiler_params=_cparams(("arbitrary", "arbitrary")),
        name="glu_mix",
    )(y_p, y_s, o_p, o_s, w_glu, w_glu, w_attn_br, proj, proj)


def _route(logits):
    lane = lax.broadcasted_iota(jnp.int32, logits.shape, 1).astype(F32)
    big = 1000.0
    first = lambda cond: jnp.min(jnp.where(cond, lane, big), axis=-1, keepdims=True)
    is_g = lane < N_EXPERT_GROUPS
    lg = jnp.where(is_g, logits, NEG)
    mg = jnp.max(lg, axis=-1, keepdims=True)
    g_sel = first(lg == mg)
    p_group = 1.0 / jnp.sum(jnp.where(is_g, jnp.exp(lg - mg), 0.0), axis=-1, keepdims=True)
    e_lo = N_EXPERT_GROUPS + EXPERTS_PER_GROUP * g_sel
    le = jnp.where((lane >= e_lo) & (lane < e_lo + EXPERTS_PER_GROUP), logits, NEG)
    v1 = jnp.max(le, axis=-1, keepdims=True)
    i1 = first(le == v1)
    le2 = jnp.where(lane == i1, NEG, le)
    v2 = jnp.max(le2, axis=-1, keepdims=True)
    i2 = first(le2 == v2)
    e2 = jnp.exp(v2 - v1)
    w1 = p_group / (1.0 + e2)
    w2 = p_group * e2 / (1.0 + e2)
    pick = lambda k, val: jnp.where(lane == k, val, 0.0)
    return (pick(ROUTE_E1, i1 - N_EXPERT_GROUPS) + pick(ROUTE_E2, i2 - N_EXPERT_GROUPS)
            + pick(ROUTE_W1, w1) + pick(ROUTE_W2, w2))


def _outproj_kernel(mix_ref, w_ref, xp_ref, xs_ref, g_ref, wr_ref, br_ref,
                    gtp_ref, scp_ref, shp_ref, gts_ref, scs_ref, shs_ref,
                    x1_ref, h2_ref, route_ref, *, n_prompt_tiles):
    is_s = pl.program_id(0) >= n_prompt_tiles

    def split(v):
        high = v.astype(BF16)
        return high, (v - high.astype(F32)).astype(BF16)

    x = jnp.where(is_s, xs_ref[...], xp_ref[...])
    gt = jnp.where(is_s, gts_ref[...], gtp_ref[0:1, :])
    sc = jnp.where(is_s, scs_ref[...], scp_ref[0:1, :])
    sh = jnp.where(is_s, shs_ref[...], shp_ref[0:1, :])
    x1 = x + gt * jnp.dot(mix_ref[...], w_ref[...], preferred_element_type=F32)
    x1_ref[...] = x1
    ms = jnp.mean(x1 * x1, axis=-1, keepdims=True)
    h2 = (x1 * lax.rsqrt(ms + EPS) * g_ref[...]) * (1.0 + sc) + sh
    h2_ref[...] = h2.astype(h2_ref.dtype)
    r = jnp.dot(jnp.concatenate(split(h2), axis=0), jnp.concatenate(split(wr_ref[...]), axis=1),
                preferred_element_type=F32)
    n = h2.shape[0]
    logits = (r[:n, :LANES] + r[:n, LANES:]) + (r[n:, :LANES] + r[n:, LANES:]) + br_ref[...]
    route_ref[...] = _route(logits)


def _outproj(mixed, w_out_bf, x_p, x_s, g2, w_router, b_router, mod):
    tp, ts = x_p.shape[0], x_s.shape[0]
    tm = SMALL_TOK_TILE
    npt, nst = tp // tm, ts // tm
    prow = lambda i: (jnp.minimum(i, npt - 1), 0)
    srow = lambda i: (jnp.maximum(i - npt, 0), 0)
    const = lambda i: (0, 0)
    vec = pl.BlockSpec((1, D_MODEL), const)
    gt_p, gt_s = _mod_specs(MOD_GT1, tm, ts, npt)
    sc_p, sc_s = _mod_specs(MOD_SC2, tm, ts, npt)
    sh_p, sh_s = _mod_specs(MOD_SH2, tm, ts, npt)
    full = lambda i: (i, 0)
    return pl.pallas_call(
        functools.partial(_outproj_kernel, n_prompt_tiles=npt),
        grid=(npt + nst,),
        in_specs=[pl.BlockSpec((tm, D_MODEL), full), pl.BlockSpec((D_MODEL, D_MODEL), const),
                  pl.BlockSpec((tm, D_MODEL), prow), pl.BlockSpec((tm, D_MODEL), srow),
                  vec, pl.BlockSpec((D_MODEL, LANES), const), pl.BlockSpec((1, LANES), const),
                  gt_p, sc_p, sh_p, gt_s, sc_s, sh_s],
        out_specs=[pl.BlockSpec((tm, D_MODEL), full), pl.BlockSpec((tm, D_MODEL), full),
                   pl.BlockSpec((tm, LANES), full)],
        out_shape=[jax.ShapeDtypeStruct((tp + ts, D_MODEL), F32),
                   jax.ShapeDtypeStruct((tp + ts, D_MODEL), F32),
                   jax.ShapeDtypeStruct((tp + ts, LANES), F32)],
        compiler_params=_cparams(("arbitrary",)),
        name="out_proj_norm2_router",
    )(mixed, w_out_bf, x_p, x_s, g2, w_router, b_router, mod, mod, mod, mod, mod, mod)


def _dispatch_plan(route, tm):
    e = route[:, ROUTE_E1:ROUTE_E2 + 1].astype(jnp.int32).reshape(-1)
    n_pairs = e.shape[0]
    onehot = (e[:, None] == jnp.arange(N_EXPERTS, dtype=jnp.int32)[None, :]).astype(jnp.int32)
    csum = jnp.cumsum(onehot, axis=0)
    rank = jnp.sum(onehot * csum, axis=1) - 1
    tiles_per_expert = (csum[-1] + tm - 1) // tm
    tile_end = jnp.cumsum(tiles_per_expert)
    tile_start = tile_end - tiles_per_expert
    dest = (tile_start[e] * tm + rank).astype(jnp.int32)
    max_tiles = n_pairs // tm + N_EXPERTS
    k = jnp.arange(max_tiles, dtype=jnp.int32)
    tile_expert = jnp.minimum(jnp.sum((k[:, None] >= tile_end[None, :]).astype(jnp.int32), axis=1), N_EXPERTS - 1)
    n_used = tile_end[-1].astype(jnp.int32)
    last_expert = jnp.take(tile_expert, n_used - 1)
    tile_expert = jnp.where(k < n_used, tile_expert, last_expert).astype(jnp.int32)
    pad_end = (jnp.concatenate([tile_end, tile_end[-1:]]) * tm).astype(jnp.int32)
    pad_len = (tiles_per_expert * tm - csum[-1]).astype(jnp.int32)
    ids = jnp.arange(N_EXPERTS, dtype=jnp.int32)
    has_tiles = jnp.where(tiles_per_expert > 0, ids, N_EXPERTS)
    following = jnp.min(jnp.where(ids[None, :] > ids[:, None], has_tiles[None, :], N_EXPERTS), axis=1)
    next_expert = jnp.where(following < N_EXPERTS, following, -1)[tile_expert].astype(jnp.int32)
    return dest, pad_end, pad_len, tile_expert, n_used.reshape(1), next_expert, max_tiles


def _start_pair_copies(dest_ref, tile, rows, make):
    def body(r, _):
        for k in range(2):
            make(r, k, dest_ref[(tile * rows + r) * 2 + k]).start()
        return 0
    lax.fori_loop(0, rows, body, 0, unroll=8)


def _pair_copies(dest_ref, tile, rows, make, make_all):
    _start_pair_copies(dest_ref, tile, rows, make)
    for k in range(2):
        make_all(k).wait()


def _dispatch_kernel(dest_ref, pad_end_ref, pad_len_ref, h_ref, xs_ref, zero_buf, sem, pad_sem):
    rows = h_ref.shape[0]

    @pl.when(pl.program_id(0) == 0)
    def _():
        zero_buf[...] = jnp.zeros_like(zero_buf)

        def pad_copies(fn):
            for e in range(N_EXPERTS):
                end, left = pad_end_ref[e], pad_len_ref[e]
                size = zero_buf.shape[0]
                while size >= SUBLANES:
                    take = (left & size) != 0
                    end = end - jnp.where(take, size, 0)

                    @pl.when(take)
                    def _(start=end, size=size):
                        fn(pltpu.make_async_copy(zero_buf.at[pl.ds(0, size)],
                                                 xs_ref.at[pl.ds(pl.multiple_of(start, size), size)], pad_sem))

                    size //= 2
                for r in range(1, SUBLANES):
                    @pl.when((left & (SUBLANES - 1)) >= r)
                    def _(row=end - r):
                        fn(pltpu.make_async_copy(zero_buf.at[pl.ds(0, 1)], xs_ref.at[pl.ds(row, 1)], pad_sem))
            size = zero_buf.shape[0]
            tail = pad_end_ref[N_EXPERTS]
            for t in range(N_EXPERTS * MOE_TM // size):
                @pl.when(tail + t * size < xs_ref.shape[0])
                def _(start=tail + t * size):
                    fn(pltpu.make_async_copy(zero_buf, xs_ref.at[pl.ds(pl.multiple_of(start, size), size)], pad_sem))

        pad_copies(lambda c: c.start())
        pad_copies(lambda c: c.wait())

    make = lambda r, k, d: pltpu.make_async_copy(h_ref.at[pl.ds(r, 1)], xs_ref.at[pl.ds(d, 1)], sem)
    make_all = lambda k: pltpu.make_async_copy(h_ref, xs_ref.at[pl.ds(0, rows)], sem)
    _pair_copies(dest_ref, pl.program_id(0), rows, make, make_all)


def _dispatch(dest, pad_end, pad_len, h2, n_slots):
    n_tok = h2.shape[0]
    tm = SMALL_TOK_TILE
    return pl.pallas_call(
        _dispatch_kernel,
        grid_spec=pltpu.PrefetchScalarGridSpec(
            num_scalar_prefetch=3, grid=(n_tok // tm,),
            in_specs=[pl.BlockSpec((tm, D_MODEL), lambda i, d, ps, pn: (i, 0))],
            out_specs=pl.BlockSpec(memory_space=pl.ANY),
            scratch_shapes=[pltpu.VMEM((MOE_TM // 2, D_MODEL), F32),
                            pltpu.SemaphoreType.DMA(()), pltpu.SemaphoreType.DMA(())]),
        out_shape=jax.ShapeDtypeStruct((n_slots, D_MODEL), F32),
        compiler_params=_cparams(("arbitrary",)),
        name="moe_dispatch",
    )(dest, pad_end, pad_len, h2)


def _experts_kernel(te_ref, used_ref, next_ref, xs_ref, wgu_hbm, wd_hbm, y_ref, wgu_f32, wd_f32, wgu_bf, wd_bf, sem):
    k = pl.program_id(0)
    new_expert = (k == 0) | (te_ref[k] != te_ref[jnp.maximum(k - 1, 0)])

    def fetch(e):
        return (pltpu.make_async_copy(wgu_hbm.at[e], wgu_f32, sem.at[0]),
                pltpu.make_async_copy(wd_hbm.at[e], wd_f32, sem.at[1]))

    @pl.when(k == 0)
    def _():
        for c in fetch(te_ref[0]):
            c.start()

    @pl.when(new_expert)
    def _():
        for c in fetch(te_ref[k]):
            c.wait()
        wgu_bf[...] = wgu_f32[...].astype(BF16)
        wd_bf[...] = wd_f32[...].astype(BF16)

    @pl.when(new_expert & (next_ref[k] >= 0))
    def _():
        for c in fetch(next_ref[k]):
            c.start()

    @pl.when(k < used_ref[0])
    def _():
        gu = jnp.dot(xs_ref[...].astype(BF16), wgu_bf[...], preferred_element_type=F32)
        gate, up = gu[:, :EXPERT_FF], gu[:, EXPERT_FF:]
        act = (gate * _sigmoid(gate)) * up
        y_ref[...] = jnp.dot(act.astype(BF16), wd_bf[...], preferred_element_type=F32)

    @pl.when(k >= used_ref[0])
    def _():
        y_ref[...] = jnp.zeros_like(y_ref)


def _experts(tile_expert, n_used, next_expert, xs, w_gu, w_down, max_tiles):
    tm = MOE_TM
    row = lambda k, te, nu, nx: (jnp.minimum(k, nu[0] - 1), 0)
    return pl.pallas_call(
        _experts_kernel,
        grid_spec=pltpu.PrefetchScalarGridSpec(
            num_scalar_prefetch=3, grid=(max_tiles,),
            in_specs=[pl.BlockSpec((tm, D_MODEL), row),
                      pl.BlockSpec(memory_space=pl.ANY), pl.BlockSpec(memory_space=pl.ANY)],
            out_specs=pl.BlockSpec((tm, D_MODEL), lambda k, te, nu, nx: (k, 0)),
            scratch_shapes=[pltpu.VMEM((D_MODEL, 2 * EXPERT_FF), F32), pltpu.VMEM((EXPERT_FF, D_MODEL), F32),
                            pltpu.VMEM((D_MODEL, 2 * EXPERT_FF), BF16), pltpu.VMEM((EXPERT_FF, D_MODEL), BF16),
                            pltpu.SemaphoreType.DMA((2,))]),
        out_shape=jax.ShapeDtypeStruct((max_tiles * tm, D_MODEL), F32),
        compiler_params=_cparams(("arbitrary",)),
        name="moe_experts",
    )(tile_expert, n_used, next_expert, xs, w_gu, w_down)


def _combine_kernel(dest_ref, y_hbm, x1_ref, route_ref, gtp_ref, gts_ref, yp_ref, ys_ref, buf, sem, *, n_prompt_tiles):
    i = pl.program_id(0)
    rows = x1_ref.shape[0]

    def gather(tile, slot):
        make = lambda r, k, d: pltpu.make_async_copy(y_hbm.at[pl.ds(d, 1)], buf.at[slot, k, pl.ds(r, 1)],
                                                     sem.at[slot])
        _start_pair_copies(dest_ref, tile, rows, make)

    @pl.when(i == 0)
    def _():
        gather(0, 0)

    slot = lax.rem(i, 2)

    @pl.when(i + 1 < pl.num_programs(0))
    def _():
        gather(i + 1, 1 - slot)

    for k in range(2):
        pltpu.make_async_copy(y_hbm.at[pl.ds(0, rows)], buf.at[slot, k], sem.at[slot]).wait()
    route = route_ref[...]
    lane = lax.broadcasted_iota(jnp.int32, route.shape, 1)
    w1 = jnp.sum(jnp.where(lane == ROUTE_W1, route, 0.0), axis=-1, keepdims=True)
    w2 = jnp.sum(jnp.where(lane == ROUTE_W2, route, 0.0), axis=-1, keepdims=True)
    moe = w1 * buf[slot, 0] + w2 * buf[slot, 1]

    @pl.when(i < n_prompt_tiles)
    def _():
        yp_ref[...] = x1_ref[...] + gtp_ref[0:1, :] * moe

    @pl.when(i >= n_prompt_tiles)
    def _():
        ys_ref[...] = x1_ref[...] + gts_ref[...] * moe


def _combine(dest, y_slots, x1, route, mod, tp, ts):
    tm = SMALL_TOK_TILE
    npt, nst = tp // tm, ts // tm
    row = lambda i, d: (i, 0)
    gt_p, gt_s = _mod_specs(MOD_GT2, tm, ts, npt)
    return pl.pallas_call(
        functools.partial(_combine_kernel, n_prompt_tiles=npt),
        grid_spec=pltpu.PrefetchScalarGridSpec(
            num_scalar_prefetch=1, grid=(npt + nst,),
            in_specs=[pl.BlockSpec(memory_space=pl.ANY),
                      pl.BlockSpec((tm, D_MODEL), row), pl.BlockSpec((tm, LANES), row), gt_p, gt_s],
            out_specs=[pl.BlockSpec((tm, D_MODEL), lambda i, d: (jnp.minimum(i, npt - 1), 0)),
                       pl.BlockSpec((tm, D_MODEL), lambda i, d: (jnp.maximum(i - npt, 0), 0))],
            scratch_shapes=[pltpu.VMEM((2, 2, tm, D_MODEL), F32), pltpu.SemaphoreType.DMA((2,))]),
        out_shape=[jax.ShapeDtypeStruct((tp, D_MODEL), F32), jax.ShapeDtypeStruct((ts, D_MODEL), F32)],
        compiler_params=_cparams(("arbitrary",)),
        name="moe_combine",
    )(dest, y_slots, x1, route, mod, mod)


def _moe(h2, route, x1, w_gu, w_down, mod, tp, ts):
    dest, pad_end, pad_len, tile_expert, n_used, next_expert, max_tiles = _dispatch_plan(route, MOE_TM)
    xs = _dispatch(dest, pad_end, pad_len, h2, max_tiles * MOE_TM)
    y_slots = _experts(tile_expert, n_used, next_expert, xs, w_gu, w_down, max_tiles)
    return _combine(dest, y_slots, x1, route, mod, tp, ts)


def _kv_pack_kernel(k_ref, v_ref, o_ref):
    n = k_ref.shape[0]
    for h in range(HEADS_PER_GROUP):
        cols = slice(h * HEAD_DIM, (h + 1) * HEAD_DIM)
        o_ref[pl.ds(h, n, stride=KV_PLANES), :] = k_ref[:, cols]
        o_ref[pl.ds(HEADS_PER_GROUP + h, n, stride=KV_PLANES), :] = v_ref[:, cols]


def _kv_pack(proj, row0, rows, g):
    tr = min(rows, SMALL_TOK_TILE)
    assert rows % tr == 0 and row0 % tr == 0
    col = lambda off: (off + g * GROUP_WIDTH) // GROUP_WIDTH
    spec = lambda off: pl.BlockSpec((tr, GROUP_WIDTH), lambda i: (row0 // tr + i, col(off)))
    flat = pl.pallas_call(
        _kv_pack_kernel,
        grid=(rows // tr,),
        in_specs=[spec(OFF_K), spec(OFF_V)],
        out_specs=pl.BlockSpec((tr * KV_PLANES, HEAD_DIM), lambda i: (i, 0)),
        out_shape=jax.ShapeDtypeStruct((rows * KV_PLANES, HEAD_DIM), F32),
        compiler_params=_cparams(("arbitrary",)),
        name="kv_pack",
    )(proj, proj)
    return flat.reshape(rows, 2, HEADS_PER_GROUP, HEAD_DIM)


def _rope_tables(n_prompt, n_seq, n_step):
    half = ROT_DIM // 2
    inv_freq = ROPE_THETA ** (-jnp.arange(half, dtype=F32) / half)
    angles = lambda pos: pos.astype(F32)[:, None] * inv_freq[None, :]
    fine = 128
    assert n_prompt % fine == 0
    coarse_ang = angles(jnp.arange(n_prompt // fine, dtype=jnp.int32) * fine)[:, None, :]
    fine_ang = angles(jnp.arange(fine, dtype=jnp.int32))[None, :, :]
    cc, sc, cf, sf = jnp.cos(coarse_ang), jnp.sin(coarse_ang), jnp.cos(fine_ang), jnp.sin(fine_ang)
    cos_p = (cc * cf - sc * sf).reshape(n_prompt, half)
    sin_p = (sc * cf + cc * sf).reshape(n_prompt, half)
    step_ang = angles(PAST_LEN + jnp.arange(n_step, dtype=jnp.int32))
    cos = jnp.concatenate([cos_p, jnp.tile(jnp.cos(step_ang), (n_seq, 1))], axis=0)
    sin = jnp.concatenate([sin_p, jnp.tile(jnp.sin(step_ang), (n_seq, 1))], axis=0)
    n = n_prompt + n_seq * n_step
    one = jnp.ones((n, HEAD_DIM - ROT_DIM), F32)
    zero = jnp.zeros((n, HEAD_DIM - ROT_DIM), F32)
    zh = jnp.zeros((n, half), F32)
    rc = jnp.concatenate([cos, cos, one], axis=1)
    rs1 = jnp.concatenate([-sin, zh, zero], axis=1)
    rs2 = jnp.concatenate([zh, sin, zero], axis=1)
    return rc, rs1, rs2


def kernel(x_prompt, x_sample, cache_kv_w128, cache_kv_w512, cache_kv_w2048, state_ssm_re, state_ssm_im,
           c_prompt, c_sample, w_ada, b_ada, norm1_g, norm2_g, w_in, ssm_a_re, ssm_a_im, ssm_log_dt,
           ssm_b_re, ssm_b_im, ssm_c_re, ssm_c_im, ssm_d, w_glu, q_norm_g, k_norm_g, w_attn_br, w_out,
           w_router_group, b_router_group, w_router_expert, b_router_expert, w_expert_gate_up, w_expert_down):
    assert x_prompt.shape[0] == 1 and w_ada.shape[0] == 1
    n_prompt = x_prompt.shape[1]
    n_seq, n_step = x_sample.shape[0], x_sample.shape[1]
    n_samp = n_seq * n_step
    assert n_samp % TOK_TILE == 0 and n_prompt % ATT_SB == 0 and (n_prompt + n_samp) % PROJ_TM == 0
    x_p = x_prompt.reshape(n_prompt, D_MODEL)
    x_s = x_sample.reshape(n_samp, D_MODEL)

    c_all = jnp.concatenate([jnp.repeat(c_sample, n_step, axis=0),
                             jnp.broadcast_to(c_prompt, (SUBLANES, D_MODEL))], axis=0)
    mod = _ada(c_all, w_ada[0], b_ada[0])

    h1 = _modnorm(x_p, x_s, norm1_g[0].reshape(1, D_MODEL), mod)
    rc, rs1, rs2 = _rope_tables(n_prompt, n_seq, n_step)
    proj = _inproj(h1, w_in[0], rc, rs1, rs2, q_norm_g[0].reshape(1, HEAD_DIM), k_norm_g[0].reshape(1, HEAD_DIM))

    pw_re, pw_im, bb_re, bb_im = _ssm_prep(ssm_a_re[0], ssm_a_im[0], ssm_log_dt[0], ssm_b_re[0], ssm_b_im[0])
    b_mat, c_mat = _ssm_block_matrices(bb_re, bb_im, ssm_c_re[0], ssm_c_im[0])
    d_skip = ssm_d[0].reshape(1, SSM_WIDTH)
    yg_p, fre_p, fim_p = _s5_prompt(proj, n_prompt, d_skip, pw_re, pw_im, b_mat, c_mat)
    yg_s, fre_s, fim_s = _s5_sample(proj, n_prompt, n_seq, n_step, d_skip, pw_re, pw_im, b_mat, c_mat,
                                    state_ssm_re[0].reshape(n_seq, SSM_FLAT), state_ssm_im[0].reshape(n_seq, SSM_FLAT))

    o_p = _attn_prompt(proj, n_prompt)
    o_s = _attn_sample(proj, n_prompt, n_seq, n_step, (cache_kv_w128[0], cache_kv_w512[0], cache_kv_w2048[0]))

    mixed = _mix(yg_p, yg_s, o_p, o_s, w_glu[0], w_attn_br[0], proj)

    w_router = jnp.concatenate([w_router_group[0], w_router_expert[0],
                                jnp.zeros((D_MODEL, LANES - N_EXPERT_GROUPS - N_EXPERTS), F32)], axis=1)
    b_router = jnp.concatenate([b_router_group[0], b_router_expert[0],
                                jnp.zeros((LANES - N_EXPERT_GROUPS - N_EXPERTS,), F32)]).reshape(1, LANES)
    x1, h2, route = _outproj(mixed, w_out[0].astype(BF16), x_p, x_s, norm2_g[0].reshape(1, D_MODEL),
                             w_router, b_router, mod)
    y_p, y_s = _moe(h2, route, x1, w_expert_gate_up[0], w_expert_down[0], mod, n_prompt, n_samp)

    kv_p, kv_s = [], []
    for g, (window, _) in enumerate(DILATION_PATTERNS):
        keep = min(window, n_prompt)
        kv_p.append(_kv_pack(proj, n_prompt - keep, keep, g).reshape(1, 1, keep, 2, HEADS_PER_GROUP, HEAD_DIM))
        kv_s.append(_kv_pack(proj, n_prompt, n_samp, g).reshape(1, n_seq, n_step, 2, HEADS_PER_GROUP, HEAD_DIM))

    state_shape_p = (1, 1, SSM_GROUPS, SSM_STATE)
    state_shape_s = (1, n_seq, SSM_GROUPS, SSM_STATE)
    return (y_p.reshape(1, n_prompt, D_MODEL), y_s.reshape(n_seq, n_step, D_MODEL),
            kv_p[0], kv_p[1], kv_p[2], fre_p.reshape(state_shape_p), fim_p.reshape(state_shape_p),
            kv_s[0], kv_s[1], kv_s[2], fre_s.reshape(state_shape_s), fim_s.reshape(state_shape_s))
```

```python
import functools
import math

import numpy as np
import jax
import jax.numpy as jnp
from jax import lax
from jax.experimental import pallas as pl
from jax.experimental.pallas import tpu as pltpu

F32 = jnp.float32
BF16 = jnp.bfloat16

D_MODEL = 2048
PAST_LEN = 2048
SSM_WIDTH = D_MODEL // 2
SSM_GROUP = 16
SSM_GROUPS = SSM_WIDTH // SSM_GROUP
SSM_STATE = 64
SSM_FLAT = SSM_GROUPS * SSM_STATE
HEAD_DIM = 128
DILATION_PATTERNS = ((128, 1), (512, 4), (2048, 16))
N_PATTERNS = 3
HEADS_PER_GROUP = 4
GROUP_WIDTH = HEADS_PER_GROUP * HEAD_DIM
ATTN_WIDTH = N_PATTERNS * GROUP_WIDTH
ROT_DIM = HEAD_DIM // 4
ROPE_THETA = 500000.0
OFF_Q = SSM_WIDTH
OFF_K = OFF_Q + ATTN_WIDTH
OFF_V = OFF_K + ATTN_WIDTH
OFF_G = OFF_V + ATTN_WIDTH
IN_COLS = OFF_G + 2 * D_MODEL
N_EXPERT_GROUPS = 4
EXPERTS_PER_GROUP = 4
N_EXPERTS = 16
EXPERT_FF = D_MODEL // 4
EPS = 1e-6
NEG = -1e30

LANES = 128
SUBLANES = 8
VMEM_LIMIT = 56 * 1024 * 1024

TOK_TILE = 512
SMALL_TOK_TILE = 256
PROJ_TM = 544
PROJ_TN = 2432
SCAN_L = 32
SCAN_NC = 16
SCAN_TT = SCAN_L * SCAN_NC
SSM_BLK = 8
SSM_BLK_STATE = SSM_BLK * SSM_STATE
SSM_PAR = 4
MOE_TM = 512
ROUTE_E1, ROUTE_E2, ROUTE_W1, ROUTE_W2 = 0, 1, 2, 3
MOD_SH1, MOD_SC1, MOD_GT1, MOD_SH2, MOD_SC2, MOD_GT2 = range(6)
ATT_SB = 2048
BAND = 128
ATT_UNROLL = 16


def _cparams(sem, vmem=VMEM_LIMIT):
    return pltpu.CompilerParams(dimension_semantics=sem, vmem_limit_bytes=vmem)


def _sigmoid(x):
    return 0.5 * jnp.tanh(0.5 * x) + 0.5


def _row_halves(rows):
    half = rows // 2
    return (slice(0, half), slice(half, rows))


def _gelu_tanh(x):
    c = math.sqrt(2.0 / math.pi)
    return 0.5 * x * (1.0 + jnp.tanh(c * (x + 0.044715 * (x * x * x))))


def _ada_kernel(c_ref, w_ref, b_ref, o_ref, cs_ref):
    @pl.when(pl.program_id(0) == 0)
    def _():
        c = c_ref[...]
        cs_ref[...] = (c * _sigmoid(c)).astype(BF16)

    o_ref[...] = jnp.dot(cs_ref[...], w_ref[...].astype(BF16), preferred_element_type=F32) + b_ref[...]


def _ada(c_all, w_ada, b_ada):
    rows = c_all.shape[0]
    n_out = w_ada.shape[1]
    tn = 1024
    return pl.pallas_call(
        _ada_kernel,
        grid=(n_out // tn,),
        in_specs=[pl.BlockSpec((rows, D_MODEL), lambda n: (0, 0)),
                  pl.BlockSpec((D_MODEL, tn), lambda n: (0, n)),
                  pl.BlockSpec((1, tn), lambda n: (0, n))],
        out_specs=pl.BlockSpec((rows, tn), lambda n: (0, n)),
        out_shape=jax.ShapeDtypeStruct((rows, n_out), F32),
        scratch_shapes=[pltpu.VMEM((rows, D_MODEL), BF16)],
        compiler_params=_cparams(("arbitrary",)),
        name="ada_mod",
    )(c_all, w_ada, b_ada.reshape(1, n_out))


def _modnorm_kernel(xp_ref, xs_ref, g_ref, scp_ref, shp_ref, scs_ref, shs_ref, o_ref, *, n_prompt_tiles):
    is_s = pl.program_id(0) >= n_prompt_tiles

    def norm(x_ref, sc_ref, sh_ref, rows):
        x = x_ref[...]
        ms = jnp.mean(x * x, axis=-1, keepdims=True)
        y = x * lax.rsqrt(ms + EPS) * g_ref[...]
        o_ref[...] = (y * (1.0 + sc_ref[rows, :]) + sh_ref[rows, :]).astype(o_ref.dtype)

    pl.when(jnp.logical_not(is_s))(functools.partial(norm, xp_ref, scp_ref, shp_ref, slice(0, 1)))
    pl.when(is_s)(functools.partial(norm, xs_ref, scs_ref, shs_ref, slice(None)))


def _mod_specs(k, tm, n_samp, n_prompt_tiles):
    prompt = pl.BlockSpec((SUBLANES, D_MODEL), lambda i, *_: (n_samp // SUBLANES, k))
    sample = pl.BlockSpec((tm, D_MODEL), lambda i, *_: (jnp.maximum(i - n_prompt_tiles, 0), k))
    return prompt, sample


def _modnorm(x_p, x_s, g, mod):
    tp, ts = x_p.shape[0], x_s.shape[0]
    tm = TOK_TILE
    npt, nst = tp // tm, ts // tm
    row = lambda i: (jnp.minimum(i, npt - 1), 0)
    srow = lambda i: (jnp.maximum(i - npt, 0), 0)
    const = lambda i: (0, 0)
    sc_p, sc_s = _mod_specs(MOD_SC1, tm, ts, npt)
    sh_p, sh_s = _mod_specs(MOD_SH1, tm, ts, npt)
    return pl.pallas_call(
        functools.partial(_modnorm_kernel, n_prompt_tiles=npt),
        grid=(npt + nst,),
        in_specs=[pl.BlockSpec((tm, D_MODEL), row), pl.BlockSpec((tm, D_MODEL), srow),
                  pl.BlockSpec((1, D_MODEL), const), sc_p, sh_p, sc_s, sh_s],
        out_specs=pl.BlockSpec((tm, D_MODEL), lambda i: (i, 0)),
        out_shape=jax.ShapeDtypeStruct((tp + ts, D_MODEL), BF16),
        compiler_params=_cparams(("arbitrary",)),
        name="modnorm1",
    )(x_p, x_s, g, mod, mod, mod, mod)


def _inproj_kernel(h_ref, w_hbm, rc_ref, rs1_ref, rs2_ref, qg_ref, kg_ref, o_ref, wf32_ref, wbf_ref, sem, *,
                   heads_per_tile):
    n, m = pl.program_id(0), pl.program_id(1)
    tn = heads_per_tile * HEAD_DIM
    fetch = lambda t: pltpu.make_async_copy(w_hbm.at[:, pl.ds(pl.multiple_of(t * tn, LANES), tn)], wf32_ref, sem)

    @pl.when((m == 0) & (n == 0))
    def _():
        fetch(0).start()

    @pl.when(m == 0)
    def _():
        fetch(n).wait()
        wbf_ref[...] = wf32_ref[...].astype(BF16)

    @pl.when((m == 0) & (n + 1 < pl.num_programs(0)))
    def _():
        fetch(n + 1).start()

    half = ROT_DIM // 2
    q_heads = range(OFF_Q // HEAD_DIM, OFF_K // HEAD_DIM)
    k_heads = range(OFF_K // HEAD_DIM, OFF_V // HEAD_DIM)
    pair = 2 * HEAD_DIM

    def tile(col_tile):
        h = h_ref[...]
        for c0 in range(0, heads_per_tile * HEAD_DIM, pair):
            width = min(pair, heads_per_tile * HEAD_DIM - c0)
            acc = jnp.dot(h, wbf_ref[:, c0:c0 + width], preferred_element_type=F32)
            for c in range(c0, c0 + width, HEAD_DIM):
                slot = col_tile * heads_per_tile + c // HEAD_DIM
                x = acc[:, c - c0:c - c0 + HEAD_DIM]
                if slot in q_heads or slot in k_heads:
                    gain = qg_ref[...] if slot in q_heads else kg_ref[...]
                    ms = jnp.mean(x * x, axis=-1, keepdims=True)
                    y = x * lax.rsqrt(ms + EPS) * gain
                    up = pltpu.roll(y, HEAD_DIM - half, 1)
                    dn = pltpu.roll(y, half, 1)
                    x = y * rc_ref[...] + up * rs1_ref[...] + dn * rs2_ref[...]
                o_ref[:, c:c + HEAD_DIM] = x

    for col_tile in range(IN_COLS // (heads_per_tile * HEAD_DIM)):
        pl.when(n == col_tile)(functools.partial(tile, col_tile))


def _inproj(h, w_in, rc, rs1, rs2, qg, kg):
    n_tok = h.shape[0]
    tm, tn = PROJ_TM, PROJ_TN
    tab = pl.BlockSpec((tm, HEAD_DIM), lambda n, m: (m, 0))
    gain = pl.BlockSpec((1, HEAD_DIM), lambda n, m: (0, 0))
    return pl.pallas_call(
        functools.partial(_inproj_kernel, heads_per_tile=tn // HEAD_DIM),
        grid=(IN_COLS // tn, n_tok // tm),
        in_specs=[pl.BlockSpec((tm, D_MODEL), lambda n, m: (m, 0)),
                  pl.BlockSpec(memory_space=pl.ANY),
                  tab, tab, tab, gain, gain],
        out_specs=pl.BlockSpec((tm, tn), lambda n, m: (m, n)),
        out_shape=jax.ShapeDtypeStruct((n_tok, IN_COLS), F32),
        scratch_shapes=[pltpu.VMEM((D_MODEL, tn), F32), pltpu.VMEM((D_MODEL, tn), BF16),
                        pltpu.SemaphoreType.DMA(())],
        compiler_params=_cparams(("arbitrary", "arbitrary")),
        name="in_proj",
    )(h, w_in, rc, rs1, rs2, qg, kg)


def _ssm_prep_kernel(are_ref, aim_ref, ldt_ref, arer_ref, aimr_ref, ldtr_ref, bre_ref, bim_ref,
                     pre_ref, pim_ref, bbre_ref, bbim_ref):
    def discretise(a_re, a_im, log_dt):
        dt = jnp.exp(log_dt)
        mag = jnp.exp(a_re * dt)
        return mag * jnp.cos(a_im * dt), mag * jnp.sin(a_im * dt)

    ab_re, ab_im = discretise(are_ref[...], aim_ref[...], ldt_ref[...])
    p_re, p_im = ab_re, ab_im
    for i in range(SCAN_L):
        pre_ref[i:i + 1, :] = p_re
        pim_ref[i:i + 1, :] = p_im
        p_re, p_im = p_re * ab_re - p_im * ab_im, p_re * ab_im + p_im * ab_re

    a_re, a_im = arer_ref[...], aimr_ref[...]
    r_re, r_im = discretise(a_re, a_im, ldtr_ref[...])
    nr, ni = r_re - 1.0, r_im
    den = a_re * a_re + a_im * a_im
    z_re = (nr * a_re + ni * a_im) / den
    z_im = (ni * a_re - nr * a_im) / den
    b_re, b_im = bre_ref[...], bim_ref[...]
    bbre_ref[...] = z_re * b_re - z_im * b_im
    bbim_ref[...] = z_re * b_im + z_im * b_re


def _ssm_prep(a_re, a_im, log_dt, b_re, b_im):
    g, p, n = b_re.shape
    flat = lambda x: x.reshape(1, g * p)
    rep = lambda x: jnp.repeat(x, n, axis=1)
    ldt_gp = jnp.broadcast_to(log_dt[:, None], (g, p))
    ldt_rep = jnp.broadcast_to(log_dt[:, None], (g, p * n))
    out_shape = [jax.ShapeDtypeStruct((SCAN_L, g * p), F32)] * 2 + [jax.ShapeDtypeStruct((g, p * n), F32)] * 2
    return pl.pallas_call(_ssm_prep_kernel, out_shape=out_shape, name="ssm_prep")(
        flat(a_re), flat(a_im), flat(ldt_gp), rep(a_re), rep(a_im), ldt_rep,
        b_re.reshape(g, p * n), b_im.reshape(g, p * n))


def _ssm_block_matrices(bb_re, bb_im, c_re, c_im):
    g, p, n = SSM_GROUPS, SSM_STATE, SSM_GROUP
    nb = g // SSM_BLK
    eye = jnp.eye(SSM_BLK, dtype=F32)

    def in_mat(bb):
        x = bb.reshape(nb, SSM_BLK, p, n)
        return jnp.einsum('bgpm,gh->bgmhp', x, eye).reshape(nb, SSM_BLK * n, SSM_BLK * p)

    def out_mat(c):
        x = c.reshape(nb, SSM_BLK, n, p)
        return jnp.einsum('bgnp,gh->bgphn', x, eye).reshape(nb, SSM_BLK * p, SSM_BLK * n)

    b_mat = jnp.concatenate([in_mat(bb_re), in_mat(bb_im)], axis=2).astype(BF16)
    c_mat = jnp.concatenate([out_mat(c_re), -out_mat(c_im)], axis=1).astype(BF16)
    return b_mat, c_mat


def _cmul_add(a_re, a_im, s_re, s_im, b_re, b_im):
    return a_re * s_re - a_im * s_im + b_re, a_re * s_im + a_im * s_re + b_im


def _s5_prompt_kernel(*refs):
    par = SSM_PAR
    u_refs = refs[:par]
    (d_ref, pre_ref, pim_ref, bm_ref, cm_ref, y_ref, fre_ref, fim_ref,
     up_scr, bu_scr, lhs_scr, in_re_scr, in_im_scr, car_re, car_im, yn_scr) = refs[par:]
    nc, ln, w = SCAN_NC, SCAN_L, SSM_BLK_STATE

    @pl.when(pl.program_id(1) == 0)
    def _():
        car_re[...] = jnp.zeros_like(car_re)
        car_im[...] = jnp.zeros_like(car_im)

    for b in range(par):
        lanes = slice(b * w, (b + 1) * w)
        rows = lambda i: slice(i * nc, (i + 1) * nc)
        for i in range(ln):
            up_scr[b, rows(i), :] = u_refs[b][pl.ds(i, nc, stride=ln), :]
        up = up_scr[b]
        bu_scr[b] = jnp.dot(up.astype(BF16), bm_ref[b], preferred_element_type=F32)

        a_re = jnp.broadcast_to(pre_ref[0:1, lanes], (nc, w))
        a_im = jnp.broadcast_to(pim_ref[0:1, lanes], (nc, w))
        s_re = s_im = jnp.zeros((nc, w), F32)
        for i in range(ln):
            s_re, s_im = _cmul_add(a_re, a_im, s_re, s_im, bu_scr[b, rows(i), 0:w], bu_scr[b, rows(i), w:2 * w])
            bu_scr[b, rows(i), 0:w] = s_re
            bu_scr[b, rows(i), w:2 * w] = s_im

        al_re, al_im = pre_ref[ln - 1:ln, lanes], pim_ref[ln - 1:ln, lanes]
        c_re, c_im = car_re[b], car_im[b]
        for c in range(nc):
            in_re_scr[b, c:c + 1, :] = c_re
            in_im_scr[b, c:c + 1, :] = c_im
            c_re, c_im = _cmul_add(al_re, al_im, c_re, c_im, s_re[c:c + 1, :], s_im[c:c + 1, :])
        car_re[b] = c_re
        car_im[b] = c_im
        fre_ref[:, lanes] = c_re
        fim_ref[:, lanes] = c_im
        in_re, in_im = in_re_scr[b], in_im_scr[b]

        for i in range(ln):
            p_re = jnp.broadcast_to(pre_ref[i:i + 1, lanes], (nc, w))
            p_im = jnp.broadcast_to(pim_ref[i:i + 1, lanes], (nc, w))
            f_re, f_im = _cmul_add(p_re, p_im, in_re, in_im, bu_scr[b, rows(i), 0:w], bu_scr[b, rows(i), w:2 * w])
            lhs_scr[b, rows(i), 0:w] = f_re.astype(BF16)
            lhs_scr[b, rows(i), w:2 * w] = f_im.astype(BF16)

        y = (jnp.dot(lhs_scr[b], cm_ref[b], preferred_element_type=F32)
             + d_ref[:, b * LANES:(b + 1) * LANES] * up)
        for i in range(ln):
            yn_scr[b, pl.ds(i, nc, stride=ln), :] = y[rows(i), :]
        y_ref[:, b * LANES:(b + 1) * LANES] = _gelu_tanh(yn_scr[b]).astype(y_ref.dtype)


def _s5_prompt(proj, n_prompt, d_skip, pw_re, pw_im, b_mat, c_mat):
    par = SSM_PAR
    nb = SSM_GROUPS // SSM_BLK // par
    tt, w = SCAN_TT, SSM_BLK_STATE
    u_spec = lambda b: pl.BlockSpec((tt, LANES), lambda j, i: (i, par * j + b))
    return pl.pallas_call(
        _s5_prompt_kernel,
        grid=(nb, n_prompt // tt),
        in_specs=[u_spec(b) for b in range(par)] + [
            pl.BlockSpec((1, par * LANES), lambda j, i: (0, j)),
            pl.BlockSpec((SCAN_L, par * w), lambda j, i: (0, j)),
            pl.BlockSpec((SCAN_L, par * w), lambda j, i: (0, j)),
            pl.BlockSpec((par, LANES, 2 * w), lambda j, i: (j, 0, 0)),
            pl.BlockSpec((par, 2 * w, LANES), lambda j, i: (j, 0, 0))],
        out_specs=[pl.BlockSpec((tt, par * LANES), lambda j, i: (i, j)),
                   pl.BlockSpec((1, par * w), lambda j, i: (0, j)),
                   pl.BlockSpec((1, par * w), lambda j, i: (0, j))],
        out_shape=[jax.ShapeDtypeStruct((n_prompt, SSM_WIDTH), BF16),
                   jax.ShapeDtypeStruct((1, SSM_FLAT), F32),
                   jax.ShapeDtypeStruct((1, SSM_FLAT), F32)],
        scratch_shapes=[pltpu.VMEM((par, tt, LANES), F32), pltpu.VMEM((par, tt, 2 * w), F32),
                        pltpu.VMEM((par, tt, 2 * w), BF16),
                        pltpu.VMEM((par, SCAN_NC, w), F32), pltpu.VMEM((par, SCAN_NC, w), F32),
                        pltpu.VMEM((par, 1, w), F32), pltpu.VMEM((par, 1, w), F32),
                        pltpu.VMEM((par, tt, LANES), F32)],
        compiler_params=_cparams(("arbitrary", "arbitrary")),
        name="s5_prompt",
    )(*([proj] * par), d_skip, pw_re, pw_im, b_mat, c_mat)


def _s5_sample_kernel(u_ref, d_ref, pre_ref, pim_ref, bm_ref, cm_ref, s0re_ref, s0im_ref,
                      y_ref, fre_ref, fim_ref, up_scr, bu_scr, lhs_scr, yn_scr, *, n_seq, n_step):
    w = SSM_BLK_STATE
    rb = 16
    for s in range(n_step):
        up_scr[s * n_seq:(s + 1) * n_seq, :] = u_ref[pl.ds(s, n_seq, stride=n_step), :]
    up = up_scr[...]
    bu_scr[...] = jnp.dot(up.astype(BF16), bm_ref[0], preferred_element_type=F32)
    a_re = jnp.broadcast_to(pre_ref[0:1, :], (rb, w))
    a_im = jnp.broadcast_to(pim_ref[0:1, :], (rb, w))

    def seq_block(b, _):
        r0 = pl.multiple_of(b * rb, rb)
        s_re, s_im = s0re_ref[pl.ds(r0, rb), :], s0im_ref[pl.ds(r0, rb), :]
        for s in range(n_step):
            rows = pl.ds(pl.multiple_of(s * n_seq + r0, rb), rb)
            s_re, s_im = _cmul_add(a_re, a_im, s_re, s_im, bu_scr[rows, 0:w], bu_scr[rows, w:2 * w])
            lhs_scr[rows, 0:w] = s_re.astype(BF16)
            lhs_scr[rows, w:2 * w] = s_im.astype(BF16)
        fre_ref[pl.ds(r0, rb), :] = s_re
        fim_ref[pl.ds(r0, rb), :] = s_im
        return 0

    lax.fori_loop(0, n_seq // rb, seq_block, 0)
    y = jnp.dot(lhs_scr[...], cm_ref[0], preferred_element_type=F32) + d_ref[...] * up
    for s in range(n_step):
        yn_scr[pl.ds(s, n_seq, stride=n_step), :] = y[s * n_seq:(s + 1) * n_seq, :]
    y_ref[...] = _gelu_tanh(yn_scr[...]).astype(y_ref.dtype)


def _s5_sample(proj, n_prompt, n_seq, n_step, d_skip, pw_re, pw_im, b_mat, c_mat, s0_re, s0_im):
    nb = SSM_GROUPS // SSM_BLK
    rows, w = n_seq * n_step, SSM_BLK_STATE
    rblk = n_prompt // rows
    return pl.pallas_call(
        functools.partial(_s5_sample_kernel, n_seq=n_seq, n_step=n_step),
        grid=(nb,),
        in_specs=[pl.BlockSpec((rows, LANES), lambda j: (rblk, j)),
                  pl.BlockSpec((1, LANES), lambda j: (0, j)),
                  pl.BlockSpec((SCAN_L, w), lambda j: (0, j)),
                  pl.BlockSpec((SCAN_L, w), lambda j: (0, j)),
                  pl.BlockSpec((1, LANES, 2 * w), lambda j: (j, 0, 0)),
                  pl.BlockSpec((1, 2 * w, LANES), lambda j: (j, 0, 0)),
                  pl.BlockSpec((n_seq, w), lambda j: (0, j)),
                  pl.BlockSpec((n_seq, w), lambda j: (0, j))],
        out_specs=[pl.BlockSpec((rows, LANES), lambda j: (0, j)),
                   pl.BlockSpec((n_seq, w), lambda j: (0, j)),
                   pl.BlockSpec((n_seq, w), lambda j: (0, j))],
        out_shape=[jax.ShapeDtypeStruct((rows, SSM_WIDTH), BF16),
                   jax.ShapeDtypeStruct((n_seq, SSM_FLAT), F32),
                   jax.ShapeDtypeStruct((n_seq, SSM_FLAT), F32)],
        scratch_shapes=[pltpu.VMEM((rows, LANES), F32), pltpu.VMEM((rows, 2 * w), F32),
                        pltpu.VMEM((rows, 2 * w), BF16), pltpu.VMEM((rows, LANES), F32)],
        compiler_params=_cparams(("arbitrary",)),
        name="s5_sample",
    )(proj, d_skip, pw_re, pw_im, b_mat, c_mat, s0_re, s0_im)


def _attn_prompt_kernel(*refs):
    ins, o_ref, scr = refs[:15], refs[15], refs[16:]
    sb = pl.program_id(0)
    scale = HEAD_DIM ** -0.5
    qi = lax.broadcasted_iota(jnp.int32, (BAND, 2 * BAND), 0)
    kj = lax.broadcasted_iota(jnp.int32, (BAND, 2 * BAND), 1)
    dist = qi + BAND - kj
    band_ok = (dist >= 0) & (dist <= BAND)

    for g, (_, dil) in enumerate(DILATION_PATTERNS):
        q_ref, k_ref, v_ref, kp_ref, vp_ref = ins[5 * g:5 * g + 5]
        kbuf, vbuf, o_scr, m_scr, l_scr = scr[5 * g:5 * g + 5]
        pre = BAND * dil
        kbuf[0:pre, :] = kp_ref[...]
        kbuf[pre:pre + ATT_SB, :] = k_ref[...]
        vbuf[0:pre, :] = vp_ref[...]
        vbuf[pre:pre + ATT_SB, :] = v_ref[...]
        nblk = ATT_SB // pre

        def block(idx, _, dil=dil, pre=pre, nblk=nblk, q_ref=q_ref, kbuf=kbuf, vbuf=vbuf,
                  o_scr=o_scr, m_scr=m_scr, l_scr=l_scr):
            r = idx // nblk
            b = idx - r * nblk
            row0 = r + b * pre
            if dil == 1:
                q_rows = pl.ds(pl.multiple_of(row0, BAND), BAND)
                kv_rows = pl.ds(pl.multiple_of(row0, BAND), 2 * BAND)
            else:
                q_rows = pl.ds(row0, BAND, stride=dil)
                kv_rows = pl.ds(row0, 2 * BAND, stride=dil)
            q = (q_ref[q_rows, :] * scale).astype(BF16)
            kw = kbuf[kv_rows, :].astype(BF16)
            vw = vbuf[kv_rows, :].astype(BF16)
            s = lax.dot_general(q, kw, (((1,), (1,)), ((), ())), preferred_element_type=F32)
            s = jnp.where(band_ok & ((kj >= BAND) | (sb > 0) | (b > 0)), s, NEG)
            m = jnp.max(s, axis=-1, keepdims=True)
            p = jnp.exp(s - m)
            l = jnp.sum(p, axis=-1, keepdims=True)
            o = jnp.dot(p.astype(BF16), vw, preferred_element_type=F32)
            o_scr[q_rows, :] = o
            m_scr[q_rows, :] = jnp.broadcast_to(m, (BAND, HEAD_DIM))
            l_scr[q_rows, :] = jnp.broadcast_to(l, (BAND, HEAD_DIM))
            return 0

        lax.fori_loop(0, ATT_SB // BAND, block, 0, unroll=ATT_UNROLL)

    ms = [scr[5 * g + 3][...] for g in range(N_PATTERNS)]
    mx = jnp.maximum(jnp.maximum(ms[0], ms[1]), ms[2])
    num = jnp.zeros((ATT_SB, HEAD_DIM), F32)
    den = jnp.zeros((ATT_SB, HEAD_DIM), F32)
    for g in range(N_PATTERNS):
        wgt = jnp.exp(ms[g] - mx)
        num = num + wgt * scr[5 * g + 2][...]
        den = den + wgt * scr[5 * g + 4][...]
    o_ref[...] = num / den


def _attn_prompt(proj, n_prompt):
    hcol = lambda off, g, j: (off + g * GROUP_WIDTH) // HEAD_DIM + j
    in_specs, scratch = [], []
    for g, (_, dil) in enumerate(DILATION_PATTERNS):
        pre = BAND * dil
        per = ATT_SB // pre
        cur = lambda off, g=g: pl.BlockSpec((ATT_SB, HEAD_DIM), lambda sb, j: (sb, hcol(off, g, j)))
        prev = lambda off, g=g, per=per, pre=pre: pl.BlockSpec(
            (pre, HEAD_DIM), lambda sb, j: (jnp.maximum(sb * per - 1, 0), hcol(off, g, j)))
        in_specs += [cur(OFF_Q), cur(OFF_K), cur(OFF_V), prev(OFF_K), prev(OFF_V)]
        scratch += [pltpu.VMEM((pre + ATT_SB, HEAD_DIM), F32), pltpu.VMEM((pre + ATT_SB, HEAD_DIM), F32),
                    pltpu.VMEM((ATT_SB, HEAD_DIM), F32), pltpu.VMEM((ATT_SB, HEAD_DIM), F32),
                    pltpu.VMEM((ATT_SB, HEAD_DIM), F32)]
    return pl.pallas_call(
        _attn_prompt_kernel,
        grid=(n_prompt // ATT_SB, HEADS_PER_GROUP),
        in_specs=in_specs,
        out_specs=pl.BlockSpec((ATT_SB, HEAD_DIM), lambda sb, j: (sb, j)),
        out_shape=jax.ShapeDtypeStruct((n_prompt, GROUP_WIDTH), F32),
        scratch_shapes=scratch,
        compiler_params=_cparams(("arbitrary", "arbitrary")),
        name="attn_prompt",
    )(*([proj] * 15))


SEQ_PER_STEP = 2
KV_PLANES = 2 * HEADS_PER_GROUP


def _sample_bias(n_step):
    rows = HEADS_PER_GROUP * n_step
    step = np.arange(rows) % n_step
    cache_bias, new_bias = [], []
    for (window, dil) in DILATION_PATTERNS:
        wb = min(window, PAST_LEN)
        band = window // dil
        if dil > n_step:
            res, i = np.meshgrid(np.arange(n_step), np.arange(wb // dil), indexing='ij')
            c = (i * dil + res).reshape(-1)
        else:
            c = np.arange(wb)
        delta = wb + step[:, None] - c[None, :]
        ok = (delta >= 0) & (delta % dil == 0) & (delta // dil <= band)
        cache_bias.append(np.where(ok, 0.0, NEG).astype(np.float32))
        nb = np.full((SEQ_PER_STEP, rows, LANES), NEG, np.float32)
        for a in range(SEQ_PER_STEP):
            for sp in range(n_step):
                dl = step - sp
                okn = (dl >= 0) & (dl % dil == 0) & (dl // dil <= band)
                nb[a, :, a * n_step + sp] = np.where(okn, 0.0, NEG)
        new_bias.append(nb)
    return cache_bias, new_bias


def _attn_sample_kernel(*refs, n_step, cache_pieces):
    (q0, k0, v0, q1, k1, v1, q2, k2, v2) = refs[:9]
    qs, ks, vs = (q0, q1, q2), (k0, k1, k2), (v0, v1, v2)
    refs = list(refs[9:])
    caches = [[refs.pop(0) for _ in range(n)] for n in cache_pieces]
    (cb0, cb1, cb2, nb0, nb1, nb2, o_ref) = refs
    cbias, nbias = (cb0, cb1, cb2), (nb0, nb1, nb2)
    rows = HEADS_PER_GROUP * n_step
    gw = GROUP_WIDTH
    scale = HEAD_DIM ** -0.5
    row_head = lax.broadcasted_iota(jnp.int32, (rows, gw), 0) // n_step
    lane_head = lax.broadcasted_iota(jnp.int32, (rows, gw), 1) // HEAD_DIM
    own_head = row_head == lane_head
    nt = (((1,), (1,)), ((), ()))
    pad = jnp.zeros((LANES - SEQ_PER_STEP * n_step, gw), F32)

    def planes(load):
        k = jnp.concatenate([load(h) for h in range(HEADS_PER_GROUP)], axis=1)
        v = jnp.concatenate([load(HEADS_PER_GROUP + h) for h in range(HEADS_PER_GROUP)], axis=1)
        return k.astype(BF16), v.astype(BF16)

    for a in range(SEQ_PER_STEP):
        pieces = []
        for g, (window, dil) in enumerate(DILATION_PATTERNS):
            q = qs[g][a * n_step:(a + 1) * n_step, :] * scale
            qbd = jnp.where(own_head, jnp.concatenate([q] * HEADS_PER_GROUP, axis=0), 0.0).astype(BF16)
            kvs = []
            for cache in caches[g]:
                if len(cache.shape) == 3:
                    cache = cache.reshape(cache.shape[0] * KV_PLANES, HEAD_DIM)
                per_seq = cache.shape[0] // SEQ_PER_STEP
                kvs.append(planes(lambda p, cache=cache, per_seq=per_seq: cache[
                    pl.ds(a * per_seq + p, per_seq // KV_PLANES, stride=KV_PLANES), :]))
            sc = jnp.concatenate([lax.dot_general(qbd, k, nt, preferred_element_type=F32) for k, _ in kvs], axis=1)
            pieces.append((sc + cbias[g][...], [v for _, v in kvs]))
            k_new = jnp.concatenate([ks[g][...], pad], axis=0).astype(BF16)
            v_new = jnp.concatenate([vs[g][...], pad], axis=0).astype(BF16)
            sn = lax.dot_general(qbd, k_new, nt, preferred_element_type=F32) + nbias[g][a]
            pieces.append((sn, [v_new]))
        m = functools.reduce(jnp.maximum, [jnp.max(s, axis=-1, keepdims=True) for s, _ in pieces])
        l = jnp.zeros((rows, 1), F32)
        acc = jnp.zeros((rows, gw), F32)
        for s, vals in pieces:
            p = jnp.exp(s - m)
            l = l + jnp.sum(p, axis=-1, keepdims=True)
            pb = p.astype(BF16)
            nk = pb.shape[1] // len(vals)
            for r, v in enumerate(vals):
                acc = acc + jnp.dot(pb[:, r * nk:(r + 1) * nk], v, preferred_element_type=F32)
        acc = jnp.where(own_head, acc, 0.0)
        o16 = functools.reduce(lambda x, y: x + y,
                               [acc[:, h * HEAD_DIM:(h + 1) * HEAD_DIM] for h in range(HEADS_PER_GROUP)]) / l
        for h in range(HEADS_PER_GROUP):
            o_ref[a * n_step:(a + 1) * n_step, h * HEAD_DIM:(h + 1) * HEAD_DIM] = o16[h * n_step:(h + 1) * n_step, :]


def _attn_sample(proj, n_prompt, n_seq, n_step, caches):
    rows = SEQ_PER_STEP * n_step
    rblk = n_prompt // rows
    cache_bias, new_bias = _sample_bias(n_step)
    tok = lambda off, g: pl.BlockSpec((rows, GROUP_WIDTH), lambda i: (rblk + i, (off + g * GROUP_WIDTH) // GROUP_WIDTH))
    in_specs, args, cache_pieces = [], [], []
    for g in range(N_PATTERNS):
        in_specs += [tok(OFF_Q, g), tok(OFF_K, g), tok(OFF_V, g)]
        args += [proj, proj, proj]
    for g, (window, dil) in enumerate(DILATION_PATTERNS):
        c = caches[g]
        wb = c.shape[1]
        if dil > n_step:
            assert wb % dil == 0
            c = c.reshape(n_seq * (wb // dil), dil * KV_PLANES, HEAD_DIM)
            for r in range(n_step):
                in_specs.append(pl.BlockSpec((SEQ_PER_STEP * (wb // dil), KV_PLANES, HEAD_DIM),
                                             lambda i, r=r: (i, r, 0)))
                args.append(c)
            cache_pieces.append(n_step)
        else:
            c = c.reshape(n_seq * wb * KV_PLANES, HEAD_DIM)
            in_specs.append(pl.BlockSpec((SEQ_PER_STEP * wb * KV_PLANES, HEAD_DIM), lambda i: (i, 0)))
            args.append(c)
            cache_pieces.append(1)
    for b in cache_bias:
        in_specs.append(pl.BlockSpec(b.shape, lambda i: (0, 0)))
        args.append(jnp.asarray(b))
    for b in new_bias:
        in_specs.append(pl.BlockSpec(b.shape, lambda i: (0, 0, 0)))
        args.append(jnp.asarray(b))
    return pl.pallas_call(
        functools.partial(_attn_sample_kernel, n_step=n_step, cache_pieces=tuple(cache_pieces)),
        grid=(n_seq // SEQ_PER_STEP,),
        in_specs=in_specs,
        out_specs=pl.BlockSpec((rows, GROUP_WIDTH), lambda i: (i, 0)),
        out_shape=jax.ShapeDtypeStruct((n_seq * n_step, GROUP_WIDTH), F32),
        compiler_params=_cparams(("arbitrary",)),
        name="attn_sample",
    )(*args)


def _mix_kernel(yp_ref, ys_ref, op_ref, os_ref, wa_ref, wb_ref, wbr_ref, ga_ref, gb_ref, o_ref,
                wa_bf, wb_bf, wbr_bf, *, n_prompt_tiles):
    i = pl.program_id(1)

    @pl.when(i == 0)
    def _():
        wa_bf[...] = wa_ref[...].astype(BF16)
        wb_bf[...] = wb_ref[...].astype(BF16)
        wbr_bf[...] = wbr_ref[...].astype(BF16)

    is_s = i >= n_prompt_tiles
    for rows in _row_halves(o_ref.shape[0]):
        y = jnp.where(is_s, ys_ref[rows, :], yp_ref[rows, :])
        o = jnp.where(is_s, os_ref[rows, :], op_ref[rows, :]).astype(BF16)
        glu_a = jnp.dot(y, wa_bf[...], preferred_element_type=F32)
        glu_b = jnp.dot(y, wb_bf[...], preferred_element_type=F32)
        branch_a = glu_a * _sigmoid(glu_b)
        branch_b = jnp.dot(o, wbr_bf[...], preferred_element_type=F32)
        o_ref[rows, :] = (_sigmoid(ga_ref[rows, :]) * branch_a
                          + _sigmoid(gb_ref[rows, :]) * branch_b).astype(o_ref.dtype)


def _mix(y_p, y_s, o_p, o_s, w_glu, w_attn_br, proj):
    tp, ts = y_p.shape[0], y_s.shape[0]
    tm, tn = TOK_TILE, 512
    assert OFF_G % tn == 0
    npt, nst = tp // tm, ts // tm
    ncol = D_MODEL // tn
    prow = lambda n, i: (jnp.minimum(i, npt - 1), 0)
    srow = lambda n, i: (jnp.maximum(i - npt, 0), 0)
    return pl.pallas_call(
        functools.partial(_mix_kernel, n_prompt_tiles=npt),
        grid=(ncol, npt + nst),
        in_specs=[pl.BlockSpec((tm, SSM_WIDTH), prow), pl.BlockSpec((tm, SSM_WIDTH), srow),
                  pl.BlockSpec((tm, GROUP_WIDTH), prow), pl.BlockSpec((tm, GROUP_WIDTH), srow),
                  pl.BlockSpec((SSM_WIDTH, tn), lambda n, i: (0, n)),
                  pl.BlockSpec((SSM_WIDTH, tn), lambda n, i: (0, ncol + n)),
                  pl.BlockSpec((GROUP_WIDTH, tn), lambda n, i: (0, n)),
                  pl.BlockSpec((tm, tn), lambda n, i: (i, OFF_G // tn + n)),
                  pl.BlockSpec((tm, tn), lambda n, i: (i, OFF_G // tn + ncol + n))],
        out_specs=pl.BlockSpec((tm, tn), lambda n, i: (i, n)),
        out_shape=jax.ShapeDtypeStruct((tp + ts, D_MODEL), BF16),
        scratch_shapes=[pltpu.VMEM((SSM_WIDTH, tn), BF16), pltpu.VMEM((SSM_WIDTH, tn), BF16),
                        pltpu.VMEM((GROUP_WIDTH, tn), BF16)],
        compiler_params=_cparams(("arbitrary", "arbitrary")),
        name="glu_mix",
    )(y_p, y_s, o_p, o_s, w_glu, w_glu, w_attn_br, proj, proj)


def _route(logits):
    lane = lax.broadcasted_iota(jnp.int32, logits.shape, 1).astype(F32)
    big = 1000.0
    first = lambda cond: jnp.min(jnp.where(cond, lane, big), axis=-1, keepdims=True)
    is_g = lane < N_EXPERT_GROUPS
    lg = jnp.where(is_g, logits, NEG)
    mg = jnp.max(lg, axis=-1, keepdims=True)
    g_sel = first(lg == mg)
    p_group = 1.0 / jnp.sum(jnp.where(is_g, jnp.exp(lg - mg), 0.0), axis=-1, keepdims=True)
    e_lo = N_EXPERT_GROUPS + EXPERTS_PER_GROUP * g_sel
    le = jnp.where((lane >= e_lo) & (lane < e_lo + EXPERTS_PER_GROUP), logits, NEG)
    v1 = jnp.max(le, axis=-1, keepdims=True)
    i1 = first(le == v1)
    le2 = jnp.where(lane == i1, NEG, le)
    v2 = jnp.max(le2, axis=-1, keepdims=True)
    i2 = first(le2 == v2)
    e2 = jnp.exp(v2 - v1)
    w1 = p_group / (1.0 + e2)
    w2 = p_group * e2 / (1.0 + e2)
    pick = lambda k, val: jnp.where(lane == k, val, 0.0)
    return (pick(ROUTE_E1, i1 - N_EXPERT_GROUPS) + pick(ROUTE_E2, i2 - N_EXPERT_GROUPS)
            + pick(ROUTE_W1, w1) + pick(ROUTE_W2, w2))


def _outproj_kernel(mix_ref, w_ref, xp_ref, xs_ref, g_ref, wr_ref, br_ref,
                    gtp_ref, scp_ref, shp_ref, gts_ref, scs_ref, shs_ref,
                    x1_ref, h2_ref, route_ref, *, n_prompt_tiles):
    is_s = pl.program_id(0) >= n_prompt_tiles

    def split(v):
        high = v.astype(BF16)
        return high, (v - high.astype(F32)).astype(BF16)

    x = jnp.where(is_s, xs_ref[...], xp_ref[...])
    gt = jnp.where(is_s, gts_ref[...], gtp_ref[0:1, :])
    sc = jnp.where(is_s, scs_ref[...], scp_ref[0:1, :])
    sh = jnp.where(is_s, shs_ref[...], shp_ref[0:1, :])
    x1 = x + gt * jnp.dot(mix_ref[...], w_ref[...], preferred_element_type=F32)
    x1_ref[...] = x1
    ms = jnp.mean(x1 * x1, axis=-1, keepdims=True)
    h2 = (x1 * lax.rsqrt(ms + EPS) * g_ref[...]) * (1.0 + sc) + sh
    h2_ref[...] = h2.astype(h2_ref.dtype)
    r = jnp.dot(jnp.concatenate(split(h2), axis=0), jnp.concatenate(split(wr_ref[...]), axis=1),
                preferred_element_type=F32)
    n = h2.shape[0]
    logits = (r[:n, :LANES] + r[:n, LANES:]) + (r[n:, :LANES] + r[n:, LANES:]) + br_ref[...]
    route_ref[...] = _route(logits)


def _outproj(mixed, w_out_bf, x_p, x_s, g2, w_router, b_router, mod):
    tp, ts = x_p.shape[0], x_s.shape[0]
    tm = SMALL_TOK_TILE
    npt, nst = tp // tm, ts // tm
    prow = lambda i: (jnp.minimum(i, npt - 1), 0)
    srow = lambda i: (jnp.maximum(i - npt, 0), 0)
    const = lambda i: (0, 0)
    vec = pl.BlockSpec((1, D_MODEL), const)
    gt_p, gt_s = _mod_specs(MOD_GT1, tm, ts, npt)
    sc_p, sc_s = _mod_specs(MOD_SC2, tm, ts, npt)
    sh_p, sh_s = _mod_specs(MOD_SH2, tm, ts, npt)
    full = lambda i: (i, 0)
    return pl.pallas_call(
        functools.partial(_outproj_kernel, n_prompt_tiles=npt),
        grid=(npt + nst,),
        in_specs=[pl.BlockSpec((tm, D_MODEL), full), pl.BlockSpec((D_MODEL, D_MODEL), const),
                  pl.BlockSpec((tm, D_MODEL), prow), pl.BlockSpec((tm, D_MODEL), srow),
                  vec, pl.BlockSpec((D_MODEL, LANES), const), pl.BlockSpec((1, LANES), const),
                  gt_p, sc_p, sh_p, gt_s, sc_s, sh_s],
        out_specs=[pl.BlockSpec((tm, D_MODEL), full), pl.BlockSpec((tm, D_MODEL), full),
                   pl.BlockSpec((tm, LANES), full)],
        out_shape=[jax.ShapeDtypeStruct((tp + ts, D_MODEL), F32),
                   jax.ShapeDtypeStruct((tp + ts, D_MODEL), F32),
                   jax.ShapeDtypeStruct((tp + ts, LANES), F32)],
        compiler_params=_cparams(("arbitrary",)),
        name="out_proj_norm2_router",
    )(mixed, w_out_bf, x_p, x_s, g2, w_router, b_router, mod, mod, mod, mod, mod, mod)


def _dispatch_plan(route, tm):
    e = route[:, ROUTE_E1:ROUTE_E2 + 1].astype(jnp.int32).reshape(-1)
    n_pairs = e.shape[0]
    onehot = (e[:, None] == jnp.arange(N_EXPERTS, dtype=jnp.int32)[None, :]).astype(jnp.int32)
    csum = jnp.cumsum(onehot, axis=0)
    rank = jnp.sum(onehot * csum, axis=1) - 1
    tiles_per_expert = (csum[-1] + tm - 1) // tm
    tile_end = jnp.cumsum(tiles_per_expert)
    tile_start = tile_end - tiles_per_expert
    dest = (tile_start[e] * tm + rank).astype(jnp.int32)
    max_tiles = n_pairs // tm + N_EXPERTS
    k = jnp.arange(max_tiles, dtype=jnp.int32)
    tile_expert = jnp.minimum(jnp.sum((k[:, None] >= tile_end[None, :]).astype(jnp.int32), axis=1), N_EXPERTS - 1)
    n_used = tile_end[-1].astype(jnp.int32)
    last_expert = jnp.take(tile_expert, n_used - 1)
    tile_expert = jnp.where(k < n_used, tile_expert, last_expert).astype(jnp.int32)
    pad_end = (jnp.concatenate([tile_end, tile_end[-1:]]) * tm).astype(jnp.int32)
    pad_len = (tiles_per_expert * tm - csum[-1]).astype(jnp.int32)
    ids = jnp.arange(N_EXPERTS, dtype=jnp.int32)
    has_tiles = jnp.where(tiles_per_expert > 0, ids, N_EXPERTS)
    following = jnp.min(jnp.where(ids[None, :] > ids[:, None], has_tiles[None, :], N_EXPERTS), axis=1)
    next_expert = jnp.where(following < N_EXPERTS, following, -1)[tile_expert].astype(jnp.int32)
    return dest, pad_end, pad_len, tile_expert, n_used.reshape(1), next_expert, max_tiles


def _start_pair_copies(dest_ref, tile, rows, make):
    def body(r, _):
        for k in range(2):
            make(r, k, dest_ref[(tile * rows + r) * 2 + k]).start(priority=k)
        return 0
    lax.fori_loop(0, rows, body, 0, unroll=8)


def _pair_copies(dest_ref, tile, rows, make, make_all):
    _start_pair_copies(dest_ref, tile, rows, make)
    for k in range(2):
        make_all(k).wait()


def _dispatch_kernel(dest_ref, pad_end_ref, pad_len_ref, h_ref, xs_ref, zero_buf, sem, pad_sem):
    rows = h_ref.shape[0]

    @pl.when(pl.program_id(0) == 0)
    def _():
        zero_buf[...] = jnp.zeros_like(zero_buf)

        def pad_copies(fn):
            for e in range(N_EXPERTS):
                end, left = pad_end_ref[e], pad_len_ref[e]
                size = zero_buf.shape[0]
                while size >= SUBLANES:
                    take = (left & size) != 0
                    end = end - jnp.where(take, size, 0)

                    @pl.when(take)
                    def _(start=end, size=size):
                        fn(pltpu.make_async_copy(zero_buf.at[pl.ds(0, size)],
                                                 xs_ref.at[pl.ds(pl.multiple_of(start, size), size)], pad_sem))

                    size //= 2
                for r in range(1, SUBLANES):
                    @pl.when((left & (SUBLANES - 1)) >= r)
                    def _(row=end - r):
                        fn(pltpu.make_async_copy(zero_buf.at[pl.ds(0, 1)], xs_ref.at[pl.ds(row, 1)], pad_sem))
            size = zero_buf.shape[0]
            tail = pad_end_ref[N_EXPERTS]
            for t in range(N_EXPERTS * MOE_TM // size):
                @pl.when(tail + t * size < xs_ref.shape[0])
                def _(start=tail + t * size):
                    fn(pltpu.make_async_copy(zero_buf, xs_ref.at[pl.ds(pl.multiple_of(start, size), size)], pad_sem))

        pad_copies(lambda c: c.start())
        pad_copies(lambda c: c.wait())

    make = lambda r, k, d: pltpu.make_async_copy(h_ref.at[pl.ds(r, 1)], xs_ref.at[pl.ds(d, 1)], sem)
    make_all = lambda k: pltpu.make_async_copy(h_ref, xs_ref.at[pl.ds(0, rows)], sem)
    _pair_copies(dest_ref, pl.program_id(0), rows, make, make_all)


def _dispatch(dest, pad_end, pad_len, h2, n_slots):
    n_tok = h2.shape[0]
    tm = SMALL_TOK_TILE
    return pl.pallas_call(
        _dispatch_kernel,
        grid_spec=pltpu.PrefetchScalarGridSpec(
            num_scalar_prefetch=3, grid=(n_tok // tm,),
            in_specs=[pl.BlockSpec((tm, D_MODEL), lambda i, d, ps, pn: (i, 0))],
            out_specs=pl.BlockSpec(memory_space=pl.ANY),
            scratch_shapes=[pltpu.VMEM((MOE_TM // 2, D_MODEL), F32),
                            pltpu.SemaphoreType.DMA(()), pltpu.SemaphoreType.DMA(())]),
        out_shape=jax.ShapeDtypeStruct((n_slots, D_MODEL), F32),
        compiler_params=_cparams(("arbitrary",)),
        name="moe_dispatch",
    )(dest, pad_end, pad_len, h2)


def _experts_kernel(te_ref, used_ref, next_ref, xs_ref, wgu_hbm, wd_hbm, y_ref, wgu_f32, wd_f32, wgu_bf, wd_bf, sem):
    k = pl.program_id(0)
    new_expert = (k == 0) | (te_ref[k] != te_ref[jnp.maximum(k - 1, 0)])

    def fetch(e):
        return (pltpu.make_async_copy(wgu_hbm.at[e], wgu_f32, sem.at[0]),
                pltpu.make_async_copy(wd_hbm.at[e], wd_f32, sem.at[1]))

    @pl.when(k == 0)
    def _():
        for c in fetch(te_ref[0]):
            c.start()

    @pl.when(new_expert)
    def _():
        for c in fetch(te_ref[k]):
            c.wait()
        wgu_bf[...] = wgu_f32[...].astype(BF16)
        wd_bf[...] = wd_f32[...].astype(BF16)

    @pl.when(new_expert & (next_ref[k] >= 0))
    def _():
        for c in fetch(next_ref[k]):
            c.start()

    @pl.when(k < used_ref[0])
    def _():
        gu = jnp.dot(xs_ref[...].astype(BF16), wgu_bf[...], preferred_element_type=F32)
        gate, up = gu[:, :EXPERT_FF], gu[:, EXPERT_FF:]
        act = (gate * _sigmoid(gate)) * up
        y_ref[...] = jnp.dot(act.astype(BF16), wd_bf[...], preferred_element_type=F32)

    @pl.when(k >= used_ref[0])
    def _():
        y_ref[...] = jnp.zeros_like(y_ref)


def _experts(tile_expert, n_used, next_expert, xs, w_gu, w_down, max_tiles):
    tm = MOE_TM
    row = lambda k, te, nu, nx: (jnp.minimum(k, nu[0] - 1), 0)
    return pl.pallas_call(
        _experts_kernel,
        grid_spec=pltpu.PrefetchScalarGridSpec(
            num_scalar_prefetch=3, grid=(max_tiles,),
            in_specs=[pl.BlockSpec((tm, D_MODEL), row),
                      pl.BlockSpec(memory_space=pl.ANY), pl.BlockSpec(memory_space=pl.ANY)],
            out_specs=pl.BlockSpec((tm, D_MODEL), lambda k, te, nu, nx: (k, 0)),
            scratch_shapes=[pltpu.VMEM((D_MODEL, 2 * EXPERT_FF), F32), pltpu.VMEM((EXPERT_FF, D_MODEL), F32),
                            pltpu.VMEM((D_MODEL, 2 * EXPERT_FF), BF16), pltpu.VMEM((EXPERT_FF, D_MODEL), BF16),
                            pltpu.SemaphoreType.DMA((2,))]),
        out_shape=jax.ShapeDtypeStruct((max_tiles * tm, D_MODEL), F32),
        compiler_params=_cparams(("arbitrary",)),
        name="moe_experts",
    )(tile_expert, n_used, next_expert, xs, w_gu, w_down)


def _combine_kernel(dest_ref, y_hbm, x1_ref, route_ref, gtp_ref, gts_ref, yp_ref, ys_ref, buf, sem, *, n_prompt_tiles):
    i = pl.program_id(0)
    rows = x1_ref.shape[0]

    def gather(tile, slot):
        make = lambda r, k, d: pltpu.make_async_copy(y_hbm.at[pl.ds(d, 1)], buf.at[slot, k, pl.ds(r, 1)],
                                                     sem.at[slot])
        _start_pair_copies(dest_ref, tile, rows, make)

    @pl.when(i == 0)
    def _():
        gather(0, 0)

    slot = lax.rem(i, 2)

    @pl.when(i + 1 < pl.num_programs(0))
    def _():
        gather(i + 1, 1 - slot)

    for k in range(2):
        pltpu.make_async_copy(y_hbm.at[pl.ds(0, rows)], buf.at[slot, k], sem.at[slot]).wait()
    route = route_ref[...]
    lane = lax.broadcasted_iota(jnp.int32, route.shape, 1)
    w1 = jnp.sum(jnp.where(lane == ROUTE_W1, route, 0.0), axis=-1, keepdims=True)
    w2 = jnp.sum(jnp.where(lane == ROUTE_W2, route, 0.0), axis=-1, keepdims=True)
    moe = w1 * buf[slot, 0] + w2 * buf[slot, 1]

    @pl.when(i < n_prompt_tiles)
    def _():
        yp_ref[...] = x1_ref[...] + gtp_ref[0:1, :] * moe

    @pl.when(i >= n_prompt_tiles)
    def _():
        ys_ref[...] = x1_ref[...] + gts_ref[...] * moe


def _combine(dest, y_slots, x1, route, mod, tp, ts):
    tm = SMALL_TOK_TILE
    npt, nst = tp // tm, ts // tm
    row = lambda i, d: (i, 0)
    gt_p, gt_s = _mod_specs(MOD_GT2, tm, ts, npt)
    return pl.pallas_call(
        functools.partial(_combine_kernel, n_prompt_tiles=npt),
        grid_spec=pltpu.PrefetchScalarGridSpec(
            num_scalar_prefetch=1, grid=(npt + nst,),
            in_specs=[pl.BlockSpec(memory_space=pl.ANY),
                      pl.BlockSpec((tm, D_MODEL), row), pl.BlockSpec((tm, LANES), row), gt_p, gt_s],
            out_specs=[pl.BlockSpec((tm, D_MODEL), lambda i, d: (jnp.minimum(i, npt - 1), 0)),
                       pl.BlockSpec((tm, D_MODEL), lambda i, d: (jnp.maximum(i - npt, 0), 0))],
            scratch_shapes=[pltpu.VMEM((2, 2, tm, D_MODEL), F32), pltpu.SemaphoreType.DMA((2,))]),
        out_shape=[jax.ShapeDtypeStruct((tp, D_MODEL), F32), jax.ShapeDtypeStruct((ts, D_MODEL), F32)],
        compiler_params=_cparams(("arbitrary",)),
        name="moe_combine",
    )(dest, y_slots, x1, route, mod, mod)


def _moe(h2, route, x1, w_gu, w_down, mod, tp, ts):
    dest, pad_end, pad_len, tile_expert, n_used, next_expert, max_tiles = _dispatch_plan(route, MOE_TM)
    xs = _dispatch(dest, pad_end, pad_len, h2, max_tiles * MOE_TM)
    y_slots = _experts(tile_expert, n_used, next_expert, xs, w_gu, w_down, max_tiles)
    return _combine(dest, y_slots, x1, route, mod, tp, ts)


def _kv_pack_kernel(k_ref, v_ref, o_ref):
    n = k_ref.shape[0]
    for h in range(HEADS_PER_GROUP):
        cols = slice(h * HEAD_DIM, (h + 1) * HEAD_DIM)
        o_ref[pl.ds(h, n, stride=KV_PLANES), :] = k_ref[:, cols]
        o_ref[pl.ds(HEADS_PER_GROUP + h, n, stride=KV_PLANES), :] = v_ref[:, cols]


def _kv_pack(proj, row0, rows, g):
    tr = min(rows, SMALL_TOK_TILE)
    assert rows % tr == 0 and row0 % tr == 0
    col = lambda off: (off + g * GROUP_WIDTH) // GROUP_WIDTH
    spec = lambda off: pl.BlockSpec((tr, GROUP_WIDTH), lambda i: (row0 // tr + i, col(off)))
    flat = pl.pallas_call(
        _kv_pack_kernel,
        grid=(rows // tr,),
        in_specs=[spec(OFF_K), spec(OFF_V)],
        out_specs=pl.BlockSpec((tr * KV_PLANES, HEAD_DIM), lambda i: (i, 0)),
        out_shape=jax.ShapeDtypeStruct((rows * KV_PLANES, HEAD_DIM), F32),
        compiler_params=_cparams(("arbitrary",)),
        name="kv_pack",
    )(proj, proj)
    return flat.reshape(rows, 2, HEADS_PER_GROUP, HEAD_DIM)


def _rope_tables(n_prompt, n_seq, n_step):
    half = ROT_DIM // 2
    inv_freq = ROPE_THETA ** (-jnp.arange(half, dtype=F32) / half)
    angles = lambda pos: pos.astype(F32)[:, None] * inv_freq[None, :]
    fine = 128
    assert n_prompt % fine == 0
    coarse_ang = angles(jnp.arange(n_prompt // fine, dtype=jnp.int32) * fine)[:, None, :]
    fine_ang = angles(jnp.arange(fine, dtype=jnp.int32))[None, :, :]
    cc, sc, cf, sf = jnp.cos(coarse_ang), jnp.sin(coarse_ang), jnp.cos(fine_ang), jnp.sin(fine_ang)
    cos_p = (cc * cf - sc * sf).reshape(n_prompt, half)
    sin_p = (sc * cf + cc * sf).reshape(n_prompt, half)
    step_ang = angles(PAST_LEN + jnp.arange(n_step, dtype=jnp.int32))
    cos = jnp.concatenate([cos_p, jnp.tile(jnp.cos(step_ang), (n_seq, 1))], axis=0)
    sin = jnp.concatenate([sin_p, jnp.tile(jnp.sin(step_ang), (n_seq, 1))], axis=0)
    n = n_prompt + n_seq * n_step
    one = jnp.ones((n, HEAD_DIM - ROT_DIM), F32)
    zero = jnp.zeros((n, HEAD_DIM - ROT_DIM), F32)
    zh = jnp.zeros((n, half), F32)
    rc = jnp.concatenate([cos, cos, one], axis=1)
    rs1 = jnp.concatenate([-sin, zh, zero], axis=1)
    rs2 = jnp.concatenate([zh, sin, zero], axis=1)
    return rc, rs1, rs2


def kernel(x_prompt, x_sample, cache_kv_w128, cache_kv_w512, cache_kv_w2048, state_ssm_re, state_ssm_im,
           c_prompt, c_sample, w_ada, b_ada, norm1_g, norm2_g, w_in, ssm_a_re, ssm_a_im, ssm_log_dt,
           ssm_b_re, ssm_b_im, ssm_c_re, ssm_c_im, ssm_d, w_glu, q_norm_g, k_norm_g, w_attn_br, w_out,
           w_router_group, b_router_group, w_router_expert, b_router_expert, w_expert_gate_up, w_expert_down):
    assert x_prompt.shape[0] == 1 and w_ada.shape[0] == 1
    n_prompt = x_prompt.shape[1]
    n_seq, n_step = x_sample.shape[0], x_sample.shape[1]
    n_samp = n_seq * n_step
    assert n_samp % TOK_TILE == 0 and n_prompt % ATT_SB == 0 and (n_prompt + n_samp) % PROJ_TM == 0
    x_p = x_prompt.reshape(n_prompt, D_MODEL)
    x_s = x_sample.reshape(n_samp, D_MODEL)

    c_all = jnp.concatenate([jnp.repeat(c_sample, n_step, axis=0),
                             jnp.broadcast_to(c_prompt, (SUBLANES, D_MODEL))], axis=0)
    mod = _ada(c_all, w_ada[0], b_ada[0])

    h1 = _modnorm(x_p, x_s, norm1_g[0].reshape(1, D_MODEL), mod)
    rc, rs1, rs2 = _rope_tables(n_prompt, n_seq, n_step)
    proj = _inproj(h1, w_in[0], rc, rs1, rs2, q_norm_g[0].reshape(1, HEAD_DIM), k_norm_g[0].reshape(1, HEAD_DIM))

    pw_re, pw_im, bb_re, bb_im = _ssm_prep(ssm_a_re[0], ssm_a_im[0], ssm_log_dt[0], ssm_b_re[0], ssm_b_im[0])
    b_mat, c_mat = _ssm_block_matrices(bb_re, bb_im, ssm_c_re[0], ssm_c_im[0])
    d_skip = ssm_d[0].reshape(1, SSM_WIDTH)
    yg_p, fre_p, fim_p = _s5_prompt(proj, n_prompt, d_skip, pw_re, pw_im, b_mat, c_mat)
    yg_s, fre_s, fim_s = _s5_sample(proj, n_prompt, n_seq, n_step, d_skip, pw_re, pw_im, b_mat, c_mat,
                                    state_ssm_re[0].reshape(n_seq, SSM_FLAT), state_ssm_im[0].reshape(n_seq, SSM_FLAT))

    o_p = _attn_prompt(proj, n_prompt)
    o_s = _attn_sample(proj, n_prompt, n_seq, n_step, (cache_kv_w128[0], cache_kv_w512[0], cache_kv_w2048[0]))

    mixed = _mix(yg_p, yg_s, o_p, o_s, w_glu[0], w_attn_br[0], proj)

    w_router = jnp.concatenate([w_router_group[0], w_router_expert[0],
                                jnp.zeros((D_MODEL, LANES - N_EXPERT_GROUPS - N_EXPERTS), F32)], axis=1)
    b_router = jnp.concatenate([b_router_group[0], b_router_expert[0],
                                jnp.zeros((LANES - N_EXPERT_GROUPS - N_EXPERTS,), F32)]).reshape(1, LANES)
    x1, h2, route = _outproj(mixed, w_out[0].astype(BF16), x_p, x_s, norm2_g[0].reshape(1, D_MODEL),
                             w_router, b_router, mod)
    y_p, y_s = _moe(h2, route, x1, w_expert_gate_up[0], w_expert_down[0], mod, n_prompt, n_samp)

    kv_p, kv_s = [], []
    for g, (window, _) in enumerate(DILATION_PATTERNS):
        keep = min(window, n_prompt)
        kv_p.append(_kv_pack(proj, n_prompt - keep, keep, g).reshape(1, 1, keep, 2, HEADS_PER_GROUP, HEAD_DIM))
        kv_s.append(_kv_pack(proj, n_prompt, n_samp, g).reshape(1, n_seq, n_step, 2, HEADS_PER_GROUP, HEAD_DIM))

    state_shape_p = (1, 1, SSM_GROUPS, SSM_STATE)
    state_shape_s = (1, n_seq, SSM_GROUPS, SSM_STATE)
    return (y_p.reshape(1, n_prompt, D_MODEL), y_s.reshape(n_seq, n_step, D_MODEL),
            kv_p[0], kv_p[1], kv_p[2], fre_p.reshape(state_shape_p), fim_p.reshape(state_shape_p),
            kv_s[0], kv_s[1], kv_s[2], fre_s.reshape(state_shape_s), fim_s.reshape(state_shape_s))
```
